```python
import math
import jax, jax.numpy as jnp
from jax import lax
import numpy as np

D_MODEL = 1024
BATCH = 4
SEQ = 4096
DEPTH = 2
DEC_BATCH = 128
DEC_SEQ = 8
PAST_LEN = 8192
PAGE_SIZE = 128

N_HEADS = 16
N_KV_HEADS = 2
HEAD_DIM = 64
GROUP = N_HEADS // N_KV_HEADS
ATTN_WIDTH = N_HEADS * HEAD_DIM
KV_WIDTH = N_KV_HEADS * HEAD_DIM
CONV_DIM = D_MODEL
CONV_GROUPS = 16
CONV_WIDTH = 3
WINDOW = 128
BLOCK = 128
N_BUCKETS = 32
MAX_DISTANCE = 128
D_FF = 4 * D_MODEL
N_BRANCH = 2
N_MOD = 6
RMS_EPS = 1e-6
NEG_INF = -1e30
PROJ_COLS = 3 * CONV_DIM + ATTN_WIDTH + 2 * KV_WIDTH + N_BRANCH * D_MODEL
SPLITS = (CONV_DIM, 2 * CONV_DIM, 3 * CONV_DIM, 3 * CONV_DIM + ATTN_WIDTH,
          3 * CONV_DIM + ATTN_WIDTH + KV_WIDTH, 3 * CONV_DIM + ATTN_WIDTH + 2 * KV_WIDTH)

kernel_name = "hybrid_conv_swa_sink_decoder_step"


def rms_norm(x, g):
    xf = x.astype(jnp.float32)
    y = xf * lax.rsqrt(jnp.mean(xf * xf, axis=-1, keepdims=True) + RMS_EPS)
    return (y * g.astype(jnp.float32)).astype(x.dtype)


def rel_bucket(dist):
    n = jnp.maximum(dist, 0)
    max_exact = N_BUCKETS // 2
    nf = jnp.maximum(n, 1).astype(jnp.float32)
    large = max_exact + (jnp.log(nf / max_exact) / math.log(MAX_DISTANCE / max_exact)
                         * (N_BUCKETS - max_exact)).astype(jnp.int32)
    large = jnp.minimum(large, N_BUCKETS - 1)
    return jnp.where(n < max_exact, n, large)


def window_attention(q, k, v, dist, key_ok, rel_table, sinks):
    n, lq = q.shape[:2]
    lk = k.shape[1]
    qg = q.reshape(n, lq, N_KV_HEADS, GROUP, HEAD_DIM)
    s = jnp.einsum('nqkgd,nskd->nkgqs', qg, k, preferred_element_type=jnp.float32) * (HEAD_DIM ** -0.5)
    bias = rel_table[rel_bucket(dist)].astype(jnp.float32)
    bias = jnp.transpose(bias, (2, 0, 1)).reshape(N_KV_HEADS, GROUP, lq, lk)
    valid = (dist >= 0) & (dist < WINDOW)
    mask = valid[None] & key_ok[:, None, :]
    s = jnp.where(mask[:, None, None], s + bias, NEG_INF)
    sink = jnp.broadcast_to(sinks.astype(jnp.float32).reshape(1, N_KV_HEADS, GROUP, 1, 1),
                            s.shape[:-1] + (1,))
    p = jax.nn.softmax(jnp.concatenate([s, sink], axis=-1), axis=-1)[..., :-1]
    o = jnp.einsum('nkgqs,nskd->nqkgd', p.astype(v.dtype), v)
    return o.reshape(n, lq, ATTN_WIDTH)


def prompt_attention(q, k, v, rel_table, sinks):
    b, s = q.shape[:2]
    nb = s // BLOCK
    qb = q.reshape(b * nb, BLOCK, N_HEADS, HEAD_DIM)

    def band(t):
        tb = t.reshape(b, nb, BLOCK, N_KV_HEADS, HEAD_DIM)
        prev = jnp.pad(tb, ((0, 0), (1, 0), (0, 0), (0, 0), (0, 0)))[:, :-1]
        return jnp.concatenate([prev, tb], axis=2).reshape(b * nb, 2 * BLOCK, N_KV_HEADS, HEAD_DIM)

    dist = (jnp.arange(BLOCK)[:, None] + BLOCK) - jnp.arange(2 * BLOCK)[None, :]
    key_ok = (jnp.arange(nb)[:, None] > 0) | (jnp.arange(2 * BLOCK)[None, :] >= BLOCK)
    key_ok = jnp.tile(key_ok, (b, 1))
    o = window_attention(qb, band(k), band(v), dist, key_ok, rel_table, sinks)
    w = min(WINDOW, s)
    return o.reshape(b, s, ATTN_WIDTH), k[:, -w:], v[:, -w:]


def sample_attention(q, k, v, k_buf, v_buf, rel_table, sinks):
    buf = k_buf.shape[1]
    t = q.shape[1]
    kk = jnp.concatenate([k_buf, k.astype(k_buf.dtype)], axis=1)
    vv = jnp.concatenate([v_buf, v.astype(v_buf.dtype)], axis=1)
    q_pos = buf + jnp.arange(t)
    k_pos = jnp.arange(buf + t)
    dist = q_pos[:, None] - k_pos[None, :]
    key_ok = jnp.ones((1, buf + t), dtype=bool)
    o = window_attention(q, kk, vv, dist, key_ok, rel_table, sinks)
    return o, kk[:, -buf:], vv[:, -buf:]


def short_conv(u, prefix, conv_w):
    ext = jnp.concatenate([prefix.astype(u.dtype), u], axis=1)
    L = u.shape[1]
    y = conv_w[0] * ext[:, 0:L]
    for j in range(1, CONV_WIDTH):
        y = y + conv_w[j] * ext[:, j:j + L]
    return y, ext[:, -(CONV_WIDTH - 1):]


def trunk_layer(x, c, conv_prefix, attn_fn, w_ada, b_ada, g_pre1, w_in, conv_w, w_br_conv,
                w_br_attn, w_o, sinks, g_post1, g_pre2, w_ff1, w_ff2, g_post2, rel_table):
    n, L = x.shape[:2]
    mod = jnp.einsum('bd,de->be', jax.nn.silu(c), w_ada) + b_ada
    sh1, sc1, ga1, sh2, sc2, ga2 = [m[:, None, :] for m in jnp.split(mod, N_MOD, axis=-1)]
    h = rms_norm(x, g_pre1) * (1 + sc1) + sh1
    proj = jnp.einsum('bld,de->ble', h, w_in)
    b_g, c_g, xc, q, k, v, gates = jnp.split(proj, SPLITS, axis=-1)
    conv_out, conv_tail = short_conv(c_g * xc, conv_prefix, conv_w)
    y_conv = jnp.einsum('blc,cd->bld', b_g * conv_out, w_br_conv)
    attn_out, k_tail, v_tail = attn_fn(q.reshape(n, L, N_HEADS, HEAD_DIM),
                                       k.reshape(n, L, N_KV_HEADS, HEAD_DIM),
                                       v.reshape(n, L, N_KV_HEADS, HEAD_DIM), rel_table, sinks)
    y_attn = jnp.einsum('bla,ad->bld', attn_out, w_br_attn)
    g_conv, g_attn = jnp.split(jax.nn.sigmoid(gates), N_BRANCH, axis=-1)
    mixed = jnp.einsum('bld,de->ble', g_conv * y_conv + g_attn * y_attn, w_o)
    x = x + ga1 * rms_norm(mixed, g_post1)
    h2 = rms_norm(x, g_pre2) * (1 + sc2) + sh2
    ff = jnp.einsum('blf,fd->bld', jnp.square(jax.nn.relu(jnp.einsum('bld,df->blf', h2, w_ff1))), w_ff2)
    x = x + ga2 * rms_norm(ff, g_post2)
    return x, conv_tail, k_tail, v_tail


def setup_inputs(seed: int = 0) -> dict:
    key = jax.random.key(seed)
    ks = jax.random.split(key, 26)

    def nrm(k, shape, scale):
        return jax.random.normal(k, shape, jnp.float32) * scale

    buf = min(WINDOW, PAST_LEN)
    return {
        "x_prompt": nrm(ks[0], (BATCH, SEQ, D_MODEL), 1.0),
        "x_sample": nrm(ks[1], (DEC_BATCH, DEC_SEQ, D_MODEL), 1.0),
        "c_prompt": nrm(ks[2], (BATCH, D_MODEL), 1.0),
        "c_sample": nrm(ks[3], (DEC_BATCH, D_MODEL), 1.0),
        "state_conv": nrm(ks[4], (DEPTH, DEC_BATCH, CONV_WIDTH - 1, CONV_DIM), 1.0),
        "cache_k": nrm(ks[5], (DEPTH, DEC_BATCH, buf, N_KV_HEADS, HEAD_DIM), 1.0),
        "cache_v": nrm(ks[6], (DEPTH, DEC_BATCH, buf, N_KV_HEADS, HEAD_DIM), 1.0),
        "w_ada": nrm(ks[7], (DEPTH, D_MODEL, N_MOD * D_MODEL), 0.3 * D_MODEL ** -0.5),
        "b_ada": nrm(ks[8], (DEPTH, N_MOD * D_MODEL), 0.02),
        "g_pre1": 1.0 + nrm(ks[9], (DEPTH, D_MODEL), 0.05),
        "w_in": nrm(ks[10], (DEPTH, D_MODEL, PROJ_COLS), D_MODEL ** -0.5),
        "conv_w": nrm(ks[11], (DEPTH, CONV_WIDTH, CONV_DIM), CONV_WIDTH ** -0.5),
        "w_br_conv": nrm(ks[12], (DEPTH, CONV_DIM, D_MODEL), CONV_DIM ** -0.5),
        "w_br_attn": nrm(ks[13], (DEPTH, ATTN_WIDTH, D_MODEL), ATTN_WIDTH ** -0.5),
        "w_o": nrm(ks[14], (DEPTH, D_MODEL, D_MODEL), D_MODEL ** -0.5),
        "sinks": nrm(ks[15], (DEPTH, N_HEADS), 0.5),
        "g_post1": 1.0 + nrm(ks[16], (DEPTH, D_MODEL), 0.05),
        "g_pre2": 1.0 + nrm(ks[17], (DEPTH, D_MODEL), 0.05),
        "w_ff1": nrm(ks[18], (DEPTH, D_MODEL, D_FF), D_MODEL ** -0.5),
        "w_ff2": nrm(ks[19], (DEPTH, D_FF, D_MODEL), D_FF ** -0.5),
        "g_post2": 1.0 + nrm(ks[20], (DEPTH, D_MODEL), 0.05),
        "rel_table": nrm(ks[21], (N_BUCKETS, N_HEADS), 0.5),
    }


def reference(x_prompt, x_sample, c_prompt, c_sample, state_conv, cache_k, cache_v,
              w_ada, b_ada, g_pre1, w_in, conv_w, w_br_conv, w_br_attn, w_o, sinks,
              g_post1, g_pre2, w_ff1, w_ff2, g_post2, rel_table):
    xp, xs = x_prompt, x_sample
    conv_p, k_p, v_p, conv_s, k_s, v_s = [], [], [], [], [], []
    zero_prefix = jnp.zeros((xp.shape[0], CONV_WIDTH - 1, CONV_DIM), xp.dtype)
    for l in range(DEPTH):
        weights = (w_ada[l], b_ada[l], g_pre1[l], w_in[l], conv_w[l], w_br_conv[l], w_br_attn[l],
                   w_o[l], sinks[l], g_post1[l], g_pre2[l], w_ff1[l], w_ff2[l], g_post2[l], rel_table)
        xp, ct, kt, vt = trunk_layer(xp, c_prompt, zero_prefix, prompt_attention, *weights)
        conv_p.append(ct); k_p.append(kt); v_p.append(vt)

        def samp_attn(q, k, v, tbl, snk, kb=cache_k[l], vb=cache_v[l]):
            return sample_attention(q, k, v, kb, vb, tbl, snk)

        xs, ct, kt, vt = trunk_layer(xs, c_sample, state_conv[l], samp_attn, *weights)
        conv_s.append(ct); k_s.append(kt); v_s.append(vt)
    conv_prompt = jnp.stack(conv_p)
    k_prompt = jnp.stack(k_p)
    v_prompt = jnp.stack(v_p)
    conv_sample = jnp.stack(conv_s)
    k_sample = jnp.stack(k_s)
    v_sample = jnp.stack(v_s)
    return (xp, xs, conv_prompt, k_prompt, v_prompt, conv_sample, k_sample, v_sample)
```

```python
import functools
import math

import jax
import jax.numpy as jnp
from jax import lax
from jax.experimental import pallas as pl
from jax.experimental.pallas import tpu as pltpu

D_MODEL = 1024
N_HEADS = 16
N_KV_HEADS = 2
HEAD_DIM = 64
GROUP = N_HEADS // N_KV_HEADS
ATTN_WIDTH = N_HEADS * HEAD_DIM
KV_WIDTH = N_KV_HEADS * HEAD_DIM
CONV_DIM = D_MODEL
CONV_WIDTH = 3
WINDOW = 128
BLOCK = 128
N_BUCKETS = 32
MAX_DISTANCE = 128
D_FF = 4 * D_MODEL
N_MOD = 6
RMS_EPS = 1e-6
NEG_INF = -1e30
PROJ_COLS = 3 * CONV_DIM + ATTN_WIDTH + 2 * KV_WIDTH + 2 * D_MODEL
Q_OFF = 3 * CONV_DIM
KV_OFF = Q_OFF + ATTN_WIDTH
GATE_OFF = KV_OFF + 2 * KV_WIDTH

V7X_SUBLANES = 8
V7X_VMEM_BYTES = 64 * 1024 * 1024

PROMPT_TILE = 512
SAMPLE_POS_TILE = 4
SAMPLE_SEQ_TILE = 16
CHUNK = 512

F32 = jnp.float32
BF16 = jnp.bfloat16


def _vmem_limit(block_bytes, temp_bytes):
    return int(min(2 * block_bytes + temp_bytes, V7X_VMEM_BYTES - 4 * 1024 * 1024))


def _nbytes(shape, dtype):
    return math.prod(shape) * jnp.dtype(dtype).itemsize


def _rms(x):
    return x * lax.rsqrt(jnp.mean(x * x, axis=-1, keepdims=True) + RMS_EPS)


def _dot(a, b):
    return jnp.dot(a, b, preferred_element_type=F32)


def _flat(a):
    return a.reshape(-1, a.shape[-1])


def _ada_kernel(c_ref, w_ref, b_ref, o_ref):
    c = c_ref[...]
    s = c * (1.0 / (1.0 + jnp.exp(-c)))
    o_ref[...] = _dot(s.astype(BF16), w_ref[...].astype(BF16)) + b_ref[...]


def _ada(c_all, w_ada, b_ada):
    depth = w_ada.shape[0]
    rows = c_all.shape[0]
    blocks = (_nbytes((rows, D_MODEL), F32) * 2 + _nbytes((D_MODEL, D_MODEL), F32))
    return pl.pallas_call(
        _ada_kernel,
        grid=(depth, N_MOD),
        in_specs=[
            pl.BlockSpec((rows, D_MODEL), lambda l, j: (0, 0)),
            pl.BlockSpec((None, D_MODEL, D_MODEL), lambda l, j: (l, 0, j)),
            pl.BlockSpec((None, None, 1, D_MODEL), lambda l, j: (l, j, 0, 0)),
        ],
        out_specs=pl.BlockSpec((None, None, rows, D_MODEL), lambda l, j: (l, j, 0, 0)),
        out_shape=jax.ShapeDtypeStruct((depth, N_MOD, rows, D_MODEL), F32),
        compiler_params=pltpu.CompilerParams(
            dimension_semantics=("arbitrary", "arbitrary"),
            vmem_limit_bytes=_vmem_limit(blocks, _nbytes((D_MODEL, D_MODEL), BF16) * 2)),
        name="ada_mod",
    )(c_all, w_ada, b_ada.reshape(depth, N_MOD, 1, D_MODEL))


def _bias_kernel(tab_ref, bucket_ref, o_ref):
    bucket = bucket_ref[...]
    for h in range(N_HEADS):
        acc = jnp.zeros(bucket.shape, F32)
        for b in range(N_BUCKETS):
            acc = jnp.where(bucket == b, tab_ref[b * N_HEADS + h], acc)
        o_ref[h] = acc


def _bias_table(rel_table, bucket):
    return pl.pallas_call(
        _bias_kernel,
        in_specs=[
            pl.BlockSpec(memory_space=pltpu.SMEM),
            pl.BlockSpec(bucket.shape, lambda: (0, 0)),
        ],
        out_specs=pl.BlockSpec((N_HEADS,) + bucket.shape, lambda: (0, 0, 0)),
        out_shape=jax.ShapeDtypeStruct((N_HEADS,) + bucket.shape, F32),
        name="bias_table",
    )(rel_table.reshape(-1), bucket)


def _rel_bucket(dist):
    n = jnp.maximum(dist, 0)
    max_exact = N_BUCKETS // 2
    nf = jnp.maximum(n, 1).astype(F32)
    large = max_exact + (jnp.log(nf / max_exact) / math.log(MAX_DISTANCE / max_exact)
                         * (N_BUCKETS - max_exact)).astype(jnp.int32)
    large = jnp.minimum(large, N_BUCKETS - 1)
    return jnp.where(n < max_exact, n, large)


def _in_proj_kernel(*refs, sample):
    if sample:
        (x_ref, mod_ref, g_ref, w_ref, cw_ref, pre_ref,
         z_ref, q_ref, kv_ref, gate_ref, tail_ref, carry_ref) = refs
    else:
        (x_ref, mod_ref, g_ref, w_ref, cw_ref,
         z_ref, q_ref, kv_ref, gate_ref, tail_ref, carry_ref) = refs
    step = pl.program_id(0) if sample else pl.program_id(1)

    @pl.when(step == 0)
    def _():
        if sample:
            carry_ref[...] = pre_ref[...]
        else:
            carry_ref[...] = jnp.zeros(carry_ref.shape, F32)

    x = x_ref[...]
    h = (_rms(x) * g_ref[...]) * (1.0 + mod_ref[1]) + mod_ref[0]
    hb = _flat(h).astype(BF16)
    rows = hb.shape[0]

    def store(ref, lo, val):
        ref[..., lo:lo + val.shape[-1]] = val.reshape(ref.shape[:-1] + (val.shape[-1],))

    for lo in range(0, CONV_DIM, CHUNK):
        b_g = _dot(hb, w_ref[:, lo:lo + CHUNK])
        c_g = _dot(hb, w_ref[:, CONV_DIM + lo:CONV_DIM + lo + CHUNK])
        x_c = _dot(hb, w_ref[:, 2 * CONV_DIM + lo:2 * CONV_DIM + lo + CHUNK])
        u = c_g * x_c
        if sample:
            pos = x_ref.shape[0]
            u3 = u.reshape(pos, -1, CHUNK)
            ext = jnp.concatenate([carry_ref[:, :, lo:lo + CHUNK], u3], axis=0)
            u2 = _flat(ext[0:pos])
            u1 = _flat(ext[1:pos + 1])
            new_tail = u3[pos - (CONV_WIDTH - 1):]
            carry_ref[:, :, lo:lo + CHUNK] = new_tail
            tail_ref[:, :, lo:lo + CHUNK] = new_tail
        else:
            prev = carry_ref[:, lo:lo + CHUNK]
            p1 = prev[V7X_SUBLANES - 1:V7X_SUBLANES]
            p2 = prev[V7X_SUBLANES - 2:V7X_SUBLANES - 1]
            r = lax.broadcasted_iota(jnp.int32, u.shape, 0)
            u1 = jnp.where(r == 0, p1, pltpu.roll(u, 1, 0))
            u2 = jnp.where(r == 0, p2, jnp.where(r == 1, p1, pltpu.roll(u, 2, 0)))
            new_tail = u[rows - V7X_SUBLANES:]
            carry_ref[:, lo:lo + CHUNK] = new_tail
            tail_ref[:, lo:lo + CHUNK] = new_tail
        conv = (cw_ref[0:1, lo:lo + CHUNK] * u2 + cw_ref[1:2, lo:lo + CHUNK] * u1
                + cw_ref[2:3, lo:lo + CHUNK] * u)
        store(z_ref, lo, (b_g * conv).astype(BF16))

    for lo in range(0, ATTN_WIDTH, CHUNK):
        q = _dot(hb, w_ref[:, Q_OFF + lo:Q_OFF + lo + CHUNK])
        store(q_ref, lo, (q * (HEAD_DIM ** -0.5)).astype(BF16))

    store(kv_ref, 0, _dot(hb, w_ref[:, KV_OFF:KV_OFF + 2 * KV_WIDTH]))

    for lo in range(0, 2 * D_MODEL, CHUNK):
        g = _dot(hb, w_ref[:, GATE_OFF + lo:GATE_OFF + lo + CHUNK])
        store(gate_ref, lo, 1.0 / (1.0 + jnp.exp(-g)))


def _in_proj(x, mod, g_pre1, w_in, conv_w, pre=None):
    sample = pre is not None
    lead, mid = x.shape[0], x.shape[1]
    if sample:
        tile = (SAMPLE_POS_TILE, mid)
        grid = (lead // SAMPLE_POS_TILE,)
        rows = SAMPLE_POS_TILE * mid
        act = lambda w: pl.BlockSpec(tile + (w,), lambda i: (i, 0, 0))
        const2 = lambda shape: pl.BlockSpec(shape, lambda i: (0, 0))
        mod_spec = pl.BlockSpec((N_MOD, mid, D_MODEL), lambda i: (0, 0, 0))
        tail_shape = (CONV_WIDTH - 1, mid, CONV_DIM)
        tail_spec = pl.BlockSpec(tail_shape, lambda i: (0, 0, 0))
        extra_in = [pre]
        extra_specs = [pl.BlockSpec(tail_shape, lambda i: (0, 0, 0))]
        carry = pltpu.VMEM(tail_shape, F32)
        sem = ("arbitrary",)
    else:
        tile = (None, PROMPT_TILE)
        grid = (lead, mid // PROMPT_TILE)
        rows = PROMPT_TILE
        act = lambda w: pl.BlockSpec(tile + (w,), lambda b, t: (b, t, 0))
        const2 = lambda shape: pl.BlockSpec(shape, lambda b, t: (0, 0))
        mod_spec = pl.BlockSpec((N_MOD, None, 1, D_MODEL), lambda b, t: (0, b, 0, 0))
        tail_shape = (lead, V7X_SUBLANES, CONV_DIM)
        tail_spec = pl.BlockSpec((None, V7X_SUBLANES, CONV_DIM), lambda b, t: (b, 0, 0))
        extra_in, extra_specs = [], []
        carry = pltpu.VMEM((V7X_SUBLANES, CONV_DIM), F32)
        sem = ("arbitrary", "arbitrary")

    widths = (CONV_DIM, ATTN_WIDTH, 2 * KV_WIDTH, 2 * D_MODEL)
    dtypes = (BF16, BF16, F32, F32)
    out_shape = [jax.ShapeDtypeStruct(x.shape[:-1] + (w,), dt) for w, dt in zip(widths, dtypes)]
    out_shape.append(jax.ShapeDtypeStruct(tail_shape, F32))
    blocks = (_nbytes((rows, D_MODEL), F32) + _nbytes(w_in.shape, BF16)
              + sum(_nbytes((rows, w), dt) for w, dt in zip(widths, dtypes)))
    temps = _nbytes((rows, D_MODEL), F32) * 2 + _nbytes((rows, CHUNK), F32) * 10
    return pl.pallas_call(
        functools.partial(_in_proj_kernel, sample=sample),
        grid=grid,
        in_specs=[act(D_MODEL), mod_spec, const2((1, D_MODEL)), const2(w_in.shape),
                  const2(conv_w.shape)] + extra_specs,
        out_specs=[act(w) for w in widths] + [tail_spec],
        out_shape=out_shape,
        scratch_shapes=[carry],
        compiler_params=pltpu.CompilerParams(
            dimension_semantics=sem, vmem_limit_bytes=_vmem_limit(blocks, temps)),
        name="in_proj_sample" if sample else "in_proj_prompt",
    )(x, mod, g_pre1.reshape(1, D_MODEL), w_in, conv_w, *extra_in)


def _sink_softmax(s, sink):
    m = jnp.maximum(jnp.max(s, axis=-1, keepdims=True), sink)
    e = jnp.exp(s - m)
    den = jnp.sum(e, axis=-1, keepdims=True) + jnp.exp(sink - m)
    return e * (1.0 / den)


def _prompt_attn_kernel(sink_ref, q_ref, kvp_ref, kvc_ref, bias_ref, o_ref):
    blk = pl.program_id(1)
    kv = jnp.concatenate([kvp_ref[...], kvc_ref[...]], axis=0).astype(BF16)
    qi = lax.broadcasted_iota(jnp.int32, (BLOCK, 2 * BLOCK), 0)
    kj = lax.broadcasted_iota(jnp.int32, (BLOCK, 2 * BLOCK), 1)
    dist = qi + BLOCK - kj
    mask = (dist >= 0) & (dist < WINDOW) & ((kj >= BLOCK) | (blk > 0))
    outs = []
    for h in range(N_HEADS):
        g = h // GROUP
        k_g = kv[:, g * HEAD_DIM:(g + 1) * HEAD_DIM]
        v_g = kv[:, KV_WIDTH + g * HEAD_DIM:KV_WIDTH + (g + 1) * HEAD_DIM]
        q_h = q_ref[:, h * HEAD_DIM:(h + 1) * HEAD_DIM]
        s = lax.dot_general(q_h, k_g, (((1,), (1,)), ((), ())), preferred_element_type=F32)
        s = jnp.where(mask, s + bias_ref[h], NEG_INF)
        p = _sink_softmax(s, sink_ref[h])
        outs.append(_dot(p.astype(BF16), v_g))
    o_ref[...] = jnp.concatenate(outs, axis=-1).astype(BF16)


def _prompt_attn(q, kv, bias, sinks):
    batch, seq, _ = q.shape
    blocks = (_nbytes((BLOCK, ATTN_WIDTH), BF16) * 2 + _nbytes((BLOCK, 2 * KV_WIDTH), F32) * 2
              + _nbytes(bias.shape, F32))
    temps = _nbytes((BLOCK, 2 * BLOCK), F32) * 16 + _nbytes((BLOCK, ATTN_WIDTH), F32) * 2
    return pl.pallas_call(
        _prompt_attn_kernel,
        grid=(batch, seq // BLOCK),
        in_specs=[
            pl.BlockSpec(memory_space=pltpu.SMEM),
            pl.BlockSpec((None, BLOCK, ATTN_WIDTH), lambda b, i: (b, i, 0)),
            pl.BlockSpec((None, BLOCK, 2 * KV_WIDTH), lambda b, i: (b, jnp.maximum(i - 1, 0), 0)),
            pl.BlockSpec((None, BLOCK, 2 * KV_WIDTH), lambda b, i: (b, i, 0)),
            pl.BlockSpec(bias.shape, lambda b, i: (0, 0, 0)),
        ],
        out_specs=pl.BlockSpec((None, BLOCK, ATTN_WIDTH), lambda b, i: (b, i, 0)),
        out_shape=jax.ShapeDtypeStruct(q.shape, BF16),
        compiler_params=pltpu.CompilerParams(
            dimension_semantics=("arbitrary", "arbitrary"),
            vmem_limit_bytes=_vmem_limit(blocks, temps)),
        name="attn_prompt",
    )(sinks, q, kv, kv, bias)


def _sample_attn_kernel(q_ref, kc_ref, vc_ref, kn_ref, vn_ref, bias_ref, sink_ref,
                        o_ref, ko_ref, vo_ref):
    buf = kc_ref.shape[1]
    new = kn_ref.shape[1]
    kk = jnp.concatenate([kc_ref[...], kn_ref[...]], axis=1)
    vv = jnp.concatenate([vc_ref[...], vn_ref[...]], axis=1)
    ko_ref[...] = kk[:, new:]
    vo_ref[...] = vv[:, new:]
    rows = new * GROUP
    pos = lax.broadcasted_iota(jnp.int32, (rows, buf + new), 0) // GROUP
    kj = lax.broadcasted_iota(jnp.int32, (rows, buf + new), 1)
    dist = buf + pos - kj
    mask = ((dist >= 0) & (dist < WINDOW))[None]
    kb = kk.astype(BF16)
    vb = vv.astype(BF16)
    for g in range(N_KV_HEADS):
        k_g = kb[:, :, g * HEAD_DIM:(g + 1) * HEAD_DIM]
        v_g = vb[:, :, g * HEAD_DIM:(g + 1) * HEAD_DIM]
        s = jnp.einsum('nqd,nkd->nqk', q_ref[:, g], k_g, preferred_element_type=F32)
        s = jnp.where(mask, s + bias_ref[g][None], NEG_INF)
        p = _sink_softmax(s, sink_ref[g][None])
        o = jnp.einsum('nqk,nkd->nqd', p.astype(BF16), v_g, preferred_element_type=F32)
        o_ref[:, g] = o.astype(BF16)


def _sample_attn(q, k_cache, v_cache, k_new, v_new, bias, sink_rows):
    n, buf, _ = k_cache.shape
    new = k_new.shape[1]
    rows = new * GROUP
    nt = SAMPLE_SEQ_TILE
    seq3 = lambda a, b: pl.BlockSpec((nt, a, b), lambda i: (i, 0, 0))
    q_spec = pl.BlockSpec((nt, N_KV_HEADS, rows, HEAD_DIM), lambda i: (i, 0, 0, 0))
    blocks = (_nbytes((nt, buf, KV_WIDTH), F32) * 4 + _nbytes((nt, new, KV_WIDTH), F32) * 2
              + _nbytes((nt, N_KV_HEADS, rows, HEAD_DIM), BF16) * 2 + _nbytes(bias.shape, F32))
    temps = (_nbytes((nt, buf + new, KV_WIDTH), F32) * 4
             + _nbytes((nt, rows, 2 * BLOCK), F32) * 6)
    return pl.pallas_call(
        _sample_attn_kernel,
        grid=(n // nt,),
        in_specs=[q_spec, seq3(buf, KV_WIDTH), seq3(buf, KV_WIDTH), seq3(new, KV_WIDTH),
                  seq3(new, KV_WIDTH),
                  pl.BlockSpec(bias.shape, lambda i: (0, 0, 0)),
                  pl.BlockSpec(sink_rows.shape, lambda i: (0, 0, 0))],
        out_specs=[q_spec, seq3(buf, KV_WIDTH), seq3(buf, KV_WIDTH)],
        out_shape=[jax.ShapeDtypeStruct(q.shape, BF16),
                   jax.ShapeDtypeStruct(k_cache.shape, F32),
                   jax.ShapeDtypeStruct(v_cache.shape, F32)],
        compiler_params=pltpu.CompilerParams(
            dimension_semantics=("arbitrary",), vmem_limit_bytes=_vmem_limit(blocks, temps)),
        name="attn_sample",
    )(q, k_cache, v_cache, k_new, v_new, bias, sink_rows)


def _mix_kernel(x_ref, z_ref, a_ref, gate_ref, mod_ref, g_ref, wbc_ref, wba_ref, wo_ref, o_ref):
    y_conv = _dot(_flat(z_ref[...]), wbc_ref[...])
    y_attn = _dot(_flat(a_ref[...]), wba_ref[...])
    gates = _flat(gate_ref[...])
    merged = gates[:, :D_MODEL] * y_conv + gates[:, D_MODEL:] * y_attn
    mixed = _dot(merged.astype(BF16), wo_ref[...])
    r = (_rms(mixed) * g_ref[...]).reshape(x_ref.shape)
    o_ref[...] = x_ref[...] + mod_ref[2] * r


def _mlp_kernel(x_ref, mod_ref, gpre_ref, gpost_ref, w1_ref, w2_ref, o_ref, hid_ref):
    x = x_ref[...]
    h = (_rms(x) * gpre_ref[...]) * (1.0 + mod_ref[4]) + mod_ref[3]
    hb = _flat(h).astype(BF16)
    for lo in range(0, D_FF, 2 * CHUNK):
        a = jnp.maximum(_dot(hb, w1_ref[:, lo:lo + 2 * CHUNK]), 0.0)
        hid_ref[:, lo:lo + 2 * CHUNK] = (a * a).astype(BF16)
    ff = _dot(hid_ref[...], w2_ref[...])
    r = (_rms(ff) * gpost_ref[...]).reshape(x.shape)
    o_ref[...] = x + mod_ref[5] * r


def _dense_specs(x, sample):
    lead, mid = x.shape[0], x.shape[1]
    if sample:
        grid = (lead // SAMPLE_POS_TILE,)
        rows = SAMPLE_POS_TILE * mid
        act = lambda w: pl.BlockSpec((SAMPLE_POS_TILE, mid, w), lambda i: (i, 0, 0))
        const2 = lambda shape: pl.BlockSpec(shape, lambda i: (0, 0))
        mod_spec = pl.BlockSpec((N_MOD, mid, D_MODEL), lambda i: (0, 0, 0))
        sem = ("arbitrary",)
    else:
        grid = (lead, mid // PROMPT_TILE)
        rows = PROMPT_TILE
        act = lambda w: pl.BlockSpec((None, PROMPT_TILE, w), lambda b, t: (b, t, 0))
        const2 = lambda shape: pl.BlockSpec(shape, lambda b, t: (0, 0))
        mod_spec = pl.BlockSpec((N_MOD, None, 1, D_MODEL), lambda b, t: (0, b, 0, 0))
        sem = ("arbitrary", "arbitrary")
    return grid, rows, act, const2, mod_spec, sem


def _mix(x, z, a, gates, mod, g_post1, w_bc, w_ba, w_o, sample):
    grid, rows, act, const2, mod_spec, sem = _dense_specs(x, sample)
    sq = (D_MODEL, D_MODEL)
    blocks = (_nbytes((rows, D_MODEL), F32) * 2 + _nbytes((rows, D_MODEL), BF16) * 2
              + _nbytes((rows, 2 * D_MODEL), F32) + _nbytes(sq, BF16) * 3)
    temps = _nbytes((rows, D_MODEL), F32) * 8
    return pl.pallas_call(
        _mix_kernel,
        grid=grid,
        in_specs=[act(D_MODEL), act(CONV_DIM), act(ATTN_WIDTH), act(2 * D_MODEL), mod_spec,
                  const2((1, D_MODEL)), const2(sq), const2(sq), const2(sq)],
        out_specs=act(D_MODEL),
        out_shape=jax.ShapeDtypeStruct(x.shape, F32),
        compiler_params=pltpu.CompilerParams(
            dimension_semantics=sem, vmem_limit_bytes=_vmem_limit(blocks, temps)),
        name="mix_sample" if sample else "mix_prompt",
    )(x, z, a, gates, mod, g_post1.reshape(1, D_MODEL), w_bc, w_ba, w_o)


def _mlp(x, mod, g_pre2, g_post2, w1, w2, sample):
    grid, rows, act, const2, mod_spec, sem = _dense_specs(x, sample)
    blocks = (_nbytes((rows, D_MODEL), F32) * 2 + _nbytes(w1.shape, BF16) * 2)
    temps = (_nbytes((rows, D_FF), BF16) + _nbytes((rows, 2 * CHUNK), F32) * 3
             + _nbytes((rows, D_MODEL), F32) * 6)
    return pl.pallas_call(
        _mlp_kernel,
        grid=grid,
        in_specs=[act(D_MODEL), mod_spec, const2((1, D_MODEL)), const2((1, D_MODEL)),
                  const2(w1.shape), const2(w2.shape)],
        out_specs=act(D_MODEL),
        out_shape=jax.ShapeDtypeStruct(x.shape, F32),
        scratch_shapes=[pltpu.VMEM((rows, D_FF), BF16)],
        compiler_params=pltpu.CompilerParams(
            dimension_semantics=sem, vmem_limit_bytes=_vmem_limit(blocks, temps)),
        name="mlp_sample" if sample else "mlp_prompt",
    )(x, mod, g_pre2.reshape(1, D_MODEL), g_post2.reshape(1, D_MODEL), w1, w2)


def kernel(x_prompt, x_sample, c_prompt, c_sample, state_conv, cache_k, cache_v, w_ada, b_ada,
           g_pre1, w_in, conv_w, w_br_conv, w_br_attn, w_o, sinks, g_post1, g_pre2, w_ff1, w_ff2,
           g_post2, rel_table):
    depth = w_ada.shape[0]
    batch, seq, _ = x_prompt.shape
    n_seq, n_new, _ = x_sample.shape
    buf = cache_k.shape[2]

    pad = (-(n_seq + batch)) % V7X_SUBLANES
    c_all = jnp.concatenate([c_sample, c_prompt, jnp.zeros((pad, D_MODEL), F32)], axis=0)
    mod_all = _ada(c_all, w_ada, b_ada)

    dist_p = (jnp.arange(BLOCK)[:, None] + BLOCK) - jnp.arange(2 * BLOCK)[None, :]
    bias_p = _bias_table(rel_table, _rel_bucket(dist_p))
    dist_s = (buf + jnp.arange(n_new))[:, None] - jnp.arange(buf + n_new)[None, :]
    bias_s = _bias_table(rel_table, _rel_bucket(dist_s))
    bias_s = (bias_s.reshape(N_KV_HEADS, GROUP, n_new, buf + n_new)
              .transpose(0, 2, 1, 3).reshape(N_KV_HEADS, n_new * GROUP, buf + n_new))

    to_bf16 = lambda w: w.astype(BF16)
    w_in_b, w_bc_b, w_ba_b, w_o_b, w1_b, w2_b = map(
        to_bf16, (w_in, w_br_conv, w_br_attn, w_o, w_ff1, w_ff2))

    xp = x_prompt
    xs = x_sample.transpose(1, 0, 2)
    conv_p, k_p, v_p, conv_s, k_s, v_s = [], [], [], [], [], []
    win = min(WINDOW, seq)
    for l in range(depth):
        mod_s = mod_all[l]
        mod_p = mod_all[l][:, n_seq:n_seq + batch].reshape(N_MOD, batch, 1, D_MODEL)

        z, q, kv, gates, tail = _in_proj(xp, mod_p, g_pre1[l], w_in_b[l], conv_w[l])
        attn = _prompt_attn(q, kv, bias_p, sinks[l])
        x1 = _mix(xp, z, attn, gates, mod_p, g_post1[l], w_bc_b[l], w_ba_b[l], w_o_b[l], False)
        xp = _mlp(x1, mod_p, g_pre2[l], g_post2[l], w1_b[l], w2_b[l], False)
        conv_p.append(tail[:, V7X_SUBLANES - (CONV_WIDTH - 1):])
        k_p.append(kv[:, seq - win:, :KV_WIDTH].reshape(batch, win, N_KV_HEADS, HEAD_DIM))
        v_p.append(kv[:, seq - win:, KV_WIDTH:].reshape(batch, win, N_KV_HEADS, HEAD_DIM))

        pre = state_conv[l].transpose(1, 0, 2)
        z, q, kv, gates, tail = _in_proj(xs, mod_s, g_pre1[l], w_in_b[l], conv_w[l], pre)
        q_s = (q.reshape(n_new, n_seq, N_KV_HEADS, GROUP, HEAD_DIM)
               .transpose(1, 2, 0, 3, 4).reshape(n_seq, N_KV_HEADS, n_new * GROUP, HEAD_DIM))
        kv_s = kv.transpose(1, 0, 2)
        sink_rows = jnp.tile(sinks[l].reshape(N_KV_HEADS, 1, GROUP),
                             (1, n_new, 1)).reshape(N_KV_HEADS, n_new * GROUP, 1)
        o, k_out, v_out = _sample_attn(
            q_s, cache_k[l].reshape(n_seq, buf, KV_WIDTH), cache_v[l].reshape(n_seq, buf, KV_WIDTH),
            kv_s[:, :, :KV_WIDTH], kv_s[:, :, KV_WIDTH:], bias_s, sink_rows)
        attn = (o.reshape(n_seq, N_KV_HEADS, n_new, GROUP, HEAD_DIM)
                .transpose(2, 0, 1, 3, 4).reshape(n_new, n_seq, ATTN_WIDTH))
        x1 = _mix(xs, z, attn, gates, mod_s, g_post1[l], w_bc_b[l], w_ba_b[l], w_o_b[l], True)
        xs = _mlp(x1, mod_s, g_pre2[l], g_post2[l], w1_b[l], w2_b[l], True)
        conv_s.append(tail.transpose(1, 0, 2))
        k_s.append(k_out.reshape(n_seq, buf, N_KV_HEADS, HEAD_DIM))
        v_s.append(v_out.reshape(n_seq, buf, N_KV_HEADS, HEAD_DIM))

    return (xp, xs.transpose(1, 0, 2), jnp.stack(conv_p), jnp.stack(k_p), jnp.stack(v_p),
            jnp.stack(conv_s), jnp.stack(k_s), jnp.stack(v_s))
```

```python
import functools
import math
from typing import Any, Callable, NamedTuple

import jax
import jax.numpy as jnp
from jax import lax
from jax.experimental import pallas as pl
from jax.experimental.pallas import tpu as pltpu

D_MODEL = 1024
N_HEADS = 16
N_KV_HEADS = 2
HEAD_DIM = 64
GROUP = N_HEADS // N_KV_HEADS
ATTN_WIDTH = N_HEADS * HEAD_DIM
KV_WIDTH = N_KV_HEADS * HEAD_DIM
CONV_DIM = D_MODEL
CONV_WIDTH = 3
WINDOW = 128
BLOCK = 128
N_BUCKETS = 32
MAX_DISTANCE = 128
D_FF = 4 * D_MODEL
N_MOD = 6
RMS_EPS = 1e-6
NEG_INF = -1e30
PROJ_COLS = 3 * CONV_DIM + ATTN_WIDTH + 2 * KV_WIDTH + 2 * D_MODEL
Q_OFF = 3 * CONV_DIM
KV_OFF = Q_OFF + ATTN_WIDTH
GATE_OFF = KV_OFF + 2 * KV_WIDTH

V7X_SUBLANES = 8
V7X_VMEM_BYTES = 64 * 1024 * 1024

PROMPT_TILE = 512
ATTN_TILE = 512
SAMPLE_POS_TILE = 4
SAMPLE_SEQ_TILE = 16
CHUNK = 512

F32 = jnp.float32
BF16 = jnp.bfloat16


def _vmem_limit(block_bytes, temp_bytes):
    return int(min(2 * block_bytes + temp_bytes, V7X_VMEM_BYTES - 4 * 1024 * 1024))


def _nbytes(shape, dtype):
    return math.prod(shape) * jnp.dtype(dtype).itemsize


def _rms(x):
    return x * lax.rsqrt(jnp.mean(x * x, axis=-1, keepdims=True) + RMS_EPS)


def _dot(a, b):
    return jnp.dot(a, b, preferred_element_type=F32)


def _flat(a):
    return a.reshape(-1, a.shape[-1])


def _ada_kernel(c_ref, w_ref, b_ref, o_ref):
    c = c_ref[...]
    s = c * (1.0 / (1.0 + jnp.exp(-c)))
    o_ref[...] = _dot(s.astype(BF16), w_ref[...].astype(BF16)) + b_ref[...]


def _ada(c_all, w_ada, b_ada):
    depth = w_ada.shape[0]
    rows = c_all.shape[0]
    blocks = (_nbytes((rows, D_MODEL), F32) * 2 + _nbytes((D_MODEL, D_MODEL), F32))
    return pl.pallas_call(
        _ada_kernel,
        grid=(depth, N_MOD),
        in_specs=[
            pl.BlockSpec((rows, D_MODEL), lambda l, j: (0, 0)),
            pl.BlockSpec((None, D_MODEL, D_MODEL), lambda l, j: (l, 0, j)),
            pl.BlockSpec((None, None, 1, D_MODEL), lambda l, j: (l, j, 0, 0)),
        ],
        out_specs=pl.BlockSpec((None, None, rows, D_MODEL), lambda l, j: (l, j, 0, 0)),
        out_shape=jax.ShapeDtypeStruct((depth, N_MOD, rows, D_MODEL), F32),
        compiler_params=pltpu.CompilerParams(
            dimension_semantics=("arbitrary", "arbitrary"),
            vmem_limit_bytes=_vmem_limit(blocks, _nbytes((D_MODEL, D_MODEL), BF16) * 2)),
        name="ada_mod",
    )(c_all, w_ada, b_ada.reshape(depth, N_MOD, 1, D_MODEL))


def _bias_kernel(tab_ref, bucket_ref, o_ref):
    bucket = bucket_ref[...]
    for h in range(N_HEADS):
        acc = jnp.zeros(bucket.shape, F32)
        for b in range(N_BUCKETS):
            acc = jnp.where(bucket == b, tab_ref[b * N_HEADS + h], acc)
        o_ref[h] = acc


def _bias_table(rel_table, bucket):
    return pl.pallas_call(
        _bias_kernel,
        in_specs=[
            pl.BlockSpec(memory_space=pltpu.SMEM),
            pl.BlockSpec(bucket.shape, lambda: (0, 0)),
        ],
        out_specs=pl.BlockSpec((N_HEADS,) + bucket.shape, lambda: (0, 0, 0)),
        out_shape=jax.ShapeDtypeStruct((N_HEADS,) + bucket.shape, F32),
        name="bias_table",
    )(rel_table.reshape(-1), bucket)


def _rel_bucket(dist):
    n = jnp.maximum(dist, 0)
    max_exact = N_BUCKETS // 2
    nf = jnp.maximum(n, 1).astype(F32)
    large = max_exact + (jnp.log(nf / max_exact) / math.log(MAX_DISTANCE / max_exact)
                         * (N_BUCKETS - max_exact)).astype(jnp.int32)
    large = jnp.minimum(large, N_BUCKETS - 1)
    return jnp.where(n < max_exact, n, large)


class _Dense(NamedTuple):
    sample: bool
    grid: tuple
    rows: int
    act: Callable[[int], Any]
    layer: Callable[..., Any]
    mod_spec: Any
    sem: tuple


def _dense(x, layer_idx, n_seq, sample):
    lead, mid = x.shape[0], x.shape[1]
    if sample:
        grid = (lead // SAMPLE_POS_TILE,)
        act = lambda w: pl.BlockSpec((SAMPLE_POS_TILE, mid, w), lambda i: (i, 0, 0))
        layer = lambda *shape: pl.BlockSpec((None,) + shape,
                                            lambda i: (layer_idx,) + (0,) * len(shape))
        mod_spec = pl.BlockSpec((None, N_MOD, n_seq, D_MODEL), lambda i: (layer_idx, 0, 0, 0))
        return _Dense(True, grid, SAMPLE_POS_TILE * mid, act, layer, mod_spec, ("arbitrary",))
    assert n_seq % V7X_SUBLANES == 0 and lead <= V7X_SUBLANES
    grid = (lead, mid // PROMPT_TILE)
    act = lambda w: pl.BlockSpec((None, PROMPT_TILE, w), lambda b, t: (b, t, 0))
    layer = lambda *shape: pl.BlockSpec((None,) + shape,
                                        lambda b, t: (layer_idx,) + (0,) * len(shape))
    mod_spec = pl.BlockSpec((None, N_MOD, V7X_SUBLANES, D_MODEL),
                            lambda b, t: (layer_idx, 0, n_seq // V7X_SUBLANES, 0))
    return _Dense(False, grid, PROMPT_TILE, act, layer, mod_spec, ("arbitrary", "arbitrary"))


def _mod(mod_ref, j, sample):
    if sample:
        return mod_ref[j]
    return mod_ref[j, pl.ds(pl.program_id(0), 1), :]


def _in_proj_kernel(*refs, sample):
    if sample:
        (x_ref, mod_ref, g_ref, w_ref, cw_ref, pre_ref,
         z_ref, q_ref, kv_ref, gate_ref, tail_ref, carry_ref) = refs
    else:
        (x_ref, mod_ref, g_ref, w_ref, cw_ref,
         z_ref, q_ref, kv_ref, gate_ref, tail_ref, carry_ref) = refs
    step = pl.program_id(0) if sample else pl.program_id(1)

    @pl.when(step == 0)
    def _():
        if sample:
            carry_ref[...] = pre_ref[...]
        else:
            carry_ref[...] = jnp.zeros(carry_ref.shape, F32)

    x = x_ref[...]
    h = (_rms(x) * g_ref[...]) * (1.0 + _mod(mod_ref, 1, sample)) + _mod(mod_ref, 0, sample)
    hb = _flat(h).astype(BF16)
    rows = hb.shape[0]

    def store(ref, lo, val):
        ref[..., lo:lo + val.shape[-1]] = val.reshape(ref.shape[:-1] + (val.shape[-1],))

    for lo in range(0, CONV_DIM, CHUNK):
        b_g = _dot(hb, w_ref[:, lo:lo + CHUNK])
        c_g = _dot(hb, w_ref[:, CONV_DIM + lo:CONV_DIM + lo + CHUNK])
        x_c = _dot(hb, w_ref[:, 2 * CONV_DIM + lo:2 * CONV_DIM + lo + CHUNK])
        u = c_g * x_c
        if sample:
            pos = x_ref.shape[0]
            u3 = u.reshape(pos, -1, CHUNK)
            ext = jnp.concatenate([carry_ref[:, :, lo:lo + CHUNK], u3], axis=0)
            u2 = _flat(ext[0:pos])
            u1 = _flat(ext[1:pos + 1])
            new_tail = u3[pos - (CONV_WIDTH - 1):]
            carry_ref[:, :, lo:lo + CHUNK] = new_tail
            tail_ref[:, :, lo:lo + CHUNK] = new_tail
        else:
            prev = carry_ref[:, lo:lo + CHUNK]
            p1 = prev[V7X_SUBLANES - 1:V7X_SUBLANES]
            p2 = prev[V7X_SUBLANES - 2:V7X_SUBLANES - 1]
            r = lax.broadcasted_iota(jnp.int32, u.shape, 0)
            u1 = jnp.where(r == 0, p1, pltpu.roll(u, 1, 0))
            u2 = jnp.where(r == 0, p2, jnp.where(r == 1, p1, pltpu.roll(u, 2, 0)))
            new_tail = u[rows - V7X_SUBLANES:]
            carry_ref[:, lo:lo + CHUNK] = new_tail
            tail_ref[:, lo:lo + CHUNK] = new_tail
        conv = (cw_ref[0:1, lo:lo + CHUNK] * u2 + cw_ref[1:2, lo:lo + CHUNK] * u1
                + cw_ref[2:3, lo:lo + CHUNK] * u)
        store(z_ref, lo, (b_g * conv).astype(BF16))

    for lo in range(0, ATTN_WIDTH, CHUNK):
        q = _dot(hb, w_ref[:, Q_OFF + lo:Q_OFF + lo + CHUNK])
        store(q_ref, lo, (q * (HEAD_DIM ** -0.5)).astype(BF16))

    store(kv_ref, 0, _dot(hb, w_ref[:, KV_OFF:KV_OFF + 2 * KV_WIDTH]))

    for lo in range(0, 2 * D_MODEL, CHUNK):
        g = _dot(hb, w_ref[:, GATE_OFF + lo:GATE_OFF + lo + CHUNK])
        store(gate_ref, lo, 1.0 / (1.0 + jnp.exp(-g)))


def _in_proj(x, mod_all, g_pre1, w_in, conv_w, layer_idx, n_seq, pre=None):
    cfg = _dense(x, layer_idx, n_seq, pre is not None)
    lead, mid = x.shape[0], x.shape[1]
    if cfg.sample:
        tail_shape = (CONV_WIDTH - 1, mid, CONV_DIM)
        tail_spec = pl.BlockSpec(tail_shape, lambda i: (0, 0, 0))
        extra_in = [pre]
        extra_specs = [cfg.layer(*tail_shape)]
        carry = pltpu.VMEM(tail_shape, F32)
    else:
        tail_shape = (lead, V7X_SUBLANES, CONV_DIM)
        tail_spec = pl.BlockSpec((None, V7X_SUBLANES, CONV_DIM), lambda b, t: (b, 0, 0))
        extra_in, extra_specs = [], []
        carry = pltpu.VMEM((V7X_SUBLANES, CONV_DIM), F32)

    widths = (CONV_DIM, ATTN_WIDTH, 2 * KV_WIDTH, 2 * D_MODEL)
    dtypes = (BF16, BF16, F32, F32)
    out_shape = [jax.ShapeDtypeStruct(x.shape[:-1] + (w,), dt) for w, dt in zip(widths, dtypes)]
    out_shape.append(jax.ShapeDtypeStruct(tail_shape, F32))
    blocks = (_nbytes((cfg.rows, D_MODEL), F32) + _nbytes(w_in.shape[1:], BF16)
              + sum(_nbytes((cfg.rows, w), dt) for w, dt in zip(widths, dtypes)))
    temps = _nbytes((cfg.rows, D_MODEL), F32) * 2 + _nbytes((cfg.rows, CHUNK), F32) * 10
    return pl.pallas_call(
        functools.partial(_in_proj_kernel, sample=cfg.sample),
        grid=cfg.grid,
        in_specs=[cfg.act(D_MODEL), cfg.mod_spec, cfg.layer(1, D_MODEL),
                  cfg.layer(*w_in.shape[1:]), cfg.layer(*conv_w.shape[1:])] + extra_specs,
        out_specs=[cfg.act(w) for w in widths] + [tail_spec],
        out_shape=out_shape,
        scratch_shapes=[carry],
        compiler_params=pltpu.CompilerParams(
            dimension_semantics=cfg.sem, vmem_limit_bytes=_vmem_limit(blocks, temps)),
        name="in_proj_sample" if cfg.sample else "in_proj_prompt",
    )(x, mod_all, g_pre1, w_in, conv_w, *extra_in)


def _sink_softmax(s, sink):
    m = jnp.maximum(jnp.max(s, axis=-1, keepdims=True), sink)
    e = jnp.exp(s - m)
    den = jnp.sum(e, axis=-1, keepdims=True) + jnp.exp(sink - m)
    return e * (1.0 / den)


def _prompt_attn_kernel(sink_ref, q_ref, kvp_ref, kvc_ref, bias_ref, o_ref, kv_ref):
    first_tile = pl.program_id(1) == 0
    n_blk = q_ref.shape[0] // BLOCK
    kv_ref[0:BLOCK] = kvp_ref[...].astype(BF16)
    kv_ref[BLOCK:] = kvc_ref[...].astype(BF16)
    qi = lax.broadcasted_iota(jnp.int32, (BLOCK, BLOCK), 0)
    kj = lax.broadcasted_iota(jnp.int32, (BLOCK, BLOCK), 1)
    upper = kj > qi

    def block(c, carry):
        r0 = pl.multiple_of(c * BLOCK, BLOCK)
        no_prev = upper & (first_tile & (c == 0))
        q = q_ref[pl.ds(r0, BLOCK), :]
        kv = kv_ref[pl.ds(r0, 2 * BLOCK), :]
        outs = []
        for h in range(N_HEADS):
            g = h // GROUP
            k_g = kv[:, g * HEAD_DIM:(g + 1) * HEAD_DIM]
            v_g = kv[:, KV_WIDTH + g * HEAD_DIM:KV_WIDTH + (g + 1) * HEAD_DIM]
            q_h = q[:, h * HEAD_DIM:(h + 1) * HEAD_DIM]
            s = lax.dot_general(q_h, k_g, (((1,), (1,)), ((), ())), preferred_element_type=F32)
            s = jnp.where(upper, s[:, :BLOCK], s[:, BLOCK:]) + bias_ref[h]
            s = jnp.where(no_prev, NEG_INF, s)
            sink = sink_ref[h]
            m = jnp.maximum(jnp.max(s, axis=-1, keepdims=True), sink)
            e = jnp.exp(s - m)
            den = jnp.sum(e, axis=-1, keepdims=True) + jnp.exp(sink - m)
            p = jnp.concatenate([jnp.where(upper, e, 0.0), jnp.where(upper, 0.0, e)], axis=-1)
            outs.append(_dot(p.astype(BF16), v_g) * (1.0 / den))
        o_ref[pl.ds(r0, BLOCK), :] = jnp.concatenate(outs, axis=-1).astype(BF16)
        return carry

    lax.fori_loop(0, n_blk, block, 0)


def _prompt_attn(q, kv, bias, sinks, layer_idx):
    batch, seq, _ = q.shape
    tq = ATTN_TILE
    per_tile = tq // BLOCK
    blocks = (_nbytes((tq, ATTN_WIDTH), BF16) * 2 + _nbytes((tq + BLOCK, 2 * KV_WIDTH), F32)
              + _nbytes(bias.shape, F32))
    temps = (_nbytes((tq + BLOCK, 2 * KV_WIDTH), BF16) + _nbytes((BLOCK, 2 * BLOCK), F32) * 16
             + _nbytes((BLOCK, ATTN_WIDTH), F32) * 2)
    return pl.pallas_call(
        _prompt_attn_kernel,
        grid=(batch, seq // tq),
        in_specs=[
            pl.BlockSpec(memory_space=pltpu.SMEM),
            pl.BlockSpec((None, tq, ATTN_WIDTH), lambda b, t: (b, t, 0)),
            pl.BlockSpec((None, BLOCK, 2 * KV_WIDTH),
                         lambda b, t: (b, jnp.maximum(t * per_tile - 1, 0), 0)),
            pl.BlockSpec((None, tq, 2 * KV_WIDTH), lambda b, t: (b, t, 0)),
            pl.BlockSpec(bias.shape, lambda b, t: (0, 0, 0)),
        ],
        out_specs=pl.BlockSpec((None, tq, ATTN_WIDTH), lambda b, t: (b, t, 0)),
        out_shape=jax.ShapeDtypeStruct(q.shape, BF16),
        scratch_shapes=[pltpu.VMEM((tq + BLOCK, 2 * KV_WIDTH), BF16)],
        compiler_params=pltpu.CompilerParams(
            dimension_semantics=("arbitrary", "arbitrary"),
            vmem_limit_bytes=_vmem_limit(blocks, temps)),
        name="attn_prompt",
    )(sinks[layer_idx], q, kv, kv, bias)


def _sample_attn_kernel(q_ref, kc_ref, vc_ref, kn_ref, vn_ref, bias_ref, sink_ref,
                        o_ref, ko_ref, vo_ref):
    buf = kc_ref.shape[1]
    new = kn_ref.shape[1]
    kk = jnp.concatenate([kc_ref[...], kn_ref[...]], axis=1)
    vv = jnp.concatenate([vc_ref[...], vn_ref[...]], axis=1)
    ko_ref[...] = kk[:, new:]
    vo_ref[...] = vv[:, new:]
    rows = new * GROUP
    pos = lax.broadcasted_iota(jnp.int32, (rows, buf + new), 0) // GROUP
    kj = lax.broadcasted_iota(jnp.int32, (rows, buf + new), 1)
    dist = buf + pos - kj
    mask = ((dist >= 0) & (dist < WINDOW))[None]
    kb = kk.astype(BF16)
    vb = vv.astype(BF16)
    for g in range(N_KV_HEADS):
        k_g = kb[:, :, g * HEAD_DIM:(g + 1) * HEAD_DIM]
        v_g = vb[:, :, g * HEAD_DIM:(g + 1) * HEAD_DIM]
        s = jnp.einsum('nqd,nkd->nqk', q_ref[:, g], k_g, preferred_element_type=F32)
        s = jnp.where(mask, s + bias_ref[g][None], NEG_INF)
        p = _sink_softmax(s, sink_ref[g][None])
        o = jnp.einsum('nqk,nkd->nqd', p.astype(BF16), v_g, preferred_element_type=F32)
        o_ref[:, g] = o.astype(BF16)


def _sample_attn(q, k_cache, v_cache, k_new, v_new, bias, sink_rows, layer_idx):
    _, n, buf, _ = k_cache.shape
    new = k_new.shape[1]
    rows = new * GROUP
    nt = SAMPLE_SEQ_TILE
    seq3 = lambda a, b: pl.BlockSpec((nt, a, b), lambda i: (i, 0, 0))
    cache = pl.BlockSpec((None, nt, buf, KV_WIDTH), lambda i: (layer_idx, i, 0, 0))
    q_spec = pl.BlockSpec((nt, N_KV_HEADS, rows, HEAD_DIM), lambda i: (i, 0, 0, 0))
    blocks = (_nbytes((nt, buf, KV_WIDTH), F32) * 4 + _nbytes((nt, new, KV_WIDTH), F32) * 2
              + _nbytes((nt, N_KV_HEADS, rows, HEAD_DIM), BF16) * 2 + _nbytes(bias.shape, F32))
    temps = (_nbytes((nt, buf + new, KV_WIDTH), F32) * 4
             + _nbytes((nt, rows, 2 * BLOCK), F32) * 6)
    return pl.pallas_call(
        _sample_attn_kernel,
        grid=(n // nt,),
        in_specs=[q_spec, cache, cache, seq3(new, KV_WIDTH), seq3(new, KV_WIDTH),
                  pl.BlockSpec(bias.shape, lambda i: (0, 0, 0)),
                  pl.BlockSpec(sink_rows.shape, lambda i: (0, 0, 0))],
        out_specs=[q_spec, seq3(buf, KV_WIDTH), seq3(buf, KV_WIDTH)],
        out_shape=[jax.ShapeDtypeStruct(q.shape, BF16),
                   jax.ShapeDtypeStruct(k_cache.shape[1:], F32),
                   jax.ShapeDtypeStruct(v_cache.shape[1:], F32)],
        compiler_params=pltpu.CompilerParams(
            dimension_semantics=("arbitrary",), vmem_limit_bytes=_vmem_limit(blocks, temps)),
        name="attn_sample",
    )(q, k_cache, v_cache, k_new, v_new, bias, sink_rows)


def _mix_kernel(x_ref, z_ref, a_ref, gate_ref, mod_ref, g_ref, wbc_ref, wba_ref, wo_ref, o_ref,
                *, sample):
    y_conv = _dot(_flat(z_ref[...]), wbc_ref[...])
    y_attn = _dot(_flat(a_ref[...]), wba_ref[...])
    gates = _flat(gate_ref[...])
    merged = gates[:, :D_MODEL] * y_conv + gates[:, D_MODEL:] * y_attn
    mixed = _dot(merged.astype(BF16), wo_ref[...])
    r = (_rms(mixed) * g_ref[...]).reshape(x_ref.shape)
    o_ref[...] = x_ref[...] + _mod(mod_ref, 2, sample) * r


def _mix(x, z, a, gates, mod_all, g_post1, w_bc, w_ba, w_o, layer_idx, n_seq, sample):
    cfg = _dense(x, layer_idx, n_seq, sample)
    sq = (D_MODEL, D_MODEL)
    blocks = (_nbytes((cfg.rows, D_MODEL), F32) * 2 + _nbytes((cfg.rows, D_MODEL), BF16) * 2
              + _nbytes((cfg.rows, 2 * D_MODEL), F32) + _nbytes(sq, BF16) * 3)
    temps = _nbytes((cfg.rows, D_MODEL), F32) * 8
    return pl.pallas_call(
        functools.partial(_mix_kernel, sample=cfg.sample),
        grid=cfg.grid,
        in_specs=[cfg.act(D_MODEL), cfg.act(CONV_DIM), cfg.act(ATTN_WIDTH),
                  cfg.act(2 * D_MODEL), cfg.mod_spec, cfg.layer(1, D_MODEL),
                  cfg.layer(*sq), cfg.layer(*sq), cfg.layer(*sq)],
        out_specs=cfg.act(D_MODEL),
        out_shape=jax.ShapeDtypeStruct(x.shape, F32),
        compiler_params=pltpu.CompilerParams(
            dimension_semantics=cfg.sem, vmem_limit_bytes=_vmem_limit(blocks, temps)),
        name="mix_sample" if cfg.sample else "mix_prompt",
    )(x, z, a, gates, mod_all, g_post1, w_bc, w_ba, w_o)


def _mlp_kernel(x_ref, mod_ref, gpre_ref, gpost_ref, w1_ref, w2_ref, o_ref, hid_ref, *, sample):
    x = x_ref[...]
    h = (_rms(x) * gpre_ref[...]) * (1.0 + _mod(mod_ref, 4, sample)) + _mod(mod_ref, 3, sample)
    hb = _flat(h).astype(BF16)
    for lo in range(0, D_FF, 2 * CHUNK):
        a = jnp.maximum(_dot(hb, w1_ref[:, lo:lo + 2 * CHUNK]), 0.0)
        hid_ref[:, lo:lo + 2 * CHUNK] = (a * a).astype(BF16)
    ff = _dot(hid_ref[...], w2_ref[...])
    r = (_rms(ff) * gpost_ref[...]).reshape(x.shape)
    o_ref[...] = x + _mod(mod_ref, 5, sample) * r


def _mlp(x, mod_all, g_pre2, g_post2, w1, w2, layer_idx, n_seq, sample):
    cfg = _dense(x, layer_idx, n_seq, sample)
    blocks = (_nbytes((cfg.rows, D_MODEL), F32) * 2 + _nbytes(w1.shape[1:], BF16) * 2)
    temps = (_nbytes((cfg.rows, D_FF), BF16) + _nbytes((cfg.rows, 2 * CHUNK), F32) * 3
             + _nbytes((cfg.rows, D_MODEL), F32) * 6)
    return pl.pallas_call(
        functools.partial(_mlp_kernel, sample=cfg.sample),
        grid=cfg.grid,
        in_specs=[cfg.act(D_MODEL), cfg.mod_spec, cfg.layer(1, D_MODEL), cfg.layer(1, D_MODEL),
                  cfg.layer(*w1.shape[1:]), cfg.layer(*w2.shape[1:])],
        out_specs=cfg.act(D_MODEL),
        out_shape=jax.ShapeDtypeStruct(x.shape, F32),
        scratch_shapes=[pltpu.VMEM((cfg.rows, D_FF), BF16)],
        compiler_params=pltpu.CompilerParams(
            dimension_semantics=cfg.sem, vmem_limit_bytes=_vmem_limit(blocks, temps)),
        name="mlp_sample" if cfg.sample else "mlp_prompt",
    )(x, mod_all, g_pre2, g_post2, w1, w2)


def kernel(x_prompt, x_sample, c_prompt, c_sample, state_conv, cache_k, cache_v, w_ada, b_ada,
           g_pre1, w_in, conv_w, w_br_conv, w_br_attn, w_o, sinks, g_post1, g_pre2, w_ff1, w_ff2,
           g_post2, rel_table):
    depth = w_ada.shape[0]
    batch, seq, _ = x_prompt.shape
    n_seq, n_new, _ = x_sample.shape
    buf = cache_k.shape[2]

    pad = (-(n_seq + batch)) % V7X_SUBLANES
    c_all = jnp.concatenate([c_sample, c_prompt, jnp.zeros((pad, D_MODEL), F32)], axis=0)
    mod_all = _ada(c_all, w_ada, b_ada)

    dist_p = (jnp.arange(BLOCK)[:, None] + BLOCK) - jnp.arange(2 * BLOCK)[None, :]
    bucket_p = _rel_bucket(dist_p)
    upper = jnp.arange(BLOCK)[None, :] > jnp.arange(BLOCK)[:, None]
    bias_p = _bias_table(rel_table, jnp.where(upper, bucket_p[:, :BLOCK], bucket_p[:, BLOCK:]))
    dist_s = (buf + jnp.arange(n_new))[:, None] - jnp.arange(buf + n_new)[None, :]
    bias_s = _bias_table(rel_table, _rel_bucket(dist_s))
    bias_s = (bias_s.reshape(N_KV_HEADS, GROUP, n_new, buf + n_new)
              .transpose(0, 2, 1, 3).reshape(N_KV_HEADS, n_new * GROUP, buf + n_new))

    to_bf16 = lambda w: w.astype(BF16)
    w_in_b, w_bc_b, w_ba_b, w_o_b, w1_b, w2_b = map(
        to_bf16, (w_in, w_br_conv, w_br_attn, w_o, w_ff1, w_ff2))
    vec = lambda g: g.reshape(depth, 1, D_MODEL)
    g_pre1, g_post1, g_pre2, g_post2 = map(vec, (g_pre1, g_post1, g_pre2, g_post2))
    pre_s = state_conv.transpose(0, 2, 1, 3)
    cache_k = cache_k.reshape(depth, n_seq, buf, KV_WIDTH)
    cache_v = cache_v.reshape(depth, n_seq, buf, KV_WIDTH)

    xp = x_prompt
    xs = x_sample.transpose(1, 0, 2)
    conv_p, k_p, v_p, conv_s, k_s, v_s = [], [], [], [], [], []
    win = min(WINDOW, seq)
    for l in range(depth):
        z, q, kv, gates, tail = _in_proj(xp, mod_all, g_pre1, w_in_b, conv_w, l, n_seq)
        attn = _prompt_attn(q, kv, bias_p, sinks, l)
        x1 = _mix(xp, z, attn, gates, mod_all, g_post1, w_bc_b, w_ba_b, w_o_b, l, n_seq, False)
        xp = _mlp(x1, mod_all, g_pre2, g_post2, w1_b, w2_b, l, n_seq, False)
        conv_p.append(tail[:, V7X_SUBLANES - (CONV_WIDTH - 1):])
        k_p.append(kv[:, seq - win:, :KV_WIDTH].reshape(batch, win, N_KV_HEADS, HEAD_DIM))
        v_p.append(kv[:, seq - win:, KV_WIDTH:].reshape(batch, win, N_KV_HEADS, HEAD_DIM))

        z, q, kv, gates, tail = _in_proj(xs, mod_all, g_pre1, w_in_b, conv_w, l, n_seq, pre_s)
        q_s = (q.reshape(n_new, n_seq, N_KV_HEADS, GROUP, HEAD_DIM)
               .transpose(1, 2, 0, 3, 4).reshape(n_seq, N_KV_HEADS, n_new * GROUP, HEAD_DIM))
        kv_s = kv.transpose(1, 0, 2)
        sink_rows = jnp.tile(sinks[l].reshape(N_KV_HEADS, 1, GROUP),
                             (1, n_new, 1)).reshape(N_KV_HEADS, n_new * GROUP, 1)
        o, k_out, v_out = _sample_attn(q_s, cache_k, cache_v, kv_s[:, :, :KV_WIDTH],
                                       kv_s[:, :, KV_WIDTH:], bias_s, sink_rows, l)
        attn = (o.reshape(n_seq, N_KV_HEADS, n_new, GROUP, HEAD_DIM)
                .transpose(2, 0, 1, 3, 4).reshape(n_new, n_seq, ATTN_WIDTH))
        x1 = _mix(xs, z, attn, gates, mod_all, g_post1, w_bc_b, w_ba_b, w_o_b, l, n_seq, True)
        xs = _mlp(x1, mod_all, g_pre2, g_post2, w1_b, w2_b, l, n_seq, True)
        conv_s.append(tail.transpose(1, 0, 2))
        k_s.append(k_out.reshape(n_seq, buf, N_KV_HEADS, HEAD_DIM))
        v_s.append(v_out.reshape(n_seq, buf, N_KV_HEADS, HEAD_DIM))

    return (xp, xs.transpose(1, 0, 2), jnp.stack(conv_p), jnp.stack(k_p), jnp.stack(v_p),
            jnp.stack(conv_s), jnp.stack(k_s), jnp.stack(v_s))
```

```python
import functools
import math
from typing import Any, Callable, NamedTuple

import jax
import jax.numpy as jnp
from jax import lax
from jax.experimental import pallas as pl
from jax.experimental.pallas import tpu as pltpu

D_MODEL = 1024
N_HEADS = 16
N_KV_HEADS = 2
HEAD_DIM = 64
GROUP = N_HEADS // N_KV_HEADS
ATTN_WIDTH = N_HEADS * HEAD_DIM
KV_WIDTH = N_KV_HEADS * HEAD_DIM
CONV_DIM = D_MODEL
CONV_WIDTH = 3
WINDOW = 128
BLOCK = 128
N_BUCKETS = 32
MAX_DISTANCE = 128
D_FF = 4 * D_MODEL
N_MOD = 6
RMS_EPS = 1e-6
NEG_INF = -1e30
PROJ_COLS = 3 * CONV_DIM + ATTN_WIDTH + 2 * KV_WIDTH + 2 * D_MODEL
Q_OFF = 3 * CONV_DIM
KV_OFF = Q_OFF + ATTN_WIDTH
GATE_OFF = KV_OFF + 2 * KV_WIDTH

V7X_SUBLANES = 8
V7X_VMEM_BYTES = 64 * 1024 * 1024

PROMPT_TILE = 512
ATTN_TILE = 512
SAMPLE_POS_TILE = 4
SAMPLE_SEQ_TILE = 16
CHUNK = 512

F32 = jnp.float32
BF16 = jnp.bfloat16


def _vmem_limit(block_bytes, temp_bytes):
    return int(min(2 * block_bytes + temp_bytes, V7X_VMEM_BYTES - 4 * 1024 * 1024))


def _nbytes(shape, dtype):
    return math.prod(shape) * jnp.dtype(dtype).itemsize


def _rms(x):
    return x * lax.rsqrt(jnp.mean(x * x, axis=-1, keepdims=True) + RMS_EPS)


def _dot(a, b):
    return jnp.dot(a, b, preferred_element_type=F32)


def _flat(a):
    return a.reshape(-1, a.shape[-1])


def _ada_kernel(c_ref, w_ref, b_ref, o_ref):
    c = c_ref[...]
    s = c * (1.0 / (1.0 + jnp.exp(-c)))
    o_ref[...] = _dot(s.astype(BF16), w_ref[...].astype(BF16)) + b_ref[...]


def _ada(c_all, w_ada, b_ada):
    depth = w_ada.shape[0]
    rows = c_all.shape[0]
    blocks = (_nbytes((rows, D_MODEL), F32) * 2 + _nbytes((D_MODEL, D_MODEL), F32))
    return pl.pallas_call(
        _ada_kernel,
        grid=(depth, N_MOD),
        in_specs=[
            pl.BlockSpec((rows, D_MODEL), lambda l, j: (0, 0)),
            pl.BlockSpec((None, D_MODEL, D_MODEL), lambda l, j: (l, 0, j)),
            pl.BlockSpec((None, None, 1, D_MODEL), lambda l, j: (l, j, 0, 0)),
        ],
        out_specs=pl.BlockSpec((None, None, rows, D_MODEL), lambda l, j: (l, j, 0, 0)),
        out_shape=jax.ShapeDtypeStruct((depth, N_MOD, rows, D_MODEL), F32),
        compiler_params=pltpu.CompilerParams(
            dimension_semantics=("arbitrary", "arbitrary"),
            vmem_limit_bytes=_vmem_limit(blocks, _nbytes((D_MODEL, D_MODEL), BF16) * 2)),
        name="ada_mod",
    )(c_all, w_ada, b_ada.reshape(depth, N_MOD, 1, D_MODEL))


def _bias_kernel(tab_ref, bucket_ref, o_ref):
    bucket = bucket_ref[...]
    for h in range(N_HEADS):
        acc = jnp.zeros(bucket.shape, F32)
        for b in range(N_BUCKETS):
            acc = jnp.where(bucket == b, tab_ref[b * N_HEADS + h], acc)
        o_ref[h] = acc


def _bias_table(rel_table, bucket):
    return pl.pallas_call(
        _bias_kernel,
        in_specs=[
            pl.BlockSpec(memory_space=pltpu.SMEM),
            pl.BlockSpec(bucket.shape, lambda: (0, 0)),
        ],
        out_specs=pl.BlockSpec((N_HEADS,) + bucket.shape, lambda: (0, 0, 0)),
        out_shape=jax.ShapeDtypeStruct((N_HEADS,) + bucket.shape, F32),
        name="bias_table",
    )(rel_table.reshape(-1), bucket)


def _rel_bucket(dist):
    n = jnp.maximum(dist, 0)
    max_exact = N_BUCKETS // 2
    nf = jnp.maximum(n, 1).astype(F32)
    large = max_exact + (jnp.log(nf / max_exact) / math.log(MAX_DISTANCE / max_exact)
                         * (N_BUCKETS - max_exact)).astype(jnp.int32)
    large = jnp.minimum(large, N_BUCKETS - 1)
    return jnp.where(n < max_exact, n, large)


class _Dense(NamedTuple):
    sample: bool
    grid: tuple
    rows: int
    act: Callable[[int], Any]
    layer: Callable[..., Any]
    mod_spec: Any
    sem: tuple


def _dense(x, layer_idx, n_seq, sample):
    lead, mid = x.shape[0], x.shape[1]
    if sample:
        grid = (lead // SAMPLE_POS_TILE,)
        act = lambda w: pl.BlockSpec((SAMPLE_POS_TILE, mid, w), lambda i: (i, 0, 0))
        layer = lambda *shape: pl.BlockSpec((None,) + shape,
                                            lambda i: (layer_idx,) + (0,) * len(shape))
        mod_spec = pl.BlockSpec((None, N_MOD, n_seq, D_MODEL), lambda i: (layer_idx, 0, 0, 0))
        return _Dense(True, grid, SAMPLE_POS_TILE * mid, act, layer, mod_spec, ("arbitrary",))
    assert n_seq % V7X_SUBLANES == 0 and lead <= V7X_SUBLANES
    grid = (lead, mid // PROMPT_TILE)
    act = lambda w: pl.BlockSpec((None, PROMPT_TILE, w), lambda b, t: (b, t, 0))
    layer = lambda *shape: pl.BlockSpec((None,) + shape,
                                        lambda b, t: (layer_idx,) + (0,) * len(shape))
    mod_spec = pl.BlockSpec((None, N_MOD, V7X_SUBLANES, D_MODEL),
                            lambda b, t: (layer_idx, 0, n_seq // V7X_SUBLANES, 0))
    return _Dense(False, grid, PROMPT_TILE, act, layer, mod_spec, ("arbitrary", "arbitrary"))


def _mod(mod_ref, j, sample):
    if sample:
        return mod_ref[j]
    return mod_ref[j, pl.ds(pl.program_id(0), 1), :]


def _in_proj_kernel(*refs, sample):
    if sample:
        (x_ref, mod_ref, g_ref, w_ref, cw_ref, pre_ref,
         z_ref, q_ref, kv_ref, gate_ref, tail_ref, carry_ref) = refs
    else:
        (x_ref, mod_ref, g_ref, w_ref, cw_ref,
         z_ref, q_ref, kv_ref, gate_ref, tail_ref, carry_ref) = refs
    step = pl.program_id(0) if sample else pl.program_id(1)

    @pl.when(step == 0)
    def _():
        if sample:
            carry_ref[...] = pre_ref[...]
        else:
            carry_ref[...] = jnp.zeros(carry_ref.shape, F32)

    x = x_ref[...]
    h = (_rms(x) * g_ref[...]) * (1.0 + _mod(mod_ref, 1, sample)) + _mod(mod_ref, 0, sample)
    hb = _flat(h).astype(BF16)
    rows = hb.shape[0]

    def store(ref, lo, val):
        ref[..., lo:lo + val.shape[-1]] = val.reshape(ref.shape[:-1] + (val.shape[-1],))

    for lo in range(0, CONV_DIM, CHUNK):
        b_g = _dot(hb, w_ref[:, lo:lo + CHUNK])
        c_g = _dot(hb, w_ref[:, CONV_DIM + lo:CONV_DIM + lo + CHUNK])
        x_c = _dot(hb, w_ref[:, 2 * CONV_DIM + lo:2 * CONV_DIM + lo + CHUNK])
        u = c_g * x_c
        if sample:
            pos = x_ref.shape[0]
            u3 = u.reshape(pos, -1, CHUNK)
            ext = jnp.concatenate([carry_ref[:, :, lo:lo + CHUNK], u3], axis=0)
            u2 = _flat(ext[0:pos])
            u1 = _flat(ext[1:pos + 1])
            new_tail = u3[pos - (CONV_WIDTH - 1):]
            carry_ref[:, :, lo:lo + CHUNK] = new_tail
            tail_ref[:, :, lo:lo + CHUNK] = new_tail
        else:
            prev = carry_ref[:, lo:lo + CHUNK]
            p1 = prev[V7X_SUBLANES - 1:V7X_SUBLANES]
            p2 = prev[V7X_SUBLANES - 2:V7X_SUBLANES - 1]
            r = lax.broadcasted_iota(jnp.int32, u.shape, 0)
            u1 = jnp.where(r == 0, p1, pltpu.roll(u, 1, 0))
            u2 = jnp.where(r == 0, p2, jnp.where(r == 1, p1, pltpu.roll(u, 2, 0)))
            new_tail = u[rows - V7X_SUBLANES:]
            carry_ref[:, lo:lo + CHUNK] = new_tail
            tail_ref[:, lo:lo + CHUNK] = new_tail
        conv = (cw_ref[0:1, lo:lo + CHUNK] * u2 + cw_ref[1:2, lo:lo + CHUNK] * u1
                + cw_ref[2:3, lo:lo + CHUNK] * u)
        store(z_ref, lo, (b_g * conv).astype(BF16))

    for lo in range(0, ATTN_WIDTH, CHUNK):
        q = _dot(hb, w_ref[:, Q_OFF + lo:Q_OFF + lo + CHUNK])
        store(q_ref, lo, (q * (HEAD_DIM ** -0.5)).astype(BF16))

    store(kv_ref, 0, _dot(hb, w_ref[:, KV_OFF:KV_OFF + 2 * KV_WIDTH]))

    for lo in range(0, 2 * D_MODEL, CHUNK):
        g = _dot(hb, w_ref[:, GATE_OFF + lo:GATE_OFF + lo + CHUNK])
        store(gate_ref, lo, 1.0 / (1.0 + jnp.exp(-g)))


def _in_proj(x, mod_all, g_pre1, w_in, conv_w, layer_idx, n_seq, pre=None):
    cfg = _dense(x, layer_idx, n_seq, pre is not None)
    lead, mid = x.shape[0], x.shape[1]
    if cfg.sample:
        tail_shape = (CONV_WIDTH - 1, mid, CONV_DIM)
        tail_spec = pl.BlockSpec(tail_shape, lambda i: (0, 0, 0))
        extra_in = [pre]
        extra_specs = [cfg.layer(*tail_shape)]
        carry = pltpu.VMEM(tail_shape, F32)
    else:
        tail_shape = (lead, V7X_SUBLANES, CONV_DIM)
        tail_spec = pl.BlockSpec((None, V7X_SUBLANES, CONV_DIM), lambda b, t: (b, 0, 0))
        extra_in, extra_specs = [], []
        carry = pltpu.VMEM((V7X_SUBLANES, CONV_DIM), F32)

    widths = (CONV_DIM, ATTN_WIDTH, 2 * KV_WIDTH, 2 * D_MODEL)
    dtypes = (BF16, BF16, F32, F32)
    out_shape = [jax.ShapeDtypeStruct(x.shape[:-1] + (w,), dt) for w, dt in zip(widths, dtypes)]
    out_shape.append(jax.ShapeDtypeStruct(tail_shape, F32))
    blocks = (_nbytes((cfg.rows, D_MODEL), F32) + _nbytes(w_in.shape[1:], BF16)
              + sum(_nbytes((cfg.rows, w), dt) for w, dt in zip(widths, dtypes)))
    temps = _nbytes((cfg.rows, D_MODEL), F32) * 2 + _nbytes((cfg.rows, CHUNK), F32) * 10
    return pl.pallas_call(
        functools.partial(_in_proj_kernel, sample=cfg.sample),
        grid=cfg.grid,
        in_specs=[cfg.act(D_MODEL), cfg.mod_spec, cfg.layer(1, D_MODEL),
                  cfg.layer(*w_in.shape[1:]), cfg.layer(*conv_w.shape[1:])] + extra_specs,
        out_specs=[cfg.act(w) for w in widths] + [tail_spec],
        out_shape=out_shape,
        scratch_shapes=[carry],
        compiler_params=pltpu.CompilerParams(
            dimension_semantics=cfg.sem, vmem_limit_bytes=_vmem_limit(blocks, temps)),
        name="in_proj_sample" if cfg.sample else "in_proj_prompt",
    )(x, mod_all, g_pre1, w_in, conv_w, *extra_in)


def _sink_softmax(s, sink):
    m = jnp.maximum(jnp.max(s, axis=-1, keepdims=True), sink)
    e = jnp.exp(s - m)
    den = jnp.sum(e, axis=-1, keepdims=True) + jnp.exp(sink - m)
    return e * (1.0 / den)


def _prompt_attn_kernel(sink_ref, q_ref, kvp_ref, kvc_ref, bias_ref, o_ref,
                        kd_ref, vt_ref, s_ref):
    first_tile = pl.program_id(1) == 0
    n_blk = q_ref.shape[0] // BLOCK
    kv = jnp.concatenate([kvp_ref[...], kvc_ref[...]], axis=0)
    for g in range(N_KV_HEADS):
        k_g = kv[:, g * HEAD_DIM:(g + 1) * HEAD_DIM]
        kd_ref[g] = jnp.concatenate([k_g, k_g], axis=1).astype(BF16)
    v_t = kv[:, KV_WIDTH:].T.astype(BF16)
    for blk in range(n_blk + 1):
        vt_ref[blk] = v_t[:, blk * BLOCK:(blk + 1) * BLOCK]
    kj = lax.broadcasted_iota(jnp.int32, (BLOCK, BLOCK), 0)
    qi = lax.broadcasted_iota(jnp.int32, (BLOCK, BLOCK), 1)
    upper = kj > qi
    low_lanes = lax.broadcasted_iota(jnp.int32, (BLOCK, 2 * HEAD_DIM), 1) < HEAD_DIM

    def score_phase(c):
        q = q_ref[c * BLOCK:(c + 1) * BLOCK, :]
        for pair in range(N_HEADS // 2):
            g = (2 * pair) // GROUP
            q_pair = q[:, pair * 2 * HEAD_DIM:(pair + 1) * 2 * HEAD_DIM]
            zero = jnp.zeros_like(q_pair)
            q_both = jnp.concatenate([jnp.where(low_lanes, q_pair, zero),
                                      jnp.where(low_lanes, zero, q_pair)], axis=0)
            s_both = lax.dot_general(kd_ref[g, c * BLOCK:(c + 2) * BLOCK, :], q_both,
                                     (((1,), (1,)), ((), ())), preferred_element_type=F32)
            for par in range(2):
                h = 2 * pair + par
                s_h = s_both[:, par * BLOCK:(par + 1) * BLOCK]
                s = jnp.where(upper, s_h[:BLOCK], s_h[BLOCK:]) + bias_ref[h]
                if c == 0:
                    s = jnp.where(upper & first_tile, NEG_INF, s)
                s_ref[c, h] = s

    def value_phase(c):
        v_cat = jnp.concatenate([vt_ref[c], vt_ref[c + 1]], axis=1)
        outs = []
        for h in range(N_HEADS):
            g = h // GROUP
            s = s_ref[c, h]
            sink = sink_ref[h]
            m = jnp.maximum(jnp.max(s, axis=0, keepdims=True), sink)
            e = jnp.exp(s - m)
            den = jnp.sum(e, axis=0, keepdims=True) + jnp.exp(sink - m)
            p_t = jnp.concatenate([jnp.where(upper, e, 0.0), jnp.where(upper, 0.0, e)],
                                  axis=0).astype(BF16)
            o_t = _dot(v_cat[g * HEAD_DIM:(g + 1) * HEAD_DIM], p_t)
            outs.append(o_t * (1.0 / den))
        o_ref[c * BLOCK:(c + 1) * BLOCK, :] = jnp.concatenate(outs, axis=0).T.astype(BF16)

    score_phase(0)
    for c in range(n_blk):
        if c + 1 < n_blk:
            score_phase(c + 1)
        value_phase(c)


def _prompt_attn(q, kv, bias, sinks, layer_idx):
    batch, seq, _ = q.shape
    tq = ATTN_TILE
    per_tile = tq // BLOCK
    blocks = (_nbytes((tq, ATTN_WIDTH), BF16) * 2 + _nbytes((tq + BLOCK, 2 * KV_WIDTH), F32)
              + _nbytes(bias.shape, F32))
    temps = (_nbytes((tq + BLOCK, 2 * KV_WIDTH), BF16) * 2
             + _nbytes((per_tile, N_HEADS, BLOCK, BLOCK), F32)
             + _nbytes((BLOCK, 2 * BLOCK), F32) * 16 + _nbytes((BLOCK, ATTN_WIDTH), F32) * 4)
    return pl.pallas_call(
        _prompt_attn_kernel,
        grid=(batch, seq // tq),
        in_specs=[
            pl.BlockSpec(memory_space=pltpu.SMEM),
            pl.BlockSpec((None, tq, ATTN_WIDTH), lambda b, t: (b, t, 0)),
            pl.BlockSpec((None, BLOCK, 2 * KV_WIDTH),
                         lambda b, t: (b, jnp.maximum(t * per_tile - 1, 0), 0)),
            pl.BlockSpec((None, tq, 2 * KV_WIDTH), lambda b, t: (b, t, 0)),
            pl.BlockSpec(bias.shape, lambda b, t: (0, 0, 0)),
        ],
        out_specs=pl.BlockSpec((None, tq, ATTN_WIDTH), lambda b, t: (b, t, 0)),
        out_shape=jax.ShapeDtypeStruct(q.shape, BF16),
        scratch_shapes=[pltpu.VMEM((N_KV_HEADS, tq + BLOCK, 2 * HEAD_DIM), BF16),
                        pltpu.VMEM((per_tile + 1, KV_WIDTH, BLOCK), BF16),
                        pltpu.VMEM((per_tile, N_HEADS, BLOCK, BLOCK), F32)],
        compiler_params=pltpu.CompilerParams(
            dimension_semantics=("arbitrary", "arbitrary"),
            vmem_limit_bytes=_vmem_limit(blocks, temps)),
        name="attn_prompt",
    )(sinks[layer_idx], q, kv, kv, bias)


def _sample_attn_kernel(q_ref, kc_ref, vc_ref, kn_ref, vn_ref, bias_ref, sink_ref,
                        o_ref, ko_ref, vo_ref):
    buf = kc_ref.shape[1]
    new = kn_ref.shape[1]
    kk = jnp.concatenate([kc_ref[...], kn_ref[...]], axis=1)
    vv = jnp.concatenate([vc_ref[...], vn_ref[...]], axis=1)
    ko_ref[...] = kk[:, new:]
    vo_ref[...] = vv[:, new:]
    rows = new * GROUP
    pos = lax.broadcasted_iota(jnp.int32, (rows, buf + new), 0) // GROUP
    kj = lax.broadcasted_iota(jnp.int32, (rows, buf + new), 1)
    dist = buf + pos - kj
    mask = ((dist >= 0) & (dist < WINDOW))[None]
    kb = kk.astype(BF16)
    vb = vv.astype(BF16)
    for g in range(N_KV_HEADS):
        k_g = kb[:, :, g * HEAD_DIM:(g + 1) * HEAD_DIM]
        v_g = vb[:, :, g * HEAD_DIM:(g + 1) * HEAD_DIM]
        s = jnp.einsum('nqd,nkd->nqk', q_ref[:, g], k_g, preferred_element_type=F32)
        s = jnp.where(mask, s + bias_ref[g][None], NEG_INF)
        p = _sink_softmax(s, sink_ref[g][None])
        o = jnp.einsum('nqk,nkd->nqd', p.astype(BF16), v_g, preferred_element_type=F32)
        o_ref[:, g] = o.astype(BF16)


def _sample_attn(q, k_cache, v_cache, k_new, v_new, bias, sink_rows, layer_idx):
    _, n, buf, _ = k_cache.shape
    new = k_new.shape[1]
    rows = new * GROUP
    nt = SAMPLE_SEQ_TILE
    seq3 = lambda a, b: pl.BlockSpec((nt, a, b), lambda i: (i, 0, 0))
    cache = pl.BlockSpec((None, nt, buf, KV_WIDTH), lambda i: (layer_idx, i, 0, 0))
    q_spec = pl.BlockSpec((nt, N_KV_HEADS, rows, HEAD_DIM), lambda i: (i, 0, 0, 0))
    blocks = (_nbytes((nt, buf, KV_WIDTH), F32) * 4 + _nbytes((nt, new, KV_WIDTH), F32) * 2
              + _nbytes((nt, N_KV_HEADS, rows, HEAD_DIM), BF16) * 2 + _nbytes(bias.shape, F32))
    temps = (_nbytes((nt, buf + new, KV_WIDTH), F32) * 4
             + _nbytes((nt, rows, 2 * BLOCK), F32) * 6)
    return pl.pallas_call(
        _sample_attn_kernel,
        grid=(n // nt,),
        in_specs=[q_spec, cache, cache, seq3(new, KV_WIDTH), seq3(new, KV_WIDTH),
                  pl.BlockSpec(bias.shape, lambda i: (0, 0, 0)),
                  pl.BlockSpec(sink_rows.shape, lambda i: (0, 0, 0))],
        out_specs=[q_spec, seq3(buf, KV_WIDTH), seq3(buf, KV_WIDTH)],
        out_shape=[jax.ShapeDtypeStruct(q.shape, BF16),
                   jax.ShapeDtypeStruct(k_cache.shape[1:], F32),
                   jax.ShapeDtypeStruct(v_cache.shape[1:], F32)],
        compiler_params=pltpu.CompilerParams(
            dimension_semantics=("arbitrary",), vmem_limit_bytes=_vmem_limit(blocks, temps)),
        name="attn_sample",
    )(q, k_cache, v_cache, k_new, v_new, bias, sink_rows)


def _mix_kernel(x_ref, z_ref, a_ref, gate_ref, mod_ref, g_ref, wbc_ref, wba_ref, wo_ref, o_ref,
                *, sample):
    y_conv = _dot(_flat(z_ref[...]), wbc_ref[...])
    y_attn = _dot(_flat(a_ref[...]), wba_ref[...])
    gates = _flat(gate_ref[...])
    merged = gates[:, :D_MODEL] * y_conv + gates[:, D_MODEL:] * y_attn
    mixed = _dot(merged.astype(BF16), wo_ref[...])
    r = (_rms(mixed) * g_ref[...]).reshape(x_ref.shape)
    o_ref[...] = x_ref[...] + _mod(mod_ref, 2, sample) * r


def _mix(x, z, a, gates, mod_all, g_post1, w_bc, w_ba, w_o, layer_idx, n_seq, sample):
    cfg = _dense(x, layer_idx, n_seq, sample)
    sq = (D_MODEL, D_MODEL)
    blocks = (_nbytes((cfg.rows, D_MODEL), F32) * 2 + _nbytes((cfg.rows, D_MODEL), BF16) * 2
              + _nbytes((cfg.rows, 2 * D_MODEL), F32) + _nbytes(sq, BF16) * 3)
    temps = _nbytes((cfg.rows, D_MODEL), F32) * 8
    return pl.pallas_call(
        functools.partial(_mix_kernel, sample=cfg.sample),
        grid=cfg.grid,
        in_specs=[cfg.act(D_MODEL), cfg.act(CONV_DIM), cfg.act(ATTN_WIDTH),
                  cfg.act(2 * D_MODEL), cfg.mod_spec, cfg.layer(1, D_MODEL),
                  cfg.layer(*sq), cfg.layer(*sq), cfg.layer(*sq)],
        out_specs=cfg.act(D_MODEL),
        out_shape=jax.ShapeDtypeStruct(x.shape, F32),
        compiler_params=pltpu.CompilerParams(
            dimension_semantics=cfg.sem, vmem_limit_bytes=_vmem_limit(blocks, temps)),
        name="mix_sample" if cfg.sample else "mix_prompt",
    )(x, z, a, gates, mod_all, g_post1, w_bc, w_ba, w_o)


def _mlp_kernel(x_ref, mod_ref, gpre_ref, gpost_ref, w1_ref, w2_ref, o_ref, hid_ref, *, sample):
    x = x_ref[...]
    h = (_rms(x) * gpre_ref[...]) * (1.0 + _mod(mod_ref, 4, sample)) + _mod(mod_ref, 3, sample)
    hb = _flat(h).astype(BF16)
    for lo in range(0, D_FF, 2 * CHUNK):
        a = jnp.maximum(_dot(hb, w1_ref[:, lo:lo + 2 * CHUNK]), 0.0)
        hid_ref[:, lo:lo + 2 * CHUNK] = (a * a).astype(BF16)
    ff = _dot(hid_ref[...], w2_ref[...])
    r = (_rms(ff) * gpost_ref[...]).reshape(x.shape)
    o_ref[...] = x + _mod(mod_ref, 5, sample) * r


def _mlp(x, mod_all, g_pre2, g_post2, w1, w2, layer_idx, n_seq, sample):
    cfg = _dense(x, layer_idx, n_seq, sample)
    blocks = (_nbytes((cfg.rows, D_MODEL), F32) * 2 + _nbytes(w1.shape[1:], BF16) * 2)
    temps = (_nbytes((cfg.rows, D_FF), BF16) + _nbytes((cfg.rows, 2 * CHUNK), F32) * 3
             + _nbytes((cfg.rows, D_MODEL), F32) * 6)
    return pl.pallas_call(
        functools.partial(_mlp_kernel, sample=cfg.sample),
        grid=cfg.grid,
        in_specs=[cfg.act(D_MODEL), cfg.mod_spec, cfg.layer(1, D_MODEL), cfg.layer(1, D_MODEL),
                  cfg.layer(*w1.shape[1:]), cfg.layer(*w2.shape[1:])],
        out_specs=cfg.act(D_MODEL),
        out_shape=jax.ShapeDtypeStruct(x.shape, F32),
        scratch_shapes=[pltpu.VMEM((cfg.rows, D_FF), BF16)],
        compiler_params=pltpu.CompilerParams(
            dimension_semantics=cfg.sem, vmem_limit_bytes=_vmem_limit(blocks, temps)),
        name="mlp_sample" if cfg.sample else "mlp_prompt",
    )(x, mod_all, g_pre2, g_post2, w1, w2)


def kernel(x_prompt, x_sample, c_prompt, c_sample, state_conv, cache_k, cache_v, w_ada, b_ada,
           g_pre1, w_in, conv_w, w_br_conv, w_br_attn, w_o, sinks, g_post1, g_pre2, w_ff1, w_ff2,
           g_post2, rel_table):
    depth = w_ada.shape[0]
    batch, seq, _ = x_prompt.shape
    n_seq, n_new, _ = x_sample.shape
    buf = cache_k.shape[2]

    pad = (-(n_seq + batch)) % V7X_SUBLANES
    c_all = jnp.concatenate([c_sample, c_prompt, jnp.zeros((pad, D_MODEL), F32)], axis=0)
    mod_all = _ada(c_all, w_ada, b_ada)

    dist_p = (jnp.arange(BLOCK)[:, None] + BLOCK) - jnp.arange(2 * BLOCK)[None, :]
    bucket_p = _rel_bucket(dist_p)
    upper = jnp.arange(BLOCK)[None, :] > jnp.arange(BLOCK)[:, None]
    bucket_merged = jnp.where(upper, bucket_p[:, :BLOCK], bucket_p[:, BLOCK:])
    bias_p = _bias_table(rel_table, bucket_merged.T)
    dist_s = (buf + jnp.arange(n_new))[:, None] - jnp.arange(buf + n_new)[None, :]
    bias_s = _bias_table(rel_table, _rel_bucket(dist_s))
    bias_s = (bias_s.reshape(N_KV_HEADS, GROUP, n_new, buf + n_new)
              .transpose(0, 2, 1, 3).reshape(N_KV_HEADS, n_new * GROUP, buf + n_new))

    to_bf16 = lambda w: w.astype(BF16)
    w_in_b, w_bc_b, w_ba_b, w_o_b, w1_b, w2_b = map(
        to_bf16, (w_in, w_br_conv, w_br_attn, w_o, w_ff1, w_ff2))
    vec = lambda g: g.reshape(depth, 1, D_MODEL)
    g_pre1, g_post1, g_pre2, g_post2 = map(vec, (g_pre1, g_post1, g_pre2, g_post2))
    pre_s = state_conv.transpose(0, 2, 1, 3)
    cache_k = cache_k.reshape(depth, n_seq, buf, KV_WIDTH)
    cache_v = cache_v.reshape(depth, n_seq, buf, KV_WIDTH)

    xp = x_prompt
    xs = x_sample.transpose(1, 0, 2)
    conv_p, k_p, v_p, conv_s, k_s, v_s = [], [], [], [], [], []
    win = min(WINDOW, seq)
    for l in range(depth):
        z, q, kv, gates, tail = _in_proj(xp, mod_all, g_pre1, w_in_b, conv_w, l, n_seq)
        attn = _prompt_attn(q, kv, bias_p, sinks, l)
        x1 = _mix(xp, z, attn, gates, mod_all, g_post1, w_bc_b, w_ba_b, w_o_b, l, n_seq, False)
        xp = _mlp(x1, mod_all, g_pre2, g_post2, w1_b, w2_b, l, n_seq, False)
        conv_p.append(tail[:, V7X_SUBLANES - (CONV_WIDTH - 1):])
        k_p.append(kv[:, seq - win:, :KV_WIDTH].reshape(batch, win, N_KV_HEADS, HEAD_DIM))
        v_p.append(kv[:, seq - win:, KV_WIDTH:].reshape(batch, win, N_KV_HEADS, HEAD_DIM))

        z, q, kv, gates, tail = _in_proj(xs, mod_all, g_pre1, w_in_b, conv_w, l, n_seq, pre_s)
        q_s = (q.reshape(n_new, n_seq, N_KV_HEADS, GROUP, HEAD_DIM)
               .transpose(1, 2, 0, 3, 4).reshape(n_seq, N_KV_HEADS, n_new * GROUP, HEAD_DIM))
        kv_s = kv.transpose(1, 0, 2)
        sink_rows = jnp.tile(sinks[l].reshape(N_KV_HEADS, 1, GROUP),
                             (1, n_new, 1)).reshape(N_KV_HEADS, n_new * GROUP, 1)
        o, k_out, v_out = _sample_attn(q_s, cache_k, cache_v, kv_s[:, :, :KV_WIDTH],
                                       kv_s[:, :, KV_WIDTH:], bias_s, sink_rows, l)
        attn = (o.reshape(n_seq, N_KV_HEADS, n_new, GROUP, HEAD_DIM)
                .transpose(2, 0, 1, 3, 4).reshape(n_new, n_seq, ATTN_WIDTH))
        x1 = _mix(xs, z, attn, gates, mod_all, g_post1, w_bc_b, w_ba_b, w_o_b, l, n_seq, True)
        xs = _mlp(x1, mod_all, g_pre2, g_post2, w1_b, w2_b, l, n_seq, True)
        conv_s.append(tail.transpose(1, 0, 2))
        k_s.append(k_out.reshape(n_seq, buf, N_KV_HEADS, HEAD_DIM))
        v_s.append(v_out.reshape(n_seq, buf, N_KV_HEADS, HEAD_DIM))

    return (xp, xs.transpose(1, 0, 2), jnp.stack(conv_p), jnp.stack(k_p), jnp.stack(v_p),
            jnp.stack(conv_s), jnp.stack(k_s), jnp.stack(v_s))
```

```python
import functools
import math
from typing import Any, Callable, NamedTuple

import jax
import jax.numpy as jnp
from jax import lax
from jax.experimental import pallas as pl
from jax.experimental.pallas import tpu as pltpu

D_MODEL = 1024
N_HEADS = 16
N_KV_HEADS = 2
HEAD_DIM = 64
GROUP = N_HEADS // N_KV_HEADS
ATTN_WIDTH = N_HEADS * HEAD_DIM
KV_WIDTH = N_KV_HEADS * HEAD_DIM
CONV_DIM = D_MODEL
CONV_WIDTH = 3
WINDOW = 128
BLOCK = 128
N_BUCKETS = 32
MAX_DISTANCE = 128
D_FF = 4 * D_MODEL
N_MOD = 6
RMS_EPS = 1e-6
NEG_INF = -1e30
PROJ_COLS = 3 * CONV_DIM + ATTN_WIDTH + 2 * KV_WIDTH + 2 * D_MODEL
Q_OFF = 3 * CONV_DIM
KV_OFF = Q_OFF + ATTN_WIDTH
GATE_OFF = KV_OFF + 2 * KV_WIDTH

V7X_SUBLANES = 8
V7X_VMEM_BYTES = 64 * 1024 * 1024

PROMPT_TILE = 512
ATTN_TILE = 512
SAMPLE_POS_TILE = 4
SAMPLE_SEQ_TILE = 16
CHUNK = 512

F32 = jnp.float32
BF16 = jnp.bfloat16


def _vmem_limit(block_bytes, temp_bytes):
    return int(min(2 * block_bytes + temp_bytes, V7X_VMEM_BYTES - 4 * 1024 * 1024))


def _nbytes(shape, dtype):
    return math.prod(shape) * jnp.dtype(dtype).itemsize


def _rms(x):
    return x * lax.rsqrt(jnp.mean(x * x, axis=-1, keepdims=True) + RMS_EPS)


def _dot(a, b):
    return jnp.dot(a, b, preferred_element_type=F32)


def _flat(a):
    return a.reshape(-1, a.shape[-1])


def _ada_kernel(c_ref, w_ref, b_ref, o_ref):
    c = c_ref[...]
    s = c * (1.0 / (1.0 + jnp.exp(-c)))
    o_ref[...] = _dot(s.astype(BF16), w_ref[...].astype(BF16)) + b_ref[...]


def _ada(c_all, w_ada, b_ada):
    depth = w_ada.shape[0]
    rows = c_all.shape[0]
    blocks = (_nbytes((rows, D_MODEL), F32) * 2 + _nbytes((D_MODEL, D_MODEL), F32))
    return pl.pallas_call(
        _ada_kernel,
        grid=(depth, N_MOD),
        in_specs=[
            pl.BlockSpec((rows, D_MODEL), lambda l, j: (0, 0)),
            pl.BlockSpec((None, D_MODEL, D_MODEL), lambda l, j: (l, 0, j)),
            pl.BlockSpec((None, None, 1, D_MODEL), lambda l, j: (l, j, 0, 0)),
        ],
        out_specs=pl.BlockSpec((None, None, rows, D_MODEL), lambda l, j: (l, j, 0, 0)),
        out_shape=jax.ShapeDtypeStruct((depth, N_MOD, rows, D_MODEL), F32),
        compiler_params=pltpu.CompilerParams(
            dimension_semantics=("arbitrary", "arbitrary"),
            vmem_limit_bytes=_vmem_limit(blocks, _nbytes((D_MODEL, D_MODEL), BF16) * 2)),
        name="ada_mod",
    )(c_all, w_ada, b_ada.reshape(depth, N_MOD, 1, D_MODEL))


def _bias_kernel(tab_ref, bucket_ref, o_ref):
    bucket = bucket_ref[...]
    for h in range(N_HEADS):
        acc = jnp.zeros(bucket.shape, F32)
        for b in range(N_BUCKETS):
            acc = jnp.where(bucket == b, tab_ref[b * N_HEADS + h], acc)
        o_ref[h] = acc


def _bias_table(rel_table, bucket):
    return pl.pallas_call(
        _bias_kernel,
        in_specs=[
            pl.BlockSpec(memory_space=pltpu.SMEM),
            pl.BlockSpec(bucket.shape, lambda: (0, 0)),
        ],
        out_specs=pl.BlockSpec((N_HEADS,) + bucket.shape, lambda: (0, 0, 0)),
        out_shape=jax.ShapeDtypeStruct((N_HEADS,) + bucket.shape, F32),
        name="bias_table",
    )(rel_table.reshape(-1), bucket)


def _rel_bucket(dist):
    n = jnp.maximum(dist, 0)
    max_exact = N_BUCKETS // 2
    nf = jnp.maximum(n, 1).astype(F32)
    large = max_exact + (jnp.log(nf / max_exact) / math.log(MAX_DISTANCE / max_exact)
                         * (N_BUCKETS - max_exact)).astype(jnp.int32)
    large = jnp.minimum(large, N_BUCKETS - 1)
    return jnp.where(n < max_exact, n, large)


class _Dense(NamedTuple):
    sample: bool
    grid: tuple
    rows: int
    act: Callable[[int], Any]
    layer: Callable[..., Any]
    mod_spec: Any
    sem: tuple


def _dense(x, layer_idx, n_seq, sample):
    lead, mid = x.shape[0], x.shape[1]
    if sample:
        grid = (lead // SAMPLE_POS_TILE,)
        act = lambda w: pl.BlockSpec((SAMPLE_POS_TILE, mid, w), lambda i: (i, 0, 0))
        layer = lambda *shape: pl.BlockSpec((None,) + shape,
                                            lambda i: (layer_idx,) + (0,) * len(shape),
                                            pipeline_mode=pl.Buffered(1))
        mod_spec = pl.BlockSpec((None, N_MOD, n_seq, D_MODEL), lambda i: (layer_idx, 0, 0, 0))
        return _Dense(True, grid, SAMPLE_POS_TILE * mid, act, layer, mod_spec, ("arbitrary",))
    assert n_seq % V7X_SUBLANES == 0 and lead <= V7X_SUBLANES
    grid = (lead, mid // PROMPT_TILE)
    act = lambda w: pl.BlockSpec((None, PROMPT_TILE, w), lambda b, t: (b, t, 0))
    layer = lambda *shape: pl.BlockSpec((None,) + shape,
                                        lambda b, t: (layer_idx,) + (0,) * len(shape),
                                        pipeline_mode=pl.Buffered(1))
    mod_spec = pl.BlockSpec((None, N_MOD, V7X_SUBLANES, D_MODEL),
                            lambda b, t: (layer_idx, 0, n_seq // V7X_SUBLANES, 0))
    return _Dense(False, grid, PROMPT_TILE, act, layer, mod_spec, ("arbitrary", "arbitrary"))


def _mod(mod_ref, j, sample):
    if sample:
        return mod_ref[j]
    return mod_ref[j, pl.ds(pl.program_id(0), 1), :]


def _in_proj_kernel(*refs, sample):
    if sample:
        (x_ref, mod_ref, g_ref, w_ref, cw_ref, pre_ref,
         z_ref, q_ref, kv_ref, gate_ref, tail_ref, carry_ref) = refs
    else:
        (x_ref, mod_ref, g_ref, w_ref, cw_ref,
         z_ref, q_ref, kv_ref, gate_ref, tail_ref, carry_ref) = refs
    step = pl.program_id(0) if sample else pl.program_id(1)

    @pl.when(step == 0)
    def _():
        if sample:
            carry_ref[...] = pre_ref[...]
        else:
            carry_ref[...] = jnp.zeros(carry_ref.shape, F32)

    x = x_ref[...]
    h = (_rms(x) * g_ref[...]) * (1.0 + _mod(mod_ref, 1, sample)) + _mod(mod_ref, 0, sample)
    hb = _flat(h).astype(BF16)
    rows = hb.shape[0]

    def store(ref, lo, val):
        ref[..., lo:lo + val.shape[-1]] = val.reshape(ref.shape[:-1] + (val.shape[-1],))

    for lo in range(0, CONV_DIM, CHUNK):
        b_g = _dot(hb, w_ref[:, lo:lo + CHUNK])
        c_g = _dot(hb, w_ref[:, CONV_DIM + lo:CONV_DIM + lo + CHUNK])
        x_c = _dot(hb, w_ref[:, 2 * CONV_DIM + lo:2 * CONV_DIM + lo + CHUNK])
        u = c_g * x_c
        if sample:
            pos = x_ref.shape[0]
            u3 = u.reshape(pos, -1, CHUNK)
            ext = jnp.concatenate([carry_ref[:, :, lo:lo + CHUNK], u3], axis=0)
            u2 = _flat(ext[0:pos])
            u1 = _flat(ext[1:pos + 1])
            new_tail = u3[pos - (CONV_WIDTH - 1):]
            carry_ref[:, :, lo:lo + CHUNK] = new_tail
            tail_ref[:, :, lo:lo + CHUNK] = new_tail
        else:
            prev = carry_ref[:, lo:lo + CHUNK]
            p1 = prev[V7X_SUBLANES - 1:V7X_SUBLANES]
            p2 = prev[V7X_SUBLANES - 2:V7X_SUBLANES - 1]
            r = lax.broadcasted_iota(jnp.int32, u.shape, 0)
            u1 = jnp.where(r == 0, p1, pltpu.roll(u, 1, 0))
            u2 = jnp.where(r == 0, p2, jnp.where(r == 1, p1, pltpu.roll(u, 2, 0)))
            new_tail = u[rows - V7X_SUBLANES:]
            carry_ref[:, lo:lo + CHUNK] = new_tail
            tail_ref[:, lo:lo + CHUNK] = new_tail
        conv = (cw_ref[0:1, lo:lo + CHUNK] * u2 + cw_ref[1:2, lo:lo + CHUNK] * u1
                + cw_ref[2:3, lo:lo + CHUNK] * u)
        store(z_ref, lo, (b_g * conv).astype(BF16))

    for lo in range(0, ATTN_WIDTH, CHUNK):
        q = _dot(hb, w_ref[:, Q_OFF + lo:Q_OFF + lo + CHUNK])
        store(q_ref, lo, (q * (HEAD_DIM ** -0.5)).astype(BF16))

    store(kv_ref, 0, _dot(hb, w_ref[:, KV_OFF:KV_OFF + 2 * KV_WIDTH]))

    for lo in range(0, 2 * D_MODEL, CHUNK):
        g = _dot(hb, w_ref[:, GATE_OFF + lo:GATE_OFF + lo + CHUNK])
        store(gate_ref, lo, 1.0 / (1.0 + jnp.exp(-g)))


def _in_proj(x, mod_all, g_pre1, w_in, conv_w, layer_idx, n_seq, pre=None):
    cfg = _dense(x, layer_idx, n_seq, pre is not None)
    lead, mid = x.shape[0], x.shape[1]
    if cfg.sample:
        tail_shape = (CONV_WIDTH - 1, mid, CONV_DIM)
        tail_spec = pl.BlockSpec(tail_shape, lambda i: (0, 0, 0))
        extra_in = [pre]
        extra_specs = [cfg.layer(*tail_shape)]
        carry = pltpu.VMEM(tail_shape, F32)
    else:
        tail_shape = (lead, V7X_SUBLANES, CONV_DIM)
        tail_spec = pl.BlockSpec((None, V7X_SUBLANES, CONV_DIM), lambda b, t: (b, 0, 0))
        extra_in, extra_specs = [], []
        carry = pltpu.VMEM((V7X_SUBLANES, CONV_DIM), F32)

    widths = (CONV_DIM, ATTN_WIDTH, 2 * KV_WIDTH, 2 * D_MODEL)
    dtypes = (BF16, BF16, F32, F32)
    out_shape = [jax.ShapeDtypeStruct(x.shape[:-1] + (w,), dt) for w, dt in zip(widths, dtypes)]
    out_shape.append(jax.ShapeDtypeStruct(tail_shape, F32))
    blocks = (_nbytes((cfg.rows, D_MODEL), F32) + _nbytes(w_in.shape[1:], BF16)
              + sum(_nbytes((cfg.rows, w), dt) for w, dt in zip(widths, dtypes)))
    temps = _nbytes((cfg.rows, D_MODEL), F32) * 2 + _nbytes((cfg.rows, CHUNK), F32) * 10
    return pl.pallas_call(
        functools.partial(_in_proj_kernel, sample=cfg.sample),
        grid=cfg.grid,
        in_specs=[cfg.act(D_MODEL), cfg.mod_spec, cfg.layer(1, D_MODEL),
                  cfg.layer(*w_in.shape[1:]), cfg.layer(*conv_w.shape[1:])] + extra_specs,
        out_specs=[cfg.act(w) for w in widths] + [tail_spec],
        out_shape=out_shape,
        scratch_shapes=[carry],
        compiler_params=pltpu.CompilerParams(
            dimension_semantics=cfg.sem, vmem_limit_bytes=_vmem_limit(blocks, temps)),
        name="in_proj_sample" if cfg.sample else "in_proj_prompt",
    )(x, mod_all, g_pre1, w_in, conv_w, *extra_in)


def _sink_softmax(s, sink):
    m = jnp.maximum(jnp.max(s, axis=-1, keepdims=True), sink)
    e = jnp.exp(s - m)
    den = jnp.sum(e, axis=-1, keepdims=True) + jnp.exp(sink - m)
    return e * (1.0 / den)


def _prompt_attn_kernel(sink_ref, q_ref, kvp_ref, kvc_ref, bias_ref, o_ref,
                        kd_ref, vt_ref, s_ref):
    first_tile = pl.program_id(1) == 0
    n_blk = q_ref.shape[0] // BLOCK
    kv = jnp.concatenate([kvp_ref[...], kvc_ref[...]], axis=0)
    for g in range(N_KV_HEADS):
        k_g = kv[:, g * HEAD_DIM:(g + 1) * HEAD_DIM]
        kd_ref[g] = jnp.concatenate([k_g, k_g], axis=1).astype(BF16)
    v_t = kv[:, KV_WIDTH:].T.astype(BF16)
    for blk in range(n_blk + 1):
        vt_ref[blk] = v_t[:, blk * BLOCK:(blk + 1) * BLOCK]
    kj = lax.broadcasted_iota(jnp.int32, (BLOCK, BLOCK), 0)
    qi = lax.broadcasted_iota(jnp.int32, (BLOCK, BLOCK), 1)
    upper = kj > qi
    low_lanes = lax.broadcasted_iota(jnp.int32, (BLOCK, 2 * HEAD_DIM), 1) < HEAD_DIM

    def score_phase(c):
        q = q_ref[c * BLOCK:(c + 1) * BLOCK, :]
        for pair in range(N_HEADS // 2):
            g = (2 * pair) // GROUP
            q_pair = q[:, pair * 2 * HEAD_DIM:(pair + 1) * 2 * HEAD_DIM]
            zero = jnp.zeros_like(q_pair)
            q_both = jnp.concatenate([jnp.where(low_lanes, q_pair, zero),
                                      jnp.where(low_lanes, zero, q_pair)], axis=0)
            s_both = lax.dot_general(kd_ref[g, c * BLOCK:(c + 2) * BLOCK, :], q_both,
                                     (((1,), (1,)), ((), ())), preferred_element_type=F32)
            for par in range(2):
                h = 2 * pair + par
                s_h = s_both[:, par * BLOCK:(par + 1) * BLOCK]
                s = jnp.where(upper, s_h[:BLOCK], s_h[BLOCK:]) + bias_ref[h]
                if c == 0:
                    s = jnp.where(upper & first_tile, NEG_INF, s)
                s_ref[c, h] = s

    def value_phase(c):
        v_cat = jnp.concatenate([vt_ref[c], vt_ref[c + 1]], axis=1)
        outs = []
        for h in range(N_HEADS):
            g = h // GROUP
            s = s_ref[c, h]
            sink = sink_ref[h]
            m = jnp.maximum(jnp.max(s, axis=0, keepdims=True), sink)
            e = jnp.exp(s - m)
            den = jnp.sum(e, axis=0, keepdims=True) + jnp.exp(sink - m)
            p_t = jnp.concatenate([jnp.where(upper, e, 0.0), jnp.where(upper, 0.0, e)],
                                  axis=0).astype(BF16)
            o_t = _dot(v_cat[g * HEAD_DIM:(g + 1) * HEAD_DIM], p_t)
            outs.append(o_t * (1.0 / den))
        o_ref[c * BLOCK:(c + 1) * BLOCK, :] = jnp.concatenate(outs, axis=0).T.astype(BF16)

    score_phase(0)
    for c in range(n_blk):
        if c + 1 < n_blk:
            score_phase(c + 1)
        value_phase(c)


def _prompt_attn(q, kv, bias, sinks, layer_idx):
    batch, seq, _ = q.shape
    tq = ATTN_TILE
    per_tile = tq // BLOCK
    blocks = (_nbytes((tq, ATTN_WIDTH), BF16) * 2 + _nbytes((tq + BLOCK, 2 * KV_WIDTH), F32)
              + _nbytes(bias.shape, F32))
    temps = (_nbytes((tq + BLOCK, 2 * KV_WIDTH), BF16) * 2
             + _nbytes((per_tile, N_HEADS, BLOCK, BLOCK), F32)
             + _nbytes((BLOCK, 2 * BLOCK), F32) * 16 + _nbytes((BLOCK, ATTN_WIDTH), F32) * 4)
    return pl.pallas_call(
        _prompt_attn_kernel,
        grid=(batch, seq // tq),
        in_specs=[
            pl.BlockSpec(memory_space=pltpu.SMEM),
            pl.BlockSpec((None, tq, ATTN_WIDTH), lambda b, t: (b, t, 0)),
            pl.BlockSpec((None, BLOCK, 2 * KV_WIDTH),
                         lambda b, t: (b, jnp.maximum(t * per_tile - 1, 0), 0)),
            pl.BlockSpec((None, tq, 2 * KV_WIDTH), lambda b, t: (b, t, 0)),
            pl.BlockSpec(bias.shape, lambda b, t: (0, 0, 0)),
        ],
        out_specs=pl.BlockSpec((None, tq, ATTN_WIDTH), lambda b, t: (b, t, 0)),
        out_shape=jax.ShapeDtypeStruct(q.shape, BF16),
        scratch_shapes=[pltpu.VMEM((N_KV_HEADS, tq + BLOCK, 2 * HEAD_DIM), BF16),
                        pltpu.VMEM((per_tile + 1, KV_WIDTH, BLOCK), BF16),
                        pltpu.VMEM((per_tile, N_HEADS, BLOCK, BLOCK), F32)],
        compiler_params=pltpu.CompilerParams(
            dimension_semantics=("arbitrary", "arbitrary"),
            vmem_limit_bytes=_vmem_limit(blocks, temps)),
        name="attn_prompt",
    )(sinks[layer_idx], q, kv, kv, bias)


def _sample_attn_kernel(q_ref, kc_ref, vc_ref, kn_ref, vn_ref, bias_ref, sink_ref, *rest):
    o_ref, ko_ref, vo_ref = rest[-3:]
    buf = kc_ref.shape[1]
    new = kn_ref.shape[1]
    kk = jnp.concatenate([kc_ref[...], kn_ref[...]], axis=1)
    vv = jnp.concatenate([vc_ref[...], vn_ref[...]], axis=1)
    ko_ref[...] = kk[:, new:]
    vo_ref[...] = vv[:, new:]
    rows = new * GROUP
    pos = lax.broadcasted_iota(jnp.int32, (rows, buf + new), 0) // GROUP
    kj = lax.broadcasted_iota(jnp.int32, (rows, buf + new), 1)
    dist = buf + pos - kj
    mask = ((dist >= 0) & (dist < WINDOW))[None]
    kb = kk.astype(BF16)
    vb = vv.astype(BF16)
    for g in range(N_KV_HEADS):
        k_g = kb[:, :, g * HEAD_DIM:(g + 1) * HEAD_DIM]
        v_g = vb[:, :, g * HEAD_DIM:(g + 1) * HEAD_DIM]
        s = jnp.einsum('nqd,nkd->nqk', q_ref[:, g], k_g, preferred_element_type=F32)
        s = jnp.where(mask, s + bias_ref[g][None], NEG_INF)
        p = _sink_softmax(s, sink_ref[g][None])
        o = jnp.einsum('nqk,nkd->nqd', p.astype(BF16), v_g, preferred_element_type=F32)
        o_ref[:, g] = o.astype(BF16)


def _sample_attn(q, k_cache, v_cache, k_new, v_new, bias, sink_rows, layer_idx, windows):
    _, n, buf, _ = k_cache.shape
    new = k_new.shape[1]
    rows = new * GROUP
    nt = SAMPLE_SEQ_TILE
    seq3 = lambda a, b: pl.BlockSpec((nt, a, b), lambda i: (i, 0, 0))
    cache = pl.BlockSpec((None, nt, buf, KV_WIDTH), lambda i: (layer_idx, i, 0, 0))
    q_spec = pl.BlockSpec((nt, N_KV_HEADS, rows, HEAD_DIM), lambda i: (i, 0, 0, 0))
    in_specs = [q_spec, cache, cache, seq3(new, KV_WIDTH), seq3(new, KV_WIDTH),
                pl.BlockSpec(bias.shape, lambda i: (0, 0, 0)),
                pl.BlockSpec(sink_rows.shape, lambda i: (0, 0, 0))]
    args = [q, k_cache, v_cache, k_new, v_new, bias, sink_rows]
    aliases = {}
    if windows is not None:
        aliases = {len(args): 1, len(args) + 1: 2}
        in_specs += [pl.BlockSpec(memory_space=pl.ANY)] * 2
        args += list(windows)
    blocks = (_nbytes((nt, buf, KV_WIDTH), F32) * 4 + _nbytes((nt, new, KV_WIDTH), F32) * 2
              + _nbytes((nt, N_KV_HEADS, rows, HEAD_DIM), BF16) * 2 + _nbytes(bias.shape, F32))
    temps = (_nbytes((nt, buf + new, KV_WIDTH), F32) * 4
             + _nbytes((nt, rows, 2 * BLOCK), F32) * 6)
    return pl.pallas_call(
        _sample_attn_kernel,
        grid=(n // nt,),
        in_specs=in_specs,
        out_specs=[q_spec, cache, cache],
        out_shape=[jax.ShapeDtypeStruct(q.shape, BF16),
                   jax.ShapeDtypeStruct(k_cache.shape, F32),
                   jax.ShapeDtypeStruct(v_cache.shape, F32)],
        input_output_aliases=aliases,
        compiler_params=pltpu.CompilerParams(
            dimension_semantics=("arbitrary",), vmem_limit_bytes=_vmem_limit(blocks, temps)),
        name="attn_sample",
    )(*args)


def _mix_kernel(x_ref, z_ref, a_ref, gate_ref, mod_ref, g_ref, wbc_ref, wba_ref, wo_ref, o_ref,
                *, sample):
    y_conv = _dot(_flat(z_ref[...]), wbc_ref[...])
    y_attn = _dot(_flat(a_ref[...]), wba_ref[...])
    gates = _flat(gate_ref[...])
    merged = gates[:, :D_MODEL] * y_conv + gates[:, D_MODEL:] * y_attn
    mixed = _dot(merged.astype(BF16), wo_ref[...])
    r = (_rms(mixed) * g_ref[...]).reshape(x_ref.shape)
    o_ref[...] = x_ref[...] + _mod(mod_ref, 2, sample) * r


def _mix(x, z, a, gates, mod_all, g_post1, w_bc, w_ba, w_o, layer_idx, n_seq, sample):
    cfg = _dense(x, layer_idx, n_seq, sample)
    sq = (D_MODEL, D_MODEL)
    blocks = (_nbytes((cfg.rows, D_MODEL), F32) * 2 + _nbytes((cfg.rows, D_MODEL), BF16) * 2
              + _nbytes((cfg.rows, 2 * D_MODEL), F32) + _nbytes(sq, BF16) * 3)
    temps = _nbytes((cfg.rows, D_MODEL), F32) * 8
    return pl.pallas_call(
        functools.partial(_mix_kernel, sample=cfg.sample),
        grid=cfg.grid,
        in_specs=[cfg.act(D_MODEL), cfg.act(CONV_DIM), cfg.act(ATTN_WIDTH),
                  cfg.act(2 * D_MODEL), cfg.mod_spec, cfg.layer(1, D_MODEL),
                  cfg.layer(*sq), cfg.layer(*sq), cfg.layer(*sq)],
        out_specs=cfg.act(D_MODEL),
        out_shape=jax.ShapeDtypeStruct(x.shape, F32),
        compiler_params=pltpu.CompilerParams(
            dimension_semantics=cfg.sem, vmem_limit_bytes=_vmem_limit(blocks, temps)),
        name="mix_sample" if cfg.sample else "mix_prompt",
    )(x, z, a, gates, mod_all, g_post1, w_bc, w_ba, w_o)


def _mlp_kernel(x_ref, mod_ref, gpre_ref, gpost_ref, w1_ref, w2_ref, o_ref, hid_ref, *, sample):
    x = x_ref[...]
    h = (_rms(x) * gpre_ref[...]) * (1.0 + _mod(mod_ref, 4, sample)) + _mod(mod_ref, 3, sample)
    hb = _flat(h).astype(BF16)
    for lo in range(0, D_FF, 2 * CHUNK):
        a = jnp.maximum(_dot(hb, w1_ref[:, lo:lo + 2 * CHUNK]), 0.0)
        hid_ref[:, lo:lo + 2 * CHUNK] = (a * a).astype(BF16)
    ff = _dot(hid_ref[...], w2_ref[...])
    r = (_rms(ff) * gpost_ref[...]).reshape(x.shape)
    o_ref[...] = x + _mod(mod_ref, 5, sample) * r


def _mlp(x, mod_all, g_pre2, g_post2, w1, w2, layer_idx, n_seq, sample):
    cfg = _dense(x, layer_idx, n_seq, sample)
    blocks = (_nbytes((cfg.rows, D_MODEL), F32) * 2 + _nbytes(w1.shape[1:], BF16) * 2)
    temps = (_nbytes((cfg.rows, D_FF), BF16) + _nbytes((cfg.rows, 2 * CHUNK), F32) * 3
             + _nbytes((cfg.rows, D_MODEL), F32) * 6)
    return pl.pallas_call(
        functools.partial(_mlp_kernel, sample=cfg.sample),
        grid=cfg.grid,
        in_specs=[cfg.act(D_MODEL), cfg.mod_spec, cfg.layer(1, D_MODEL), cfg.layer(1, D_MODEL),
                  cfg.layer(*w1.shape[1:]), cfg.layer(*w2.shape[1:])],
        out_specs=cfg.act(D_MODEL),
        out_shape=jax.ShapeDtypeStruct(x.shape, F32),
        scratch_shapes=[pltpu.VMEM((cfg.rows, D_FF), BF16)],
        compiler_params=pltpu.CompilerParams(
            dimension_semantics=cfg.sem, vmem_limit_bytes=_vmem_limit(blocks, temps)),
        name="mlp_sample" if cfg.sample else "mlp_prompt",
    )(x, mod_all, g_pre2, g_post2, w1, w2)


def kernel(x_prompt, x_sample, c_prompt, c_sample, state_conv, cache_k, cache_v, w_ada, b_ada,
           g_pre1, w_in, conv_w, w_br_conv, w_br_attn, w_o, sinks, g_post1, g_pre2, w_ff1, w_ff2,
           g_post2, rel_table):
    depth = w_ada.shape[0]
    batch, seq, _ = x_prompt.shape
    n_seq, n_new, _ = x_sample.shape
    buf = cache_k.shape[2]

    pad = (-(n_seq + batch)) % V7X_SUBLANES
    c_all = jnp.concatenate([c_sample, c_prompt, jnp.zeros((pad, D_MODEL), F32)], axis=0)
    mod_all = _ada(c_all, w_ada, b_ada)

    dist_p = (jnp.arange(BLOCK)[:, None] + BLOCK) - jnp.arange(2 * BLOCK)[None, :]
    bucket_p = _rel_bucket(dist_p)
    upper = jnp.arange(BLOCK)[None, :] > jnp.arange(BLOCK)[:, None]
    bucket_merged = jnp.where(upper, bucket_p[:, :BLOCK], bucket_p[:, BLOCK:])
    bias_p = _bias_table(rel_table, bucket_merged.T)
    dist_s = (buf + jnp.arange(n_new))[:, None] - jnp.arange(buf + n_new)[None, :]
    bias_s = _bias_table(rel_table, _rel_bucket(dist_s))
    bias_s = (bias_s.reshape(N_KV_HEADS, GROUP, n_new, buf + n_new)
              .transpose(0, 2, 1, 3).reshape(N_KV_HEADS, n_new * GROUP, buf + n_new))

    to_bf16 = lambda w: w.astype(BF16)
    w_in_b, w_bc_b, w_ba_b, w_o_b, w1_b, w2_b = map(
        to_bf16, (w_in, w_br_conv, w_br_attn, w_o, w_ff1, w_ff2))
    vec = lambda g: g.reshape(depth, 1, D_MODEL)
    g_pre1, g_post1, g_pre2, g_post2 = map(vec, (g_pre1, g_post1, g_pre2, g_post2))
    pre_s = state_conv.transpose(0, 2, 1, 3)
    cache_k = cache_k.reshape(depth, n_seq, buf, KV_WIDTH)
    cache_v = cache_v.reshape(depth, n_seq, buf, KV_WIDTH)

    xp = x_prompt
    xs = x_sample.transpose(1, 0, 2)
    conv_p, k_p, v_p, conv_s = [], [], [], []
    windows = None
    win = min(WINDOW, seq)
    for l in range(depth):
        z, q, kv, gates, tail = _in_proj(xp, mod_all, g_pre1, w_in_b, conv_w, l, n_seq)
        attn = _prompt_attn(q, kv, bias_p, sinks, l)
        x1 = _mix(xp, z, attn, gates, mod_all, g_post1, w_bc_b, w_ba_b, w_o_b, l, n_seq, False)
        xp = _mlp(x1, mod_all, g_pre2, g_post2, w1_b, w2_b, l, n_seq, False)
        conv_p.append(tail[:, V7X_SUBLANES - (CONV_WIDTH - 1):])
        k_p.append(kv[:, seq - win:, :KV_WIDTH].reshape(batch, win, N_KV_HEADS, HEAD_DIM))
        v_p.append(kv[:, seq - win:, KV_WIDTH:].reshape(batch, win, N_KV_HEADS, HEAD_DIM))

        z, q, kv, gates, tail = _in_proj(xs, mod_all, g_pre1, w_in_b, conv_w, l, n_seq, pre_s)
        q_s = (q.reshape(n_new, n_seq, N_KV_HEADS, GROUP, HEAD_DIM)
               .transpose(1, 2, 0, 3, 4).reshape(n_seq, N_KV_HEADS, n_new * GROUP, HEAD_DIM))
        kv_s = kv.transpose(1, 0, 2)
        sink_rows = jnp.tile(sinks[l].reshape(N_KV_HEADS, 1, GROUP),
                             (1, n_new, 1)).reshape(N_KV_HEADS, n_new * GROUP, 1)
        o, *windows = _sample_attn(q_s, cache_k, cache_v, kv_s[:, :, :KV_WIDTH],
                                   kv_s[:, :, KV_WIDTH:], bias_s, sink_rows, l, windows)
        attn = (o.reshape(n_seq, N_KV_HEADS, n_new, GROUP, HEAD_DIM)
                .transpose(2, 0, 1, 3, 4).reshape(n_new, n_seq, ATTN_WIDTH))
        x1 = _mix(xs, z, attn, gates, mod_all, g_post1, w_bc_b, w_ba_b, w_o_b, l, n_seq, True)
        xs = _mlp(x1, mod_all, g_pre2, g_post2, w1_b, w2_b, l, n_seq, True)
        conv_s.append(tail.transpose(1, 0, 2))

    k_s, v_s = (w.reshape(depth, n_seq, buf, N_KV_HEADS, HEAD_DIM) for w in windows)
    return (xp, xs.transpose(1, 0, 2), jnp.stack(conv_p), jnp.stack(k_p), jnp.stack(v_p),
            jnp.stack(conv_s), k_s, v_s)
```

```python
import functools
import math
from typing import Any, Callable, NamedTuple

import jax
import jax.numpy as jnp
from jax import lax
from jax.experimental import pallas as pl
from jax.experimental.pallas import tpu as pltpu

D_MODEL = 1024
N_HEADS = 16
N_KV_HEADS = 2
HEAD_DIM = 64
GROUP = N_HEADS // N_KV_HEADS
ATTN_WIDTH = N_HEADS * HEAD_DIM
KV_WIDTH = N_KV_HEADS * HEAD_DIM
CONV_DIM = D_MODEL
CONV_WIDTH = 3
WINDOW = 128
BLOCK = 128
N_BUCKETS = 32
MAX_DISTANCE = 128
D_FF = 4 * D_MODEL
N_MOD = 6
RMS_EPS = 1e-6
NEG_INF = -1e30
PROJ_COLS = 3 * CONV_DIM + ATTN_WIDTH + 2 * KV_WIDTH + 2 * D_MODEL
Q_OFF = 3 * CONV_DIM
KV_OFF = Q_OFF + ATTN_WIDTH
GATE_OFF = KV_OFF + 2 * KV_WIDTH

V7X_SUBLANES = 8
V7X_VMEM_BYTES = 64 * 1024 * 1024

PROMPT_TILE = 512
SAMPLE_POS_TILE = 4
SAMPLE_SEQ_TILE = 16
CHUNK = 512

F32 = jnp.float32
BF16 = jnp.bfloat16


def _vmem_limit(block_bytes, temp_bytes):
    return int(min(2 * block_bytes + temp_bytes, V7X_VMEM_BYTES - 4 * 1024 * 1024))


def _nbytes(shape, dtype):
    return math.prod(shape) * jnp.dtype(dtype).itemsize


def _rms(x):
    return x * lax.rsqrt(jnp.mean(x * x, axis=-1, keepdims=True) + RMS_EPS)


def _dot(a, b):
    return jnp.dot(a, b, preferred_element_type=F32)


def _flat(a):
    return a.reshape(-1, a.shape[-1])


def _ada_kernel(c_ref, w_ref, b_ref, o_ref):
    c = c_ref[...]
    s = c * (1.0 / (1.0 + jnp.exp(-c)))
    o_ref[...] = _dot(s.astype(BF16), w_ref[...].astype(BF16)) + b_ref[...]


def _ada(c_all, w_ada, b_ada):
    depth = w_ada.shape[0]
    rows = c_all.shape[0]
    blocks = (_nbytes((rows, D_MODEL), F32) * 2 + _nbytes((D_MODEL, D_MODEL), F32))
    return pl.pallas_call(
        _ada_kernel,
        grid=(depth, N_MOD),
        in_specs=[
            pl.BlockSpec((rows, D_MODEL), lambda l, j: (0, 0)),
            pl.BlockSpec((None, D_MODEL, D_MODEL), lambda l, j: (l, 0, j)),
            pl.BlockSpec((None, None, 1, D_MODEL), lambda l, j: (l, j, 0, 0)),
        ],
        out_specs=pl.BlockSpec((None, None, rows, D_MODEL), lambda l, j: (l, j, 0, 0)),
        out_shape=jax.ShapeDtypeStruct((depth, N_MOD, rows, D_MODEL), F32),
        compiler_params=pltpu.CompilerParams(
            dimension_semantics=("arbitrary", "arbitrary"),
            vmem_limit_bytes=_vmem_limit(blocks, _nbytes((D_MODEL, D_MODEL), BF16) * 2)),
        name="ada_mod",
    )(c_all, w_ada, b_ada.reshape(depth, N_MOD, 1, D_MODEL))


def _bias_kernel(tab_ref, bucket_ref, o_ref):
    bucket = bucket_ref[...]
    for h in range(N_HEADS):
        acc = jnp.zeros(bucket.shape, F32)
        for b in range(N_BUCKETS):
            acc = jnp.where(bucket == b, tab_ref[b * N_HEADS + h], acc)
        o_ref[h] = acc


def _bias_table(rel_table, bucket):
    return pl.pallas_call(
        _bias_kernel,
        in_specs=[
            pl.BlockSpec(memory_space=pltpu.SMEM),
            pl.BlockSpec(bucket.shape, lambda: (0, 0)),
        ],
        out_specs=pl.BlockSpec((N_HEADS,) + bucket.shape, lambda: (0, 0, 0)),
        out_shape=jax.ShapeDtypeStruct((N_HEADS,) + bucket.shape, F32),
        name="bias_table",
    )(rel_table.reshape(-1), bucket)


def _rel_bucket(dist):
    n = jnp.maximum(dist, 0)
    max_exact = N_BUCKETS // 2
    nf = jnp.maximum(n, 1).astype(F32)
    large = max_exact + (jnp.log(nf / max_exact) / math.log(MAX_DISTANCE / max_exact)
                         * (N_BUCKETS - max_exact)).astype(jnp.int32)
    large = jnp.minimum(large, N_BUCKETS - 1)
    return jnp.where(n < max_exact, n, large)


class _Dense(NamedTuple):
    sample: bool
    grid: tuple
    rows: int
    act: Callable[[int], Any]
    layer: Callable[..., Any]
    mod_spec: Any
    sem: tuple


def _dense(x, layer_idx, n_seq, sample):
    lead, mid = x.shape[0], x.shape[1]
    if sample:
        grid = (lead // SAMPLE_POS_TILE,)
        act = lambda w: pl.BlockSpec((SAMPLE_POS_TILE, mid, w), lambda i: (i, 0, 0))
        layer = lambda *shape: pl.BlockSpec((None,) + shape,
                                            lambda i: (layer_idx,) + (0,) * len(shape),
                                            pipeline_mode=pl.Buffered(1))
        mod_spec = pl.BlockSpec((None, N_MOD, n_seq, D_MODEL), lambda i: (layer_idx, 0, 0, 0))
        return _Dense(True, grid, SAMPLE_POS_TILE * mid, act, layer, mod_spec, ("arbitrary",))
    assert n_seq % V7X_SUBLANES == 0 and lead <= V7X_SUBLANES
    grid = (lead, mid // PROMPT_TILE)
    act = lambda w: pl.BlockSpec((None, PROMPT_TILE, w), lambda b, t: (b, t, 0))
    layer = lambda *shape: pl.BlockSpec((None,) + shape,
                                        lambda b, t: (layer_idx,) + (0,) * len(shape),
                                        pipeline_mode=pl.Buffered(1))
    mod_spec = pl.BlockSpec((None, N_MOD, V7X_SUBLANES, D_MODEL),
                            lambda b, t: (layer_idx, 0, n_seq // V7X_SUBLANES, 0))
    return _Dense(False, grid, PROMPT_TILE, act, layer, mod_spec, ("arbitrary", "arbitrary"))


def _mod(mod_ref, j, sample):
    if sample:
        return mod_ref[j]
    return mod_ref[j, pl.ds(pl.program_id(0), 1), :]


def _sink_softmax(s, sink):
    m = jnp.maximum(jnp.max(s, axis=-1, keepdims=True), sink)
    e = jnp.exp(s - m)
    den = jnp.sum(e, axis=-1, keepdims=True) + jnp.exp(sink - m)
    return e * (1.0 / den)


def _stage_keys(kv, kd_ref, vt_ref, first_tile):
    n_blk = kv.shape[0] // BLOCK

    @pl.when(first_tile)
    def _():
        kd_ref[:, 0:BLOCK] = jnp.zeros((N_KV_HEADS, BLOCK, 2 * HEAD_DIM), BF16)
        vt_ref[0] = jnp.zeros((KV_WIDTH, BLOCK), BF16)

    @pl.when(jnp.logical_not(first_tile))
    def _():
        kd_ref[:, 0:BLOCK] = kd_ref[:, n_blk * BLOCK:]
        vt_ref[0] = vt_ref[n_blk]

    for g in range(N_KV_HEADS):
        k_g = kv[:, g * HEAD_DIM:(g + 1) * HEAD_DIM]
        kd_ref[g, BLOCK:] = jnp.concatenate([k_g, k_g], axis=1).astype(BF16)
    v_t = kv[:, KV_WIDTH:].T.astype(BF16)
    for blk in range(n_blk):
        vt_ref[blk + 1] = v_t[:, blk * BLOCK:(blk + 1) * BLOCK]


def _attention_phases(q_ref, kd_ref, vt_ref, s_ref, bias_ref, sink_ref, o_ref, first_tile):
    kj = lax.broadcasted_iota(jnp.int32, (BLOCK, BLOCK), 0)
    qi = lax.broadcasted_iota(jnp.int32, (BLOCK, BLOCK), 1)
    upper = kj > qi
    low_lanes = lax.broadcasted_iota(jnp.int32, (BLOCK, 2 * HEAD_DIM), 1) < HEAD_DIM

    def score_phase(c):
        q = q_ref[c * BLOCK:(c + 1) * BLOCK, :]
        for pair in range(N_HEADS // 2):
            g = (2 * pair) // GROUP
            q_pair = q[:, pair * 2 * HEAD_DIM:(pair + 1) * 2 * HEAD_DIM]
            zero = jnp.zeros_like(q_pair)
            q_both = jnp.concatenate([jnp.where(low_lanes, q_pair, zero),
                                      jnp.where(low_lanes, zero, q_pair)], axis=0)
            s_both = lax.dot_general(kd_ref[g, c * BLOCK:(c + 2) * BLOCK, :], q_both,
                                     (((1,), (1,)), ((), ())), preferred_element_type=F32)
            for par in range(2):
                h = 2 * pair + par
                s_h = s_both[:, par * BLOCK:(par + 1) * BLOCK]
                s = jnp.where(upper, s_h[:BLOCK], s_h[BLOCK:]) + bias_ref[h]
                if c == 0:
                    s = jnp.where(upper & first_tile, NEG_INF, s)
                s_ref[c, h] = s

    def value_phase(c, pair):
        g = (2 * pair) // GROUP
        v_cat = jnp.concatenate([vt_ref[c, g * HEAD_DIM:(g + 1) * HEAD_DIM],
                                 vt_ref[c + 1, g * HEAD_DIM:(g + 1) * HEAD_DIM]], axis=1)
        weights, scales = [], []
        for h in (2 * pair, 2 * pair + 1):
            s = s_ref[c, h]
            sink = sink_ref[h]
            m = jnp.maximum(jnp.max(s, axis=0, keepdims=True), sink)
            e = jnp.exp(s - m)
            den = jnp.sum(e, axis=0, keepdims=True) + jnp.exp(sink - m)
            weights.append(jnp.concatenate([jnp.where(upper, e, 0.0), jnp.where(upper, 0.0, e)],
                                           axis=0).astype(BF16))
            scales.append(1.0 / den)
        o_t = _dot(v_cat, jnp.concatenate(weights, axis=1))
        o_t = jnp.concatenate([o_t[:, :BLOCK] * scales[0], o_t[:, BLOCK:] * scales[1]], axis=0)
        o_ref[c * BLOCK:(c + 1) * BLOCK, pair * 2 * HEAD_DIM:(pair + 1) * 2 * HEAD_DIM] = (
            o_t.T.astype(BF16))

    return score_phase, value_phase


def _in_proj_kernel(*refs, sample):
    if sample:
        (x_ref, mod_ref, g_ref, w_ref, cw_ref, pre_ref,
         z_ref, q_ref, kv_ref, gate_ref, tail_ref, carry_ref) = refs
    else:
        (x_ref, mod_ref, g_ref, w_ref, cw_ref, sink_ref, bias_ref,
         z_ref, a_ref, kv_ref, gate_ref, tail_ref,
         carry_ref, q_ref, kd_ref, vt_ref, s_ref) = refs
    step = pl.program_id(0) if sample else pl.program_id(1)

    @pl.when(step == 0)
    def _():
        if sample:
            carry_ref[...] = pre_ref[...]
        else:
            carry_ref[...] = jnp.zeros(carry_ref.shape, F32)

    x = x_ref[...]
    h = (_rms(x) * g_ref[...]) * (1.0 + _mod(mod_ref, 1, sample)) + _mod(mod_ref, 0, sample)
    hb = _flat(h).astype(BF16)
    rows = hb.shape[0]

    def store(ref, lo, val):
        ref[..., lo:lo + val.shape[-1]] = val.reshape(ref.shape[:-1] + (val.shape[-1],))

    def conv_chunk(lo):
        b_g = _dot(hb, w_ref[:, lo:lo + CHUNK])
        c_g = _dot(hb, w_ref[:, CONV_DIM + lo:CONV_DIM + lo + CHUNK])
        x_c = _dot(hb, w_ref[:, 2 * CONV_DIM + lo:2 * CONV_DIM + lo + CHUNK])
        u = c_g * x_c
        if sample:
            pos = x_ref.shape[0]
            u3 = u.reshape(pos, -1, CHUNK)
            ext = jnp.concatenate([carry_ref[:, :, lo:lo + CHUNK], u3], axis=0)
            u2 = _flat(ext[0:pos])
            u1 = _flat(ext[1:pos + 1])
            new_tail = u3[pos - (CONV_WIDTH - 1):]
            carry_ref[:, :, lo:lo + CHUNK] = new_tail
            tail_ref[:, :, lo:lo + CHUNK] = new_tail
        else:
            prev = carry_ref[:, lo:lo + CHUNK]
            p1 = prev[V7X_SUBLANES - 1:V7X_SUBLANES]
            p2 = prev[V7X_SUBLANES - 2:V7X_SUBLANES - 1]
            r = lax.broadcasted_iota(jnp.int32, u.shape, 0)
            u1 = jnp.where(r == 0, p1, pltpu.roll(u, 1, 0))
            u2 = jnp.where(r == 0, p2, jnp.where(r == 1, p1, pltpu.roll(u, 2, 0)))
            new_tail = u[rows - V7X_SUBLANES:]
            carry_ref[:, lo:lo + CHUNK] = new_tail
            tail_ref[:, lo:lo + CHUNK] = new_tail
        conv = (cw_ref[0:1, lo:lo + CHUNK] * u2 + cw_ref[1:2, lo:lo + CHUNK] * u1
                + cw_ref[2:3, lo:lo + CHUNK] * u)
        store(z_ref, lo, (b_g * conv).astype(BF16))

    def q_chunk(lo):
        q = _dot(hb, w_ref[:, Q_OFF + lo:Q_OFF + lo + CHUNK])
        store(q_ref, lo, (q * (HEAD_DIM ** -0.5)).astype(BF16))

    def gate_chunk(lo):
        g = _dot(hb, w_ref[:, GATE_OFF + lo:GATE_OFF + lo + CHUNK])
        store(gate_ref, lo, 1.0 / (1.0 + jnp.exp(-g)))

    kv = _dot(hb, w_ref[:, KV_OFF:KV_OFF + 2 * KV_WIDTH])
    store(kv_ref, 0, kv)
    for lo in range(0, ATTN_WIDTH, CHUNK):
        q_chunk(lo)
    dense = ([functools.partial(conv_chunk, lo) for lo in range(0, CONV_DIM, CHUNK)]
             + [functools.partial(gate_chunk, lo) for lo in range(0, 2 * D_MODEL, CHUNK)])
    if sample:
        for work in dense:
            work()
        return

    first_tile = step == 0
    _stage_keys(kv, kd_ref, vt_ref, first_tile)
    score_phase, value_phase = _attention_phases(q_ref, kd_ref, vt_ref, s_ref, bias_ref,
                                                 sink_ref, a_ref, first_tile)
    n_blk = rows // BLOCK
    attention = [functools.partial(score_phase, 0)]
    for c in range(n_blk):
        if c + 1 < n_blk:
            attention.append(functools.partial(score_phase, c + 1))
        attention += [functools.partial(value_phase, c, pair) for pair in range(N_HEADS // 2)]
    done = 0
    for i, work in enumerate(attention):
        work()
        while done < len(dense) and done * len(attention) < (i + 1) * len(dense):
            dense[done]()
            done += 1


def _in_proj(x, mod_all, g_pre1, w_in, conv_w, layer_idx, n_seq, pre=None, attn=None):
    cfg = _dense(x, layer_idx, n_seq, pre is not None)
    lead, mid = x.shape[0], x.shape[1]
    widths = (CONV_DIM, ATTN_WIDTH, 2 * KV_WIDTH, 2 * D_MODEL)
    dtypes = (BF16, BF16, F32, F32)
    blocks = (_nbytes((cfg.rows, D_MODEL), F32) + _nbytes(w_in.shape[1:], BF16)
              + sum(_nbytes((cfg.rows, w), dt) for w, dt in zip(widths, dtypes)))
    temps = _nbytes((cfg.rows, D_MODEL), F32) * 2 + _nbytes((cfg.rows, CHUNK), F32) * 10
    if cfg.sample:
        tail_shape = (CONV_WIDTH - 1, mid, CONV_DIM)
        tail_spec = pl.BlockSpec(tail_shape, lambda i: (0, 0, 0))
        extra_in = [pre]
        extra_specs = [cfg.layer(*tail_shape)]
        scratch = [pltpu.VMEM(tail_shape, F32)]
    else:
        sinks, bias = attn
        per_tile = cfg.rows // BLOCK
        tail_shape = (lead, V7X_SUBLANES, CONV_DIM)
        tail_spec = pl.BlockSpec((None, V7X_SUBLANES, CONV_DIM), lambda b, t: (b, 0, 0))
        extra_in = [sinks[layer_idx], bias]
        extra_specs = [pl.BlockSpec(memory_space=pltpu.SMEM),
                       pl.BlockSpec(bias.shape, lambda b, t: (0, 0, 0),
                                    pipeline_mode=pl.Buffered(1))]
        scratch = [pltpu.VMEM((V7X_SUBLANES, CONV_DIM), F32),
                   pltpu.VMEM((cfg.rows, ATTN_WIDTH), BF16),
                   pltpu.VMEM((N_KV_HEADS, cfg.rows + BLOCK, 2 * HEAD_DIM), BF16),
                   pltpu.VMEM((per_tile + 1, KV_WIDTH, BLOCK), BF16),
                   pltpu.VMEM((per_tile, N_HEADS, BLOCK, BLOCK), F32)]
        temps += (_nbytes(bias.shape, F32) + _nbytes((cfg.rows, ATTN_WIDTH), BF16)
                  + _nbytes((per_tile, N_HEADS, BLOCK, BLOCK), F32)
                  + _nbytes((BLOCK, 2 * BLOCK), F32) * 16 + _nbytes((BLOCK, ATTN_WIDTH), F32) * 4)

    out_shape = [jax.ShapeDtypeStruct(x.shape[:-1] + (w,), dt) for w, dt in zip(widths, dtypes)]
    out_shape.append(jax.ShapeDtypeStruct(tail_shape, F32))
    return pl.pallas_call(
        functools.partial(_in_proj_kernel, sample=cfg.sample),
        grid=cfg.grid,
        in_specs=[cfg.act(D_MODEL), cfg.mod_spec, cfg.layer(1, D_MODEL),
                  cfg.layer(*w_in.shape[1:]), cfg.layer(*conv_w.shape[1:])] + extra_specs,
        out_specs=[cfg.act(w) for w in widths] + [tail_spec],
        out_shape=out_shape,
        scratch_shapes=scratch,
        compiler_params=pltpu.CompilerParams(
            dimension_semantics=cfg.sem, vmem_limit_bytes=_vmem_limit(blocks, temps)),
        name="in_proj_sample" if cfg.sample else "in_proj_attn_prompt",
    )(x, mod_all, g_pre1, w_in, conv_w, *extra_in)


def _sample_attn_kernel(q_ref, kc_ref, vc_ref, kn_ref, vn_ref, bias_ref, sink_ref, *rest):
    o_ref, ko_ref, vo_ref = rest[-3:]
    buf = kc_ref.shape[1]
    new = kn_ref.shape[1]
    kk = jnp.concatenate([kc_ref[...], kn_ref[...]], axis=1)
    vv = jnp.concatenate([vc_ref[...], vn_ref[...]], axis=1)
    ko_ref[...] = kk[:, new:]
    vo_ref[...] = vv[:, new:]
    rows = new * GROUP
    pos = lax.broadcasted_iota(jnp.int32, (rows, buf + new), 0) // GROUP
    kj = lax.broadcasted_iota(jnp.int32, (rows, buf + new), 1)
    dist = buf + pos - kj
    mask = ((dist >= 0) & (dist < WINDOW))[None]
    kb = kk.astype(BF16)
    vb = vv.astype(BF16)
    for g in range(N_KV_HEADS):
        k_g = kb[:, :, g * HEAD_DIM:(g + 1) * HEAD_DIM]
        v_g = vb[:, :, g * HEAD_DIM:(g + 1) * HEAD_DIM]
        s = jnp.einsum('nqd,nkd->nqk', q_ref[:, g], k_g, preferred_element_type=F32)
        s = jnp.where(mask, s + bias_ref[g][None], NEG_INF)
        p = _sink_softmax(s, sink_ref[g][None])
        o = jnp.einsum('nqk,nkd->nqd', p.astype(BF16), v_g, preferred_element_type=F32)
        o_ref[:, g] = o.astype(BF16)


def _sample_attn(q, k_cache, v_cache, k_new, v_new, bias, sink_rows, layer_idx, windows):
    _, n, buf, _ = k_cache.shape
    new = k_new.shape[1]
    rows = new * GROUP
    nt = SAMPLE_SEQ_TILE
    seq3 = lambda a, b: pl.BlockSpec((nt, a, b), lambda i: (i, 0, 0))
    cache = pl.BlockSpec((None, nt, buf, KV_WIDTH), lambda i: (layer_idx, i, 0, 0))
    q_spec = pl.BlockSpec((nt, N_KV_HEADS, rows, HEAD_DIM), lambda i: (i, 0, 0, 0))
    in_specs = [q_spec, cache, cache, seq3(new, KV_WIDTH), seq3(new, KV_WIDTH),
                pl.BlockSpec(bias.shape, lambda i: (0, 0, 0)),
                pl.BlockSpec(sink_rows.shape, lambda i: (0, 0, 0))]
    args = [q, k_cache, v_cache, k_new, v_new, bias, sink_rows]
    aliases = {}
    if windows is not None:
        aliases = {len(args): 1, len(args) + 1: 2}
        in_specs += [pl.BlockSpec(memory_space=pl.ANY)] * 2
        args += list(windows)
    blocks = (_nbytes((nt, buf, KV_WIDTH), F32) * 4 + _nbytes((nt, new, KV_WIDTH), F32) * 2
              + _nbytes((nt, N_KV_HEADS, rows, HEAD_DIM), BF16) * 2 + _nbytes(bias.shape, F32))
    temps = (_nbytes((nt, buf + new, KV_WIDTH), F32) * 4
             + _nbytes((nt, rows, 2 * BLOCK), F32) * 6)
    return pl.pallas_call(
        _sample_attn_kernel,
        grid=(n // nt,),
        in_specs=in_specs,
        out_specs=[q_spec, cache, cache],
        out_shape=[jax.ShapeDtypeStruct(q.shape, BF16),
                   jax.ShapeDtypeStruct(k_cache.shape, F32),
                   jax.ShapeDtypeStruct(v_cache.shape, F32)],
        input_output_aliases=aliases,
        compiler_params=pltpu.CompilerParams(
            dimension_semantics=("arbitrary",), vmem_limit_bytes=_vmem_limit(blocks, temps)),
        name="attn_sample",
    )(*args)


def _mix_kernel(x_ref, z_ref, a_ref, gate_ref, mod_ref, g_ref, wbc_ref, wba_ref, wo_ref, o_ref,
                *, sample):
    y_conv = _dot(_flat(z_ref[...]), wbc_ref[...])
    y_attn = _dot(_flat(a_ref[...]), wba_ref[...])
    gates = _flat(gate_ref[...])
    merged = gates[:, :D_MODEL] * y_conv + gates[:, D_MODEL:] * y_attn
    mixed = _dot(merged.astype(BF16), wo_ref[...])
    r = (_rms(mixed) * g_ref[...]).reshape(x_ref.shape)
    o_ref[...] = x_ref[...] + _mod(mod_ref, 2, sample) * r


def _mix(x, z, a, gates, mod_all, g_post1, w_bc, w_ba, w_o, layer_idx, n_seq, sample):
    cfg = _dense(x, layer_idx, n_seq, sample)
    sq = (D_MODEL, D_MODEL)
    blocks = (_nbytes((cfg.rows, D_MODEL), F32) * 2 + _nbytes((cfg.rows, D_MODEL), BF16) * 2
              + _nbytes((cfg.rows, 2 * D_MODEL), F32) + _nbytes(sq, BF16) * 3)
    temps = _nbytes((cfg.rows, D_MODEL), F32) * 8
    return pl.pallas_call(
        functools.partial(_mix_kernel, sample=cfg.sample),
        grid=cfg.grid,
        in_specs=[cfg.act(D_MODEL), cfg.act(CONV_DIM), cfg.act(ATTN_WIDTH),
                  cfg.act(2 * D_MODEL), cfg.mod_spec, cfg.layer(1, D_MODEL),
                  cfg.layer(*sq), cfg.layer(*sq), cfg.layer(*sq)],
        out_specs=cfg.act(D_MODEL),
        out_shape=jax.ShapeDtypeStruct(x.shape, F32),
        compiler_params=pltpu.CompilerParams(
            dimension_semantics=cfg.sem, vmem_limit_bytes=_vmem_limit(blocks, temps)),
        name="mix_sample" if cfg.sample else "mix_prompt",
    )(x, z, a, gates, mod_all, g_post1, w_bc, w_ba, w_o)


def _mlp_kernel(x_ref, mod_ref, gpre_ref, gpost_ref, w1_ref, w2_ref, o_ref, hid_ref, *, sample):
    x = x_ref[...]
    h = (_rms(x) * gpre_ref[...]) * (1.0 + _mod(mod_ref, 4, sample)) + _mod(mod_ref, 3, sample)
    hb = _flat(h).astype(BF16)
    for lo in range(0, D_FF, 2 * CHUNK):
        a = jnp.maximum(_dot(hb, w1_ref[:, lo:lo + 2 * CHUNK]), 0.0)
        hid_ref[:, lo:lo + 2 * CHUNK] = (a * a).astype(BF16)
    ff = _dot(hid_ref[...], w2_ref[...])
    r = (_rms(ff) * gpost_ref[...]).reshape(x.shape)
    o_ref[...] = x + _mod(mod_ref, 5, sample) * r


def _mlp(x, mod_all, g_pre2, g_post2, w1, w2, layer_idx, n_seq, sample):
    cfg = _dense(x, layer_idx, n_seq, sample)
    blocks = (_nbytes((cfg.rows, D_MODEL), F32) * 2 + _nbytes(w1.shape[1:], BF16) * 2)
    temps = (_nbytes((cfg.rows, D_FF), BF16) + _nbytes((cfg.rows, 2 * CHUNK), F32) * 3
             + _nbytes((cfg.rows, D_MODEL), F32) * 6)
    return pl.pallas_call(
        functools.partial(_mlp_kernel, sample=cfg.sample),
        grid=cfg.grid,
        in_specs=[cfg.act(D_MODEL), cfg.mod_spec, cfg.layer(1, D_MODEL), cfg.layer(1, D_MODEL),
                  cfg.layer(*w1.shape[1:]), cfg.layer(*w2.shape[1:])],
        out_specs=cfg.act(D_MODEL),
        out_shape=jax.ShapeDtypeStruct(x.shape, F32),
        scratch_shapes=[pltpu.VMEM((cfg.rows, D_FF), BF16)],
        compiler_params=pltpu.CompilerParams(
            dimension_semantics=cfg.sem, vmem_limit_bytes=_vmem_limit(blocks, temps)),
        name="mlp_sample" if cfg.sample else "mlp_prompt",
    )(x, mod_all, g_pre2, g_post2, w1, w2)


def kernel(x_prompt, x_sample, c_prompt, c_sample, state_conv, cache_k, cache_v, w_ada, b_ada,
           g_pre1, w_in, conv_w, w_br_conv, w_br_attn, w_o, sinks, g_post1, g_pre2, w_ff1, w_ff2,
           g_post2, rel_table):
    depth = w_ada.shape[0]
    batch, seq, _ = x_prompt.shape
    n_seq, n_new, _ = x_sample.shape
    buf = cache_k.shape[2]

    pad = (-(n_seq + batch)) % V7X_SUBLANES
    c_all = jnp.concatenate([c_sample, c_prompt, jnp.zeros((pad, D_MODEL), F32)], axis=0)
    mod_all = _ada(c_all, w_ada, b_ada)

    dist_p = (jnp.arange(BLOCK)[:, None] + BLOCK) - jnp.arange(2 * BLOCK)[None, :]
    bucket_p = _rel_bucket(dist_p)
    upper = jnp.arange(BLOCK)[None, :] > jnp.arange(BLOCK)[:, None]
    bucket_merged = jnp.where(upper, bucket_p[:, :BLOCK], bucket_p[:, BLOCK:])
    bias_p = _bias_table(rel_table, bucket_merged.T)
    dist_s = (buf + jnp.arange(n_new))[:, None] - jnp.arange(buf + n_new)[None, :]
    bias_s = _bias_table(rel_table, _rel_bucket(dist_s))
    bias_s = (bias_s.reshape(N_KV_HEADS, GROUP, n_new, buf + n_new)
              .transpose(0, 2, 1, 3).reshape(N_KV_HEADS, n_new * GROUP, buf + n_new))

    to_bf16 = lambda w: w.astype(BF16)
    w_in_b, w_bc_b, w_ba_b, w_o_b, w1_b, w2_b = map(
        to_bf16, (w_in, w_br_conv, w_br_attn, w_o, w_ff1, w_ff2))
    vec = lambda g: g.reshape(depth, 1, D_MODEL)
    g_pre1, g_post1, g_pre2, g_post2 = map(vec, (g_pre1, g_post1, g_pre2, g_post2))
    pre_s = state_conv.transpose(0, 2, 1, 3)
    cache_k = cache_k.reshape(depth, n_seq, buf, KV_WIDTH)
    cache_v = cache_v.reshape(depth, n_seq, buf, KV_WIDTH)

    xp = x_prompt
    xs = x_sample.transpose(1, 0, 2)
    conv_p, k_p, v_p, conv_s = [], [], [], []
    windows = None
    win = min(WINDOW, seq)
    for l in range(depth):
        z, attn, kv, gates, tail = _in_proj(xp, mod_all, g_pre1, w_in_b, conv_w, l, n_seq,
                                            attn=(sinks, bias_p))
        x1 = _mix(xp, z, attn, gates, mod_all, g_post1, w_bc_b, w_ba_b, w_o_b, l, n_seq, False)
        xp = _mlp(x1, mod_all, g_pre2, g_post2, w1_b, w2_b, l, n_seq, False)
        conv_p.append(tail[:, V7X_SUBLANES - (CONV_WIDTH - 1):])
        k_p.append(kv[:, seq - win:, :KV_WIDTH].reshape(batch, win, N_KV_HEADS, HEAD_DIM))
        v_p.append(kv[:, seq - win:, KV_WIDTH:].reshape(batch, win, N_KV_HEADS, HEAD_DIM))

        z, q, kv, gates, tail = _in_proj(xs, mod_all, g_pre1, w_in_b, conv_w, l, n_seq, pre_s)
        q_s = (q.reshape(n_new, n_seq, N_KV_HEADS, GROUP, HEAD_DIM)
               .transpose(1, 2, 0, 3, 4).reshape(n_seq, N_KV_HEADS, n_new * GROUP, HEAD_DIM))
        kv_s = kv.transpose(1, 0, 2)
        sink_rows = jnp.tile(sinks[l].reshape(N_KV_HEADS, 1, GROUP),
                             (1, n_new, 1)).reshape(N_KV_HEADS, n_new * GROUP, 1)
        o, *windows = _sample_attn(q_s, cache_k, cache_v, kv_s[:, :, :KV_WIDTH],
                                   kv_s[:, :, KV_WIDTH:], bias_s, sink_rows, l, windows)
        attn = (o.reshape(n_seq, N_KV_HEADS, n_new, GROUP, HEAD_DIM)
                .transpose(2, 0, 1, 3, 4).reshape(n_new, n_seq, ATTN_WIDTH))
        x1 = _mix(xs, z, attn, gates, mod_all, g_post1, w_bc_b, w_ba_b, w_o_b, l, n_seq, True)
        xs = _mlp(x1, mod_all, g_pre2, g_post2, w1_b, w2_b, l, n_seq, True)
        conv_s.append(tail.transpose(1, 0, 2))

    k_s, v_s = (w.reshape(depth, n_seq, buf, N_KV_HEADS, HEAD_DIM) for w in windows)
    return (xp, xs.transpose(1, 0, 2), jnp.stack(conv_p), jnp.stack(k_p), jnp.stack(v_p),
            jnp.stack(conv_s), k_s, v_s)
```

```python
import functools
import math
from typing import Any, Callable, NamedTuple

import jax
import jax.numpy as jnp
from jax import lax
from jax.experimental import pallas as pl
from jax.experimental.pallas import tpu as pltpu

D_MODEL = 1024
N_HEADS = 16
N_KV_HEADS = 2
HEAD_DIM = 64
GROUP = N_HEADS // N_KV_HEADS
ATTN_WIDTH = N_HEADS * HEAD_DIM
KV_WIDTH = N_KV_HEADS * HEAD_DIM
CONV_DIM = D_MODEL
CONV_WIDTH = 3
WINDOW = 128
BLOCK = 128
N_BUCKETS = 32
MAX_DISTANCE = 128
D_FF = 4 * D_MODEL
N_MOD = 6
RMS_EPS = 1e-6
NEG_INF = -1e30
PROJ_COLS = 3 * CONV_DIM + ATTN_WIDTH + 2 * KV_WIDTH + 2 * D_MODEL
Q_OFF = 3 * CONV_DIM
KV_OFF = Q_OFF + ATTN_WIDTH
GATE_OFF = KV_OFF + 2 * KV_WIDTH

V7X_SUBLANES = 8
V7X_VMEM_BYTES = 64 * 1024 * 1024

PROMPT_TILE = 512
SAMPLE_POS_TILE = 4
SAMPLE_SEQ_TILE = 16
CHUNK = 512

F32 = jnp.float32
BF16 = jnp.bfloat16


def _vmem_limit(block_bytes, temp_bytes):
    return int(min(2 * block_bytes + temp_bytes, V7X_VMEM_BYTES - 4 * 1024 * 1024))


def _nbytes(shape, dtype):
    return math.prod(shape) * jnp.dtype(dtype).itemsize


def _rms(x):
    return x * lax.rsqrt(jnp.mean(x * x, axis=-1, keepdims=True) + RMS_EPS)


def _dot(a, b):
    return jnp.dot(a, b, preferred_element_type=F32)


def _flat(a):
    return a.reshape(-1, a.shape[-1])


def _ada_kernel(c_ref, w_ref, b_ref, o_ref):
    c = c_ref[...]
    s = c * (1.0 / (1.0 + jnp.exp(-c)))
    o_ref[...] = _dot(s.astype(BF16), w_ref[...].astype(BF16)) + b_ref[...]


def _ada(c_all, w_ada, b_ada):
    depth = w_ada.shape[0]
    rows = c_all.shape[0]
    blocks = (_nbytes((rows, D_MODEL), F32) * 2 + _nbytes((D_MODEL, D_MODEL), F32))
    return pl.pallas_call(
        _ada_kernel,
        grid=(depth, N_MOD),
        in_specs=[
            pl.BlockSpec((rows, D_MODEL), lambda l, j: (0, 0)),
            pl.BlockSpec((None, D_MODEL, D_MODEL), lambda l, j: (l, 0, j)),
            pl.BlockSpec((None, None, 1, D_MODEL), lambda l, j: (l, j, 0, 0)),
        ],
        out_specs=pl.BlockSpec((None, None, rows, D_MODEL), lambda l, j: (l, j, 0, 0)),
        out_shape=jax.ShapeDtypeStruct((depth, N_MOD, rows, D_MODEL), F32),
        compiler_params=pltpu.CompilerParams(
            dimension_semantics=("arbitrary", "arbitrary"),
            vmem_limit_bytes=_vmem_limit(blocks, _nbytes((D_MODEL, D_MODEL), BF16) * 2)),
        name="ada_mod",
    )(c_all, w_ada, b_ada.reshape(depth, N_MOD, 1, D_MODEL))


def _bias_kernel(tab_ref, bucket_ref, o_ref):
    bucket = bucket_ref[...]
    for h in range(N_HEADS):
        acc = jnp.zeros(bucket.shape, F32)
        for b in range(N_BUCKETS):
            acc = jnp.where(bucket == b, tab_ref[b * N_HEADS + h], acc)
        o_ref[h] = acc


def _bias_table(rel_table, bucket):
    return pl.pallas_call(
        _bias_kernel,
        in_specs=[
            pl.BlockSpec(memory_space=pltpu.SMEM),
            pl.BlockSpec(bucket.shape, lambda: (0, 0)),
        ],
        out_specs=pl.BlockSpec((N_HEADS,) + bucket.shape, lambda: (0, 0, 0)),
        out_shape=jax.ShapeDtypeStruct((N_HEADS,) + bucket.shape, F32),
        name="bias_table",
    )(rel_table.reshape(-1), bucket)


def _rel_bucket(dist):
    n = jnp.maximum(dist, 0)
    max_exact = N_BUCKETS // 2
    nf = jnp.maximum(n, 1).astype(F32)
    large = max_exact + (jnp.log(nf / max_exact) / math.log(MAX_DISTANCE / max_exact)
                         * (N_BUCKETS - max_exact)).astype(jnp.int32)
    large = jnp.minimum(large, N_BUCKETS - 1)
    return jnp.where(n < max_exact, n, large)


class _Dense(NamedTuple):
    sample: bool
    grid: tuple
    rows: int
    act: Callable[[int], Any]
    layer: Callable[..., Any]
    mod_spec: Any
    sem: tuple


def _dense(x, layer_idx, n_seq, sample):
    lead, mid = x.shape[0], x.shape[1]
    if sample:
        grid = (lead // SAMPLE_POS_TILE,)
        act = lambda w: pl.BlockSpec((SAMPLE_POS_TILE, mid, w), lambda i: (i, 0, 0))
        layer = lambda *shape: pl.BlockSpec((None,) + shape,
                                            lambda i: (layer_idx,) + (0,) * len(shape),
                                            pipeline_mode=pl.Buffered(1))
        mod_spec = pl.BlockSpec((None, N_MOD, n_seq, D_MODEL), lambda i: (layer_idx, 0, 0, 0))
        return _Dense(True, grid, SAMPLE_POS_TILE * mid, act, layer, mod_spec, ("arbitrary",))
    assert n_seq % V7X_SUBLANES == 0 and lead <= V7X_SUBLANES
    grid = (lead, mid // PROMPT_TILE)
    act = lambda w: pl.BlockSpec((None, PROMPT_TILE, w), lambda b, t: (b, t, 0))
    layer = lambda *shape: pl.BlockSpec((None,) + shape,
                                        lambda b, t: (layer_idx,) + (0,) * len(shape),
                                        pipeline_mode=pl.Buffered(1))
    mod_spec = pl.BlockSpec((None, N_MOD, V7X_SUBLANES, D_MODEL),
                            lambda b, t: (layer_idx, 0, n_seq // V7X_SUBLANES, 0))
    return _Dense(False, grid, PROMPT_TILE, act, layer, mod_spec, ("arbitrary", "arbitrary"))


def _mod(mod_ref, j, sample):
    if sample:
        return mod_ref[j]
    return mod_ref[j, pl.ds(pl.program_id(0), 1), :]


def _sink_softmax(s, sink):
    m = jnp.maximum(jnp.max(s, axis=-1, keepdims=True), sink)
    e = jnp.exp(s - m)
    den = jnp.sum(e, axis=-1, keepdims=True) + jnp.exp(sink - m)
    return e * (1.0 / den)


def _stage_keys(kv, kd_ref, vt_ref, first_tile):
    n_blk = kv.shape[0] // BLOCK

    @pl.when(first_tile)
    def _():
        kd_ref[:, 0:BLOCK] = jnp.zeros((N_KV_HEADS, BLOCK, 2 * HEAD_DIM), BF16)
        vt_ref[0] = jnp.zeros((KV_WIDTH, BLOCK), BF16)

    @pl.when(jnp.logical_not(first_tile))
    def _():
        kd_ref[:, 0:BLOCK] = kd_ref[:, n_blk * BLOCK:]
        vt_ref[0] = vt_ref[n_blk]

    for g in range(N_KV_HEADS):
        k_g = kv[:, g * HEAD_DIM:(g + 1) * HEAD_DIM]
        kd_ref[g, BLOCK:] = jnp.concatenate([k_g, k_g], axis=1).astype(BF16)
    v_t = kv[:, KV_WIDTH:].T.astype(BF16)
    for blk in range(n_blk):
        vt_ref[blk + 1] = v_t[:, blk * BLOCK:(blk + 1) * BLOCK]


def _attention_phases(q_ref, kd_ref, vt_ref, s_ref, bias_ref, sink_ref, o_ref, first_tile):
    kj = lax.broadcasted_iota(jnp.int32, (BLOCK, BLOCK), 0)
    qi = lax.broadcasted_iota(jnp.int32, (BLOCK, BLOCK), 1)
    upper = kj > qi
    low_lanes = lax.broadcasted_iota(jnp.int32, (BLOCK, 2 * HEAD_DIM), 1) < HEAD_DIM

    def score_phase(c):
        q = q_ref[c * BLOCK:(c + 1) * BLOCK, :]
        for pair in range(N_HEADS // 2):
            g = (2 * pair) // GROUP
            q_pair = q[:, pair * 2 * HEAD_DIM:(pair + 1) * 2 * HEAD_DIM]
            zero = jnp.zeros_like(q_pair)
            q_both = jnp.concatenate([jnp.where(low_lanes, q_pair, zero),
                                      jnp.where(low_lanes, zero, q_pair)], axis=0)
            s_both = lax.dot_general(kd_ref[g, c * BLOCK:(c + 2) * BLOCK, :], q_both,
                                     (((1,), (1,)), ((), ())), preferred_element_type=F32)
            for par in range(2):
                h = 2 * pair + par
                s_h = s_both[:, par * BLOCK:(par + 1) * BLOCK]
                s = jnp.where(upper, s_h[:BLOCK], s_h[BLOCK:]) + bias_ref[h]
                if c == 0:
                    s = jnp.where(upper & first_tile, NEG_INF, s)
                s_ref[c, h] = s

    def value_phase(c, pair):
        g = (2 * pair) // GROUP
        v_cat = jnp.concatenate([vt_ref[c, g * HEAD_DIM:(g + 1) * HEAD_DIM],
                                 vt_ref[c + 1, g * HEAD_DIM:(g + 1) * HEAD_DIM]], axis=1)
        weights, scales = [], []
        for h in (2 * pair, 2 * pair + 1):
            s = s_ref[c, h]
            sink = sink_ref[h]
            m = jnp.maximum(jnp.max(s, axis=0, keepdims=True), sink)
            e = jnp.exp(s - m)
            den = jnp.sum(e, axis=0, keepdims=True) + jnp.exp(sink - m)
            weights.append(jnp.concatenate([jnp.where(upper, e, 0.0), jnp.where(upper, 0.0, e)],
                                           axis=0).astype(BF16))
            scales.append(1.0 / den)
        o_t = _dot(v_cat, jnp.concatenate(weights, axis=1))
        o_t = jnp.concatenate([o_t[:, :BLOCK] * scales[0], o_t[:, BLOCK:] * scales[1]], axis=0)
        o_ref[c * BLOCK:(c + 1) * BLOCK, pair * 2 * HEAD_DIM:(pair + 1) * 2 * HEAD_DIM] = (
            o_t.T.astype(BF16))

    return score_phase, value_phase


def _in_proj_kernel(*refs, sample):
    if sample:
        (x_ref, mod_ref, g_ref, w_ref, cw_ref, pre_ref,
         z_ref, q_ref, kv_ref, gate_ref, tail_ref, carry_ref) = refs
    else:
        (x_ref, mod_ref, g_ref, w_ref, cw_ref, sink_ref, bias_ref,
         z_ref, a_ref, kv_ref, gate_ref, tail_ref,
         carry_ref, q_ref, kd_ref, vt_ref, s_ref) = refs
    step = pl.program_id(0) if sample else pl.program_id(1)

    @pl.when(step == 0)
    def _():
        if sample:
            carry_ref[...] = pre_ref[...]
        else:
            carry_ref[...] = jnp.zeros(carry_ref.shape, F32)

    x = x_ref[...]
    h = (_rms(x) * g_ref[...]) * (1.0 + _mod(mod_ref, 1, sample)) + _mod(mod_ref, 0, sample)
    hb = _flat(h).astype(BF16)
    rows = hb.shape[0]

    def store(ref, lo, val):
        ref[..., lo:lo + val.shape[-1]] = val.reshape(ref.shape[:-1] + (val.shape[-1],))

    def conv_chunk(lo):
        b_g = _dot(hb, w_ref[:, lo:lo + CHUNK])
        c_g = _dot(hb, w_ref[:, CONV_DIM + lo:CONV_DIM + lo + CHUNK])
        x_c = _dot(hb, w_ref[:, 2 * CONV_DIM + lo:2 * CONV_DIM + lo + CHUNK])
        u = c_g * x_c
        if sample:
            pos = x_ref.shape[0]
            u3 = u.reshape(pos, -1, CHUNK)
            ext = jnp.concatenate([carry_ref[:, :, lo:lo + CHUNK], u3], axis=0)
            u2 = _flat(ext[0:pos])
            u1 = _flat(ext[1:pos + 1])
            new_tail = u3[pos - (CONV_WIDTH - 1):]
            carry_ref[:, :, lo:lo + CHUNK] = new_tail
            tail_ref[:, :, lo:lo + CHUNK] = new_tail
        else:
            prev = carry_ref[:, lo:lo + CHUNK]
            p1 = prev[V7X_SUBLANES - 1:V7X_SUBLANES]
            p2 = prev[V7X_SUBLANES - 2:V7X_SUBLANES - 1]
            r = lax.broadcasted_iota(jnp.int32, u.shape, 0)
            u1 = jnp.where(r == 0, p1, pltpu.roll(u, 1, 0))
            u2 = jnp.where(r == 0, p2, jnp.where(r == 1, p1, pltpu.roll(u, 2, 0)))
            new_tail = u[rows - V7X_SUBLANES:]
            carry_ref[:, lo:lo + CHUNK] = new_tail
            tail_ref[:, lo:lo + CHUNK] = new_tail
        conv = (cw_ref[0:1, lo:lo + CHUNK] * u2 + cw_ref[1:2, lo:lo + CHUNK] * u1
                + cw_ref[2:3, lo:lo + CHUNK] * u)
        store(z_ref, lo, (b_g * conv).astype(BF16))

    def q_chunk(lo):
        q = _dot(hb, w_ref[:, Q_OFF + lo:Q_OFF + lo + CHUNK])
        store(q_ref, lo, (q * (HEAD_DIM ** -0.5)).astype(BF16))

    def gate_chunk(lo):
        g = _dot(hb, w_ref[:, GATE_OFF + lo:GATE_OFF + lo + CHUNK])
        store(gate_ref, lo, (1.0 / (1.0 + jnp.exp(-g))).astype(BF16))

    kv = _dot(hb, w_ref[:, KV_OFF:KV_OFF + 2 * KV_WIDTH])
    store(kv_ref, 0, kv)
    for lo in range(0, ATTN_WIDTH, CHUNK):
        q_chunk(lo)
    dense = ([functools.partial(conv_chunk, lo) for lo in range(0, CONV_DIM, CHUNK)]
             + [functools.partial(gate_chunk, lo) for lo in range(0, 2 * D_MODEL, CHUNK)])
    if sample:
        for work in dense:
            work()
        return

    first_tile = step == 0
    _stage_keys(kv, kd_ref, vt_ref, first_tile)
    score_phase, value_phase = _attention_phases(q_ref, kd_ref, vt_ref, s_ref, bias_ref,
                                                 sink_ref, a_ref, first_tile)
    n_blk = rows // BLOCK
    attention = [functools.partial(score_phase, 0)]
    for c in range(n_blk):
        if c + 1 < n_blk:
            attention.append(functools.partial(score_phase, c + 1))
        attention += [functools.partial(value_phase, c, pair) for pair in range(N_HEADS // 2)]
    done = 0
    for i, work in enumerate(attention):
        work()
        while done < len(dense) and done * len(attention) < (i + 1) * len(dense):
            dense[done]()
            done += 1


def _in_proj(x, mod_all, g_pre1, w_in, conv_w, layer_idx, n_seq, pre=None, attn=None):
    cfg = _dense(x, layer_idx, n_seq, pre is not None)
    lead, mid = x.shape[0], x.shape[1]
    widths = (CONV_DIM, ATTN_WIDTH, 2 * KV_WIDTH, 2 * D_MODEL)
    dtypes = (BF16, BF16, F32, BF16)
    blocks = (_nbytes((cfg.rows, D_MODEL), F32) + _nbytes(w_in.shape[1:], BF16)
              + sum(_nbytes((cfg.rows, w), dt) for w, dt in zip(widths, dtypes)))
    temps = _nbytes((cfg.rows, D_MODEL), F32) * 2 + _nbytes((cfg.rows, CHUNK), F32) * 10
    if cfg.sample:
        tail_shape = (CONV_WIDTH - 1, mid, CONV_DIM)
        tail_spec = pl.BlockSpec(tail_shape, lambda i: (0, 0, 0))
        extra_in = [pre]
        extra_specs = [cfg.layer(*tail_shape)]
        scratch = [pltpu.VMEM(tail_shape, F32)]
    else:
        sinks, bias = attn
        per_tile = cfg.rows // BLOCK
        tail_shape = (lead, V7X_SUBLANES, CONV_DIM)
        tail_spec = pl.BlockSpec((None, V7X_SUBLANES, CONV_DIM), lambda b, t: (b, 0, 0))
        extra_in = [sinks[layer_idx], bias]
        extra_specs = [pl.BlockSpec(memory_space=pltpu.SMEM),
                       pl.BlockSpec(bias.shape, lambda b, t: (0, 0, 0),
                                    pipeline_mode=pl.Buffered(1))]
        scratch = [pltpu.VMEM((V7X_SUBLANES, CONV_DIM), F32),
                   pltpu.VMEM((cfg.rows, ATTN_WIDTH), BF16),
                   pltpu.VMEM((N_KV_HEADS, cfg.rows + BLOCK, 2 * HEAD_DIM), BF16),
                   pltpu.VMEM((per_tile + 1, KV_WIDTH, BLOCK), BF16),
                   pltpu.VMEM((per_tile, N_HEADS, BLOCK, BLOCK), F32)]
        temps += (_nbytes(bias.shape, F32) + _nbytes((cfg.rows, ATTN_WIDTH), BF16)
                  + _nbytes((per_tile, N_HEADS, BLOCK, BLOCK), F32)
                  + _nbytes((BLOCK, 2 * BLOCK), F32) * 16 + _nbytes((BLOCK, ATTN_WIDTH), F32) * 4)

    out_shape = [jax.ShapeDtypeStruct(x.shape[:-1] + (w,), dt) for w, dt in zip(widths, dtypes)]
    out_shape.append(jax.ShapeDtypeStruct(tail_shape, F32))
    return pl.pallas_call(
        functools.partial(_in_proj_kernel, sample=cfg.sample),
        grid=cfg.grid,
        in_specs=[cfg.act(D_MODEL), cfg.mod_spec, cfg.layer(1, D_MODEL),
                  cfg.layer(*w_in.shape[1:]), cfg.layer(*conv_w.shape[1:])] + extra_specs,
        out_specs=[cfg.act(w) for w in widths] + [tail_spec],
        out_shape=out_shape,
        scratch_shapes=scratch,
        compiler_params=pltpu.CompilerParams(
            dimension_semantics=cfg.sem, vmem_limit_bytes=_vmem_limit(blocks, temps)),
        name="in_proj_sample" if cfg.sample else "in_proj_attn_prompt",
    )(x, mod_all, g_pre1, w_in, conv_w, *extra_in)


def _sample_attn_kernel(q_ref, kc_ref, vc_ref, kn_ref, vn_ref, bias_ref, sink_ref, *rest):
    o_ref, ko_ref, vo_ref = rest[-3:]
    buf = kc_ref.shape[1]
    new = kn_ref.shape[1]
    kk = jnp.concatenate([kc_ref[...], kn_ref[...]], axis=1)
    vv = jnp.concatenate([vc_ref[...], vn_ref[...]], axis=1)
    ko_ref[...] = kk[:, new:]
    vo_ref[...] = vv[:, new:]
    rows = new * GROUP
    pos = lax.broadcasted_iota(jnp.int32, (rows, buf + new), 0) // GROUP
    kj = lax.broadcasted_iota(jnp.int32, (rows, buf + new), 1)
    dist = buf + pos - kj
    mask = ((dist >= 0) & (dist < WINDOW))[None]
    kb = kk.astype(BF16)
    vb = vv.astype(BF16)
    for g in range(N_KV_HEADS):
        k_g = kb[:, :, g * HEAD_DIM:(g + 1) * HEAD_DIM]
        v_g = vb[:, :, g * HEAD_DIM:(g + 1) * HEAD_DIM]
        s = jnp.einsum('nqd,nkd->nqk', q_ref[:, g], k_g, preferred_element_type=F32)
        s = jnp.where(mask, s + bias_ref[g][None], NEG_INF)
        p = _sink_softmax(s, sink_ref[g][None])
        o = jnp.einsum('nqk,nkd->nqd', p.astype(BF16), v_g, preferred_element_type=F32)
        o_ref[:, g] = o.astype(BF16)


def _sample_attn(q, k_cache, v_cache, k_new, v_new, bias, sink_rows, layer_idx, windows):
    _, n, buf, _ = k_cache.shape
    new = k_new.shape[1]
    rows = new * GROUP
    nt = SAMPLE_SEQ_TILE
    seq3 = lambda a, b: pl.BlockSpec((nt, a, b), lambda i: (i, 0, 0))
    cache = pl.BlockSpec((None, nt, buf, KV_WIDTH), lambda i: (layer_idx, i, 0, 0))
    q_spec = pl.BlockSpec((nt, N_KV_HEADS, rows, HEAD_DIM), lambda i: (i, 0, 0, 0))
    in_specs = [q_spec, cache, cache, seq3(new, KV_WIDTH), seq3(new, KV_WIDTH),
                pl.BlockSpec(bias.shape, lambda i: (0, 0, 0)),
                pl.BlockSpec(sink_rows.shape, lambda i: (0, 0, 0))]
    args = [q, k_cache, v_cache, k_new, v_new, bias, sink_rows]
    aliases = {}
    if windows is not None:
        aliases = {len(args): 1, len(args) + 1: 2}
        in_specs += [pl.BlockSpec(memory_space=pl.ANY)] * 2
        args += list(windows)
    blocks = (_nbytes((nt, buf, KV_WIDTH), F32) * 4 + _nbytes((nt, new, KV_WIDTH), F32) * 2
              + _nbytes((nt, N_KV_HEADS, rows, HEAD_DIM), BF16) * 2 + _nbytes(bias.shape, F32))
    temps = (_nbytes((nt, buf + new, KV_WIDTH), F32) * 4
             + _nbytes((nt, rows, 2 * BLOCK), F32) * 6)
    return pl.pallas_call(
        _sample_attn_kernel,
        grid=(n // nt,),
        in_specs=in_specs,
        out_specs=[q_spec, cache, cache],
        out_shape=[jax.ShapeDtypeStruct(q.shape, BF16),
                   jax.ShapeDtypeStruct(k_cache.shape, F32),
                   jax.ShapeDtypeStruct(v_cache.shape, F32)],
        input_output_aliases=aliases,
        compiler_params=pltpu.CompilerParams(
            dimension_semantics=("arbitrary",), vmem_limit_bytes=_vmem_limit(blocks, temps)),
        name="attn_sample",
    )(*args)


def _post_kernel(x_ref, z_ref, a_ref, gate_ref, mod_ref, gpost1_ref, gpre2_ref, gpost2_ref,
                 wbc_ref, wba_ref, wo_ref, w1_ref, w2_ref, o_ref, hid_ref, *, sample):
    shape = x_ref.shape
    y_conv = _dot(_flat(z_ref[...]), wbc_ref[...])
    y_attn = _dot(_flat(a_ref[...]), wba_ref[...])
    gates = _flat(gate_ref[...])
    merged = gates[:, :D_MODEL] * y_conv + gates[:, D_MODEL:] * y_attn
    mixed = _dot(merged.astype(BF16), wo_ref[...])
    x1 = x_ref[...] + _mod(mod_ref, 2, sample) * (_rms(mixed) * gpost1_ref[...]).reshape(shape)

    h = (_rms(x1) * gpre2_ref[...]) * (1.0 + _mod(mod_ref, 4, sample)) + _mod(mod_ref, 3, sample)
    hb = _flat(h).astype(BF16)
    for lo in range(0, D_FF, 2 * CHUNK):
        act = jnp.maximum(_dot(hb, w1_ref[:, lo:lo + 2 * CHUNK]), 0.0)
        hid_ref[:, lo:lo + 2 * CHUNK] = (act * act).astype(BF16)
    ff = _dot(hid_ref[...], w2_ref[...])
    o_ref[...] = x1 + _mod(mod_ref, 5, sample) * (_rms(ff) * gpost2_ref[...]).reshape(shape)


def _post(x, z, a, gates, mod_all, g_post1, g_pre2, g_post2, w_bc, w_ba, w_o, w1, w2,
          layer_idx, n_seq, sample):
    cfg = _dense(x, layer_idx, n_seq, sample)
    sq = (D_MODEL, D_MODEL)
    blocks = (_nbytes((cfg.rows, D_MODEL), F32) * 2 + _nbytes((cfg.rows, D_MODEL), BF16) * 2
              + _nbytes((cfg.rows, 2 * D_MODEL), gates.dtype))
    temps = (_nbytes(sq, BF16) * 3 + _nbytes(w1.shape[1:], BF16) * 2
             + _nbytes((cfg.rows, D_FF), BF16) + _nbytes((cfg.rows, 2 * CHUNK), F32) * 2
             + _nbytes((cfg.rows, D_MODEL), F32) * 5)
    vec = cfg.layer(1, D_MODEL)
    return pl.pallas_call(
        functools.partial(_post_kernel, sample=cfg.sample),
        grid=cfg.grid,
        in_specs=[cfg.act(D_MODEL), cfg.act(CONV_DIM), cfg.act(ATTN_WIDTH),
                  cfg.act(2 * D_MODEL), cfg.mod_spec, vec, vec, vec,
                  cfg.layer(*sq), cfg.layer(*sq), cfg.layer(*sq),
                  cfg.layer(*w1.shape[1:]), cfg.layer(*w2.shape[1:])],
        out_specs=cfg.act(D_MODEL),
        out_shape=jax.ShapeDtypeStruct(x.shape, F32),
        scratch_shapes=[pltpu.VMEM((cfg.rows, D_FF), BF16)],
        compiler_params=pltpu.CompilerParams(
            dimension_semantics=cfg.sem, vmem_limit_bytes=_vmem_limit(blocks, temps)),
        name="post_sample" if cfg.sample else "post_prompt",
    )(x, z, a, gates, mod_all, g_post1, g_pre2, g_post2, w_bc, w_ba, w_o, w1, w2)


def kernel(x_prompt, x_sample, c_prompt, c_sample, state_conv, cache_k, cache_v, w_ada, b_ada,
           g_pre1, w_in, conv_w, w_br_conv, w_br_attn, w_o, sinks, g_post1, g_pre2, w_ff1, w_ff2,
           g_post2, rel_table):
    depth = w_ada.shape[0]
    batch, seq, _ = x_prompt.shape
    n_seq, n_new, _ = x_sample.shape
    buf = cache_k.shape[2]

    pad = (-(n_seq + batch)) % V7X_SUBLANES
    c_all = jnp.concatenate([c_sample, c_prompt, jnp.zeros((pad, D_MODEL), F32)], axis=0)
    mod_all = _ada(c_all, w_ada, b_ada)

    dist_p = (jnp.arange(BLOCK)[:, None] + BLOCK) - jnp.arange(2 * BLOCK)[None, :]
    bucket_p = _rel_bucket(dist_p)
    upper = jnp.arange(BLOCK)[None, :] > jnp.arange(BLOCK)[:, None]
    bucket_merged = jnp.where(upper, bucket_p[:, :BLOCK], bucket_p[:, BLOCK:])
    bias_p = _bias_table(rel_table, bucket_merged.T)
    dist_s = (buf + jnp.arange(n_new))[:, None] - jnp.arange(buf + n_new)[None, :]
    bias_s = _bias_table(rel_table, _rel_bucket(dist_s))
    bias_s = (bias_s.reshape(N_KV_HEADS, GROUP, n_new, buf + n_new)
              .transpose(0, 2, 1, 3).reshape(N_KV_HEADS, n_new * GROUP, buf + n_new))

    to_bf16 = lambda w: w.astype(BF16)
    w_in_b, w_bc_b, w_ba_b, w_o_b, w1_b, w2_b = map(
        to_bf16, (w_in, w_br_conv, w_br_attn, w_o, w_ff1, w_ff2))
    vec = lambda g: g.reshape(depth, 1, D_MODEL)
    g_pre1, g_post1, g_pre2, g_post2 = map(vec, (g_pre1, g_post1, g_pre2, g_post2))
    pre_s = state_conv.transpose(0, 2, 1, 3)
    cache_k = cache_k.reshape(depth, n_seq, buf, KV_WIDTH)
    cache_v = cache_v.reshape(depth, n_seq, buf, KV_WIDTH)

    xp = x_prompt
    xs = x_sample.transpose(1, 0, 2)
    conv_p, k_p, v_p, conv_s = [], [], [], []
    windows = None
    win = min(WINDOW, seq)
    for l in range(depth):
        z, attn, kv, gates, tail = _in_proj(xp, mod_all, g_pre1, w_in_b, conv_w, l, n_seq,
                                            attn=(sinks, bias_p))
        xp = _post(xp, z, attn, gates, mod_all, g_post1, g_pre2, g_post2, w_bc_b, w_ba_b, w_o_b,
                   w1_b, w2_b, l, n_seq, False)
        conv_p.append(tail[:, V7X_SUBLANES - (CONV_WIDTH - 1):])
        k_p.append(kv[:, seq - win:, :KV_WIDTH].reshape(batch, win, N_KV_HEADS, HEAD_DIM))
        v_p.append(kv[:, seq - win:, KV_WIDTH:].reshape(batch, win, N_KV_HEADS, HEAD_DIM))

        z, q, kv, gates, tail = _in_proj(xs, mod_all, g_pre1, w_in_b, conv_w, l, n_seq, pre_s)
        q_s = (q.reshape(n_new, n_seq, N_KV_HEADS, GROUP, HEAD_DIM)
               .transpose(1, 2, 0, 3, 4).reshape(n_seq, N_KV_HEADS, n_new * GROUP, HEAD_DIM))
        kv_s = kv.transpose(1, 0, 2)
        sink_rows = jnp.tile(sinks[l].reshape(N_KV_HEADS, 1, GROUP),
                             (1, n_new, 1)).reshape(N_KV_HEADS, n_new * GROUP, 1)
        o, *windows = _sample_attn(q_s, cache_k, cache_v, kv_s[:, :, :KV_WIDTH],
                                   kv_s[:, :, KV_WIDTH:], bias_s, sink_rows, l, windows)
        attn = (o.reshape(n_seq, N_KV_HEADS, n_new, GROUP, HEAD_DIM)
                .transpose(2, 0, 1, 3, 4).reshape(n_new, n_seq, ATTN_WIDTH))
        xs = _post(xs, z, attn, gates, mod_all, g_post1, g_pre2, g_post2, w_bc_b, w_ba_b, w_o_b,
                   w1_b, w2_b, l, n_seq, True)
        conv_s.append(tail.transpose(1, 0, 2))

    k_s, v_s = (w.reshape(depth, n_seq, buf, N_KV_HEADS, HEAD_DIM) for w in windows)
    return (xp, xs.transpose(1, 0, 2), jnp.stack(conv_p), jnp.stack(k_p), jnp.stack(v_p),
            jnp.stack(conv_s), k_s, v_s)
```

```python
import functools
import math
from typing import Any, Callable, NamedTuple

import jax
import jax.numpy as jnp
from jax import lax
from jax.experimental import pallas as pl
from jax.experimental.pallas import tpu as pltpu

D_MODEL = 1024
N_HEADS = 16
N_KV_HEADS = 2
HEAD_DIM = 64
GROUP = N_HEADS // N_KV_HEADS
ATTN_WIDTH = N_HEADS * HEAD_DIM
KV_WIDTH = N_KV_HEADS * HEAD_DIM
CONV_DIM = D_MODEL
CONV_WIDTH = 3
WINDOW = 128
BLOCK = 128
N_BUCKETS = 32
MAX_DISTANCE = 128
D_FF = 4 * D_MODEL
N_MOD = 6
RMS_EPS = 1e-6
NEG_INF = -1e30
PROJ_COLS = 3 * CONV_DIM + ATTN_WIDTH + 2 * KV_WIDTH + 2 * D_MODEL
Q_OFF = 3 * CONV_DIM
KV_OFF = Q_OFF + ATTN_WIDTH
GATE_OFF = KV_OFF + 2 * KV_WIDTH

V7X_SUBLANES = 8
V7X_VMEM_BYTES = 64 * 1024 * 1024

PROMPT_TILE = 512
SAMPLE_POS_TILE = 4
SAMPLE_SEQ_TILE = 16
CHUNK = 512

F32 = jnp.float32
BF16 = jnp.bfloat16


def _vmem_limit(block_bytes, temp_bytes):
    return int(min(2 * block_bytes + temp_bytes, V7X_VMEM_BYTES - 4 * 1024 * 1024))


def _nbytes(shape, dtype):
    return math.prod(shape) * jnp.dtype(dtype).itemsize


def _rms(x):
    return x * lax.rsqrt(jnp.mean(x * x, axis=-1, keepdims=True) + RMS_EPS)


def _dot(a, b):
    return jnp.dot(a, b, preferred_element_type=F32)


def _flat(a):
    return a.reshape(-1, a.shape[-1])


def _ada_kernel(c_ref, w_ref, b_ref, o_ref):
    c = c_ref[...]
    s = c * (1.0 / (1.0 + jnp.exp(-c)))
    o_ref[...] = _dot(s.astype(BF16), w_ref[...].astype(BF16)) + b_ref[...]


def _ada(c_all, w_ada, b_ada):
    depth = w_ada.shape[0]
    rows = c_all.shape[0]
    blocks = (_nbytes((rows, D_MODEL), F32) * 2 + _nbytes((D_MODEL, D_MODEL), F32))
    return pl.pallas_call(
        _ada_kernel,
        grid=(depth, N_MOD),
        in_specs=[
            pl.BlockSpec((rows, D_MODEL), lambda l, j: (0, 0)),
            pl.BlockSpec((None, D_MODEL, D_MODEL), lambda l, j: (l, 0, j)),
            pl.BlockSpec((None, None, 1, D_MODEL), lambda l, j: (l, j, 0, 0)),
        ],
        out_specs=pl.BlockSpec((None, None, rows, D_MODEL), lambda l, j: (l, j, 0, 0)),
        out_shape=jax.ShapeDtypeStruct((depth, N_MOD, rows, D_MODEL), F32),
        compiler_params=pltpu.CompilerParams(
            dimension_semantics=("arbitrary", "arbitrary"),
            vmem_limit_bytes=_vmem_limit(blocks, _nbytes((D_MODEL, D_MODEL), BF16) * 2)),
        name="ada_mod",
    )(c_all, w_ada, b_ada.reshape(depth, N_MOD, 1, D_MODEL))


def _bias_kernel(tab_ref, bucket_ref, o_ref):
    bucket = bucket_ref[...]
    for h in range(N_HEADS):
        acc = jnp.zeros(bucket.shape, F32)
        for b in range(N_BUCKETS):
            acc = jnp.where(bucket == b, tab_ref[b * N_HEADS + h], acc)
        o_ref[h] = acc


def _bias_table(rel_table, bucket):
    return pl.pallas_call(
        _bias_kernel,
        in_specs=[
            pl.BlockSpec(memory_space=pltpu.SMEM),
            pl.BlockSpec(bucket.shape, lambda: (0, 0)),
        ],
        out_specs=pl.BlockSpec((N_HEADS,) + bucket.shape, lambda: (0, 0, 0)),
        out_shape=jax.ShapeDtypeStruct((N_HEADS,) + bucket.shape, F32),
        name="bias_table",
    )(rel_table.reshape(-1), bucket)


def _rel_bucket(dist):
    n = jnp.maximum(dist, 0)
    max_exact = N_BUCKETS // 2
    nf = jnp.maximum(n, 1).astype(F32)
    large = max_exact + (jnp.log(nf / max_exact) / math.log(MAX_DISTANCE / max_exact)
                         * (N_BUCKETS - max_exact)).astype(jnp.int32)
    large = jnp.minimum(large, N_BUCKETS - 1)
    return jnp.where(n < max_exact, n, large)


class _Dense(NamedTuple):
    sample: bool
    grid: tuple
    rows: int
    act: Callable[[int], Any]
    layer: Callable[..., Any]
    mod_spec: Any
    sem: tuple


def _dense(x, layer_idx, n_seq, sample):
    lead, mid = x.shape[0], x.shape[1]
    if sample:
        grid = (lead // SAMPLE_POS_TILE,)
        act = lambda w: pl.BlockSpec((SAMPLE_POS_TILE, mid, w), lambda i: (i, 0, 0))
        layer = lambda *shape: pl.BlockSpec((None,) + shape,
                                            lambda i: (layer_idx,) + (0,) * len(shape),
                                            pipeline_mode=pl.Buffered(1))
        mod_spec = pl.BlockSpec((None, N_MOD, n_seq, D_MODEL), lambda i: (layer_idx, 0, 0, 0))
        return _Dense(True, grid, SAMPLE_POS_TILE * mid, act, layer, mod_spec, ("arbitrary",))
    assert n_seq % V7X_SUBLANES == 0 and lead <= V7X_SUBLANES
    grid = (lead, mid // PROMPT_TILE)
    act = lambda w: pl.BlockSpec((None, PROMPT_TILE, w), lambda b, t: (b, t, 0))
    layer = lambda *shape: pl.BlockSpec((None,) + shape,
                                        lambda b, t: (layer_idx,) + (0,) * len(shape),
                                        pipeline_mode=pl.Buffered(1))
    mod_spec = pl.BlockSpec((None, N_MOD, V7X_SUBLANES, D_MODEL),
                            lambda b, t: (layer_idx, 0, n_seq // V7X_SUBLANES, 0))
    return _Dense(False, grid, PROMPT_TILE, act, layer, mod_spec, ("arbitrary", "arbitrary"))


def _mod(mod_ref, j, sample):
    if sample:
        return mod_ref[j]
    return mod_ref[j, pl.ds(pl.program_id(0), 1), :]


def _sink_softmax(s, sink):
    m = jnp.maximum(jnp.max(s, axis=-1, keepdims=True), sink)
    e = jnp.exp(s - m)
    den = jnp.sum(e, axis=-1, keepdims=True) + jnp.exp(sink - m)
    return e * (1.0 / den)


def _stage_keys(kv, kd_ref, vt_ref, first_tile):
    n_blk = kv.shape[0] // BLOCK

    @pl.when(first_tile)
    def _():
        kd_ref[:, 0:BLOCK] = jnp.zeros((N_KV_HEADS, BLOCK, 2 * HEAD_DIM), BF16)
        vt_ref[0] = jnp.zeros((KV_WIDTH, BLOCK), BF16)

    @pl.when(jnp.logical_not(first_tile))
    def _():
        kd_ref[:, 0:BLOCK] = kd_ref[:, n_blk * BLOCK:]
        vt_ref[0] = vt_ref[n_blk]

    for g in range(N_KV_HEADS):
        k_g = kv[:, g * HEAD_DIM:(g + 1) * HEAD_DIM]
        kd_ref[g, BLOCK:] = jnp.concatenate([k_g, k_g], axis=1).astype(BF16)
    v_t = kv[:, KV_WIDTH:].T.astype(BF16)
    for blk in range(n_blk):
        vt_ref[blk + 1] = v_t[:, blk * BLOCK:(blk + 1) * BLOCK]


def _attention_phases(q_ref, kd_ref, vt_ref, s_ref, bias_ref, sink_ref, o_ref, first_tile):
    kj = lax.broadcasted_iota(jnp.int32, (BLOCK, BLOCK), 0)
    qi = lax.broadcasted_iota(jnp.int32, (BLOCK, BLOCK), 1)
    upper = kj > qi
    low_lanes = lax.broadcasted_iota(jnp.int32, (BLOCK, 2 * HEAD_DIM), 1) < HEAD_DIM

    def score_phase(c):
        q = q_ref[c * BLOCK:(c + 1) * BLOCK, :]
        for pair in range(N_HEADS // 2):
            g = (2 * pair) // GROUP
            q_pair = q[:, pair * 2 * HEAD_DIM:(pair + 1) * 2 * HEAD_DIM]
            zero = jnp.zeros_like(q_pair)
            q_both = jnp.concatenate([jnp.where(low_lanes, q_pair, zero),
                                      jnp.where(low_lanes, zero, q_pair)], axis=0)
            s_both = lax.dot_general(kd_ref[g, c * BLOCK:(c + 2) * BLOCK, :], q_both,
                                     (((1,), (1,)), ((), ())), preferred_element_type=F32)
            for par in range(2):
                h = 2 * pair + par
                s_h = s_both[:, par * BLOCK:(par + 1) * BLOCK]
                s = jnp.where(upper, s_h[:BLOCK], s_h[BLOCK:]) + bias_ref[h]
                if c == 0:
                    s = jnp.where(upper & first_tile, NEG_INF, s)
                s_ref[c, h] = s

    def value_phase(c, pair):
        g = (2 * pair) // GROUP
        v_cat = jnp.concatenate([vt_ref[c, g * HEAD_DIM:(g + 1) * HEAD_DIM],
                                 vt_ref[c + 1, g * HEAD_DIM:(g + 1) * HEAD_DIM]], axis=1)
        weights, scales = [], []
        for h in (2 * pair, 2 * pair + 1):
            s = s_ref[c, h]
            sink = sink_ref[h]
            m = jnp.maximum(jnp.max(s, axis=0, keepdims=True), sink)
            e = jnp.exp(s - m)
            den = jnp.sum(e, axis=0, keepdims=True) + jnp.exp(sink - m)
            weights.append(jnp.concatenate([jnp.where(upper, e, 0.0), jnp.where(upper, 0.0, e)],
                                           axis=0).astype(BF16))
            scales.append(1.0 / den)
        o_t = _dot(v_cat, jnp.concatenate(weights, axis=1))
        o_t = jnp.concatenate([o_t[:, :BLOCK] * scales[0], o_t[:, BLOCK:] * scales[1]], axis=0)
        o_ref[c * BLOCK:(c + 1) * BLOCK, pair * 2 * HEAD_DIM:(pair + 1) * 2 * HEAD_DIM] = (
            o_t.T.astype(BF16))

    return score_phase, value_phase


def _in_proj_kernel(*refs, sample):
    if sample:
        (x_ref, mod_ref, g_ref, w_ref, cw_ref, pre_ref,
         z_ref, q_ref, kv_ref, gate_ref, tail_ref, carry_ref) = refs
    else:
        (x_ref, mod_ref, g_ref, w_ref, cw_ref, sink_ref, bias_ref,
         z_ref, a_ref, kv_ref, gate_ref, tail_ref,
         carry_ref, q_ref, kd_ref, vt_ref, s_ref) = refs
    step = pl.program_id(0) if sample else pl.program_id(1)

    @pl.when(step == 0)
    def _():
        if sample:
            carry_ref[...] = pre_ref[...]
        else:
            carry_ref[...] = jnp.zeros(carry_ref.shape, F32)

    x = x_ref[...]
    h = (_rms(x) * g_ref[...]) * (1.0 + _mod(mod_ref, 1, sample)) + _mod(mod_ref, 0, sample)
    hb = _flat(h).astype(BF16)
    rows = hb.shape[0]

    def store(ref, lo, val):
        ref[..., lo:lo + val.shape[-1]] = val.reshape(ref.shape[:-1] + (val.shape[-1],))

    def conv_chunk(lo):
        b_g = _dot(hb, w_ref[:, lo:lo + CHUNK])
        c_g = _dot(hb, w_ref[:, CONV_DIM + lo:CONV_DIM + lo + CHUNK])
        x_c = _dot(hb, w_ref[:, 2 * CONV_DIM + lo:2 * CONV_DIM + lo + CHUNK])
        u = c_g * x_c
        if sample:
            pos = x_ref.shape[0]
            u3 = u.reshape(pos, -1, CHUNK)
            ext = jnp.concatenate([carry_ref[:, :, lo:lo + CHUNK], u3], axis=0)
            u2 = _flat(ext[0:pos])
            u1 = _flat(ext[1:pos + 1])
            new_tail = u3[pos - (CONV_WIDTH - 1):]
            carry_ref[:, :, lo:lo + CHUNK] = new_tail
            tail_ref[:, :, lo:lo + CHUNK] = new_tail
        else:
            prev = carry_ref[:, lo:lo + CHUNK]
            p1 = prev[V7X_SUBLANES - 1:V7X_SUBLANES]
            p2 = prev[V7X_SUBLANES - 2:V7X_SUBLANES - 1]
            r = lax.broadcasted_iota(jnp.int32, u.shape, 0)
            u1 = jnp.where(r == 0, p1, pltpu.roll(u, 1, 0))
            u2 = jnp.where(r == 0, p2, jnp.where(r == 1, p1, pltpu.roll(u, 2, 0)))
            new_tail = u[rows - V7X_SUBLANES:]
            carry_ref[:, lo:lo + CHUNK] = new_tail
            tail_ref[:, lo:lo + CHUNK] = new_tail
        conv = (cw_ref[0:1, lo:lo + CHUNK] * u2 + cw_ref[1:2, lo:lo + CHUNK] * u1
                + cw_ref[2:3, lo:lo + CHUNK] * u)
        store(z_ref, lo, (b_g * conv).astype(BF16))

    def q_chunk(lo):
        q = _dot(hb, w_ref[:, Q_OFF + lo:Q_OFF + lo + CHUNK])
        store(q_ref, lo, (q * (HEAD_DIM ** -0.5)).astype(BF16))

    def gate_chunk(lo):
        g = _dot(hb, w_ref[:, GATE_OFF + lo:GATE_OFF + lo + CHUNK])
        store(gate_ref, lo, (1.0 / (1.0 + jnp.exp(-g))).astype(BF16))

    kv = _dot(hb, w_ref[:, KV_OFF:KV_OFF + 2 * KV_WIDTH])
    store(kv_ref, 0, kv)
    for lo in range(0, ATTN_WIDTH, CHUNK):
        q_chunk(lo)
    dense = ([functools.partial(conv_chunk, lo) for lo in range(0, CONV_DIM, CHUNK)]
             + [functools.partial(gate_chunk, lo) for lo in range(0, 2 * D_MODEL, CHUNK)])
    if sample:
        for work in dense:
            work()
        return

    first_tile = step == 0
    _stage_keys(kv, kd_ref, vt_ref, first_tile)
    score_phase, value_phase = _attention_phases(q_ref, kd_ref, vt_ref, s_ref, bias_ref,
                                                 sink_ref, a_ref, first_tile)
    n_blk = rows // BLOCK
    attention = [functools.partial(score_phase, 0)]
    for c in range(n_blk):
        if c + 1 < n_blk:
            attention.append(functools.partial(score_phase, c + 1))
        attention += [functools.partial(value_phase, c, pair) for pair in range(N_HEADS // 2)]
    done = 0
    for i, work in enumerate(attention):
        work()
        while done < len(dense) and done * len(attention) < (i + 1) * len(dense):
            dense[done]()
            done += 1


def _in_proj(x, mod_all, g_pre1, w_in, conv_w, layer_idx, n_seq, pre=None, attn=None):
    cfg = _dense(x, layer_idx, n_seq, pre is not None)
    lead, mid = x.shape[0], x.shape[1]
    widths = (CONV_DIM, ATTN_WIDTH, 2 * KV_WIDTH, 2 * D_MODEL)
    dtypes = (BF16, BF16, F32, BF16)
    blocks = (_nbytes((cfg.rows, D_MODEL), F32) + _nbytes(w_in.shape[1:], BF16)
              + sum(_nbytes((cfg.rows, w), dt) for w, dt in zip(widths, dtypes)))
    temps = _nbytes((cfg.rows, D_MODEL), F32) * 2 + _nbytes((cfg.rows, CHUNK), F32) * 10
    if cfg.sample:
        tail_shape = (CONV_WIDTH - 1, mid, CONV_DIM)
        tail_spec = pl.BlockSpec(tail_shape, lambda i: (0, 0, 0))
        extra_in = [pre]
        extra_specs = [cfg.layer(*tail_shape)]
        scratch = [pltpu.VMEM(tail_shape, F32)]
    else:
        sinks, bias = attn
        per_tile = cfg.rows // BLOCK
        tail_shape = (lead, V7X_SUBLANES, CONV_DIM)
        tail_spec = pl.BlockSpec((None, V7X_SUBLANES, CONV_DIM), lambda b, t: (b, 0, 0))
        extra_in = [sinks[layer_idx], bias]
        extra_specs = [pl.BlockSpec(memory_space=pltpu.SMEM),
                       pl.BlockSpec(bias.shape, lambda b, t: (0, 0, 0),
                                    pipeline_mode=pl.Buffered(1))]
        scratch = [pltpu.VMEM((V7X_SUBLANES, CONV_DIM), F32),
                   pltpu.VMEM((cfg.rows, ATTN_WIDTH), BF16),
                   pltpu.VMEM((N_KV_HEADS, cfg.rows + BLOCK, 2 * HEAD_DIM), BF16),
                   pltpu.VMEM((per_tile + 1, KV_WIDTH, BLOCK), BF16),
                   pltpu.VMEM((per_tile, N_HEADS, BLOCK, BLOCK), F32)]
        temps += (_nbytes(bias.shape, F32) + _nbytes((cfg.rows, ATTN_WIDTH), BF16)
                  + _nbytes((per_tile, N_HEADS, BLOCK, BLOCK), F32)
                  + _nbytes((BLOCK, 2 * BLOCK), F32) * 16 + _nbytes((BLOCK, ATTN_WIDTH), F32) * 4)

    out_shape = [jax.ShapeDtypeStruct(x.shape[:-1] + (w,), dt) for w, dt in zip(widths, dtypes)]
    out_shape.append(jax.ShapeDtypeStruct(tail_shape, F32))
    return pl.pallas_call(
        functools.partial(_in_proj_kernel, sample=cfg.sample),
        grid=cfg.grid,
        in_specs=[cfg.act(D_MODEL), cfg.mod_spec, cfg.layer(1, D_MODEL),
                  cfg.layer(*w_in.shape[1:]), cfg.layer(*conv_w.shape[1:])] + extra_specs,
        out_specs=[cfg.act(w) for w in widths] + [tail_spec],
        out_shape=out_shape,
        scratch_shapes=scratch,
        compiler_params=pltpu.CompilerParams(
            dimension_semantics=cfg.sem, vmem_limit_bytes=_vmem_limit(blocks, temps)),
        name="in_proj_sample" if cfg.sample else "in_proj_attn_prompt",
    )(x, mod_all, g_pre1, w_in, conv_w, *extra_in)


def _sample_attn_kernel(q_ref, kc_ref, vc_ref, kn_ref, vn_ref, bias_ref, sink_ref, *rest):
    o_ref, ko_ref, vo_ref = rest[-3:]
    n_seq = q_ref.shape[0]
    buf = kc_ref.shape[-1]
    new = kn_ref.shape[1]
    rows = new * GROUP
    pos = lax.broadcasted_iota(jnp.int32, (rows, 2 * buf), 0) // GROUP
    kj = lax.broadcasted_iota(jnp.int32, (rows, 2 * buf), 1)
    dist = buf + pos - kj
    mask = (dist >= 0) & (dist < WINDOW) & (kj < buf + new)
    lane = lax.broadcasted_iota(jnp.int32, (n_seq, HEAD_DIM, buf), 2)
    pad = jnp.zeros((n_seq, HEAD_DIM, buf - new), F32)

    def extended(cache_t, fresh, out_ref, g):
        fresh_t = jnp.concatenate([jnp.swapaxes(fresh, 1, 2), pad], axis=2)
        out_ref[:, g] = pltpu.roll(jnp.where(lane < new, fresh_t, cache_t), buf - new, 2)
        return jnp.concatenate([cache_t, fresh_t], axis=2).astype(BF16)

    for g in range(N_KV_HEADS):
        lanes = slice(g * HEAD_DIM, (g + 1) * HEAD_DIM)
        k_ext = extended(kc_ref[:, g], kn_ref[:, :, lanes], ko_ref, g)
        v_ext = extended(vc_ref[:, g], vn_ref[:, :, lanes], vo_ref, g)
        s = jnp.einsum('nqd,ndk->nqk', q_ref[:, g], k_ext, preferred_element_type=F32)
        p = _sink_softmax(jnp.where(mask[None], s + bias_ref[g][None], NEG_INF),
                          sink_ref[g][None])
        o = jnp.einsum('nqk,ndk->nqd', p.astype(BF16), v_ext, preferred_element_type=F32)
        o_ref[:, g] = o.astype(BF16)


def _sample_attn(q, k_cache, v_cache, k_new, v_new, bias, sink_rows, layer_idx, windows):
    _, n, _, _, buf = k_cache.shape
    new = k_new.shape[1]
    rows = new * GROUP
    nt = SAMPLE_SEQ_TILE
    seq3 = lambda a, b: pl.BlockSpec((nt, a, b), lambda i: (i, 0, 0))
    cache = pl.BlockSpec((None, nt, N_KV_HEADS, HEAD_DIM, buf),
                         lambda i: (layer_idx, i, 0, 0, 0))
    q_spec = pl.BlockSpec((nt, N_KV_HEADS, rows, HEAD_DIM), lambda i: (i, 0, 0, 0))
    in_specs = [q_spec, cache, cache, seq3(new, KV_WIDTH), seq3(new, KV_WIDTH),
                pl.BlockSpec(bias.shape, lambda i: (0, 0, 0)),
                pl.BlockSpec(sink_rows.shape, lambda i: (0, 0, 0))]
    args = [q, k_cache, v_cache, k_new, v_new, bias, sink_rows]
    aliases = {}
    if windows is not None:
        aliases = {len(args): 1, len(args) + 1: 2}
        in_specs += [pl.BlockSpec(memory_space=pl.ANY)] * 2
        args += list(windows)
    blocks = (_nbytes((nt, N_KV_HEADS, HEAD_DIM, buf), F32) * 4
              + _nbytes((nt, new, KV_WIDTH), F32) * 2
              + _nbytes((nt, N_KV_HEADS, rows, 2 * HEAD_DIM), BF16) * 2 + _nbytes(bias.shape, F32))
    temps = _nbytes((rows, 2 * buf), F32) * 8 * nt
    return pl.pallas_call(
        _sample_attn_kernel,
        grid=(n // nt,),
        in_specs=in_specs,
        out_specs=[q_spec, cache, cache],
        out_shape=[jax.ShapeDtypeStruct(q.shape, BF16),
                   jax.ShapeDtypeStruct(k_cache.shape, F32),
                   jax.ShapeDtypeStruct(v_cache.shape, F32)],
        input_output_aliases=aliases,
        compiler_params=pltpu.CompilerParams(
            dimension_semantics=("arbitrary",), vmem_limit_bytes=_vmem_limit(blocks, temps)),
        name="attn_sample",
    )(*args)


def _post_kernel(x_ref, z_ref, a_ref, gate_ref, mod_ref, gpost1_ref, gpre2_ref, gpost2_ref,
                 wbc_ref, wba_ref, wo_ref, w1_ref, w2_ref, o_ref, hid_ref, *, sample):
    shape = x_ref.shape
    y_conv = _dot(_flat(z_ref[...]), wbc_ref[...])
    y_attn = _dot(_flat(a_ref[...]), wba_ref[...])
    gates = _flat(gate_ref[...])
    merged = gates[:, :D_MODEL] * y_conv + gates[:, D_MODEL:] * y_attn
    mixed = _dot(merged.astype(BF16), wo_ref[...])
    x1 = x_ref[...] + _mod(mod_ref, 2, sample) * (_rms(mixed) * gpost1_ref[...]).reshape(shape)

    h = (_rms(x1) * gpre2_ref[...]) * (1.0 + _mod(mod_ref, 4, sample)) + _mod(mod_ref, 3, sample)
    hb = _flat(h).astype(BF16)
    for lo in range(0, D_FF, 2 * CHUNK):
        act = jnp.maximum(_dot(hb, w1_ref[:, lo:lo + 2 * CHUNK]), 0.0)
        hid_ref[:, lo:lo + 2 * CHUNK] = (act * act).astype(BF16)
    ff = _dot(hid_ref[...], w2_ref[...])
    o_ref[...] = x1 + _mod(mod_ref, 5, sample) * (_rms(ff) * gpost2_ref[...]).reshape(shape)


def _post(x, z, a, gates, mod_all, g_post1, g_pre2, g_post2, w_bc, w_ba, w_o, w1, w2,
          layer_idx, n_seq, sample):
    cfg = _dense(x, layer_idx, n_seq, sample)
    sq = (D_MODEL, D_MODEL)
    blocks = (_nbytes((cfg.rows, D_MODEL), F32) * 2 + _nbytes((cfg.rows, D_MODEL), BF16) * 2
              + _nbytes((cfg.rows, 2 * D_MODEL), gates.dtype))
    temps = (_nbytes(sq, BF16) * 3 + _nbytes(w1.shape[1:], BF16) * 2
             + _nbytes((cfg.rows, D_FF), BF16) + _nbytes((cfg.rows, 2 * CHUNK), F32) * 2
             + _nbytes((cfg.rows, D_MODEL), F32) * 5)
    vec = cfg.layer(1, D_MODEL)
    return pl.pallas_call(
        functools.partial(_post_kernel, sample=cfg.sample),
        grid=cfg.grid,
        in_specs=[cfg.act(D_MODEL), cfg.act(CONV_DIM), cfg.act(ATTN_WIDTH),
                  cfg.act(2 * D_MODEL), cfg.mod_spec, vec, vec, vec,
                  cfg.layer(*sq), cfg.layer(*sq), cfg.layer(*sq),
                  cfg.layer(*w1.shape[1:]), cfg.layer(*w2.shape[1:])],
        out_specs=cfg.act(D_MODEL),
        out_shape=jax.ShapeDtypeStruct(x.shape, F32),
        scratch_shapes=[pltpu.VMEM((cfg.rows, D_FF), BF16)],
        compiler_params=pltpu.CompilerParams(
            dimension_semantics=cfg.sem, vmem_limit_bytes=_vmem_limit(blocks, temps)),
        name="post_sample" if cfg.sample else "post_prompt",
    )(x, z, a, gates, mod_all, g_post1, g_pre2, g_post2, w_bc, w_ba, w_o, w1, w2)


def kernel(x_prompt, x_sample, c_prompt, c_sample, state_conv, cache_k, cache_v, w_ada, b_ada,
           g_pre1, w_in, conv_w, w_br_conv, w_br_attn, w_o, sinks, g_post1, g_pre2, w_ff1, w_ff2,
           g_post2, rel_table):
    depth = w_ada.shape[0]
    batch, seq, _ = x_prompt.shape
    n_seq, n_new, _ = x_sample.shape
    buf = cache_k.shape[2]

    pad = (-(n_seq + batch)) % V7X_SUBLANES
    c_all = jnp.concatenate([c_sample, c_prompt, jnp.zeros((pad, D_MODEL), F32)], axis=0)
    mod_all = _ada(c_all, w_ada, b_ada)

    dist_p = (jnp.arange(BLOCK)[:, None] + BLOCK) - jnp.arange(2 * BLOCK)[None, :]
    bucket_p = _rel_bucket(dist_p)
    upper = jnp.arange(BLOCK)[None, :] > jnp.arange(BLOCK)[:, None]
    bucket_merged = jnp.where(upper, bucket_p[:, :BLOCK], bucket_p[:, BLOCK:])
    bias_p = _bias_table(rel_table, bucket_merged.T)
    dist_s = (buf + jnp.arange(n_new))[:, None] - jnp.arange(buf + n_new)[None, :]
    bias_s = _bias_table(rel_table, _rel_bucket(dist_s))
    bias_s = (bias_s.reshape(N_KV_HEADS, GROUP, n_new, buf + n_new)
              .transpose(0, 2, 1, 3).reshape(N_KV_HEADS, n_new * GROUP, buf + n_new))
    bias_s = jnp.pad(bias_s, ((0, 0), (0, 0), (0, buf - n_new)))

    to_bf16 = lambda w: w.astype(BF16)
    w_in_b, w_bc_b, w_ba_b, w_o_b, w1_b, w2_b = map(
        to_bf16, (w_in, w_br_conv, w_br_attn, w_o, w_ff1, w_ff2))
    vec = lambda g: g.reshape(depth, 1, D_MODEL)
    g_pre1, g_post1, g_pre2, g_post2 = map(vec, (g_pre1, g_post1, g_pre2, g_post2))
    pre_s = state_conv.transpose(0, 2, 1, 3)
    cache_k = cache_k.transpose(0, 1, 3, 4, 2)
    cache_v = cache_v.transpose(0, 1, 3, 4, 2)

    xp = x_prompt
    xs = x_sample.transpose(1, 0, 2)
    conv_p, k_p, v_p, conv_s = [], [], [], []
    windows = None
    win = min(WINDOW, seq)
    for l in range(depth):
        z, attn, kv, gates, tail = _in_proj(xp, mod_all, g_pre1, w_in_b, conv_w, l, n_seq,
                                            attn=(sinks, bias_p))
        xp = _post(xp, z, attn, gates, mod_all, g_post1, g_pre2, g_post2, w_bc_b, w_ba_b, w_o_b,
                   w1_b, w2_b, l, n_seq, False)
        conv_p.append(tail[:, V7X_SUBLANES - (CONV_WIDTH - 1):])
        k_p.append(kv[:, seq - win:, :KV_WIDTH].reshape(batch, win, N_KV_HEADS, HEAD_DIM))
        v_p.append(kv[:, seq - win:, KV_WIDTH:].reshape(batch, win, N_KV_HEADS, HEAD_DIM))

        z, q, kv, gates, tail = _in_proj(xs, mod_all, g_pre1, w_in_b, conv_w, l, n_seq, pre_s)
        q_s = (q.reshape(n_new, n_seq, N_KV_HEADS, GROUP, HEAD_DIM)
               .transpose(1, 2, 0, 3, 4).reshape(n_seq, N_KV_HEADS, n_new * GROUP, HEAD_DIM))
        kv_s = kv.transpose(1, 0, 2)
        sink_rows = jnp.tile(sinks[l].reshape(N_KV_HEADS, 1, GROUP),
                             (1, n_new, 1)).reshape(N_KV_HEADS, n_new * GROUP, 1)
        o, *windows = _sample_attn(q_s, cache_k, cache_v, kv_s[:, :, :KV_WIDTH],
                                   kv_s[:, :, KV_WIDTH:], bias_s, sink_rows, l, windows)
        attn = (o.reshape(n_seq, N_KV_HEADS, n_new, GROUP, HEAD_DIM)
                .transpose(2, 0, 1, 3, 4).reshape(n_new, n_seq, ATTN_WIDTH))
        xs = _post(xs, z, attn, gates, mod_all, g_post1, g_pre2, g_post2, w_bc_b, w_ba_b, w_o_b,
                   w1_b, w2_b, l, n_seq, True)
        conv_s.append(tail.transpose(1, 0, 2))

    k_s, v_s = (w.transpose(0, 1, 4, 2, 3) for w in windows)
    return (xp, xs.transpose(1, 0, 2), jnp.stack(conv_p), jnp.stack(k_p), jnp.stack(v_p),
            jnp.stack(conv_s), k_s, v_s)
```

```python
import functools
import math
from typing import Any, Callable, NamedTuple

import jax
import jax.numpy as jnp
from jax import lax
from jax.experimental import pallas as pl
from jax.experimental.pallas import tpu as pltpu

D_MODEL = 1024
N_HEADS = 16
N_KV_HEADS = 2
HEAD_DIM = 64
GROUP = N_HEADS // N_KV_HEADS
ATTN_WIDTH = N_HEADS * HEAD_DIM
KV_WIDTH = N_KV_HEADS * HEAD_DIM
CONV_DIM = D_MODEL
CONV_WIDTH = 3
WINDOW = 128
BLOCK = 128
N_BUCKETS = 32
MAX_DISTANCE = 128
D_FF = 4 * D_MODEL
N_MOD = 6
RMS_EPS = 1e-6
NEG_INF = -1e30
PROJ_COLS = 3 * CONV_DIM + ATTN_WIDTH + 2 * KV_WIDTH + 2 * D_MODEL
Q_OFF = 3 * CONV_DIM
KV_OFF = Q_OFF + ATTN_WIDTH
GATE_OFF = KV_OFF + 2 * KV_WIDTH

V7X_SUBLANES = 8
V7X_VMEM_BYTES = 64 * 1024 * 1024

PROMPT_TILE = 512
SAMPLE_POS_TILE = 4
SAMPLE_SEQ_TILE = 16
CHUNK = 512
MLP_STREAM_CHUNK = 512

F32 = jnp.float32
BF16 = jnp.bfloat16


def _vmem_limit(block_bytes, temp_bytes):
    return int(min(2 * block_bytes + temp_bytes, V7X_VMEM_BYTES - 4 * 1024 * 1024))


def _nbytes(shape, dtype):
    return math.prod(shape) * jnp.dtype(dtype).itemsize


def _rms(x):
    return x * lax.rsqrt(jnp.mean(x * x, axis=-1, keepdims=True) + RMS_EPS)


def _dot(a, b):
    return jnp.dot(a, b, preferred_element_type=F32)


def _flat(a):
    return a.reshape(-1, a.shape[-1])


def _ada_kernel(c_ref, w_ref, b_ref, o_ref):
    c = c_ref[...]
    s = c * (1.0 / (1.0 + jnp.exp(-c)))
    o_ref[...] = _dot(s.astype(BF16), w_ref[...].astype(BF16)) + b_ref[...]


def _ada(c_all, w_ada, b_ada):
    depth = w_ada.shape[0]
    rows = c_all.shape[0]
    blocks = (_nbytes((rows, D_MODEL), F32) * 2 + _nbytes((D_MODEL, D_MODEL), F32))
    return pl.pallas_call(
        _ada_kernel,
        grid=(depth, N_MOD),
        in_specs=[
            pl.BlockSpec((rows, D_MODEL), lambda l, j: (0, 0)),
            pl.BlockSpec((None, D_MODEL, D_MODEL), lambda l, j: (l, 0, j)),
            pl.BlockSpec((None, None, 1, D_MODEL), lambda l, j: (l, j, 0, 0)),
        ],
        out_specs=pl.BlockSpec((None, None, rows, D_MODEL), lambda l, j: (l, j, 0, 0)),
        out_shape=jax.ShapeDtypeStruct((depth, N_MOD, rows, D_MODEL), F32),
        compiler_params=pltpu.CompilerParams(
            dimension_semantics=("arbitrary", "arbitrary"),
            vmem_limit_bytes=_vmem_limit(blocks, _nbytes((D_MODEL, D_MODEL), BF16) * 2)),
        name="ada_mod",
    )(c_all, w_ada, b_ada.reshape(depth, N_MOD, 1, D_MODEL))


def _bias_kernel(tab_ref, bucket_ref, o_ref):
    bucket = bucket_ref[...]
    for h in range(N_HEADS):
        acc = jnp.zeros(bucket.shape, F32)
        for b in range(N_BUCKETS):
            acc = jnp.where(bucket == b, tab_ref[b * N_HEADS + h], acc)
        o_ref[h] = acc


def _bias_table(rel_table, bucket):
    return pl.pallas_call(
        _bias_kernel,
        in_specs=[
            pl.BlockSpec(memory_space=pltpu.SMEM),
            pl.BlockSpec(bucket.shape, lambda: (0, 0)),
        ],
        out_specs=pl.BlockSpec((N_HEADS,) + bucket.shape, lambda: (0, 0, 0)),
        out_shape=jax.ShapeDtypeStruct((N_HEADS,) + bucket.shape, F32),
        name="bias_table",
    )(rel_table.reshape(-1), bucket)


def _rel_bucket(dist):
    n = jnp.maximum(dist, 0)
    max_exact = N_BUCKETS // 2
    nf = jnp.maximum(n, 1).astype(F32)
    large = max_exact + (jnp.log(nf / max_exact) / math.log(MAX_DISTANCE / max_exact)
                         * (N_BUCKETS - max_exact)).astype(jnp.int32)
    large = jnp.minimum(large, N_BUCKETS - 1)
    return jnp.where(n < max_exact, n, large)


class _Dense(NamedTuple):
    sample: bool
    grid: tuple
    rows: int
    act: Callable[[int], Any]
    layer: Callable[..., Any]
    mod_spec: Any
    sem: tuple


def _dense(x, layer_idx, n_seq, sample):
    lead, mid = x.shape[0], x.shape[1]
    if sample:
        grid = (lead // SAMPLE_POS_TILE,)
        act = lambda w: pl.BlockSpec((SAMPLE_POS_TILE, mid, w), lambda i: (i, 0, 0))
        layer = lambda *shape: pl.BlockSpec((None,) + shape,
                                            lambda i: (layer_idx,) + (0,) * len(shape),
                                            pipeline_mode=pl.Buffered(1))
        mod_spec = pl.BlockSpec((None, N_MOD, n_seq, D_MODEL), lambda i: (layer_idx, 0, 0, 0))
        return _Dense(True, grid, SAMPLE_POS_TILE * mid, act, layer, mod_spec, ("arbitrary",))
    assert n_seq % V7X_SUBLANES == 0 and lead <= V7X_SUBLANES
    grid = (lead, mid // PROMPT_TILE)
    act = lambda w: pl.BlockSpec((None, PROMPT_TILE, w), lambda b, t: (b, t, 0))
    layer = lambda *shape: pl.BlockSpec((None,) + shape,
                                        lambda b, t: (layer_idx,) + (0,) * len(shape),
                                        pipeline_mode=pl.Buffered(1))
    mod_spec = pl.BlockSpec((None, N_MOD, V7X_SUBLANES, D_MODEL),
                            lambda b, t: (layer_idx, 0, n_seq // V7X_SUBLANES, 0))
    return _Dense(False, grid, PROMPT_TILE, act, layer, mod_spec, ("arbitrary", "arbitrary"))


def _mod(mod_ref, j, sample):
    if sample:
        return mod_ref[j]
    return mod_ref[j, pl.ds(pl.program_id(0), 1), :]


def _sink_softmax(s, sink):
    m = jnp.maximum(jnp.max(s, axis=-1, keepdims=True), sink)
    e = jnp.exp(s - m)
    den = jnp.sum(e, axis=-1, keepdims=True) + jnp.exp(sink - m)
    return e * (1.0 / den)


def _stage_keys(kv, kd_ref, vt_ref, first_tile):
    n_blk = kv.shape[0] // BLOCK

    @pl.when(first_tile)
    def _():
        kd_ref[:, 0:BLOCK] = jnp.zeros((N_KV_HEADS, BLOCK, 2 * HEAD_DIM), BF16)
        vt_ref[0] = jnp.zeros((KV_WIDTH, BLOCK), BF16)

    @pl.when(jnp.logical_not(first_tile))
    def _():
        kd_ref[:, 0:BLOCK] = kd_ref[:, n_blk * BLOCK:]
        vt_ref[0] = vt_ref[n_blk]

    for g in range(N_KV_HEADS):
        k_g = kv[:, g * HEAD_DIM:(g + 1) * HEAD_DIM]
        kd_ref[g, BLOCK:] = jnp.concatenate([k_g, k_g], axis=1).astype(BF16)
    v_t = kv[:, KV_WIDTH:].T.astype(BF16)
    for blk in range(n_blk):
        vt_ref[blk + 1] = v_t[:, blk * BLOCK:(blk + 1) * BLOCK]


def _attention_phases(q_ref, kd_ref, vt_ref, s_ref, bias_ref, sink_ref, o_ref, first_tile):
    kj = lax.broadcasted_iota(jnp.int32, (BLOCK, BLOCK), 0)
    qi = lax.broadcasted_iota(jnp.int32, (BLOCK, BLOCK), 1)
    upper = kj > qi
    low_lanes = lax.broadcasted_iota(jnp.int32, (BLOCK, 2 * HEAD_DIM), 1) < HEAD_DIM

    def score_phase(c):
        q = q_ref[c * BLOCK:(c + 1) * BLOCK, :]
        for pair in range(N_HEADS // 2):
            g = (2 * pair) // GROUP
            q_pair = q[:, pair * 2 * HEAD_DIM:(pair + 1) * 2 * HEAD_DIM]
            zero = jnp.zeros_like(q_pair)
            q_both = jnp.concatenate([jnp.where(low_lanes, q_pair, zero),
                                      jnp.where(low_lanes, zero, q_pair)], axis=0)
            s_both = lax.dot_general(kd_ref[g, c * BLOCK:(c + 2) * BLOCK, :], q_both,
                                     (((1,), (1,)), ((), ())), preferred_element_type=F32)
            for par in range(2):
                h = 2 * pair + par
                s_h = s_both[:, par * BLOCK:(par + 1) * BLOCK]
                s = jnp.where(upper, s_h[:BLOCK], s_h[BLOCK:]) + bias_ref[h]
                if c == 0:
                    s = jnp.where(upper & first_tile, NEG_INF, s)
                s_ref[c, h] = s

    def value_phase(c, pair):
        g = (2 * pair) // GROUP
        v_cat = jnp.concatenate([vt_ref[c, g * HEAD_DIM:(g + 1) * HEAD_DIM],
                                 vt_ref[c + 1, g * HEAD_DIM:(g + 1) * HEAD_DIM]], axis=1)
        weights, scales = [], []
        for h in (2 * pair, 2 * pair + 1):
            s = s_ref[c, h]
            sink = sink_ref[h]
            m = jnp.maximum(jnp.max(s, axis=0, keepdims=True), sink)
            e = jnp.exp(s - m)
            den = jnp.sum(e, axis=0, keepdims=True) + jnp.exp(sink - m)
            weights.append(jnp.concatenate([jnp.where(upper, e, 0.0), jnp.where(upper, 0.0, e)],
                                           axis=0).astype(BF16))
            scales.append(1.0 / den)
        o_t = _dot(v_cat, jnp.concatenate(weights, axis=1))
        o_t = jnp.concatenate([o_t[:, :BLOCK] * scales[0], o_t[:, BLOCK:] * scales[1]], axis=0)
        o_ref[c * BLOCK:(c + 1) * BLOCK, pair * 2 * HEAD_DIM:(pair + 1) * 2 * HEAD_DIM] = (
            o_t.T.astype(BF16))

    return score_phase, value_phase


def _in_proj_kernel(*refs, sample):
    if sample:
        (x_ref, mod_ref, g_ref, w_ref, cw_ref, pre_ref,
         z_ref, q_ref, kv_ref, gate_ref, tail_ref, carry_ref) = refs
    else:
        (x_ref, mod_ref, g_ref, w_ref, cw_ref, sink_ref, bias_ref,
         z_ref, a_ref, kv_ref, gate_ref, tail_ref,
         carry_ref, q_ref, kd_ref, vt_ref, s_ref) = refs
    step = pl.program_id(0) if sample else pl.program_id(1)

    @pl.when(step == 0)
    def _():
        if sample:
            carry_ref[...] = pre_ref[...]
        else:
            carry_ref[...] = jnp.zeros(carry_ref.shape, F32)

    x = x_ref[...]
    h = (_rms(x) * g_ref[...]) * (1.0 + _mod(mod_ref, 1, sample)) + _mod(mod_ref, 0, sample)
    hb = _flat(h).astype(BF16)
    rows = hb.shape[0]

    def store(ref, lo, val):
        ref[..., lo:lo + val.shape[-1]] = val.reshape(ref.shape[:-1] + (val.shape[-1],))

    def conv_chunk(lo):
        b_g = _dot(hb, w_ref[:, lo:lo + CHUNK])
        c_g = _dot(hb, w_ref[:, CONV_DIM + lo:CONV_DIM + lo + CHUNK])
        x_c = _dot(hb, w_ref[:, 2 * CONV_DIM + lo:2 * CONV_DIM + lo + CHUNK])
        u = c_g * x_c
        if sample:
            pos = x_ref.shape[0]
            u3 = u.reshape(pos, -1, CHUNK)
            ext = jnp.concatenate([carry_ref[:, :, lo:lo + CHUNK], u3], axis=0)
            u2 = _flat(ext[0:pos])
            u1 = _flat(ext[1:pos + 1])
            new_tail = u3[pos - (CONV_WIDTH - 1):]
            carry_ref[:, :, lo:lo + CHUNK] = new_tail
            tail_ref[:, :, lo:lo + CHUNK] = new_tail
        else:
            prev = carry_ref[:, lo:lo + CHUNK]
            p1 = prev[V7X_SUBLANES - 1:V7X_SUBLANES]
            p2 = prev[V7X_SUBLANES - 2:V7X_SUBLANES - 1]
            r = lax.broadcasted_iota(jnp.int32, u.shape, 0)
            u1 = jnp.where(r == 0, p1, pltpu.roll(u, 1, 0))
            u2 = jnp.where(r == 0, p2, jnp.where(r == 1, p1, pltpu.roll(u, 2, 0)))
            new_tail = u[rows - V7X_SUBLANES:]
            carry_ref[:, lo:lo + CHUNK] = new_tail
            tail_ref[:, lo:lo + CHUNK] = new_tail
        conv = (cw_ref[0:1, lo:lo + CHUNK] * u2 + cw_ref[1:2, lo:lo + CHUNK] * u1
                + cw_ref[2:3, lo:lo + CHUNK] * u)
        store(z_ref, lo, (b_g * conv).astype(BF16))

    def q_chunk(lo):
        q = _dot(hb, w_ref[:, Q_OFF + lo:Q_OFF + lo + CHUNK])
        store(q_ref, lo, (q * (HEAD_DIM ** -0.5)).astype(BF16))

    def gate_chunk(lo):
        g = _dot(hb, w_ref[:, GATE_OFF + lo:GATE_OFF + lo + CHUNK])
        store(gate_ref, lo, (1.0 / (1.0 + jnp.exp(-g))).astype(BF16))

    kv = _dot(hb, w_ref[:, KV_OFF:KV_OFF + 2 * KV_WIDTH])
    store(kv_ref, 0, kv)
    for lo in range(0, ATTN_WIDTH, CHUNK):
        q_chunk(lo)
    dense = ([functools.partial(conv_chunk, lo) for lo in range(0, CONV_DIM, CHUNK)]
             + [functools.partial(gate_chunk, lo) for lo in range(0, 2 * D_MODEL, CHUNK)])
    if sample:
        for work in dense:
            work()
        return

    first_tile = step == 0
    _stage_keys(kv, kd_ref, vt_ref, first_tile)
    score_phase, value_phase = _attention_phases(q_ref, kd_ref, vt_ref, s_ref, bias_ref,
                                                 sink_ref, a_ref, first_tile)
    n_blk = rows // BLOCK
    attention = [functools.partial(score_phase, 0)]
    for c in range(n_blk):
        if c + 1 < n_blk:
            attention.append(functools.partial(score_phase, c + 1))
        attention += [functools.partial(value_phase, c, pair) for pair in range(N_HEADS // 2)]
    done = 0
    for i, work in enumerate(attention):
        work()
        while done < len(dense) and done * len(attention) < (i + 1) * len(dense):
            dense[done]()
            done += 1


def _in_proj(x, mod_all, g_pre1, w_in, conv_w, layer_idx, n_seq, pre=None, attn=None):
    cfg = _dense(x, layer_idx, n_seq, pre is not None)
    lead, mid = x.shape[0], x.shape[1]
    widths = (CONV_DIM, ATTN_WIDTH, 2 * KV_WIDTH, 2 * D_MODEL)
    dtypes = (BF16, BF16, F32, BF16)
    blocks = (_nbytes((cfg.rows, D_MODEL), F32) + _nbytes(w_in.shape[1:], BF16)
              + sum(_nbytes((cfg.rows, w), dt) for w, dt in zip(widths, dtypes)))
    temps = _nbytes((cfg.rows, D_MODEL), F32) * 2 + _nbytes((cfg.rows, CHUNK), F32) * 10
    if cfg.sample:
        tail_shape = (CONV_WIDTH - 1, mid, CONV_DIM)
        tail_spec = pl.BlockSpec(tail_shape, lambda i: (0, 0, 0))
        extra_in = [pre]
        extra_specs = [cfg.layer(*tail_shape)]
        scratch = [pltpu.VMEM(tail_shape, F32)]
    else:
        sinks, bias = attn
        per_tile = cfg.rows // BLOCK
        tail_shape = (lead, V7X_SUBLANES, CONV_DIM)
        tail_spec = pl.BlockSpec((None, V7X_SUBLANES, CONV_DIM), lambda b, t: (b, 0, 0))
        extra_in = [sinks[layer_idx], bias]
        extra_specs = [pl.BlockSpec(memory_space=pltpu.SMEM),
                       pl.BlockSpec(bias.shape, lambda b, t: (0, 0, 0),
                                    pipeline_mode=pl.Buffered(1))]
        scratch = [pltpu.VMEM((V7X_SUBLANES, CONV_DIM), F32),
                   pltpu.VMEM((cfg.rows, ATTN_WIDTH), BF16),
                   pltpu.VMEM((N_KV_HEADS, cfg.rows + BLOCK, 2 * HEAD_DIM), BF16),
                   pltpu.VMEM((per_tile + 1, KV_WIDTH, BLOCK), BF16),
                   pltpu.VMEM((per_tile, N_HEADS, BLOCK, BLOCK), F32)]
        temps += (_nbytes(bias.shape, F32) + _nbytes((cfg.rows, ATTN_WIDTH), BF16)
                  + _nbytes((per_tile, N_HEADS, BLOCK, BLOCK), F32)
                  + _nbytes((BLOCK, 2 * BLOCK), F32) * 16 + _nbytes((BLOCK, ATTN_WIDTH), F32) * 4)

    out_shape = [jax.ShapeDtypeStruct(x.shape[:-1] + (w,), dt) for w, dt in zip(widths, dtypes)]
    out_shape.append(jax.ShapeDtypeStruct(tail_shape, F32))
    return pl.pallas_call(
        functools.partial(_in_proj_kernel, sample=cfg.sample),
        grid=cfg.grid,
        in_specs=[cfg.act(D_MODEL), cfg.mod_spec, cfg.layer(1, D_MODEL),
                  cfg.layer(*w_in.shape[1:]), cfg.layer(*conv_w.shape[1:])] + extra_specs,
        out_specs=[cfg.act(w) for w in widths] + [tail_spec],
        out_shape=out_shape,
        scratch_shapes=scratch,
        compiler_params=pltpu.CompilerParams(
            dimension_semantics=cfg.sem, vmem_limit_bytes=_vmem_limit(blocks, temps)),
        name="in_proj_sample" if cfg.sample else "in_proj_attn_prompt",
    )(x, mod_all, g_pre1, w_in, conv_w, *extra_in)


def _sample_attn_kernel(q_ref, kc_ref, vc_ref, kn_ref, vn_ref, bias_ref, sink_ref, *rest):
    o_ref, ko_ref, vo_ref = rest[-3:]
    n_seq = q_ref.shape[0]
    buf = kc_ref.shape[-1]
    new = kn_ref.shape[1]
    rows = new * GROUP
    pos = lax.broadcasted_iota(jnp.int32, (rows, 2 * buf), 0) // GROUP
    kj = lax.broadcasted_iota(jnp.int32, (rows, 2 * buf), 1)
    dist = buf + pos - kj
    mask = (dist >= 0) & (dist < WINDOW) & (kj < buf + new)
    lane = lax.broadcasted_iota(jnp.int32, (n_seq, HEAD_DIM, buf), 2)
    pad = jnp.zeros((n_seq, HEAD_DIM, buf - new), F32)

    def extended(cache_t, fresh, out_ref, g):
        fresh_t = jnp.concatenate([jnp.swapaxes(fresh, 1, 2), pad], axis=2)
        out_ref[:, g] = pltpu.roll(jnp.where(lane < new, fresh_t, cache_t), buf - new, 2)
        return jnp.concatenate([cache_t, fresh_t], axis=2).astype(BF16)

    for g in range(N_KV_HEADS):
        lanes = slice(g * HEAD_DIM, (g + 1) * HEAD_DIM)
        k_ext = extended(kc_ref[:, g], kn_ref[:, :, lanes], ko_ref, g)
        v_ext = extended(vc_ref[:, g], vn_ref[:, :, lanes], vo_ref, g)
        s = jnp.einsum('nqd,ndk->nqk', q_ref[:, g], k_ext, preferred_element_type=F32)
        p = _sink_softmax(jnp.where(mask[None], s + bias_ref[g][None], NEG_INF),
                          sink_ref[g][None])
        o = jnp.einsum('nqk,ndk->nqd', p.astype(BF16), v_ext, preferred_element_type=F32)
        o_ref[:, g] = o.astype(BF16)


def _sample_attn(q, k_cache, v_cache, k_new, v_new, bias, sink_rows, layer_idx, windows):
    _, n, _, _, buf = k_cache.shape
    new = k_new.shape[1]
    rows = new * GROUP
    nt = SAMPLE_SEQ_TILE
    seq3 = lambda a, b: pl.BlockSpec((nt, a, b), lambda i: (i, 0, 0))
    cache = pl.BlockSpec((None, nt, N_KV_HEADS, HEAD_DIM, buf),
                         lambda i: (layer_idx, i, 0, 0, 0))
    q_spec = pl.BlockSpec((nt, N_KV_HEADS, rows, HEAD_DIM), lambda i: (i, 0, 0, 0))
    in_specs = [q_spec, cache, cache, seq3(new, KV_WIDTH), seq3(new, KV_WIDTH),
                pl.BlockSpec(bias.shape, lambda i: (0, 0, 0)),
                pl.BlockSpec(sink_rows.shape, lambda i: (0, 0, 0))]
    args = [q, k_cache, v_cache, k_new, v_new, bias, sink_rows]
    aliases = {}
    if windows is not None:
        aliases = {len(args): 1, len(args) + 1: 2}
        in_specs += [pl.BlockSpec(memory_space=pl.ANY)] * 2
        args += list(windows)
    blocks = (_nbytes((nt, N_KV_HEADS, HEAD_DIM, buf), F32) * 4
              + _nbytes((nt, new, KV_WIDTH), F32) * 2
              + _nbytes((nt, N_KV_HEADS, rows, 2 * HEAD_DIM), BF16) * 2 + _nbytes(bias.shape, F32))
    temps = _nbytes((rows, 2 * buf), F32) * 8 * nt
    return pl.pallas_call(
        _sample_attn_kernel,
        grid=(n // nt,),
        in_specs=in_specs,
        out_specs=[q_spec, cache, cache],
        out_shape=[jax.ShapeDtypeStruct(q.shape, BF16),
                   jax.ShapeDtypeStruct(k_cache.shape, F32),
                   jax.ShapeDtypeStruct(v_cache.shape, F32)],
        input_output_aliases=aliases,
        compiler_params=pltpu.CompilerParams(
            dimension_semantics=("arbitrary",), vmem_limit_bytes=_vmem_limit(blocks, temps)),
        name="attn_sample",
    )(*args)


def _merge_residual(x_ref, z_ref, a_ref, gate_ref, mod_ref, gpost1_ref, wbc_ref, wba_ref, wo_ref,
                    sample):
    y_conv = _dot(_flat(z_ref[...]), wbc_ref[...])
    y_attn = _dot(_flat(a_ref[...]), wba_ref[...])
    gates = _flat(gate_ref[...])
    merged = gates[:, :D_MODEL] * y_conv + gates[:, D_MODEL:] * y_attn
    mixed = _dot(merged.astype(BF16), wo_ref[...])
    r = (_rms(mixed) * gpost1_ref[...]).reshape(x_ref.shape)
    return x_ref[...] + _mod(mod_ref, 2, sample) * r


def _mlp_input(x1, mod_ref, gpre2_ref, sample):
    h = (_rms(x1) * gpre2_ref[...]) * (1.0 + _mod(mod_ref, 4, sample)) + _mod(mod_ref, 3, sample)
    return _flat(h).astype(BF16)


def _mlp_residual(x1, ff, mod_ref, gpost2_ref, sample):
    return x1 + _mod(mod_ref, 5, sample) * (_rms(ff) * gpost2_ref[...]).reshape(x1.shape)


def _post_kernel(x_ref, z_ref, a_ref, gate_ref, mod_ref, gpost1_ref, gpre2_ref, gpost2_ref,
                 wbc_ref, wba_ref, wo_ref, w1_ref, w2_ref, o_ref, hid_ref, *, sample):
    x1 = _merge_residual(x_ref, z_ref, a_ref, gate_ref, mod_ref, gpost1_ref,
                         wbc_ref, wba_ref, wo_ref, sample)
    hb = _mlp_input(x1, mod_ref, gpre2_ref, sample)
    for lo in range(0, D_FF, 2 * CHUNK):
        act = jnp.maximum(_dot(hb, w1_ref[:, lo:lo + 2 * CHUNK]), 0.0)
        hid_ref[:, lo:lo + 2 * CHUNK] = (act * act).astype(BF16)
    ff = _dot(hid_ref[...], w2_ref[...])
    o_ref[...] = _mlp_residual(x1, ff, mod_ref, gpost2_ref, sample)


def _post(x, z, a, gates, mod_all, g_post1, g_pre2, g_post2, w_bc, w_ba, w_o, w1, w2,
          layer_idx, mlp_layer_idx, n_seq):
    cfg = _dense(x, layer_idx, n_seq, False)
    mlp = _dense(x, mlp_layer_idx, n_seq, False)
    sq = (D_MODEL, D_MODEL)
    blocks = (_nbytes((cfg.rows, D_MODEL), F32) * 2 + _nbytes((cfg.rows, D_MODEL), BF16) * 2
              + _nbytes((cfg.rows, 2 * D_MODEL), gates.dtype))
    temps = (_nbytes(sq, BF16) * 3 + _nbytes(w1.shape[1:], BF16) * 2
             + _nbytes((cfg.rows, D_FF), BF16) + _nbytes((cfg.rows, 2 * CHUNK), F32) * 2
             + _nbytes((cfg.rows, D_MODEL), F32) * 5)
    vec = cfg.layer(1, D_MODEL)
    return pl.pallas_call(
        functools.partial(_post_kernel, sample=False),
        grid=cfg.grid,
        in_specs=[cfg.act(D_MODEL), cfg.act(CONV_DIM), cfg.act(ATTN_WIDTH),
                  cfg.act(2 * D_MODEL), cfg.mod_spec, vec, vec, vec,
                  cfg.layer(*sq), cfg.layer(*sq), cfg.layer(*sq),
                  mlp.layer(*w1.shape[1:]), mlp.layer(*w2.shape[1:])],
        out_specs=cfg.act(D_MODEL),
        out_shape=jax.ShapeDtypeStruct(x.shape, F32),
        scratch_shapes=[pltpu.VMEM((cfg.rows, D_FF), BF16)],
        compiler_params=pltpu.CompilerParams(
            dimension_semantics=cfg.sem, vmem_limit_bytes=_vmem_limit(blocks, temps)),
        name="post_prompt",
    )(x, z, a, gates, mod_all, g_post1, g_pre2, g_post2, w_bc, w_ba, w_o, w1, w2)


def _mix_kernel(x_ref, z_ref, a_ref, gate_ref, mod_ref, gpost1_ref, wbc_ref, wba_ref, wo_ref,
                o_ref):
    o_ref[...] = _merge_residual(x_ref, z_ref, a_ref, gate_ref, mod_ref, gpost1_ref,
                                 wbc_ref, wba_ref, wo_ref, True)


def _mix_sample(x, z, a, gates, mod_all, g_post1, w_bc, w_ba, w_o, layer_idx, n_seq):
    cfg = _dense(x, layer_idx, n_seq, True)
    sq = (D_MODEL, D_MODEL)
    blocks = (_nbytes((cfg.rows, D_MODEL), F32) * 2 + _nbytes((cfg.rows, D_MODEL), BF16) * 2
              + _nbytes((cfg.rows, 2 * D_MODEL), gates.dtype))
    temps = _nbytes(sq, BF16) * 3 + _nbytes((cfg.rows, D_MODEL), F32) * 6
    return pl.pallas_call(
        _mix_kernel,
        grid=cfg.grid,
        in_specs=[cfg.act(D_MODEL), cfg.act(CONV_DIM), cfg.act(ATTN_WIDTH),
                  cfg.act(2 * D_MODEL), cfg.mod_spec, cfg.layer(1, D_MODEL),
                  cfg.layer(*sq), cfg.layer(*sq), cfg.layer(*sq)],
        out_specs=cfg.act(D_MODEL),
        out_shape=jax.ShapeDtypeStruct(x.shape, F32),
        compiler_params=pltpu.CompilerParams(
            dimension_semantics=cfg.sem, vmem_limit_bytes=_vmem_limit(blocks, temps)),
        name="mix_sample",
    )(x, z, a, gates, mod_all, g_post1, w_bc, w_ba, w_o)


def _mlp_stream_kernel(x_ref, mod_ref, gpre2_ref, gpost2_ref, w1_ref, w2_ref,
                       o_ref, w1b_ref, w2b_ref, hb_ref, ff_ref):
    chunk = pl.program_id(0)

    @pl.when(chunk == 0)
    def _():
        hb_ref[...] = _mlp_input(x_ref[...], mod_ref, gpre2_ref, True)
        ff_ref[...] = jnp.zeros(ff_ref.shape, F32)

    w1 = w1_ref[...].astype(BF16)
    w2 = w2_ref[...].astype(BF16)
    w1b_ref[...] = w1
    w2b_ref[...] = w2
    act = jnp.maximum(_dot(hb_ref[...], w1), 0.0)
    ff_ref[...] += _dot((act * act).astype(BF16), w2)

    @pl.when(chunk == pl.num_programs(0) - 1)
    def _():
        o_ref[...] = _mlp_residual(x_ref[...], ff_ref[...], mod_ref, gpost2_ref, True)


def _mlp_stream_sample(x, mod_all, g_pre2, g_post2, w1, w2, layer_idx, n_seq):
    rows = x.shape[0] * x.shape[1]
    const = lambda shape, idx: pl.BlockSpec(shape, lambda c: idx, pipeline_mode=pl.Buffered(1))
    vec = const((None, 1, D_MODEL), (layer_idx, 0, 0))
    blocks = (_nbytes((D_MODEL, MLP_STREAM_CHUNK), F32) * 2
              + _nbytes((D_MODEL, MLP_STREAM_CHUNK), BF16) * 2 + _nbytes(x.shape, F32))
    temps = (_nbytes(x.shape, F32) * 4 + _nbytes((N_MOD, n_seq, D_MODEL), F32)
             + _nbytes((rows, D_MODEL), BF16) + _nbytes((rows, MLP_STREAM_CHUNK), F32) * 3)
    return pl.pallas_call(
        _mlp_stream_kernel,
        grid=(D_FF // MLP_STREAM_CHUNK,),
        in_specs=[const(x.shape, (0, 0, 0)),
                  const((None, N_MOD, n_seq, D_MODEL), (layer_idx, 0, 0, 0)), vec, vec,
                  pl.BlockSpec((None, D_MODEL, MLP_STREAM_CHUNK), lambda c: (layer_idx, 0, c)),
                  pl.BlockSpec((None, MLP_STREAM_CHUNK, D_MODEL), lambda c: (layer_idx, c, 0))],
        out_specs=[pl.BlockSpec(x.shape, lambda c: (0, 0, 0)),
                   pl.BlockSpec((None, D_MODEL, MLP_STREAM_CHUNK), lambda c: (0, 0, c)),
                   pl.BlockSpec((None, MLP_STREAM_CHUNK, D_MODEL), lambda c: (0, c, 0))],
        out_shape=[jax.ShapeDtypeStruct(x.shape, F32),
                   jax.ShapeDtypeStruct((1, D_MODEL, D_FF), BF16),
                   jax.ShapeDtypeStruct((1, D_FF, D_MODEL), BF16)],
        scratch_shapes=[pltpu.VMEM((rows, D_MODEL), BF16), pltpu.VMEM((rows, D_MODEL), F32)],
        compiler_params=pltpu.CompilerParams(
            dimension_semantics=("arbitrary",), vmem_limit_bytes=_vmem_limit(blocks, temps)),
        name="mlp_stream_sample",
    )(x, mod_all, g_pre2, g_post2, w1, w2)


def kernel(x_prompt, x_sample, c_prompt, c_sample, state_conv, cache_k, cache_v, w_ada, b_ada,
           g_pre1, w_in, conv_w, w_br_conv, w_br_attn, w_o, sinks, g_post1, g_pre2, w_ff1, w_ff2,
           g_post2, rel_table):
    depth = w_ada.shape[0]
    batch, seq, _ = x_prompt.shape
    n_seq, n_new, _ = x_sample.shape
    buf = cache_k.shape[2]

    pad = (-(n_seq + batch)) % V7X_SUBLANES
    c_all = jnp.concatenate([c_sample, c_prompt, jnp.zeros((pad, D_MODEL), F32)], axis=0)
    mod_all = _ada(c_all, w_ada, b_ada)

    dist_p = (jnp.arange(BLOCK)[:, None] + BLOCK) - jnp.arange(2 * BLOCK)[None, :]
    bucket_p = _rel_bucket(dist_p)
    upper = jnp.arange(BLOCK)[None, :] > jnp.arange(BLOCK)[:, None]
    bucket_merged = jnp.where(upper, bucket_p[:, :BLOCK], bucket_p[:, BLOCK:])
    bias_p = _bias_table(rel_table, bucket_merged.T)
    dist_s = (buf + jnp.arange(n_new))[:, None] - jnp.arange(buf + n_new)[None, :]
    bias_s = _bias_table(rel_table, _rel_bucket(dist_s))
    bias_s = (bias_s.reshape(N_KV_HEADS, GROUP, n_new, buf + n_new)
              .transpose(0, 2, 1, 3).reshape(N_KV_HEADS, n_new * GROUP, buf + n_new))
    bias_s = jnp.pad(bias_s, ((0, 0), (0, 0), (0, buf - n_new)))

    to_bf16 = lambda w: w.astype(BF16)
    w_in_b, w_bc_b, w_ba_b, w_o_b = map(to_bf16, (w_in, w_br_conv, w_br_attn, w_o))
    vec = lambda g: g.reshape(depth, 1, D_MODEL)
    g_pre1, g_post1, g_pre2, g_post2 = map(vec, (g_pre1, g_post1, g_pre2, g_post2))
    pre_s = state_conv.transpose(0, 2, 1, 3)
    cache_k = cache_k.transpose(0, 1, 3, 4, 2)
    cache_v = cache_v.transpose(0, 1, 3, 4, 2)

    xp = x_prompt
    xs = x_sample.transpose(1, 0, 2)
    conv_p, k_p, v_p, conv_s = [], [], [], []
    windows = None
    win = min(WINDOW, seq)
    for l in range(depth):
        z, q, kv, gates, tail = _in_proj(xs, mod_all, g_pre1, w_in_b, conv_w, l, n_seq, pre_s)
        q_s = (q.reshape(n_new, n_seq, N_KV_HEADS, GROUP, HEAD_DIM)
               .transpose(1, 2, 0, 3, 4).reshape(n_seq, N_KV_HEADS, n_new * GROUP, HEAD_DIM))
        kv_s = kv.transpose(1, 0, 2)
        sink_rows = jnp.tile(sinks[l].reshape(N_KV_HEADS, 1, GROUP),
                             (1, n_new, 1)).reshape(N_KV_HEADS, n_new * GROUP, 1)
        o, *windows = _sample_attn(q_s, cache_k, cache_v, kv_s[:, :, :KV_WIDTH],
                                   kv_s[:, :, KV_WIDTH:], bias_s, sink_rows, l, windows)
        attn = (o.reshape(n_seq, N_KV_HEADS, n_new, GROUP, HEAD_DIM)
                .transpose(2, 0, 1, 3, 4).reshape(n_new, n_seq, ATTN_WIDTH))
        x1 = _mix_sample(xs, z, attn, gates, mod_all, g_post1, w_bc_b, w_ba_b, w_o_b, l, n_seq)
        xs, w1_b, w2_b = _mlp_stream_sample(x1, mod_all, g_pre2, g_post2, w_ff1, w_ff2, l, n_seq)
        conv_s.append(tail.transpose(1, 0, 2))

        z, attn, kv, gates, tail = _in_proj(xp, mod_all, g_pre1, w_in_b, conv_w, l, n_seq,
                                            attn=(sinks, bias_p))
        xp = _post(xp, z, attn, gates, mod_all, g_post1, g_pre2, g_post2, w_bc_b, w_ba_b, w_o_b,
                   w1_b, w2_b, l, 0, n_seq)
        conv_p.append(tail[:, V7X_SUBLANES - (CONV_WIDTH - 1):])
        k_p.append(kv[:, seq - win:, :KV_WIDTH].reshape(batch, win, N_KV_HEADS, HEAD_DIM))
        v_p.append(kv[:, seq - win:, KV_WIDTH:].reshape(batch, win, N_KV_HEADS, HEAD_DIM))

    k_s, v_s = (w.transpose(0, 1, 4, 2, 3) for w in windows)
    return (xp, xs.transpose(1, 0, 2), jnp.stack(conv_p), jnp.stack(k_p), jnp.stack(v_p),
            jnp.stack(conv_s), k_s, v_s)
```

```python
import functools
import math
from typing import Any, Callable, NamedTuple

import jax
import jax.numpy as jnp
from jax import lax
from jax.experimental import pallas as pl
from jax.experimental.pallas import tpu as pltpu

D_MODEL = 1024
N_HEADS = 16
N_KV_HEADS = 2
HEAD_DIM = 64
GROUP = N_HEADS // N_KV_HEADS
ATTN_WIDTH = N_HEADS * HEAD_DIM
KV_WIDTH = N_KV_HEADS * HEAD_DIM
CONV_DIM = D_MODEL
CONV_WIDTH = 3
WINDOW = 128
BLOCK = 128
N_BUCKETS = 32
MAX_DISTANCE = 128
D_FF = 4 * D_MODEL
N_MOD = 6
RMS_EPS = 1e-6
NEG_INF = -1e30
PROJ_COLS = 3 * CONV_DIM + ATTN_WIDTH + 2 * KV_WIDTH + 2 * D_MODEL
Q_OFF = 3 * CONV_DIM
KV_OFF = Q_OFF + ATTN_WIDTH
GATE_OFF = KV_OFF + 2 * KV_WIDTH

V7X_SUBLANES = 8
V7X_VMEM_BYTES = 64 * 1024 * 1024

PROMPT_TILE = 512
SAMPLE_POS_TILE = 4
SAMPLE_SEQ_TILE = 16
CHUNK = 512
MLP_STREAM_CHUNK = 512
IN_STREAM_CHUNK = 256

F32 = jnp.float32
BF16 = jnp.bfloat16


def _vmem_limit(block_bytes, temp_bytes):
    return int(min(2 * block_bytes + temp_bytes, V7X_VMEM_BYTES - 4 * 1024 * 1024))


def _nbytes(shape, dtype):
    return math.prod(shape) * jnp.dtype(dtype).itemsize


def _rms(x):
    return x * lax.rsqrt(jnp.mean(x * x, axis=-1, keepdims=True) + RMS_EPS)


def _dot(a, b):
    return jnp.dot(a, b, preferred_element_type=F32)


def _flat(a):
    return a.reshape(-1, a.shape[-1])


def _ada_kernel(c_ref, w_ref, b_ref, o_ref):
    c = c_ref[...]
    s = c * (1.0 / (1.0 + jnp.exp(-c)))
    o_ref[...] = _dot(s.astype(BF16), w_ref[...].astype(BF16)) + b_ref[...]


def _ada(c_all, w_ada, b_ada):
    depth = w_ada.shape[0]
    rows = c_all.shape[0]
    blocks = (_nbytes((rows, D_MODEL), F32) * 2 + _nbytes((D_MODEL, D_MODEL), F32))
    return pl.pallas_call(
        _ada_kernel,
        grid=(depth, N_MOD),
        in_specs=[
            pl.BlockSpec((rows, D_MODEL), lambda l, j: (0, 0)),
            pl.BlockSpec((None, D_MODEL, D_MODEL), lambda l, j: (l, 0, j)),
            pl.BlockSpec((None, None, 1, D_MODEL), lambda l, j: (l, j, 0, 0)),
        ],
        out_specs=pl.BlockSpec((None, None, rows, D_MODEL), lambda l, j: (l, j, 0, 0)),
        out_shape=jax.ShapeDtypeStruct((depth, N_MOD, rows, D_MODEL), F32),
        compiler_params=pltpu.CompilerParams(
            dimension_semantics=("arbitrary", "arbitrary"),
            vmem_limit_bytes=_vmem_limit(blocks, _nbytes((D_MODEL, D_MODEL), BF16) * 2)),
        name="ada_mod",
    )(c_all, w_ada, b_ada.reshape(depth, N_MOD, 1, D_MODEL))


def _bias_kernel(tab_ref, bucket_ref, o_ref):
    bucket = bucket_ref[...]
    for h in range(N_HEADS):
        acc = jnp.zeros(bucket.shape, F32)
        for b in range(N_BUCKETS):
            acc = jnp.where(bucket == b, tab_ref[b * N_HEADS + h], acc)
        o_ref[h] = acc


def _bias_table(rel_table, bucket):
    return pl.pallas_call(
        _bias_kernel,
        in_specs=[
            pl.BlockSpec(memory_space=pltpu.SMEM),
            pl.BlockSpec(bucket.shape, lambda: (0, 0)),
        ],
        out_specs=pl.BlockSpec((N_HEADS,) + bucket.shape, lambda: (0, 0, 0)),
        out_shape=jax.ShapeDtypeStruct((N_HEADS,) + bucket.shape, F32),
        name="bias_table",
    )(rel_table.reshape(-1), bucket)


def _rel_bucket(dist):
    n = jnp.maximum(dist, 0)
    max_exact = N_BUCKETS // 2
    nf = jnp.maximum(n, 1).astype(F32)
    large = max_exact + (jnp.log(nf / max_exact) / math.log(MAX_DISTANCE / max_exact)
                         * (N_BUCKETS - max_exact)).astype(jnp.int32)
    large = jnp.minimum(large, N_BUCKETS - 1)
    return jnp.where(n < max_exact, n, large)


class _Dense(NamedTuple):
    sample: bool
    grid: tuple
    rows: int
    act: Callable[[int], Any]
    layer: Callable[..., Any]
    mod_spec: Any
    sem: tuple


def _dense(x, layer_idx, n_seq, sample):
    lead, mid = x.shape[0], x.shape[1]
    if sample:
        grid = (lead // SAMPLE_POS_TILE,)
        act = lambda w: pl.BlockSpec((SAMPLE_POS_TILE, mid, w), lambda i: (i, 0, 0))
        layer = lambda *shape: pl.BlockSpec((None,) + shape,
                                            lambda i: (layer_idx,) + (0,) * len(shape),
                                            pipeline_mode=pl.Buffered(1))
        mod_spec = pl.BlockSpec((None, N_MOD, n_seq, D_MODEL), lambda i: (layer_idx, 0, 0, 0))
        return _Dense(True, grid, SAMPLE_POS_TILE * mid, act, layer, mod_spec, ("arbitrary",))
    assert n_seq % V7X_SUBLANES == 0 and lead <= V7X_SUBLANES
    grid = (lead, mid // PROMPT_TILE)
    act = lambda w: pl.BlockSpec((None, PROMPT_TILE, w), lambda b, t: (b, t, 0))
    layer = lambda *shape: pl.BlockSpec((None,) + shape,
                                        lambda b, t: (layer_idx,) + (0,) * len(shape),
                                        pipeline_mode=pl.Buffered(1))
    mod_spec = pl.BlockSpec((None, N_MOD, V7X_SUBLANES, D_MODEL),
                            lambda b, t: (layer_idx, 0, n_seq // V7X_SUBLANES, 0))
    return _Dense(False, grid, PROMPT_TILE, act, layer, mod_spec, ("arbitrary", "arbitrary"))


def _mod(mod_ref, j, sample):
    if sample:
        return mod_ref[j]
    return mod_ref[j, pl.ds(pl.program_id(0), 1), :]


def _sink_softmax(s, sink):
    m = jnp.maximum(jnp.max(s, axis=-1, keepdims=True), sink)
    e = jnp.exp(s - m)
    den = jnp.sum(e, axis=-1, keepdims=True) + jnp.exp(sink - m)
    return e * (1.0 / den)


def _stage_keys(kv, kd_ref, vt_ref, first_tile):
    n_blk = kv.shape[0] // BLOCK

    @pl.when(first_tile)
    def _():
        kd_ref[:, 0:BLOCK] = jnp.zeros((N_KV_HEADS, BLOCK, 2 * HEAD_DIM), BF16)
        vt_ref[0] = jnp.zeros((KV_WIDTH, BLOCK), BF16)

    @pl.when(jnp.logical_not(first_tile))
    def _():
        kd_ref[:, 0:BLOCK] = kd_ref[:, n_blk * BLOCK:]
        vt_ref[0] = vt_ref[n_blk]

    for g in range(N_KV_HEADS):
        k_g = kv[:, g * HEAD_DIM:(g + 1) * HEAD_DIM]
        kd_ref[g, BLOCK:] = jnp.concatenate([k_g, k_g], axis=1).astype(BF16)
    v_t = kv[:, KV_WIDTH:].T.astype(BF16)
    for blk in range(n_blk):
        vt_ref[blk + 1] = v_t[:, blk * BLOCK:(blk + 1) * BLOCK]


def _attention_phases(q_ref, kd_ref, vt_ref, s_ref, bias_ref, sink_ref, o_ref, first_tile):
    kj = lax.broadcasted_iota(jnp.int32, (BLOCK, BLOCK), 0)
    qi = lax.broadcasted_iota(jnp.int32, (BLOCK, BLOCK), 1)
    upper = kj > qi
    low_lanes = lax.broadcasted_iota(jnp.int32, (BLOCK, 2 * HEAD_DIM), 1) < HEAD_DIM

    def score_phase(c):
        q = q_ref[c * BLOCK:(c + 1) * BLOCK, :]
        for pair in range(N_HEADS // 2):
            g = (2 * pair) // GROUP
            q_pair = q[:, pair * 2 * HEAD_DIM:(pair + 1) * 2 * HEAD_DIM]
            zero = jnp.zeros_like(q_pair)
            q_both = jnp.concatenate([jnp.where(low_lanes, q_pair, zero),
                                      jnp.where(low_lanes, zero, q_pair)], axis=0)
            s_both = lax.dot_general(kd_ref[g, c * BLOCK:(c + 2) * BLOCK, :], q_both,
                                     (((1,), (1,)), ((), ())), preferred_element_type=F32)
            for par in range(2):
                h = 2 * pair + par
                s_h = s_both[:, par * BLOCK:(par + 1) * BLOCK]
                s = jnp.where(upper, s_h[:BLOCK], s_h[BLOCK:]) + bias_ref[h]
                if c == 0:
                    s = jnp.where(upper & first_tile, NEG_INF, s)
                s_ref[c, h] = s

    def value_phase(c, pair):
        g = (2 * pair) // GROUP
        v_cat = jnp.concatenate([vt_ref[c, g * HEAD_DIM:(g + 1) * HEAD_DIM],
                                 vt_ref[c + 1, g * HEAD_DIM:(g + 1) * HEAD_DIM]], axis=1)
        weights, scales = [], []
        for h in (2 * pair, 2 * pair + 1):
            s = s_ref[c, h]
            sink = sink_ref[h]
            m = jnp.maximum(jnp.max(s, axis=0, keepdims=True), sink)
            e = jnp.exp(s - m)
            den = jnp.sum(e, axis=0, keepdims=True) + jnp.exp(sink - m)
            weights.append(jnp.concatenate([jnp.where(upper, e, 0.0), jnp.where(upper, 0.0, e)],
                                           axis=0).astype(BF16))
            scales.append(1.0 / den)
        o_t = _dot(v_cat, jnp.concatenate(weights, axis=1))
        o_t = jnp.concatenate([o_t[:, :BLOCK] * scales[0], o_t[:, BLOCK:] * scales[1]], axis=0)
        o_ref[c * BLOCK:(c + 1) * BLOCK, pair * 2 * HEAD_DIM:(pair + 1) * 2 * HEAD_DIM] = (
            o_t.T.astype(BF16))

    return score_phase, value_phase


def _in_proj_kernel(*refs, sample):
    if sample:
        (x_ref, mod_ref, g_ref, w_ref, cw_ref, pre_ref,
         z_ref, q_ref, kv_ref, gate_ref, tail_ref, carry_ref) = refs
    else:
        (x_ref, mod_ref, g_ref, w_ref, cw_ref, sink_ref, bias_ref,
         z_ref, a_ref, kv_ref, gate_ref, tail_ref,
         carry_ref, q_ref, kd_ref, vt_ref, s_ref) = refs
    step = pl.program_id(0) if sample else pl.program_id(1)

    @pl.when(step == 0)
    def _():
        if sample:
            carry_ref[...] = pre_ref[...]
        else:
            carry_ref[...] = jnp.zeros(carry_ref.shape, F32)

    x = x_ref[...]
    h = (_rms(x) * g_ref[...]) * (1.0 + _mod(mod_ref, 1, sample)) + _mod(mod_ref, 0, sample)
    hb = _flat(h).astype(BF16)
    rows = hb.shape[0]

    def store(ref, lo, val):
        ref[..., lo:lo + val.shape[-1]] = val.reshape(ref.shape[:-1] + (val.shape[-1],))

    def conv_chunk(lo):
        b_g = _dot(hb, w_ref[:, lo:lo + CHUNK])
        c_g = _dot(hb, w_ref[:, CONV_DIM + lo:CONV_DIM + lo + CHUNK])
        x_c = _dot(hb, w_ref[:, 2 * CONV_DIM + lo:2 * CONV_DIM + lo + CHUNK])
        u = c_g * x_c
        if sample:
            pos = x_ref.shape[0]
            u3 = u.reshape(pos, -1, CHUNK)
            ext = jnp.concatenate([carry_ref[:, :, lo:lo + CHUNK], u3], axis=0)
            u2 = _flat(ext[0:pos])
            u1 = _flat(ext[1:pos + 1])
            new_tail = u3[pos - (CONV_WIDTH - 1):]
            carry_ref[:, :, lo:lo + CHUNK] = new_tail
            tail_ref[:, :, lo:lo + CHUNK] = new_tail
        else:
            prev = carry_ref[:, lo:lo + CHUNK]
            p1 = prev[V7X_SUBLANES - 1:V7X_SUBLANES]
            p2 = prev[V7X_SUBLANES - 2:V7X_SUBLANES - 1]
            r = lax.broadcasted_iota(jnp.int32, u.shape, 0)
            u1 = jnp.where(r == 0, p1, pltpu.roll(u, 1, 0))
            u2 = jnp.where(r == 0, p2, jnp.where(r == 1, p1, pltpu.roll(u, 2, 0)))
            new_tail = u[rows - V7X_SUBLANES:]
            carry_ref[:, lo:lo + CHUNK] = new_tail
            tail_ref[:, lo:lo + CHUNK] = new_tail
        conv = (cw_ref[0:1, lo:lo + CHUNK] * u2 + cw_ref[1:2, lo:lo + CHUNK] * u1
                + cw_ref[2:3, lo:lo + CHUNK] * u)
        store(z_ref, lo, (b_g * conv).astype(BF16))

    def q_chunk(lo):
        q = _dot(hb, w_ref[:, Q_OFF + lo:Q_OFF + lo + CHUNK])
        store(q_ref, lo, (q * (HEAD_DIM ** -0.5)).astype(BF16))

    def gate_chunk(lo):
        g = _dot(hb, w_ref[:, GATE_OFF + lo:GATE_OFF + lo + CHUNK])
        store(gate_ref, lo, (1.0 / (1.0 + jnp.exp(-g))).astype(BF16))

    kv = _dot(hb, w_ref[:, KV_OFF:KV_OFF + 2 * KV_WIDTH])
    store(kv_ref, 0, kv)
    for lo in range(0, ATTN_WIDTH, CHUNK):
        q_chunk(lo)
    dense = ([functools.partial(conv_chunk, lo) for lo in range(0, CONV_DIM, CHUNK)]
             + [functools.partial(gate_chunk, lo) for lo in range(0, 2 * D_MODEL, CHUNK)])
    if sample:
        for work in dense:
            work()
        return

    first_tile = step == 0
    _stage_keys(kv, kd_ref, vt_ref, first_tile)
    score_phase, value_phase = _attention_phases(q_ref, kd_ref, vt_ref, s_ref, bias_ref,
                                                 sink_ref, a_ref, first_tile)
    n_blk = rows // BLOCK
    attention = [functools.partial(score_phase, 0)]
    for c in range(n_blk):
        if c + 1 < n_blk:
            attention.append(functools.partial(score_phase, c + 1))
        attention += [functools.partial(value_phase, c, pair) for pair in range(N_HEADS // 2)]
    done = 0
    for i, work in enumerate(attention):
        work()
        while done < len(dense) and done * len(attention) < (i + 1) * len(dense):
            dense[done]()
            done += 1


def _in_proj(x, mod_all, g_pre1, w_in, conv_w, layer_idx, n_seq, pre=None, attn=None,
             w_layer_idx=None):
    cfg = _dense(x, layer_idx, n_seq, pre is not None)
    w_cfg = cfg if w_layer_idx is None else _dense(x, w_layer_idx, n_seq, pre is not None)
    lead, mid = x.shape[0], x.shape[1]
    widths = (CONV_DIM, ATTN_WIDTH, 2 * KV_WIDTH, 2 * D_MODEL)
    dtypes = (BF16, BF16, F32, BF16)
    blocks = (_nbytes((cfg.rows, D_MODEL), F32) + _nbytes(w_in.shape[1:], BF16)
              + sum(_nbytes((cfg.rows, w), dt) for w, dt in zip(widths, dtypes)))
    temps = _nbytes((cfg.rows, D_MODEL), F32) * 2 + _nbytes((cfg.rows, CHUNK), F32) * 10
    if cfg.sample:
        tail_shape = (CONV_WIDTH - 1, mid, CONV_DIM)
        tail_spec = pl.BlockSpec(tail_shape, lambda i: (0, 0, 0))
        extra_in = [pre]
        extra_specs = [cfg.layer(*tail_shape)]
        scratch = [pltpu.VMEM(tail_shape, F32)]
    else:
        sinks, bias = attn
        per_tile = cfg.rows // BLOCK
        tail_shape = (lead, V7X_SUBLANES, CONV_DIM)
        tail_spec = pl.BlockSpec((None, V7X_SUBLANES, CONV_DIM), lambda b, t: (b, 0, 0))
        extra_in = [sinks[layer_idx], bias]
        extra_specs = [pl.BlockSpec(memory_space=pltpu.SMEM),
                       pl.BlockSpec(bias.shape, lambda b, t: (0, 0, 0),
                                    pipeline_mode=pl.Buffered(1))]
        scratch = [pltpu.VMEM((V7X_SUBLANES, CONV_DIM), F32),
                   pltpu.VMEM((cfg.rows, ATTN_WIDTH), BF16),
                   pltpu.VMEM((N_KV_HEADS, cfg.rows + BLOCK, 2 * HEAD_DIM), BF16),
                   pltpu.VMEM((per_tile + 1, KV_WIDTH, BLOCK), BF16),
                   pltpu.VMEM((per_tile, N_HEADS, BLOCK, BLOCK), F32)]
        temps += (_nbytes(bias.shape, F32) + _nbytes((cfg.rows, ATTN_WIDTH), BF16)
                  + _nbytes((per_tile, N_HEADS, BLOCK, BLOCK), F32)
                  + _nbytes((BLOCK, 2 * BLOCK), F32) * 16 + _nbytes((BLOCK, ATTN_WIDTH), F32) * 4)

    out_shape = [jax.ShapeDtypeStruct(x.shape[:-1] + (w,), dt) for w, dt in zip(widths, dtypes)]
    out_shape.append(jax.ShapeDtypeStruct(tail_shape, F32))
    return pl.pallas_call(
        functools.partial(_in_proj_kernel, sample=cfg.sample),
        grid=cfg.grid,
        in_specs=[cfg.act(D_MODEL), cfg.mod_spec, cfg.layer(1, D_MODEL),
                  w_cfg.layer(*w_in.shape[1:]), cfg.layer(*conv_w.shape[1:])] + extra_specs,
        out_specs=[cfg.act(w) for w in widths] + [tail_spec],
        out_shape=out_shape,
        scratch_shapes=scratch,
        compiler_params=pltpu.CompilerParams(
            dimension_semantics=cfg.sem, vmem_limit_bytes=_vmem_limit(blocks, temps)),
        name="in_proj_sample" if cfg.sample else "in_proj_attn_prompt",
    )(x, mod_all, g_pre1, w_in, conv_w, *extra_in)


def _in_proj_stream_kernel(x_ref, mod_ref, g_ref, w_ref, cw_ref, pre_ref,
                           z_ref, q_ref, kv_ref, gate_ref, tail_ref, wb_ref,
                           hb_ref, b_ref, c_ref):
    step = pl.program_id(0)
    conv_steps = CONV_DIM // IN_STREAM_CHUNK
    q_steps = ATTN_WIDTH // IN_STREAM_CHUNK
    kv_step = (Q_OFF + ATTN_WIDTH) // IN_STREAM_CHUNK
    shape = z_ref.shape

    @pl.when(step == 0)
    def _():
        h = (_rms(x_ref[...]) * g_ref[...]) * (1.0 + mod_ref[1]) + mod_ref[0]
        hb_ref[...] = _flat(h).astype(BF16)

    w = w_ref[...].astype(BF16)
    wb_ref[...] = w
    y = _dot(hb_ref[...], w)

    @pl.when(step < conv_steps)
    def _():
        b_ref[step] = y

    @pl.when((step >= conv_steps) & (step < 2 * conv_steps))
    def _():
        c_ref[step - conv_steps] = y

    @pl.when((step >= 2 * conv_steps) & (step < 3 * conv_steps))
    def _():
        idx = step - 2 * conv_steps
        pos = shape[0]
        u3 = (c_ref[idx] * y).reshape(shape)
        ext = jnp.concatenate([pre_ref[...], u3], axis=0)
        conv = cw_ref[0:1, :] * ext[0:pos] + cw_ref[1:2, :] * ext[1:pos + 1] + cw_ref[2:3, :] * u3
        z_ref[...] = (b_ref[idx].reshape(shape) * conv).astype(BF16)
        tail_ref[...] = u3[pos - (CONV_WIDTH - 1):]

    @pl.when((step >= 3 * conv_steps) & (step < 3 * conv_steps + q_steps))
    def _():
        q_ref[...] = (y * (HEAD_DIM ** -0.5)).astype(BF16).reshape(shape)

    @pl.when(step == kv_step)
    def _():
        kv_ref[...] = y.reshape(shape)

    @pl.when(step > kv_step)
    def _():
        gate_ref[...] = (1.0 / (1.0 + jnp.exp(-y))).astype(BF16).reshape(shape)


def _in_proj_stream_sample(x, mod_all, g_pre1, w_in, conv_w, pre, layer_idx, n_seq):
    pos, n, _ = x.shape
    rows = pos * n
    ch = IN_STREAM_CHUNK
    assert 2 * KV_WIDTH == ch and CONV_DIM % ch == 0 and ATTN_WIDTH % ch == 0
    conv_steps, q_steps = CONV_DIM // ch, ATTN_WIDTH // ch
    kv_step = (Q_OFF + ATTN_WIDTH) // ch
    steps = PROJ_COLS // ch
    const = lambda shape, idx: pl.BlockSpec(shape, lambda j: idx, pipeline_mode=pl.Buffered(1))
    window = lambda first, count: (lambda j: jnp.clip(j - first, 0, count - 1))
    conv_at = window(2 * conv_steps, conv_steps)
    q_at = window(3 * conv_steps, q_steps)
    gate_at = window(kv_step + 1, steps - kv_step - 1)
    act = lambda at: pl.BlockSpec((pos, n, ch), lambda j: (0, 0, at(j)))
    blocks = (_nbytes((D_MODEL, ch), F32) + _nbytes((D_MODEL, ch), BF16)
              + _nbytes((rows, ch), F32) * 4)
    temps = (_nbytes(x.shape, F32) * 3 + _nbytes((N_MOD, n, D_MODEL), F32)
             + _nbytes((rows, D_MODEL), BF16) + _nbytes((rows, CONV_DIM), F32) * 2
             + _nbytes((rows, ch), F32) * 8)
    return pl.pallas_call(
        _in_proj_stream_kernel,
        grid=(steps,),
        in_specs=[const(x.shape, (0, 0, 0)),
                  const((None, N_MOD, n_seq, D_MODEL), (layer_idx, 0, 0, 0)),
                  const((None, 1, D_MODEL), (layer_idx, 0, 0)),
                  pl.BlockSpec((None, D_MODEL, ch), lambda j: (layer_idx, 0, j)),
                  pl.BlockSpec((None, CONV_WIDTH, ch), lambda j: (layer_idx, 0, conv_at(j))),
                  pl.BlockSpec((None, CONV_WIDTH - 1, n, ch),
                               lambda j: (layer_idx, 0, 0, conv_at(j)))],
        out_specs=[act(conv_at), act(q_at), pl.BlockSpec((pos, n, ch), lambda j: (0, 0, 0)),
                   act(gate_at),
                   pl.BlockSpec((CONV_WIDTH - 1, n, ch), lambda j: (0, 0, conv_at(j))),
                   pl.BlockSpec((None, D_MODEL, ch), lambda j: (0, 0, j))],
        out_shape=[jax.ShapeDtypeStruct((pos, n, CONV_DIM), BF16),
                   jax.ShapeDtypeStruct((pos, n, ATTN_WIDTH), BF16),
                   jax.ShapeDtypeStruct((pos, n, 2 * KV_WIDTH), F32),
                   jax.ShapeDtypeStruct((pos, n, 2 * D_MODEL), BF16),
                   jax.ShapeDtypeStruct((CONV_WIDTH - 1, n, CONV_DIM), F32),
                   jax.ShapeDtypeStruct((1, D_MODEL, PROJ_COLS), BF16)],
        scratch_shapes=[pltpu.VMEM((rows, D_MODEL), BF16),
                        pltpu.VMEM((conv_steps, rows, ch), F32),
                        pltpu.VMEM((conv_steps, rows, ch), F32)],
        compiler_params=pltpu.CompilerParams(
            dimension_semantics=("arbitrary",), vmem_limit_bytes=_vmem_limit(blocks, temps)),
        name="in_proj_stream_sample",
    )(x, mod_all, g_pre1, w_in, conv_w, pre)


def _sample_attn_kernel(q_ref, kc_ref, vc_ref, kn_ref, vn_ref, bias_ref, sink_ref, *rest):
    o_ref, ko_ref, vo_ref = rest[-3:]
    n_seq = q_ref.shape[0]
    buf = kc_ref.shape[-1]
    new = kn_ref.shape[1]
    rows = new * GROUP
    pos = lax.broadcasted_iota(jnp.int32, (rows, 2 * buf), 0) // GROUP
    kj = lax.broadcasted_iota(jnp.int32, (rows, 2 * buf), 1)
    dist = buf + pos - kj
    mask = (dist >= 0) & (dist < WINDOW) & (kj < buf + new)
    lane = lax.broadcasted_iota(jnp.int32, (n_seq, HEAD_DIM, buf), 2)
    pad = jnp.zeros((n_seq, HEAD_DIM, buf - new), F32)

    def extended(cache_t, fresh, out_ref, g):
        fresh_t = jnp.concatenate([jnp.swapaxes(fresh, 1, 2), pad], axis=2)
        out_ref[:, g] = pltpu.roll(jnp.where(lane < new, fresh_t, cache_t), buf - new, 2)
        return jnp.concatenate([cache_t, fresh_t], axis=2).astype(BF16)

    for g in range(N_KV_HEADS):
        lanes = slice(g * HEAD_DIM, (g + 1) * HEAD_DIM)
        k_ext = extended(kc_ref[:, g], kn_ref[:, :, lanes], ko_ref, g)
        v_ext = extended(vc_ref[:, g], vn_ref[:, :, lanes], vo_ref, g)
        s = jnp.einsum('nqd,ndk->nqk', q_ref[:, g], k_ext, preferred_element_type=F32)
        p = _sink_softmax(jnp.where(mask[None], s + bias_ref[g][None], NEG_INF),
                          sink_ref[g][None])
        o = jnp.einsum('nqk,ndk->nqd', p.astype(BF16), v_ext, preferred_element_type=F32)
        o_ref[:, g] = o.astype(BF16)


def _sample_attn(q, k_cache, v_cache, k_new, v_new, bias, sink_rows, layer_idx, windows):
    _, n, _, _, buf = k_cache.shape
    new = k_new.shape[1]
    rows = new * GROUP
    nt = SAMPLE_SEQ_TILE
    seq3 = lambda a, b: pl.BlockSpec((nt, a, b), lambda i: (i, 0, 0))
    cache = pl.BlockSpec((None, nt, N_KV_HEADS, HEAD_DIM, buf),
                         lambda i: (layer_idx, i, 0, 0, 0))
    q_spec = pl.BlockSpec((nt, N_KV_HEADS, rows, HEAD_DIM), lambda i: (i, 0, 0, 0))
    in_specs = [q_spec, cache, cache, seq3(new, KV_WIDTH), seq3(new, KV_WIDTH),
                pl.BlockSpec(bias.shape, lambda i: (0, 0, 0)),
                pl.BlockSpec(sink_rows.shape, lambda i: (0, 0, 0))]
    args = [q, k_cache, v_cache, k_new, v_new, bias, sink_rows]
    aliases = {}
    if windows is not None:
        aliases = {len(args): 1, len(args) + 1: 2}
        in_specs += [pl.BlockSpec(memory_space=pl.ANY)] * 2
        args += list(windows)
    blocks = (_nbytes((nt, N_KV_HEADS, HEAD_DIM, buf), F32) * 4
              + _nbytes((nt, new, KV_WIDTH), F32) * 2
              + _nbytes((nt, N_KV_HEADS, rows, 2 * HEAD_DIM), BF16) * 2 + _nbytes(bias.shape, F32))
    temps = _nbytes((rows, 2 * buf), F32) * 8 * nt
    return pl.pallas_call(
        _sample_attn_kernel,
        grid=(n // nt,),
        in_specs=in_specs,
        out_specs=[q_spec, cache, cache],
        out_shape=[jax.ShapeDtypeStruct(q.shape, BF16),
                   jax.ShapeDtypeStruct(k_cache.shape, F32),
                   jax.ShapeDtypeStruct(v_cache.shape, F32)],
        input_output_aliases=aliases,
        compiler_params=pltpu.CompilerParams(
            dimension_semantics=("arbitrary",), vmem_limit_bytes=_vmem_limit(blocks, temps)),
        name="attn_sample",
    )(*args)


def _merge_residual(x_ref, z_ref, a_ref, gate_ref, mod_ref, gpost1_ref, wbc_ref, wba_ref, wo_ref,
                    sample):
    y_conv = _dot(_flat(z_ref[...]), wbc_ref[...])
    y_attn = _dot(_flat(a_ref[...]), wba_ref[...])
    gates = _flat(gate_ref[...])
    merged = gates[:, :D_MODEL] * y_conv + gates[:, D_MODEL:] * y_attn
    mixed = _dot(merged.astype(BF16), wo_ref[...])
    r = (_rms(mixed) * gpost1_ref[...]).reshape(x_ref.shape)
    return x_ref[...] + _mod(mod_ref, 2, sample) * r


def _mlp_input(x1, mod_ref, gpre2_ref, sample):
    h = (_rms(x1) * gpre2_ref[...]) * (1.0 + _mod(mod_ref, 4, sample)) + _mod(mod_ref, 3, sample)
    return _flat(h).astype(BF16)


def _mlp_residual(x1, ff, mod_ref, gpost2_ref, sample):
    return x1 + _mod(mod_ref, 5, sample) * (_rms(ff) * gpost2_ref[...]).reshape(x1.shape)


def _post_kernel(x_ref, z_ref, a_ref, gate_ref, mod_ref, gpost1_ref, gpre2_ref, gpost2_ref,
                 wbc_ref, wba_ref, wo_ref, w1_ref, w2_ref, o_ref, hid_ref, *, sample):
    x1 = _merge_residual(x_ref, z_ref, a_ref, gate_ref, mod_ref, gpost1_ref,
                         wbc_ref, wba_ref, wo_ref, sample)
    hb = _mlp_input(x1, mod_ref, gpre2_ref, sample)
    for lo in range(0, D_FF, 2 * CHUNK):
        act = jnp.maximum(_dot(hb, w1_ref[:, lo:lo + 2 * CHUNK]), 0.0)
        hid_ref[:, lo:lo + 2 * CHUNK] = (act * act).astype(BF16)
    ff = _dot(hid_ref[...], w2_ref[...])
    o_ref[...] = _mlp_residual(x1, ff, mod_ref, gpost2_ref, sample)


def _post(x, z, a, gates, mod_all, g_post1, g_pre2, g_post2, w_bc, w_ba, w_o, w1, w2,
          layer_idx, mlp_layer_idx, n_seq):
    cfg = _dense(x, layer_idx, n_seq, False)
    mlp = _dense(x, mlp_layer_idx, n_seq, False)
    sq = (D_MODEL, D_MODEL)
    blocks = (_nbytes((cfg.rows, D_MODEL), F32) * 2 + _nbytes((cfg.rows, D_MODEL), BF16) * 2
              + _nbytes((cfg.rows, 2 * D_MODEL), gates.dtype))
    temps = (_nbytes(sq, BF16) * 3 + _nbytes(w1.shape[1:], BF16) * 2
             + _nbytes((cfg.rows, D_FF), BF16) + _nbytes((cfg.rows, 2 * CHUNK), F32) * 2
             + _nbytes((cfg.rows, D_MODEL), F32) * 5)
    vec = cfg.layer(1, D_MODEL)
    return pl.pallas_call(
        functools.partial(_post_kernel, sample=False),
        grid=cfg.grid,
        in_specs=[cfg.act(D_MODEL), cfg.act(CONV_DIM), cfg.act(ATTN_WIDTH),
                  cfg.act(2 * D_MODEL), cfg.mod_spec, vec, vec, vec,
                  cfg.layer(*sq), cfg.layer(*sq), cfg.layer(*sq),
                  mlp.layer(*w1.shape[1:]), mlp.layer(*w2.shape[1:])],
        out_specs=cfg.act(D_MODEL),
        out_shape=jax.ShapeDtypeStruct(x.shape, F32),
        scratch_shapes=[pltpu.VMEM((cfg.rows, D_FF), BF16)],
        compiler_params=pltpu.CompilerParams(
            dimension_semantics=cfg.sem, vmem_limit_bytes=_vmem_limit(blocks, temps)),
        name="post_prompt",
    )(x, z, a, gates, mod_all, g_post1, g_pre2, g_post2, w_bc, w_ba, w_o, w1, w2)


def _mix_kernel(x_ref, z_ref, a_ref, gate_ref, mod_ref, gpost1_ref, wbc_ref, wba_ref, wo_ref,
                o_ref):
    o_ref[...] = _merge_residual(x_ref, z_ref, a_ref, gate_ref, mod_ref, gpost1_ref,
                                 wbc_ref, wba_ref, wo_ref, True)


def _mix_sample(x, z, a, gates, mod_all, g_post1, w_bc, w_ba, w_o, layer_idx, n_seq):
    cfg = _dense(x, layer_idx, n_seq, True)
    sq = (D_MODEL, D_MODEL)
    blocks = (_nbytes((cfg.rows, D_MODEL), F32) * 2 + _nbytes((cfg.rows, D_MODEL), BF16) * 2
              + _nbytes((cfg.rows, 2 * D_MODEL), gates.dtype))
    temps = _nbytes(sq, BF16) * 3 + _nbytes((cfg.rows, D_MODEL), F32) * 6
    return pl.pallas_call(
        _mix_kernel,
        grid=cfg.grid,
        in_specs=[cfg.act(D_MODEL), cfg.act(CONV_DIM), cfg.act(ATTN_WIDTH),
                  cfg.act(2 * D_MODEL), cfg.mod_spec, cfg.layer(1, D_MODEL),
                  cfg.layer(*sq), cfg.layer(*sq), cfg.layer(*sq)],
        out_specs=cfg.act(D_MODEL),
        out_shape=jax.ShapeDtypeStruct(x.shape, F32),
        compiler_params=pltpu.CompilerParams(
            dimension_semantics=cfg.sem, vmem_limit_bytes=_vmem_limit(blocks, temps)),
        name="mix_sample",
    )(x, z, a, gates, mod_all, g_post1, w_bc, w_ba, w_o)


def _mlp_stream_kernel(x_ref, mod_ref, gpre2_ref, gpost2_ref, w1_ref, w2_ref,
                       o_ref, w1b_ref, w2b_ref, hb_ref, ff_ref):
    chunk = pl.program_id(0)

    @pl.when(chunk == 0)
    def _():
        hb_ref[...] = _mlp_input(x_ref[...], mod_ref, gpre2_ref, True)
        ff_ref[...] = jnp.zeros(ff_ref.shape, F32)

    w1 = w1_ref[...].astype(BF16)
    w2 = w2_ref[...].astype(BF16)
    w1b_ref[...] = w1
    w2b_ref[...] = w2
    act = jnp.maximum(_dot(hb_ref[...], w1), 0.0)
    ff_ref[...] += _dot((act * act).astype(BF16), w2)

    @pl.when(chunk == pl.num_programs(0) - 1)
    def _():
        o_ref[...] = _mlp_residual(x_ref[...], ff_ref[...], mod_ref, gpost2_ref, True)


def _mlp_stream_sample(x, mod_all, g_pre2, g_post2, w1, w2, layer_idx, n_seq):
    rows = x.shape[0] * x.shape[1]
    const = lambda shape, idx: pl.BlockSpec(shape, lambda c: idx, pipeline_mode=pl.Buffered(1))
    vec = const((None, 1, D_MODEL), (layer_idx, 0, 0))
    blocks = (_nbytes((D_MODEL, MLP_STREAM_CHUNK), F32) * 2
              + _nbytes((D_MODEL, MLP_STREAM_CHUNK), BF16) * 2 + _nbytes(x.shape, F32))
    temps = (_nbytes(x.shape, F32) * 4 + _nbytes((N_MOD, n_seq, D_MODEL), F32)
             + _nbytes((rows, D_MODEL), BF16) + _nbytes((rows, MLP_STREAM_CHUNK), F32) * 3)
    return pl.pallas_call(
        _mlp_stream_kernel,
        grid=(D_FF // MLP_STREAM_CHUNK,),
        in_specs=[const(x.shape, (0, 0, 0)),
                  const((None, N_MOD, n_seq, D_MODEL), (layer_idx, 0, 0, 0)), vec, vec,
                  pl.BlockSpec((None, D_MODEL, MLP_STREAM_CHUNK), lambda c: (layer_idx, 0, c)),
                  pl.BlockSpec((None, MLP_STREAM_CHUNK, D_MODEL), lambda c: (layer_idx, c, 0))],
        out_specs=[pl.BlockSpec(x.shape, lambda c: (0, 0, 0)),
                   pl.BlockSpec((None, D_MODEL, MLP_STREAM_CHUNK), lambda c: (0, 0, c)),
                   pl.BlockSpec((None, MLP_STREAM_CHUNK, D_MODEL), lambda c: (0, c, 0))],
        out_shape=[jax.ShapeDtypeStruct(x.shape, F32),
                   jax.ShapeDtypeStruct((1, D_MODEL, D_FF), BF16),
                   jax.ShapeDtypeStruct((1, D_FF, D_MODEL), BF16)],
        scratch_shapes=[pltpu.VMEM((rows, D_MODEL), BF16), pltpu.VMEM((rows, D_MODEL), F32)],
        compiler_params=pltpu.CompilerParams(
            dimension_semantics=("arbitrary",), vmem_limit_bytes=_vmem_limit(blocks, temps)),
        name="mlp_stream_sample",
    )(x, mod_all, g_pre2, g_post2, w1, w2)


def kernel(x_prompt, x_sample, c_prompt, c_sample, state_conv, cache_k, cache_v, w_ada, b_ada,
           g_pre1, w_in, conv_w, w_br_conv, w_br_attn, w_o, sinks, g_post1, g_pre2, w_ff1, w_ff2,
           g_post2, rel_table):
    depth = w_ada.shape[0]
    batch, seq, _ = x_prompt.shape
    n_seq, n_new, _ = x_sample.shape
    buf = cache_k.shape[2]

    pad = (-(n_seq + batch)) % V7X_SUBLANES
    c_all = jnp.concatenate([c_sample, c_prompt, jnp.zeros((pad, D_MODEL), F32)], axis=0)
    mod_all = _ada(c_all, w_ada, b_ada)

    dist_p = (jnp.arange(BLOCK)[:, None] + BLOCK) - jnp.arange(2 * BLOCK)[None, :]
    bucket_p = _rel_bucket(dist_p)
    upper = jnp.arange(BLOCK)[None, :] > jnp.arange(BLOCK)[:, None]
    bucket_merged = jnp.where(upper, bucket_p[:, :BLOCK], bucket_p[:, BLOCK:])
    bias_p = _bias_table(rel_table, bucket_merged.T)
    dist_s = (buf + jnp.arange(n_new))[:, None] - jnp.arange(buf + n_new)[None, :]
    bias_s = _bias_table(rel_table, _rel_bucket(dist_s))
    bias_s = (bias_s.reshape(N_KV_HEADS, GROUP, n_new, buf + n_new)
              .transpose(0, 2, 1, 3).reshape(N_KV_HEADS, n_new * GROUP, buf + n_new))
    bias_s = jnp.pad(bias_s, ((0, 0), (0, 0), (0, buf - n_new)))

    to_bf16 = lambda w: w.astype(BF16)
    w_bc_b, w_ba_b, w_o_b = map(to_bf16, (w_br_conv, w_br_attn, w_o))
    vec = lambda g: g.reshape(depth, 1, D_MODEL)
    g_pre1, g_post1, g_pre2, g_post2 = map(vec, (g_pre1, g_post1, g_pre2, g_post2))
    pre_s = state_conv.transpose(0, 2, 1, 3)
    cache_k = cache_k.transpose(0, 1, 3, 4, 2)
    cache_v = cache_v.transpose(0, 1, 3, 4, 2)

    xp = x_prompt
    xs = x_sample.transpose(1, 0, 2)
    conv_p, k_p, v_p, conv_s = [], [], [], []
    windows = None
    win = min(WINDOW, seq)
    for l in range(depth):
        z, q, kv, gates, tail, w_in_b = _in_proj_stream_sample(xs, mod_all, g_pre1, w_in, conv_w,
                                                               pre_s, l, n_seq)
        q_s = (q.reshape(n_new, n_seq, N_KV_HEADS, GROUP, HEAD_DIM)
               .transpose(1, 2, 0, 3, 4).reshape(n_seq, N_KV_HEADS, n_new * GROUP, HEAD_DIM))
        kv_s = kv.transpose(1, 0, 2)
        sink_rows = jnp.tile(sinks[l].reshape(N_KV_HEADS, 1, GROUP),
                             (1, n_new, 1)).reshape(N_KV_HEADS, n_new * GROUP, 1)
        o, *windows = _sample_attn(q_s, cache_k, cache_v, kv_s[:, :, :KV_WIDTH],
                                   kv_s[:, :, KV_WIDTH:], bias_s, sink_rows, l, windows)
        attn = (o.reshape(n_seq, N_KV_HEADS, n_new, GROUP, HEAD_DIM)
                .transpose(2, 0, 1, 3, 4).reshape(n_new, n_seq, ATTN_WIDTH))
        x1 = _mix_sample(xs, z, attn, gates, mod_all, g_post1, w_bc_b, w_ba_b, w_o_b, l, n_seq)
        xs, w1_b, w2_b = _mlp_stream_sample(x1, mod_all, g_pre2, g_post2, w_ff1, w_ff2, l, n_seq)
        conv_s.append(tail.transpose(1, 0, 2))

        z, attn, kv, gates, tail = _in_proj(xp, mod_all, g_pre1, w_in_b, conv_w, l, n_seq,
                                            attn=(sinks, bias_p), w_layer_idx=0)
        xp = _post(xp, z, attn, gates, mod_all, g_post1, g_pre2, g_post2, w_bc_b, w_ba_b, w_o_b,
                   w1_b, w2_b, l, 0, n_seq)
        conv_p.append(tail[:, V7X_SUBLANES - (CONV_WIDTH - 1):])
        k_p.append(kv[:, seq - win:, :KV_WIDTH].reshape(batch, win, N_KV_HEADS, HEAD_DIM))
        v_p.append(kv[:, seq - win:, KV_WIDTH:].reshape(batch, win, N_KV_HEADS, HEAD_DIM))

    k_s, v_s = (w.transpose(0, 1, 4, 2, 3) for w in windows)
    return (xp, xs.transpose(1, 0, 2), jnp.stack(conv_p), jnp.stack(k_p), jnp.stack(v_p),
            jnp.stack(conv_s), k_s, v_s)
```

```python
import functools
import math
from typing import Any, Callable, NamedTuple

import jax
import jax.numpy as jnp
from jax import lax
from jax.experimental import pallas as pl
from jax.experimental.pallas import tpu as pltpu

D_MODEL = 1024
N_HEADS = 16
N_KV_HEADS = 2
HEAD_DIM = 64
GROUP = N_HEADS // N_KV_HEADS
ATTN_WIDTH = N_HEADS * HEAD_DIM
KV_WIDTH = N_KV_HEADS * HEAD_DIM
CONV_DIM = D_MODEL
CONV_WIDTH = 3
WINDOW = 128
BLOCK = 128
N_BUCKETS = 32
MAX_DISTANCE = 128
D_FF = 4 * D_MODEL
N_MOD = 6
RMS_EPS = 1e-6
NEG_INF = -1e30
PROJ_COLS = 3 * CONV_DIM + ATTN_WIDTH + 2 * KV_WIDTH + 2 * D_MODEL
Q_OFF = 3 * CONV_DIM
KV_OFF = Q_OFF + ATTN_WIDTH
GATE_OFF = KV_OFF + 2 * KV_WIDTH

V7X_SUBLANES = 8
V7X_VMEM_BYTES = 64 * 1024 * 1024

PROMPT_TILE = 512
SAMPLE_POS_TILE = 4
SAMPLE_SEQ_TILE = 16
CHUNK = 512
MLP_STREAM_CHUNK = 512

F32 = jnp.float32
BF16 = jnp.bfloat16


def _vmem_limit(block_bytes, temp_bytes):
    return int(min(2 * block_bytes + temp_bytes, V7X_VMEM_BYTES - 4 * 1024 * 1024))


def _nbytes(shape, dtype):
    return math.prod(shape) * jnp.dtype(dtype).itemsize


def _rms(x):
    return x * lax.rsqrt(jnp.mean(x * x, axis=-1, keepdims=True) + RMS_EPS)


def _dot(a, b):
    return jnp.dot(a, b, preferred_element_type=F32)


def _flat(a):
    return a.reshape(-1, a.shape[-1])


def _ada_kernel(c_ref, w_ref, b_ref, o_ref):
    c = c_ref[...]
    s = c * (1.0 / (1.0 + jnp.exp(-c)))
    o_ref[...] = _dot(s.astype(BF16), w_ref[...].astype(BF16)) + b_ref[...]


def _ada(c_all, w_ada, b_ada):
    depth = w_ada.shape[0]
    rows = c_all.shape[0]
    blocks = (_nbytes((rows, D_MODEL), F32) * 2 + _nbytes((D_MODEL, D_MODEL), F32))
    return pl.pallas_call(
        _ada_kernel,
        grid=(depth, N_MOD),
        in_specs=[
            pl.BlockSpec((rows, D_MODEL), lambda l, j: (0, 0)),
            pl.BlockSpec((None, D_MODEL, D_MODEL), lambda l, j: (l, 0, j)),
            pl.BlockSpec((None, None, 1, D_MODEL), lambda l, j: (l, j, 0, 0)),
        ],
        out_specs=pl.BlockSpec((None, None, rows, D_MODEL), lambda l, j: (l, j, 0, 0)),
        out_shape=jax.ShapeDtypeStruct((depth, N_MOD, rows, D_MODEL), F32),
        compiler_params=pltpu.CompilerParams(
            dimension_semantics=("arbitrary", "arbitrary"),
            vmem_limit_bytes=_vmem_limit(blocks, _nbytes((D_MODEL, D_MODEL), BF16) * 2)),
        name="ada_mod",
    )(c_all, w_ada, b_ada.reshape(depth, N_MOD, 1, D_MODEL))


def _bias_kernel(tab_ref, bucket_ref, o_ref):
    bucket = bucket_ref[...]
    for h in range(N_HEADS):
        acc = jnp.zeros(bucket.shape, F32)
        for b in range(N_BUCKETS):
            acc = jnp.where(bucket == b, tab_ref[b * N_HEADS + h], acc)
        o_ref[h] = acc


def _bias_table(rel_table, bucket):
    return pl.pallas_call(
        _bias_kernel,
        in_specs=[
            pl.BlockSpec(memory_space=pltpu.SMEM),
            pl.BlockSpec(bucket.shape, lambda: (0, 0)),
        ],
        out_specs=pl.BlockSpec((N_HEADS,) + bucket.shape, lambda: (0, 0, 0)),
        out_shape=jax.ShapeDtypeStruct((N_HEADS,) + bucket.shape, F32),
        name="bias_table",
    )(rel_table.reshape(-1), bucket)


def _rel_bucket(dist):
    n = jnp.maximum(dist, 0)
    max_exact = N_BUCKETS // 2
    nf = jnp.maximum(n, 1).astype(F32)
    scaled = (jnp.log(nf / max_exact) / math.log(MAX_DISTANCE / max_exact)
              * (N_BUCKETS - max_exact))
    large = jnp.minimum(max_exact + jnp.floor(scaled).astype(jnp.int32), N_BUCKETS - 1)
    return jnp.where(n < max_exact, n, large)


class _Dense(NamedTuple):
    sample: bool
    grid: tuple
    rows: int
    act: Callable[[int], Any]
    layer: Callable[..., Any]
    mod_spec: Any
    sem: tuple


def _dense(x, layer_idx, n_seq, sample):
    lead, mid = x.shape[0], x.shape[1]
    if sample:
        grid = (lead // SAMPLE_POS_TILE,)
        act = lambda w: pl.BlockSpec((SAMPLE_POS_TILE, mid, w), lambda i: (i, 0, 0))
        layer = lambda *shape: pl.BlockSpec((None,) + shape,
                                            lambda i: (layer_idx,) + (0,) * len(shape),
                                            pipeline_mode=pl.Buffered(1))
        mod_spec = pl.BlockSpec((None, N_MOD, n_seq, D_MODEL), lambda i: (layer_idx, 0, 0, 0))
        return _Dense(True, grid, SAMPLE_POS_TILE * mid, act, layer, mod_spec, ("arbitrary",))
    assert n_seq % V7X_SUBLANES == 0 and lead <= V7X_SUBLANES
    grid = (lead, mid // PROMPT_TILE)
    act = lambda w: pl.BlockSpec((None, PROMPT_TILE, w), lambda b, t: (b, t, 0))
    layer = lambda *shape: pl.BlockSpec((None,) + shape,
                                        lambda b, t: (layer_idx,) + (0,) * len(shape),
                                        pipeline_mode=pl.Buffered(1))
    mod_spec = pl.BlockSpec((None, N_MOD, V7X_SUBLANES, D_MODEL),
                            lambda b, t: (layer_idx, 0, n_seq // V7X_SUBLANES, 0))
    return _Dense(False, grid, PROMPT_TILE, act, layer, mod_spec, ("arbitrary", "arbitrary"))


def _mod(mod_ref, j, sample):
    if sample:
        return mod_ref[j]
    return mod_ref[j, pl.ds(pl.program_id(0), 1), :]


def _sink_softmax(s, sink):
    m = jnp.maximum(jnp.max(s, axis=-1, keepdims=True), sink)
    e = jnp.exp(s - m)
    den = jnp.sum(e, axis=-1, keepdims=True) + jnp.exp(sink - m)
    return e * (1.0 / den)


def _stage_keys(kv, kd_ref, vt_ref, first_tile):
    n_blk = kv.shape[0] // BLOCK

    @pl.when(first_tile)
    def _():
        kd_ref[:, 0:BLOCK] = jnp.zeros((N_KV_HEADS, BLOCK, 2 * HEAD_DIM), BF16)
        vt_ref[0] = jnp.zeros((KV_WIDTH, BLOCK), BF16)

    @pl.when(jnp.logical_not(first_tile))
    def _():
        kd_ref[:, 0:BLOCK] = kd_ref[:, n_blk * BLOCK:]
        vt_ref[0] = vt_ref[n_blk]

    for g in range(N_KV_HEADS):
        k_g = kv[:, g * HEAD_DIM:(g + 1) * HEAD_DIM]
        kd_ref[g, BLOCK:] = jnp.concatenate([k_g, k_g], axis=1).astype(BF16)
    v_t = kv[:, KV_WIDTH:].T.astype(BF16)
    for blk in range(n_blk):
        vt_ref[blk + 1] = v_t[:, blk * BLOCK:(blk + 1) * BLOCK]


def _attention_phases(q_ref, kd_ref, vt_ref, s_ref, bias_ref, sink_ref, o_ref, first_tile):
    kj = lax.broadcasted_iota(jnp.int32, (BLOCK, BLOCK), 0)
    qi = lax.broadcasted_iota(jnp.int32, (BLOCK, BLOCK), 1)
    upper = kj > qi
    low_lanes = lax.broadcasted_iota(jnp.int32, (BLOCK, 2 * HEAD_DIM), 1) < HEAD_DIM

    def score_phase(c):
        q = q_ref[c * BLOCK:(c + 1) * BLOCK, :]
        for pair in range(N_HEADS // 2):
            g = (2 * pair) // GROUP
            q_pair = q[:, pair * 2 * HEAD_DIM:(pair + 1) * 2 * HEAD_DIM]
            zero = jnp.zeros_like(q_pair)
            q_both = jnp.concatenate([jnp.where(low_lanes, q_pair, zero),
                                      jnp.where(low_lanes, zero, q_pair)], axis=0)
            s_both = lax.dot_general(kd_ref[g, c * BLOCK:(c + 2) * BLOCK, :], q_both,
                                     (((1,), (1,)), ((), ())), preferred_element_type=F32)
            for par in range(2):
                h = 2 * pair + par
                s_h = s_both[:, par * BLOCK:(par + 1) * BLOCK]
                s = jnp.where(upper, s_h[:BLOCK], s_h[BLOCK:]) + bias_ref[h]
                if c == 0:
                    s = jnp.where(upper & first_tile, NEG_INF, s)
                s_ref[c, h] = s

    def value_phase(c, pair):
        g = (2 * pair) // GROUP
        v_cat = jnp.concatenate([vt_ref[c, g * HEAD_DIM:(g + 1) * HEAD_DIM],
                                 vt_ref[c + 1, g * HEAD_DIM:(g + 1) * HEAD_DIM]], axis=1)
        weights, scales = [], []
        for h in (2 * pair, 2 * pair + 1):
            s = s_ref[c, h]
            sink = sink_ref[h]
            m = jnp.maximum(jnp.max(s, axis=0, keepdims=True), sink)
            e = jnp.exp(s - m)
            den = jnp.sum(e, axis=0, keepdims=True) + jnp.exp(sink - m)
            weights.append(jnp.concatenate([jnp.where(upper, e, 0.0), jnp.where(upper, 0.0, e)],
                                           axis=0).astype(BF16))
            scales.append(1.0 / den)
        o_t = _dot(v_cat, jnp.concatenate(weights, axis=1))
        o_t = jnp.concatenate([o_t[:, :BLOCK] * scales[0], o_t[:, BLOCK:] * scales[1]], axis=0)
        o_ref[c * BLOCK:(c + 1) * BLOCK, pair * 2 * HEAD_DIM:(pair + 1) * 2 * HEAD_DIM] = (
            o_t.T.astype(BF16))

    return score_phase, value_phase


def _in_proj_kernel(*refs, sample):
    if sample:
        (x_ref, mod_ref, g_ref, w_ref, cw_ref, pre_ref,
         z_ref, q_ref, kv_ref, gate_ref, tail_ref, carry_ref) = refs
    else:
        (x_ref, mod_ref, g_ref, w_ref, cw_ref, sink_ref, bias_ref,
         z_ref, a_ref, kv_ref, gate_ref, tail_ref,
         carry_ref, q_ref, kd_ref, vt_ref, s_ref) = refs
    step = pl.program_id(0) if sample else pl.program_id(1)

    @pl.when(step == 0)
    def _():
        if sample:
            carry_ref[...] = pre_ref[...]
        else:
            carry_ref[...] = jnp.zeros(carry_ref.shape, F32)

    x = x_ref[...]
    h = (_rms(x) * g_ref[...]) * (1.0 + _mod(mod_ref, 1, sample)) + _mod(mod_ref, 0, sample)
    hb = _flat(h).astype(BF16)
    rows = hb.shape[0]

    def store(ref, lo, val):
        ref[..., lo:lo + val.shape[-1]] = val.reshape(ref.shape[:-1] + (val.shape[-1],))

    def conv_chunk(lo):
        b_g = _dot(hb, w_ref[:, lo:lo + CHUNK])
        c_g = _dot(hb, w_ref[:, CONV_DIM + lo:CONV_DIM + lo + CHUNK])
        x_c = _dot(hb, w_ref[:, 2 * CONV_DIM + lo:2 * CONV_DIM + lo + CHUNK])
        u = c_g * x_c
        if sample:
            pos = x_ref.shape[0]
            u3 = u.reshape(pos, -1, CHUNK)
            ext = jnp.concatenate([carry_ref[:, :, lo:lo + CHUNK], u3], axis=0)
            u2 = _flat(ext[0:pos])
            u1 = _flat(ext[1:pos + 1])
            new_tail = u3[pos - (CONV_WIDTH - 1):]
            carry_ref[:, :, lo:lo + CHUNK] = new_tail
            tail_ref[:, :, lo:lo + CHUNK] = new_tail
        else:
            prev = carry_ref[:, lo:lo + CHUNK]
            p1 = prev[V7X_SUBLANES - 1:V7X_SUBLANES]
            p2 = prev[V7X_SUBLANES - 2:V7X_SUBLANES - 1]
            r = lax.broadcasted_iota(jnp.int32, u.shape, 0)
            u1 = jnp.where(r == 0, p1, pltpu.roll(u, 1, 0))
            u2 = jnp.where(r == 0, p2, jnp.where(r == 1, p1, pltpu.roll(u, 2, 0)))
            new_tail = u[rows - V7X_SUBLANES:]
            carry_ref[:, lo:lo + CHUNK] = new_tail
            tail_ref[:, lo:lo + CHUNK] = new_tail
        conv = (cw_ref[0:1, lo:lo + CHUNK] * u2 + cw_ref[1:2, lo:lo + CHUNK] * u1
                + cw_ref[2:3, lo:lo + CHUNK] * u)
        store(z_ref, lo, (b_g * conv).astype(BF16))

    def q_chunk(lo):
        q = _dot(hb, w_ref[:, Q_OFF + lo:Q_OFF + lo + CHUNK])
        store(q_ref, lo, (q * (HEAD_DIM ** -0.5)).astype(BF16))

    def gate_chunk(lo):
        g = _dot(hb, w_ref[:, GATE_OFF + lo:GATE_OFF + lo + CHUNK])
        store(gate_ref, lo, (1.0 / (1.0 + jnp.exp(-g))).astype(BF16))

    kv = _dot(hb, w_ref[:, KV_OFF:KV_OFF + 2 * KV_WIDTH])
    store(kv_ref, 0, kv)
    for lo in range(0, ATTN_WIDTH, CHUNK):
        q_chunk(lo)
    dense = ([functools.partial(conv_chunk, lo) for lo in range(0, CONV_DIM, CHUNK)]
             + [functools.partial(gate_chunk, lo) for lo in range(0, 2 * D_MODEL, CHUNK)])
    if sample:
        for work in dense:
            work()
        return

    first_tile = step == 0
    _stage_keys(kv, kd_ref, vt_ref, first_tile)
    score_phase, value_phase = _attention_phases(q_ref, kd_ref, vt_ref, s_ref, bias_ref,
                                                 sink_ref, a_ref, first_tile)
    n_blk = rows // BLOCK
    attention = [functools.partial(score_phase, 0)]
    for c in range(n_blk):
        if c + 1 < n_blk:
            attention.append(functools.partial(score_phase, c + 1))
        attention += [functools.partial(value_phase, c, pair) for pair in range(N_HEADS // 2)]
    done = 0
    for i, work in enumerate(attention):
        work()
        while done < len(dense) and done * len(attention) < (i + 1) * len(dense):
            dense[done]()
            done += 1


def _in_proj(x, mod_all, g_pre1, w_in, conv_w, layer_idx, n_seq, pre=None, attn=None):
    cfg = _dense(x, layer_idx, n_seq, pre is not None)
    lead, mid = x.shape[0], x.shape[1]
    widths = (CONV_DIM, ATTN_WIDTH, 2 * KV_WIDTH, 2 * D_MODEL)
    dtypes = (BF16, BF16, F32, BF16)
    blocks = (_nbytes((cfg.rows, D_MODEL), F32) + _nbytes(w_in.shape[1:], BF16)
              + sum(_nbytes((cfg.rows, w), dt) for w, dt in zip(widths, dtypes)))
    temps = _nbytes((cfg.rows, D_MODEL), F32) * 2 + _nbytes((cfg.rows, CHUNK), F32) * 10
    if cfg.sample:
        tail_shape = (CONV_WIDTH - 1, mid, CONV_DIM)
        tail_spec = pl.BlockSpec(tail_shape, lambda i: (0, 0, 0))
        extra_in = [pre]
        extra_specs = [cfg.layer(*tail_shape)]
        scratch = [pltpu.VMEM(tail_shape, F32)]
    else:
        sinks, bias = attn
        per_tile = cfg.rows // BLOCK
        tail_shape = (lead, V7X_SUBLANES, CONV_DIM)
        tail_spec = pl.BlockSpec((None, V7X_SUBLANES, CONV_DIM), lambda b, t: (b, 0, 0))
        extra_in = [sinks[layer_idx], bias]
        extra_specs = [pl.BlockSpec(memory_space=pltpu.SMEM),
                       pl.BlockSpec(bias.shape, lambda b, t: (0, 0, 0),
                                    pipeline_mode=pl.Buffered(1))]
        scratch = [pltpu.VMEM((V7X_SUBLANES, CONV_DIM), F32),
                   pltpu.VMEM((cfg.rows, ATTN_WIDTH), BF16),
                   pltpu.VMEM((N_KV_HEADS, cfg.rows + BLOCK, 2 * HEAD_DIM), BF16),
                   pltpu.VMEM((per_tile + 1, KV_WIDTH, BLOCK), BF16),
                   pltpu.VMEM((per_tile, N_HEADS, BLOCK, BLOCK), F32)]
        temps += (_nbytes(bias.shape, F32) + _nbytes((cfg.rows, ATTN_WIDTH), BF16)
                  + _nbytes((per_tile, N_HEADS, BLOCK, BLOCK), F32)
                  + _nbytes((BLOCK, 2 * BLOCK), F32) * 16 + _nbytes((BLOCK, ATTN_WIDTH), F32) * 4)

    out_shape = [jax.ShapeDtypeStruct(x.shape[:-1] + (w,), dt) for w, dt in zip(widths, dtypes)]
    out_shape.append(jax.ShapeDtypeStruct(tail_shape, F32))
    return pl.pallas_call(
        functools.partial(_in_proj_kernel, sample=cfg.sample),
        grid=cfg.grid,
        in_specs=[cfg.act(D_MODEL), cfg.mod_spec, cfg.layer(1, D_MODEL),
                  cfg.layer(*w_in.shape[1:]), cfg.layer(*conv_w.shape[1:])] + extra_specs,
        out_specs=[cfg.act(w) for w in widths] + [tail_spec],
        out_shape=out_shape,
        scratch_shapes=scratch,
        compiler_params=pltpu.CompilerParams(
            dimension_semantics=cfg.sem, vmem_limit_bytes=_vmem_limit(blocks, temps)),
        name="in_proj_sample" if cfg.sample else "in_proj_attn_prompt",
    )(x, mod_all, g_pre1, w_in, conv_w, *extra_in)


def _sample_attn_kernel(q_ref, kc_ref, vc_ref, kn_ref, vn_ref, bias_ref, sink_ref, *rest,
                        layer_idx):
    o_ref, ko_ref, vo_ref = rest[-3:]
    if ko_ref.ndim > kc_ref.ndim:
        for other in range(ko_ref.shape[0]):
            if other != layer_idx:
                ko_ref[other] = jnp.zeros(ko_ref.shape[1:], F32)
                vo_ref[other] = jnp.zeros(vo_ref.shape[1:], F32)
        ko_ref, vo_ref = ko_ref.at[layer_idx], vo_ref.at[layer_idx]
    n_seq = q_ref.shape[0]
    buf = kc_ref.shape[-1]
    new = kn_ref.shape[1]
    rows = new * GROUP
    pos = lax.broadcasted_iota(jnp.int32, (rows, 2 * buf), 0) // GROUP
    kj = lax.broadcasted_iota(jnp.int32, (rows, 2 * buf), 1)
    dist = buf + pos - kj
    mask = (dist >= 0) & (dist < WINDOW) & (kj < buf + new)
    lane = lax.broadcasted_iota(jnp.int32, (n_seq, HEAD_DIM, buf), 2)
    pad = jnp.zeros((n_seq, HEAD_DIM, buf - new), F32)

    def extended(cache_t, fresh, out_ref, g):
        fresh_t = jnp.concatenate([jnp.swapaxes(fresh, 1, 2), pad], axis=2)
        out_ref[:, g] = pltpu.roll(jnp.where(lane < new, fresh_t, cache_t), buf - new, 2)
        return jnp.concatenate([cache_t, fresh_t], axis=2).astype(BF16)

    for g in range(N_KV_HEADS):
        lanes = slice(g * HEAD_DIM, (g + 1) * HEAD_DIM)
        k_ext = extended(kc_ref[:, g], kn_ref[:, :, lanes], ko_ref, g)
        v_ext = extended(vc_ref[:, g], vn_ref[:, :, lanes], vo_ref, g)
        s = jnp.einsum('nqd,ndk->nqk', q_ref[:, g], k_ext, preferred_element_type=F32)
        p = _sink_softmax(jnp.where(mask[None], s + bias_ref[g][None], NEG_INF),
                          sink_ref[g][None])
        o = jnp.einsum('nqk,ndk->nqd', p.astype(BF16), v_ext, preferred_element_type=F32)
        o_ref[:, g] = o.astype(BF16)


def _sample_attn(q, k_cache, v_cache, k_new, v_new, bias, sink_rows, layer_idx, windows):
    _, n, _, _, buf = k_cache.shape
    new = k_new.shape[1]
    rows = new * GROUP
    nt = SAMPLE_SEQ_TILE
    seq3 = lambda a, b: pl.BlockSpec((nt, a, b), lambda i: (i, 0, 0))
    cache = pl.BlockSpec((None, nt, N_KV_HEADS, HEAD_DIM, buf),
                         lambda i: (layer_idx, i, 0, 0, 0))
    q_spec = pl.BlockSpec((nt, N_KV_HEADS, rows, HEAD_DIM), lambda i: (i, 0, 0, 0))
    in_specs = [q_spec, cache, cache, seq3(new, KV_WIDTH), seq3(new, KV_WIDTH),
                pl.BlockSpec(bias.shape, lambda i: (0, 0, 0)),
                pl.BlockSpec(sink_rows.shape, lambda i: (0, 0, 0))]
    args = [q, k_cache, v_cache, k_new, v_new, bias, sink_rows]
    depth = k_cache.shape[0]
    if windows is None:
        aliases = {}
        window_out = pl.BlockSpec((depth, nt, N_KV_HEADS, HEAD_DIM, buf),
                                  lambda i: (0, i, 0, 0, 0))
    else:
        aliases = {len(args): 1, len(args) + 1: 2}
        in_specs += [pl.BlockSpec(memory_space=pl.ANY)] * 2
        args += list(windows)
        window_out = cache
    blocks = (_nbytes((nt, N_KV_HEADS, HEAD_DIM, buf), F32) * (2 + 2 * depth)
              + _nbytes((nt, new, KV_WIDTH), F32) * 2
              + _nbytes((nt, N_KV_HEADS, rows, 2 * HEAD_DIM), BF16) * 2 + _nbytes(bias.shape, F32))
    temps = _nbytes((rows, 2 * buf), F32) * 8 * nt
    return pl.pallas_call(
        functools.partial(_sample_attn_kernel, layer_idx=layer_idx),
        grid=(n // nt,),
        in_specs=in_specs,
        out_specs=[q_spec, window_out, window_out],
        out_shape=[jax.ShapeDtypeStruct(q.shape, BF16),
                   jax.ShapeDtypeStruct(k_cache.shape, F32),
                   jax.ShapeDtypeStruct(v_cache.shape, F32)],
        input_output_aliases=aliases,
        compiler_params=pltpu.CompilerParams(
            dimension_semantics=("arbitrary",), vmem_limit_bytes=_vmem_limit(blocks, temps)),
        name="attn_sample",
    )(*args)


def _merge_residual(x_ref, z_ref, a_ref, gate_ref, mod_ref, gpost1_ref, wbc_ref, wba_ref, wo_ref,
                    sample):
    y_conv = _dot(_flat(z_ref[...]), wbc_ref[...])
    y_attn = _dot(_flat(a_ref[...]), wba_ref[...])
    gates = _flat(gate_ref[...])
    merged = gates[:, :D_MODEL] * y_conv + gates[:, D_MODEL:] * y_attn
    mixed = _dot(merged.astype(BF16), wo_ref[...])
    r = (_rms(mixed) * gpost1_ref[...]).reshape(x_ref.shape)
    return x_ref[...] + _mod(mod_ref, 2, sample) * r


def _mlp_input(x1, mod_ref, gpre2_ref, sample):
    h = (_rms(x1) * gpre2_ref[...]) * (1.0 + _mod(mod_ref, 4, sample)) + _mod(mod_ref, 3, sample)
    return _flat(h).astype(BF16)


def _mlp_residual(x1, ff, mod_ref, gpost2_ref, sample):
    return x1 + _mod(mod_ref, 5, sample) * (_rms(ff) * gpost2_ref[...]).reshape(x1.shape)


def _post_kernel(x_ref, z_ref, a_ref, gate_ref, mod_ref, gpost1_ref, gpre2_ref, gpost2_ref,
                 wbc_ref, wba_ref, wo_ref, w1_ref, w2_ref, o_ref, hid_ref, *, sample):
    x1 = _merge_residual(x_ref, z_ref, a_ref, gate_ref, mod_ref, gpost1_ref,
                         wbc_ref, wba_ref, wo_ref, sample)
    hb = _mlp_input(x1, mod_ref, gpre2_ref, sample)
    for lo in range(0, D_FF, 2 * CHUNK):
        act = jnp.maximum(_dot(hb, w1_ref[:, lo:lo + 2 * CHUNK]), 0.0)
        hid_ref[:, lo:lo + 2 * CHUNK] = (act * act).astype(BF16)
    ff = _dot(hid_ref[...], w2_ref[...])
    o_ref[...] = _mlp_residual(x1, ff, mod_ref, gpost2_ref, sample)


def _post(x, z, a, gates, mod_all, g_post1, g_pre2, g_post2, w_bc, w_ba, w_o, w1, w2,
          layer_idx, mlp_layer_idx, n_seq):
    cfg = _dense(x, layer_idx, n_seq, False)
    mlp = _dense(x, mlp_layer_idx, n_seq, False)
    sq = (D_MODEL, D_MODEL)
    blocks = (_nbytes((cfg.rows, D_MODEL), F32) * 2 + _nbytes((cfg.rows, D_MODEL), BF16) * 2
              + _nbytes((cfg.rows, 2 * D_MODEL), gates.dtype))
    temps = (_nbytes(sq, BF16) * 3 + _nbytes(w1.shape[1:], BF16) * 2
             + _nbytes((cfg.rows, D_FF), BF16) + _nbytes((cfg.rows, 2 * CHUNK), F32) * 2
             + _nbytes((cfg.rows, D_MODEL), F32) * 5)
    vec = cfg.layer(1, D_MODEL)
    return pl.pallas_call(
        functools.partial(_post_kernel, sample=False),
        grid=cfg.grid,
        in_specs=[cfg.act(D_MODEL), cfg.act(CONV_DIM), cfg.act(ATTN_WIDTH),
                  cfg.act(2 * D_MODEL), cfg.mod_spec, vec, vec, vec,
                  cfg.layer(*sq), cfg.layer(*sq), cfg.layer(*sq),
                  mlp.layer(*w1.shape[1:]), mlp.layer(*w2.shape[1:])],
        out_specs=cfg.act(D_MODEL),
        out_shape=jax.ShapeDtypeStruct(x.shape, F32),
        scratch_shapes=[pltpu.VMEM((cfg.rows, D_FF), BF16)],
        compiler_params=pltpu.CompilerParams(
            dimension_semantics=cfg.sem, vmem_limit_bytes=_vmem_limit(blocks, temps)),
        name="post_prompt",
    )(x, z, a, gates, mod_all, g_post1, g_pre2, g_post2, w_bc, w_ba, w_o, w1, w2)


def _mix_kernel(x_ref, z_ref, a_ref, gate_ref, mod_ref, gpost1_ref, wbc_ref, wba_ref, wo_ref,
                o_ref):
    o_ref[...] = _merge_residual(x_ref, z_ref, a_ref, gate_ref, mod_ref, gpost1_ref,
                                 wbc_ref, wba_ref, wo_ref, True)


def _mix_sample(x, z, a, gates, mod_all, g_post1, w_bc, w_ba, w_o, layer_idx, n_seq):
    cfg = _dense(x, layer_idx, n_seq, True)
    sq = (D_MODEL, D_MODEL)
    blocks = (_nbytes((cfg.rows, D_MODEL), F32) * 2 + _nbytes((cfg.rows, D_MODEL), BF16) * 2
              + _nbytes((cfg.rows, 2 * D_MODEL), gates.dtype))
    temps = _nbytes(sq, BF16) * 3 + _nbytes((cfg.rows, D_MODEL), F32) * 6
    return pl.pallas_call(
        _mix_kernel,
        grid=cfg.grid,
        in_specs=[cfg.act(D_MODEL), cfg.act(CONV_DIM), cfg.act(ATTN_WIDTH),
                  cfg.act(2 * D_MODEL), cfg.mod_spec, cfg.layer(1, D_MODEL),
                  cfg.layer(*sq), cfg.layer(*sq), cfg.layer(*sq)],
        out_specs=cfg.act(D_MODEL),
        out_shape=jax.ShapeDtypeStruct(x.shape, F32),
        compiler_params=pltpu.CompilerParams(
            dimension_semantics=cfg.sem, vmem_limit_bytes=_vmem_limit(blocks, temps)),
        name="mix_sample",
    )(x, z, a, gates, mod_all, g_post1, w_bc, w_ba, w_o)


def _mlp_stream_kernel(x_ref, mod_ref, gpre2_ref, gpost2_ref, w1_ref, w2_ref,
                       o_ref, w1b_ref, w2b_ref, hb_ref, ff_ref):
    chunk = pl.program_id(0)

    @pl.when(chunk == 0)
    def _():
        hb_ref[...] = _mlp_input(x_ref[...], mod_ref, gpre2_ref, True)
        ff_ref[...] = jnp.zeros(ff_ref.shape, F32)

    w1 = w1_ref[...].astype(BF16)
    w2 = w2_ref[...].astype(BF16)
    w1b_ref[...] = w1
    w2b_ref[...] = w2
    act = jnp.maximum(_dot(hb_ref[...], w1), 0.0)
    ff_ref[...] += _dot((act * act).astype(BF16), w2)

    @pl.when(chunk == pl.num_programs(0) - 1)
    def _():
        o_ref[...] = _mlp_residual(x_ref[...], ff_ref[...], mod_ref, gpost2_ref, True)


def _mlp_stream_sample(x, mod_all, g_pre2, g_post2, w1, w2, layer_idx, n_seq):
    rows = x.shape[0] * x.shape[1]
    const = lambda shape, idx: pl.BlockSpec(shape, lambda c: idx, pipeline_mode=pl.Buffered(1))
    vec = const((None, 1, D_MODEL), (layer_idx, 0, 0))
    blocks = (_nbytes((D_MODEL, MLP_STREAM_CHUNK), F32) * 2
              + _nbytes((D_MODEL, MLP_STREAM_CHUNK), BF16) * 2 + _nbytes(x.shape, F32))
    temps = (_nbytes(x.shape, F32) * 4 + _nbytes((N_MOD, n_seq, D_MODEL), F32)
             + _nbytes((rows, D_MODEL), BF16) + _nbytes((rows, MLP_STREAM_CHUNK), F32) * 3)
    return pl.pallas_call(
        _mlp_stream_kernel,
        grid=(D_FF // MLP_STREAM_CHUNK,),
        in_specs=[const(x.shape, (0, 0, 0)),
                  const((None, N_MOD, n_seq, D_MODEL), (layer_idx, 0, 0, 0)), vec, vec,
                  pl.BlockSpec((None, D_MODEL, MLP_STREAM_CHUNK), lambda c: (layer_idx, 0, c)),
                  pl.BlockSpec((None, MLP_STREAM_CHUNK, D_MODEL), lambda c: (layer_idx, c, 0))],
        out_specs=[pl.BlockSpec(x.shape, lambda c: (0, 0, 0)),
                   pl.BlockSpec((None, D_MODEL, MLP_STREAM_CHUNK), lambda c: (0, 0, c)),
                   pl.BlockSpec((None, MLP_STREAM_CHUNK, D_MODEL), lambda c: (0, c, 0))],
        out_shape=[jax.ShapeDtypeStruct(x.shape, F32),
                   jax.ShapeDtypeStruct((1, D_MODEL, D_FF), BF16),
                   jax.ShapeDtypeStruct((1, D_FF, D_MODEL), BF16)],
        scratch_shapes=[pltpu.VMEM((rows, D_MODEL), BF16), pltpu.VMEM((rows, D_MODEL), F32)],
        compiler_params=pltpu.CompilerParams(
            dimension_semantics=("arbitrary",), vmem_limit_bytes=_vmem_limit(blocks, temps)),
        name="mlp_stream_sample",
    )(x, mod_all, g_pre2, g_post2, w1, w2)


def kernel(x_prompt, x_sample, c_prompt, c_sample, state_conv, cache_k, cache_v, w_ada, b_ada,
           g_pre1, w_in, conv_w, w_br_conv, w_br_attn, w_o, sinks, g_post1, g_pre2, w_ff1, w_ff2,
           g_post2, rel_table):
    depth = w_ada.shape[0]
    batch, seq, _ = x_prompt.shape
    n_seq, n_new, _ = x_sample.shape
    buf = cache_k.shape[2]

    pad = (-(n_seq + batch)) % V7X_SUBLANES
    c_all = jnp.concatenate([c_sample, c_prompt, jnp.zeros((pad, D_MODEL), F32)], axis=0)
    mod_all = _ada(c_all, w_ada, b_ada)

    dist_p = (jnp.arange(BLOCK)[:, None] + BLOCK) - jnp.arange(2 * BLOCK)[None, :]
    bucket_p = _rel_bucket(dist_p)
    upper = jnp.arange(BLOCK)[None, :] > jnp.arange(BLOCK)[:, None]
    bucket_merged = jnp.where(upper, bucket_p[:, :BLOCK], bucket_p[:, BLOCK:])
    bias_p = _bias_table(rel_table, bucket_merged.T)
    dist_s = (buf + jnp.arange(n_new))[:, None] - jnp.arange(buf + n_new)[None, :]
    bias_s = _bias_table(rel_table, _rel_bucket(dist_s))
    bias_s = (bias_s.reshape(N_KV_HEADS, GROUP, n_new, buf + n_new)
              .transpose(0, 2, 1, 3).reshape(N_KV_HEADS, n_new * GROUP, buf + n_new))
    bias_s = jnp.pad(bias_s, ((0, 0), (0, 0), (0, buf - n_new)))

    to_bf16 = lambda w: w.astype(BF16)
    w_in_b, w_bc_b, w_ba_b, w_o_b = map(to_bf16, (w_in, w_br_conv, w_br_attn, w_o))
    vec = lambda g: g.reshape(depth, 1, D_MODEL)
    g_pre1, g_post1, g_pre2, g_post2 = map(vec, (g_pre1, g_post1, g_pre2, g_post2))
    pre_s = state_conv.transpose(0, 2, 1, 3)
    cache_k = cache_k.transpose(0, 1, 3, 4, 2)
    cache_v = cache_v.transpose(0, 1, 3, 4, 2)

    xp = x_prompt
    xs = x_sample.transpose(1, 0, 2)
    conv_p, k_p, v_p, conv_s = [], [], [], []
    windows = None
    win = min(WINDOW, seq)
    for l in range(depth):
        z, q, kv, gates, tail = _in_proj(xs, mod_all, g_pre1, w_in_b, conv_w, l, n_seq, pre_s)
        q_s = (q.reshape(n_new, n_seq, N_KV_HEADS, GROUP, HEAD_DIM)
               .transpose(1, 2, 0, 3, 4).reshape(n_seq, N_KV_HEADS, n_new * GROUP, HEAD_DIM))
        kv_s = kv.transpose(1, 0, 2)
        sink_rows = jnp.tile(sinks[l].reshape(N_KV_HEADS, 1, GROUP),
                             (1, n_new, 1)).reshape(N_KV_HEADS, n_new * GROUP, 1)
        o, *windows = _sample_attn(q_s, cache_k, cache_v, kv_s[:, :, :KV_WIDTH],
                                   kv_s[:, :, KV_WIDTH:], bias_s, sink_rows, l, windows)
        attn = (o.reshape(n_seq, N_KV_HEADS, n_new, GROUP, HEAD_DIM)
                .transpose(2, 0, 1, 3, 4).reshape(n_new, n_seq, ATTN_WIDTH))
        x1 = _mix_sample(xs, z, attn, gates, mod_all, g_post1, w_bc_b, w_ba_b, w_o_b, l, n_seq)
        xs, w1_b, w2_b = _mlp_stream_sample(x1, mod_all, g_pre2, g_post2, w_ff1, w_ff2, l, n_seq)
        conv_s.append(tail.transpose(1, 0, 2))

        z, attn, kv, gates, tail = _in_proj(xp, mod_all, g_pre1, w_in_b, conv_w, l, n_seq,
                                            attn=(sinks, bias_p))
        xp = _post(xp, z, attn, gates, mod_all, g_post1, g_pre2, g_post2, w_bc_b, w_ba_b, w_o_b,
                   w1_b, w2_b, l, 0, n_seq)
        conv_p.append(tail[:, V7X_SUBLANES - (CONV_WIDTH - 1):])
        k_p.append(kv[:, seq - win:, :KV_WIDTH].reshape(batch, win, N_KV_HEADS, HEAD_DIM))
        v_p.append(kv[:, seq - win:, KV_WIDTH:].reshape(batch, win, N_KV_HEADS, HEAD_DIM))

    k_s, v_s = (w.transpose(0, 1, 4, 2, 3) for w in windows)
    return (xp, xs.transpose(1, 0, 2), jnp.stack(conv_p), jnp.stack(k_p), jnp.stack(v_p),
            jnp.stack(conv_s), k_s, v_s)
```

```python
import functools
import math
from typing import Any, Callable, NamedTuple

import jax
import jax.numpy as jnp
from jax import lax
from jax.experimental import pallas as pl
from jax.experimental.pallas import tpu as pltpu

D_MODEL = 1024
N_HEADS = 16
N_KV_HEADS = 2
HEAD_DIM = 64
GROUP = N_HEADS // N_KV_HEADS
ATTN_WIDTH = N_HEADS * HEAD_DIM
KV_WIDTH = N_KV_HEADS * HEAD_DIM
CONV_DIM = D_MODEL
CONV_WIDTH = 3
WINDOW = 128
BLOCK = 128
N_BUCKETS = 32
MAX_DISTANCE = 128
D_FF = 4 * D_MODEL
N_MOD = 6
RMS_EPS = 1e-6
NEG_INF = -1e30
PROJ_COLS = 3 * CONV_DIM + ATTN_WIDTH + 2 * KV_WIDTH + 2 * D_MODEL
Q_OFF = 3 * CONV_DIM
KV_OFF = Q_OFF + ATTN_WIDTH
GATE_OFF = KV_OFF + 2 * KV_WIDTH

V7X_SUBLANES = 8
V7X_VMEM_BYTES = 64 * 1024 * 1024

PROMPT_TILE = 512
SAMPLE_POS_TILE = 4
SAMPLE_SEQ_TILE = 32
CHUNK = 512
MLP_STREAM_CHUNK = 1024

F32 = jnp.float32
BF16 = jnp.bfloat16


def _vmem_limit(block_bytes, temp_bytes):
    return int(min(2 * block_bytes + temp_bytes, V7X_VMEM_BYTES - 4 * 1024 * 1024))


def _nbytes(shape, dtype):
    return math.prod(shape) * jnp.dtype(dtype).itemsize


def _rms(x):
    return x * lax.rsqrt(jnp.mean(x * x, axis=-1, keepdims=True) + RMS_EPS)


def _dot(a, b):
    return jnp.dot(a, b, preferred_element_type=F32)


def _flat(a):
    return a.reshape(-1, a.shape[-1])


def _ada_kernel(c_ref, w_ref, b_ref, o_ref):
    c = c_ref[...]
    s = c * (1.0 / (1.0 + jnp.exp(-c)))
    o_ref[...] = _dot(s.astype(BF16), w_ref[...].astype(BF16)) + b_ref[...]


def _ada(c_all, w_ada, b_ada):
    depth = w_ada.shape[0]
    rows = c_all.shape[0]
    blocks = (_nbytes((rows, D_MODEL), F32) * 2 + _nbytes((D_MODEL, D_MODEL), F32))
    return pl.pallas_call(
        _ada_kernel,
        grid=(depth, N_MOD),
        in_specs=[
            pl.BlockSpec((rows, D_MODEL), lambda l, j: (0, 0)),
            pl.BlockSpec((None, D_MODEL, D_MODEL), lambda l, j: (l, 0, j)),
            pl.BlockSpec((None, None, 1, D_MODEL), lambda l, j: (l, j, 0, 0)),
        ],
        out_specs=pl.BlockSpec((None, None, rows, D_MODEL), lambda l, j: (l, j, 0, 0)),
        out_shape=jax.ShapeDtypeStruct((depth, N_MOD, rows, D_MODEL), F32),
        compiler_params=pltpu.CompilerParams(
            dimension_semantics=("arbitrary", "arbitrary"),
            vmem_limit_bytes=_vmem_limit(blocks, _nbytes((D_MODEL, D_MODEL), BF16) * 2)),
        name="ada_mod",
    )(c_all, w_ada, b_ada.reshape(depth, N_MOD, 1, D_MODEL))


def _bias_kernel(tab_ref, bucket_ref, o_ref):
    bucket = bucket_ref[...]
    for h in range(N_HEADS):
        acc = jnp.zeros(bucket.shape, F32)
        for b in range(N_BUCKETS):
            acc = jnp.where(bucket == b, tab_ref[b * N_HEADS + h], acc)
        o_ref[h] = acc


def _bias_table(rel_table, bucket):
    return pl.pallas_call(
        _bias_kernel,
        in_specs=[
            pl.BlockSpec(memory_space=pltpu.SMEM),
            pl.BlockSpec(bucket.shape, lambda: (0, 0)),
        ],
        out_specs=pl.BlockSpec((N_HEADS,) + bucket.shape, lambda: (0, 0, 0)),
        out_shape=jax.ShapeDtypeStruct((N_HEADS,) + bucket.shape, F32),
        name="bias_table",
    )(rel_table.reshape(-1), bucket)


def _rel_bucket(dist):
    n = jnp.maximum(dist, 0)
    max_exact = N_BUCKETS // 2
    nf = jnp.maximum(n, 1).astype(F32)
    scaled = (jnp.log(nf / max_exact) / math.log(MAX_DISTANCE / max_exact)
              * (N_BUCKETS - max_exact))
    large = jnp.minimum(max_exact + jnp.floor(scaled).astype(jnp.int32), N_BUCKETS - 1)
    return jnp.where(n < max_exact, n, large)


class _Dense(NamedTuple):
    sample: bool
    grid: tuple
    rows: int
    act: Callable[[int], Any]
    layer: Callable[..., Any]
    mod_spec: Any
    sem: tuple


def _dense(x, layer_idx, n_seq, sample):
    lead, mid = x.shape[0], x.shape[1]
    if sample:
        grid = (lead // SAMPLE_POS_TILE,)
        act = lambda w: pl.BlockSpec((SAMPLE_POS_TILE, mid, w), lambda i: (i, 0, 0))
        layer = lambda *shape: pl.BlockSpec((None,) + shape,
                                            lambda i: (layer_idx,) + (0,) * len(shape),
                                            pipeline_mode=pl.Buffered(1))
        mod_spec = pl.BlockSpec((None, N_MOD, n_seq, D_MODEL), lambda i: (layer_idx, 0, 0, 0))
        return _Dense(True, grid, SAMPLE_POS_TILE * mid, act, layer, mod_spec, ("arbitrary",))
    assert n_seq % V7X_SUBLANES == 0 and lead <= V7X_SUBLANES
    grid = (lead, mid // PROMPT_TILE)
    act = lambda w: pl.BlockSpec((None, PROMPT_TILE, w), lambda b, t: (b, t, 0))
    layer = lambda *shape: pl.BlockSpec((None,) + shape,
                                        lambda b, t: (layer_idx,) + (0,) * len(shape),
                                        pipeline_mode=pl.Buffered(1))
    mod_spec = pl.BlockSpec((None, N_MOD, V7X_SUBLANES, D_MODEL),
                            lambda b, t: (layer_idx, 0, n_seq // V7X_SUBLANES, 0))
    return _Dense(False, grid, PROMPT_TILE, act, layer, mod_spec, ("arbitrary", "arbitrary"))


def _mod(mod_ref, j, sample):
    if sample:
        return mod_ref[j]
    return mod_ref[j, pl.ds(pl.program_id(0), 1), :]


def _sink_softmax(s, sink):
    m = jnp.maximum(jnp.max(s, axis=-1, keepdims=True), sink)
    e = jnp.exp(s - m)
    den = jnp.sum(e, axis=-1, keepdims=True) + jnp.exp(sink - m)
    return e * (1.0 / den)


def _stage_keys(kv, kd_ref, vt_ref, first_tile):
    n_blk = kv.shape[0] // BLOCK

    @pl.when(first_tile)
    def _():
        kd_ref[:, 0:BLOCK] = jnp.zeros((N_KV_HEADS, BLOCK, 2 * HEAD_DIM), BF16)
        vt_ref[0] = jnp.zeros((KV_WIDTH, BLOCK), BF16)

    @pl.when(jnp.logical_not(first_tile))
    def _():
        kd_ref[:, 0:BLOCK] = kd_ref[:, n_blk * BLOCK:]
        vt_ref[0] = vt_ref[n_blk]

    for g in range(N_KV_HEADS):
        k_g = kv[:, g * HEAD_DIM:(g + 1) * HEAD_DIM]
        kd_ref[g, BLOCK:] = jnp.concatenate([k_g, k_g], axis=1).astype(BF16)
    v_t = kv[:, KV_WIDTH:].T.astype(BF16)
    for blk in range(n_blk):
        vt_ref[blk + 1] = v_t[:, blk * BLOCK:(blk + 1) * BLOCK]


def _attention_phases(q_ref, kd_ref, vt_ref, s_ref, bias_ref, sink_ref, o_ref, first_tile):
    kj = lax.broadcasted_iota(jnp.int32, (BLOCK, BLOCK), 0)
    qi = lax.broadcasted_iota(jnp.int32, (BLOCK, BLOCK), 1)
    upper = kj > qi
    low_lanes = lax.broadcasted_iota(jnp.int32, (BLOCK, 2 * HEAD_DIM), 1) < HEAD_DIM

    def score_phase(c):
        q = q_ref[c * BLOCK:(c + 1) * BLOCK, :]
        for pair in range(N_HEADS // 2):
            g = (2 * pair) // GROUP
            q_pair = q[:, pair * 2 * HEAD_DIM:(pair + 1) * 2 * HEAD_DIM]
            zero = jnp.zeros_like(q_pair)
            q_both = jnp.concatenate([jnp.where(low_lanes, q_pair, zero),
                                      jnp.where(low_lanes, zero, q_pair)], axis=0)
            s_both = lax.dot_general(kd_ref[g, c * BLOCK:(c + 2) * BLOCK, :], q_both,
                                     (((1,), (1,)), ((), ())), preferred_element_type=F32)
            for par in range(2):
                h = 2 * pair + par
                s_h = s_both[:, par * BLOCK:(par + 1) * BLOCK]
                s = jnp.where(upper, s_h[:BLOCK], s_h[BLOCK:]) + bias_ref[h]
                if c == 0:
                    s = jnp.where(upper & first_tile, NEG_INF, s)
                s_ref[c, h] = s

    def value_phase(c, pair):
        g = (2 * pair) // GROUP
        v_cat = jnp.concatenate([vt_ref[c, g * HEAD_DIM:(g + 1) * HEAD_DIM],
                                 vt_ref[c + 1, g * HEAD_DIM:(g + 1) * HEAD_DIM]], axis=1)
        weights, scales = [], []
        for h in (2 * pair, 2 * pair + 1):
            s = s_ref[c, h]
            sink = sink_ref[h]
            m = jnp.maximum(jnp.max(s, axis=0, keepdims=True), sink)
            e = jnp.exp(s - m)
            den = jnp.sum(e, axis=0, keepdims=True) + jnp.exp(sink - m)
            weights.append(jnp.concatenate([jnp.where(upper, e, 0.0), jnp.where(upper, 0.0, e)],
                                           axis=0).astype(BF16))
            scales.append(1.0 / den)
        o_t = _dot(v_cat, jnp.concatenate(weights, axis=1))
        o_t = jnp.concatenate([o_t[:, :BLOCK] * scales[0], o_t[:, BLOCK:] * scales[1]], axis=0)
        o_ref[c * BLOCK:(c + 1) * BLOCK, pair * 2 * HEAD_DIM:(pair + 1) * 2 * HEAD_DIM] = (
            o_t.T.astype(BF16))

    return score_phase, value_phase


def _in_proj_kernel(*refs, sample):
    if sample:
        (x_ref, mod_ref, g_ref, w_ref, cw_ref, pre_ref,
         z_ref, q_ref, kv_ref, gate_ref, tail_ref, carry_ref) = refs
    else:
        (x_ref, mod_ref, g_ref, w_ref, cw_ref, sink_ref, bias_ref,
         z_ref, a_ref, kv_ref, gate_ref, tail_ref,
         carry_ref, q_ref, kd_ref, vt_ref, s_ref) = refs
    step = pl.program_id(0) if sample else pl.program_id(1)

    @pl.when(step == 0)
    def _():
        if sample:
            carry_ref[...] = pre_ref[...]
        else:
            carry_ref[...] = jnp.zeros(carry_ref.shape, F32)

    x = x_ref[...]
    h = (_rms(x) * g_ref[...]) * (1.0 + _mod(mod_ref, 1, sample)) + _mod(mod_ref, 0, sample)
    hb = _flat(h).astype(BF16)
    rows = hb.shape[0]

    def store(ref, lo, val):
        ref[..., lo:lo + val.shape[-1]] = val.reshape(ref.shape[:-1] + (val.shape[-1],))

    def conv_chunk(lo):
        b_g = _dot(hb, w_ref[:, lo:lo + CHUNK])
        c_g = _dot(hb, w_ref[:, CONV_DIM + lo:CONV_DIM + lo + CHUNK])
        x_c = _dot(hb, w_ref[:, 2 * CONV_DIM + lo:2 * CONV_DIM + lo + CHUNK])
        u = c_g * x_c
        if sample:
            pos = x_ref.shape[0]
            u3 = u.reshape(pos, -1, CHUNK)
            ext = jnp.concatenate([carry_ref[:, :, lo:lo + CHUNK], u3], axis=0)
            u2 = _flat(ext[0:pos])
            u1 = _flat(ext[1:pos + 1])
            new_tail = u3[pos - (CONV_WIDTH - 1):]
            carry_ref[:, :, lo:lo + CHUNK] = new_tail
            tail_ref[:, :, lo:lo + CHUNK] = new_tail
        else:
            prev = carry_ref[:, lo:lo + CHUNK]
            p1 = prev[V7X_SUBLANES - 1:V7X_SUBLANES]
            p2 = prev[V7X_SUBLANES - 2:V7X_SUBLANES - 1]
            r = lax.broadcasted_iota(jnp.int32, u.shape, 0)
            u1 = jnp.where(r == 0, p1, pltpu.roll(u, 1, 0))
            u2 = jnp.where(r == 0, p2, jnp.where(r == 1, p1, pltpu.roll(u, 2, 0)))
            new_tail = u[rows - V7X_SUBLANES:]
            carry_ref[:, lo:lo + CHUNK] = new_tail
            tail_ref[:, lo:lo + CHUNK] = new_tail
        conv = (cw_ref[0:1, lo:lo + CHUNK] * u2 + cw_ref[1:2, lo:lo + CHUNK] * u1
                + cw_ref[2:3, lo:lo + CHUNK] * u)
        store(z_ref, lo, (b_g * conv).astype(BF16))

    def q_chunk(lo):
        q = _dot(hb, w_ref[:, Q_OFF + lo:Q_OFF + lo + CHUNK])
        store(q_ref, lo, (q * (HEAD_DIM ** -0.5)).astype(BF16))

    def gate_chunk(lo):
        g = _dot(hb, w_ref[:, GATE_OFF + lo:GATE_OFF + lo + CHUNK])
        store(gate_ref, lo, (1.0 / (1.0 + jnp.exp(-g))).astype(BF16))

    kv = _dot(hb, w_ref[:, KV_OFF:KV_OFF + 2 * KV_WIDTH])
    store(kv_ref, 0, kv)
    for lo in range(0, ATTN_WIDTH, CHUNK):
        q_chunk(lo)
    dense = ([functools.partial(conv_chunk, lo) for lo in range(0, CONV_DIM, CHUNK)]
             + [functools.partial(gate_chunk, lo) for lo in range(0, 2 * D_MODEL, CHUNK)])
    if sample:
        for work in dense:
            work()
        return

    first_tile = step == 0
    _stage_keys(kv, kd_ref, vt_ref, first_tile)
    score_phase, value_phase = _attention_phases(q_ref, kd_ref, vt_ref, s_ref, bias_ref,
                                                 sink_ref, a_ref, first_tile)
    n_blk = rows // BLOCK
    attention = [functools.partial(score_phase, 0)]
    for c in range(n_blk):
        if c + 1 < n_blk:
            attention.append(functools.partial(score_phase, c + 1))
        attention += [functools.partial(value_phase, c, pair) for pair in range(N_HEADS // 2)]
    done = 0
    for i, work in enumerate(attention):
        work()
        while done < len(dense) and done * len(attention) < (i + 1) * len(dense):
            dense[done]()
            done += 1


def _in_proj(x, mod_all, g_pre1, w_in, conv_w, layer_idx, n_seq, pre=None, attn=None):
    cfg = _dense(x, layer_idx, n_seq, pre is not None)
    lead, mid = x.shape[0], x.shape[1]
    widths = (CONV_DIM, ATTN_WIDTH, 2 * KV_WIDTH, 2 * D_MODEL)
    dtypes = (BF16, BF16, F32, BF16)
    blocks = (_nbytes((cfg.rows, D_MODEL), F32) + _nbytes(w_in.shape[1:], BF16)
              + sum(_nbytes((cfg.rows, w), dt) for w, dt in zip(widths, dtypes)))
    temps = _nbytes((cfg.rows, D_MODEL), F32) * 2 + _nbytes((cfg.rows, CHUNK), F32) * 10
    if cfg.sample:
        tail_shape = (CONV_WIDTH - 1, mid, CONV_DIM)
        tail_spec = pl.BlockSpec(tail_shape, lambda i: (0, 0, 0))
        extra_in = [pre]
        extra_specs = [cfg.layer(*tail_shape)]
        scratch = [pltpu.VMEM(tail_shape, F32)]
    else:
        sinks, bias = attn
        per_tile = cfg.rows // BLOCK
        tail_shape = (lead, V7X_SUBLANES, CONV_DIM)
        tail_spec = pl.BlockSpec((None, V7X_SUBLANES, CONV_DIM), lambda b, t: (b, 0, 0))
        extra_in = [sinks[layer_idx], bias]
        extra_specs = [pl.BlockSpec(memory_space=pltpu.SMEM),
                       pl.BlockSpec(bias.shape, lambda b, t: (0, 0, 0),
                                    pipeline_mode=pl.Buffered(1))]
        scratch = [pltpu.VMEM((V7X_SUBLANES, CONV_DIM), F32),
                   pltpu.VMEM((cfg.rows, ATTN_WIDTH), BF16),
                   pltpu.VMEM((N_KV_HEADS, cfg.rows + BLOCK, 2 * HEAD_DIM), BF16),
                   pltpu.VMEM((per_tile + 1, KV_WIDTH, BLOCK), BF16),
                   pltpu.VMEM((per_tile, N_HEADS, BLOCK, BLOCK), F32)]
        temps += (_nbytes(bias.shape, F32) + _nbytes((cfg.rows, ATTN_WIDTH), BF16)
                  + _nbytes((per_tile, N_HEADS, BLOCK, BLOCK), F32)
                  + _nbytes((BLOCK, 2 * BLOCK), F32) * 16 + _nbytes((BLOCK, ATTN_WIDTH), F32) * 4)

    out_shape = [jax.ShapeDtypeStruct(x.shape[:-1] + (w,), dt) for w, dt in zip(widths, dtypes)]
    out_shape.append(jax.ShapeDtypeStruct(tail_shape, F32))
    return pl.pallas_call(
        functools.partial(_in_proj_kernel, sample=cfg.sample),
        grid=cfg.grid,
        in_specs=[cfg.act(D_MODEL), cfg.mod_spec, cfg.layer(1, D_MODEL),
                  cfg.layer(*w_in.shape[1:]), cfg.layer(*conv_w.shape[1:])] + extra_specs,
        out_specs=[cfg.act(w) for w in widths] + [tail_spec],
        out_shape=out_shape,
        scratch_shapes=scratch,
        compiler_params=pltpu.CompilerParams(
            dimension_semantics=cfg.sem, vmem_limit_bytes=_vmem_limit(blocks, temps)),
        name="in_proj_sample" if cfg.sample else "in_proj_attn_prompt",
    )(x, mod_all, g_pre1, w_in, conv_w, *extra_in)


def _sample_attn_kernel(q_ref, kc_ref, vc_ref, kn_ref, vn_ref, bias_ref, sink_ref, *rest,
                        layer_idx):
    o_ref, ko_ref, vo_ref = rest[-3:]
    if ko_ref.ndim > kc_ref.ndim:
        for other in range(ko_ref.shape[0]):
            if other != layer_idx:
                ko_ref[other] = jnp.zeros(ko_ref.shape[1:], F32)
                vo_ref[other] = jnp.zeros(vo_ref.shape[1:], F32)
        ko_ref, vo_ref = ko_ref.at[layer_idx], vo_ref.at[layer_idx]
    n_seq = q_ref.shape[0]
    buf = kc_ref.shape[-1]
    new = kn_ref.shape[1]
    rows = new * GROUP
    pos = lax.broadcasted_iota(jnp.int32, (rows, 2 * buf), 0) // GROUP
    kj = lax.broadcasted_iota(jnp.int32, (rows, 2 * buf), 1)
    dist = buf + pos - kj
    mask = (dist >= 0) & (dist < WINDOW) & (kj < buf + new)
    lane = lax.broadcasted_iota(jnp.int32, (n_seq, HEAD_DIM, buf), 2)
    pad = jnp.zeros((n_seq, HEAD_DIM, buf - new), F32)

    def extended(cache_t, fresh, out_ref, g):
        fresh_t = jnp.concatenate([jnp.swapaxes(fresh, 1, 2), pad], axis=2)
        out_ref[:, g] = pltpu.roll(jnp.where(lane < new, fresh_t, cache_t), buf - new, 2)
        return jnp.concatenate([cache_t, fresh_t], axis=2).astype(BF16)

    for g in range(N_KV_HEADS):
        lanes = slice(g * HEAD_DIM, (g + 1) * HEAD_DIM)
        k_ext = extended(kc_ref[:, g], kn_ref[:, :, lanes], ko_ref, g)
        v_ext = extended(vc_ref[:, g], vn_ref[:, :, lanes], vo_ref, g)
        s = jnp.einsum('nqd,ndk->nqk', q_ref[:, g], k_ext, preferred_element_type=F32)
        p = _sink_softmax(jnp.where(mask[None], s + bias_ref[g][None], NEG_INF),
                          sink_ref[g][None])
        o = jnp.einsum('nqk,ndk->nqd', p.astype(BF16), v_ext, preferred_element_type=F32)
        o_ref[:, g] = o.astype(BF16)


def _sample_attn(q, k_cache, v_cache, k_new, v_new, bias, sink_rows, layer_idx, windows):
    _, n, _, _, buf = k_cache.shape
    new = k_new.shape[1]
    rows = new * GROUP
    nt = SAMPLE_SEQ_TILE
    seq3 = lambda a, b: pl.BlockSpec((nt, a, b), lambda i: (i, 0, 0))
    cache = pl.BlockSpec((None, nt, N_KV_HEADS, HEAD_DIM, buf),
                         lambda i: (layer_idx, i, 0, 0, 0))
    q_spec = pl.BlockSpec((nt, N_KV_HEADS, rows, HEAD_DIM), lambda i: (i, 0, 0, 0))
    in_specs = [q_spec, cache, cache, seq3(new, KV_WIDTH), seq3(new, KV_WIDTH),
                pl.BlockSpec(bias.shape, lambda i: (0, 0, 0)),
                pl.BlockSpec(sink_rows.shape, lambda i: (0, 0, 0))]
    args = [q, k_cache, v_cache, k_new, v_new, bias, sink_rows]
    depth = k_cache.shape[0]
    if windows is None:
        aliases = {}
        window_out = pl.BlockSpec((depth, nt, N_KV_HEADS, HEAD_DIM, buf),
                                  lambda i: (0, i, 0, 0, 0))
    else:
        aliases = {len(args): 1, len(args) + 1: 2}
        in_specs += [pl.BlockSpec(memory_space=pl.ANY)] * 2
        args += list(windows)
        window_out = cache
    blocks = (_nbytes((nt, N_KV_HEADS, HEAD_DIM, buf), F32) * (2 + 2 * depth)
              + _nbytes((nt, new, KV_WIDTH), F32) * 2
              + _nbytes((nt, N_KV_HEADS, rows, 2 * HEAD_DIM), BF16) * 2 + _nbytes(bias.shape, F32))
    temps = _nbytes((rows, 2 * buf), F32) * 8 * nt
    return pl.pallas_call(
        functools.partial(_sample_attn_kernel, layer_idx=layer_idx),
        grid=(n // nt,),
        in_specs=in_specs,
        out_specs=[q_spec, window_out, window_out],
        out_shape=[jax.ShapeDtypeStruct(q.shape, BF16),
                   jax.ShapeDtypeStruct(k_cache.shape, F32),
                   jax.ShapeDtypeStruct(v_cache.shape, F32)],
        input_output_aliases=aliases,
        compiler_params=pltpu.CompilerParams(
            dimension_semantics=("arbitrary",), vmem_limit_bytes=_vmem_limit(blocks, temps)),
        name="attn_sample",
    )(*args)


def _merge_residual(x_ref, z_ref, a_ref, gate_ref, mod_ref, gpost1_ref, wbc_ref, wba_ref, wo_ref,
                    sample):
    y_conv = _dot(_flat(z_ref[...]), wbc_ref[...])
    y_attn = _dot(_flat(a_ref[...]), wba_ref[...])
    gates = _flat(gate_ref[...])
    merged = gates[:, :D_MODEL] * y_conv + gates[:, D_MODEL:] * y_attn
    mixed = _dot(merged.astype(BF16), wo_ref[...])
    r = (_rms(mixed) * gpost1_ref[...]).reshape(x_ref.shape)
    return x_ref[...] + _mod(mod_ref, 2, sample) * r


def _mlp_input(x1, mod_ref, gpre2_ref, sample):
    h = (_rms(x1) * gpre2_ref[...]) * (1.0 + _mod(mod_ref, 4, sample)) + _mod(mod_ref, 3, sample)
    return _flat(h).astype(BF16)


def _mlp_residual(x1, ff, mod_ref, gpost2_ref, sample):
    return x1 + _mod(mod_ref, 5, sample) * (_rms(ff) * gpost2_ref[...]).reshape(x1.shape)


def _post_kernel(x_ref, z_ref, a_ref, gate_ref, mod_ref, gpost1_ref, gpre2_ref, gpost2_ref,
                 wbc_ref, wba_ref, wo_ref, w1_ref, w2_ref, o_ref, hid_ref, *, sample):
    x1 = _merge_residual(x_ref, z_ref, a_ref, gate_ref, mod_ref, gpost1_ref,
                         wbc_ref, wba_ref, wo_ref, sample)
    hb = _mlp_input(x1, mod_ref, gpre2_ref, sample)
    for lo in range(0, D_FF, 2 * CHUNK):
        act = jnp.maximum(_dot(hb, w1_ref[:, lo:lo + 2 * CHUNK]), 0.0)
        hid_ref[:, lo:lo + 2 * CHUNK] = (act * act).astype(BF16)
    ff = _dot(hid_ref[...], w2_ref[...])
    o_ref[...] = _mlp_residual(x1, ff, mod_ref, gpost2_ref, sample)


def _post(x, z, a, gates, mod_all, g_post1, g_pre2, g_post2, w_bc, w_ba, w_o, w1, w2,
          layer_idx, mlp_layer_idx, n_seq):
    cfg = _dense(x, layer_idx, n_seq, False)
    mlp = _dense(x, mlp_layer_idx, n_seq, False)
    sq = (D_MODEL, D_MODEL)
    blocks = (_nbytes((cfg.rows, D_MODEL), F32) * 2 + _nbytes((cfg.rows, D_MODEL), BF16) * 2
              + _nbytes((cfg.rows, 2 * D_MODEL), gates.dtype))
    temps = (_nbytes(sq, BF16) * 3 + _nbytes(w1.shape[1:], BF16) * 2
             + _nbytes((cfg.rows, D_FF), BF16) + _nbytes((cfg.rows, 2 * CHUNK), F32) * 2
             + _nbytes((cfg.rows, D_MODEL), F32) * 5)
    vec = cfg.layer(1, D_MODEL)
    return pl.pallas_call(
        functools.partial(_post_kernel, sample=False),
        grid=cfg.grid,
        in_specs=[cfg.act(D_MODEL), cfg.act(CONV_DIM), cfg.act(ATTN_WIDTH),
                  cfg.act(2 * D_MODEL), cfg.mod_spec, vec, vec, vec,
                  cfg.layer(*sq), cfg.layer(*sq), cfg.layer(*sq),
                  mlp.layer(*w1.shape[1:]), mlp.layer(*w2.shape[1:])],
        out_specs=cfg.act(D_MODEL),
        out_shape=jax.ShapeDtypeStruct(x.shape, F32),
        scratch_shapes=[pltpu.VMEM((cfg.rows, D_FF), BF16)],
        compiler_params=pltpu.CompilerParams(
            dimension_semantics=cfg.sem, vmem_limit_bytes=_vmem_limit(blocks, temps)),
        name="post_prompt",
    )(x, z, a, gates, mod_all, g_post1, g_pre2, g_post2, w_bc, w_ba, w_o, w1, w2)


def _mix_kernel(x_ref, z_ref, a_ref, gate_ref, mod_ref, gpost1_ref, wbc_ref, wba_ref, wo_ref,
                o_ref):
    o_ref[...] = _merge_residual(x_ref, z_ref, a_ref, gate_ref, mod_ref, gpost1_ref,
                                 wbc_ref, wba_ref, wo_ref, True)


def _mix_sample(x, z, a, gates, mod_all, g_post1, w_bc, w_ba, w_o, layer_idx, n_seq):
    cfg = _dense(x, layer_idx, n_seq, True)
    sq = (D_MODEL, D_MODEL)
    blocks = (_nbytes((cfg.rows, D_MODEL), F32) * 2 + _nbytes((cfg.rows, D_MODEL), BF16) * 2
              + _nbytes((cfg.rows, 2 * D_MODEL), gates.dtype))
    temps = _nbytes(sq, BF16) * 3 + _nbytes((cfg.rows, D_MODEL), F32) * 6
    return pl.pallas_call(
        _mix_kernel,
        grid=cfg.grid,
        in_specs=[cfg.act(D_MODEL), cfg.act(CONV_DIM), cfg.act(ATTN_WIDTH),
                  cfg.act(2 * D_MODEL), cfg.mod_spec, cfg.layer(1, D_MODEL),
                  cfg.layer(*sq), cfg.layer(*sq), cfg.layer(*sq)],
        out_specs=cfg.act(D_MODEL),
        out_shape=jax.ShapeDtypeStruct(x.shape, F32),
        compiler_params=pltpu.CompilerParams(
            dimension_semantics=cfg.sem, vmem_limit_bytes=_vmem_limit(blocks, temps)),
        name="mix_sample",
    )(x, z, a, gates, mod_all, g_post1, w_bc, w_ba, w_o)


def _mlp_stream_kernel(x_ref, mod_ref, gpre2_ref, gpost2_ref, w1_ref, w2_ref,
                       o_ref, w1b_ref, w2b_ref, hb_ref, ff_ref):
    chunk = pl.program_id(0)

    @pl.when(chunk == 0)
    def _():
        hb_ref[...] = _mlp_input(x_ref[...], mod_ref, gpre2_ref, True)
        ff_ref[...] = jnp.zeros(ff_ref.shape, F32)

    w1 = w1_ref[...].astype(BF16)
    w2 = w2_ref[...].astype(BF16)
    w1b_ref[...] = w1
    w2b_ref[...] = w2
    act = jnp.maximum(_dot(hb_ref[...], w1), 0.0)
    ff_ref[...] += _dot((act * act).astype(BF16), w2)

    @pl.when(chunk == pl.num_programs(0) - 1)
    def _():
        o_ref[...] = _mlp_residual(x_ref[...], ff_ref[...], mod_ref, gpost2_ref, True)


def _mlp_stream_sample(x, mod_all, g_pre2, g_post2, w1, w2, layer_idx, n_seq):
    rows = x.shape[0] * x.shape[1]
    const = lambda shape, idx: pl.BlockSpec(shape, lambda c: idx, pipeline_mode=pl.Buffered(1))
    vec = const((None, 1, D_MODEL), (layer_idx, 0, 0))
    blocks = (_nbytes((D_MODEL, MLP_STREAM_CHUNK), F32) * 2
              + _nbytes((D_MODEL, MLP_STREAM_CHUNK), BF16) * 2 + _nbytes(x.shape, F32))
    temps = (_nbytes(x.shape, F32) * 4 + _nbytes((N_MOD, n_seq, D_MODEL), F32)
             + _nbytes((rows, D_MODEL), BF16) + _nbytes((rows, MLP_STREAM_CHUNK), F32) * 3)
    return pl.pallas_call(
        _mlp_stream_kernel,
        grid=(D_FF // MLP_STREAM_CHUNK,),
        in_specs=[const(x.shape, (0, 0, 0)),
                  const((None, N_MOD, n_seq, D_MODEL), (layer_idx, 0, 0, 0)), vec, vec,
                  pl.BlockSpec((None, D_MODEL, MLP_STREAM_CHUNK), lambda c: (layer_idx, 0, c)),
                  pl.BlockSpec((None, MLP_STREAM_CHUNK, D_MODEL), lambda c: (layer_idx, c, 0))],
        out_specs=[pl.BlockSpec(x.shape, lambda c: (0, 0, 0)),
                   pl.BlockSpec((None, D_MODEL, MLP_STREAM_CHUNK), lambda c: (0, 0, c)),
                   pl.BlockSpec((None, MLP_STREAM_CHUNK, D_MODEL), lambda c: (0, c, 0))],
        out_shape=[jax.ShapeDtypeStruct(x.shape, F32),
                   jax.ShapeDtypeStruct((1, D_MODEL, D_FF), BF16),
                   jax.ShapeDtypeStruct((1, D_FF, D_MODEL), BF16)],
        scratch_shapes=[pltpu.VMEM((rows, D_MODEL), BF16), pltpu.VMEM((rows, D_MODEL), F32)],
        compiler_params=pltpu.CompilerParams(
            dimension_semantics=("arbitrary",), vmem_limit_bytes=_vmem_limit(blocks, temps)),
        name="mlp_stream_sample",
    )(x, mod_all, g_pre2, g_post2, w1, w2)


def kernel(x_prompt, x_sample, c_prompt, c_sample, state_conv, cache_k, cache_v, w_ada, b_ada,
           g_pre1, w_in, conv_w, w_br_conv, w_br_attn, w_o, sinks, g_post1, g_pre2, w_ff1, w_ff2,
           g_post2, rel_table):
    depth = w_ada.shape[0]
    batch, seq, _ = x_prompt.shape
    n_seq, n_new, _ = x_sample.shape
    buf = cache_k.shape[2]

    pad = (-(n_seq + batch)) % V7X_SUBLANES
    c_all = jnp.concatenate([c_sample, c_prompt, jnp.zeros((pad, D_MODEL), F32)], axis=0)
    mod_all = _ada(c_all, w_ada, b_ada)

    dist_p = (jnp.arange(BLOCK)[:, None] + BLOCK) - jnp.arange(2 * BLOCK)[None, :]
    bucket_p = _rel_bucket(dist_p)
    upper = jnp.arange(BLOCK)[None, :] > jnp.arange(BLOCK)[:, None]
    bucket_merged = jnp.where(upper, bucket_p[:, :BLOCK], bucket_p[:, BLOCK:])
    bias_p = _bias_table(rel_table, bucket_merged.T)
    dist_s = (buf + jnp.arange(n_new))[:, None] - jnp.arange(buf + n_new)[None, :]
    bias_s = _bias_table(rel_table, _rel_bucket(dist_s))
    bias_s = (bias_s.reshape(N_KV_HEADS, GROUP, n_new, buf + n_new)
              .transpose(0, 2, 1, 3).reshape(N_KV_HEADS, n_new * GROUP, buf + n_new))
    bias_s = jnp.pad(bias_s, ((0, 0), (0, 0), (0, buf - n_new)))

    to_bf16 = lambda w: w.astype(BF16)
    w_in_b, w_bc_b, w_ba_b, w_o_b = map(to_bf16, (w_in, w_br_conv, w_br_attn, w_o))
    vec = lambda g: g.reshape(depth, 1, D_MODEL)
    g_pre1, g_post1, g_pre2, g_post2 = map(vec, (g_pre1, g_post1, g_pre2, g_post2))
    pre_s = state_conv.transpose(0, 2, 1, 3)
    cache_k = cache_k.transpose(0, 1, 3, 4, 2)
    cache_v = cache_v.transpose(0, 1, 3, 4, 2)

    xp = x_prompt
    xs = x_sample.transpose(1, 0, 2)
    conv_p, k_p, v_p, conv_s = [], [], [], []
    windows = None
    win = min(WINDOW, seq)
    for l in range(depth):
        z, q, kv, gates, tail = _in_proj(xs, mod_all, g_pre1, w_in_b, conv_w, l, n_seq, pre_s)
        q_s = (q.reshape(n_new, n_seq, N_KV_HEADS, GROUP, HEAD_DIM)
               .transpose(1, 2, 0, 3, 4).reshape(n_seq, N_KV_HEADS, n_new * GROUP, HEAD_DIM))
        kv_s = kv.transpose(1, 0, 2)
        sink_rows = jnp.tile(sinks[l].reshape(N_KV_HEADS, 1, GROUP),
                             (1, n_new, 1)).reshape(N_KV_HEADS, n_new * GROUP, 1)
        o, *windows = _sample_attn(q_s, cache_k, cache_v, kv_s[:, :, :KV_WIDTH],
                                   kv_s[:, :, KV_WIDTH:], bias_s, sink_rows, l, windows)
        attn = (o.reshape(n_seq, N_KV_HEADS, n_new, GROUP, HEAD_DIM)
                .transpose(2, 0, 1, 3, 4).reshape(n_new, n_seq, ATTN_WIDTH))
        x1 = _mix_sample(xs, z, attn, gates, mod_all, g_post1, w_bc_b, w_ba_b, w_o_b, l, n_seq)
        xs, w1_b, w2_b = _mlp_stream_sample(x1, mod_all, g_pre2, g_post2, w_ff1, w_ff2, l, n_seq)
        conv_s.append(tail.transpose(1, 0, 2))

        z, attn, kv, gates, tail = _in_proj(xp, mod_all, g_pre1, w_in_b, conv_w, l, n_seq,
                                            attn=(sinks, bias_p))
        xp = _post(xp, z, attn, gates, mod_all, g_post1, g_pre2, g_post2, w_bc_b, w_ba_b, w_o_b,
                   w1_b, w2_b, l, 0, n_seq)
        conv_p.append(tail[:, V7X_SUBLANES - (CONV_WIDTH - 1):])
        k_p.append(kv[:, seq - win:, :KV_WIDTH].reshape(batch, win, N_KV_HEADS, HEAD_DIM))
        v_p.append(kv[:, seq - win:, KV_WIDTH:].reshape(batch, win, N_KV_HEADS, HEAD_DIM))

    k_s, v_s = (w.transpose(0, 1, 4, 2, 3) for w in windows)
    return (xp, xs.transpose(1, 0, 2), jnp.stack(conv_p), jnp.stack(k_p), jnp.stack(v_p),
            jnp.stack(conv_s), k_s, v_s)
```

```python
import functools
import math
from typing import Any, Callable, NamedTuple

import jax
import jax.numpy as jnp
from jax import lax
from jax.experimental import pallas as pl
from jax.experimental.pallas import tpu as pltpu

D_MODEL = 1024
N_HEADS = 16
N_KV_HEADS = 2
HEAD_DIM = 64
GROUP = N_HEADS // N_KV_HEADS
ATTN_WIDTH = N_HEADS * HEAD_DIM
KV_WIDTH = N_KV_HEADS * HEAD_DIM
CONV_DIM = D_MODEL
CONV_WIDTH = 3
WINDOW = 128
BLOCK = 128
N_BUCKETS = 32
MAX_DISTANCE = 128
D_FF = 4 * D_MODEL
N_MOD = 6
RMS_EPS = 1e-6
NEG_INF = -1e30
PROJ_COLS = 3 * CONV_DIM + ATTN_WIDTH + 2 * KV_WIDTH + 2 * D_MODEL
Q_OFF = 3 * CONV_DIM
KV_OFF = Q_OFF + ATTN_WIDTH
GATE_OFF = KV_OFF + 2 * KV_WIDTH

V7X_SUBLANES = 8
V7X_VMEM_BYTES = 64 * 1024 * 1024

PROMPT_TILE = 512
SAMPLE_POS_TILE = 4
SAMPLE_SEQ_TILE = 32
CHUNK = 256
MLP_STREAM_CHUNK = 1024
CAST_CHUNK = 256

F32 = jnp.float32
BF16 = jnp.bfloat16


def _vmem_limit(block_bytes, temp_bytes):
    return int(min(2 * block_bytes + temp_bytes, V7X_VMEM_BYTES - 4 * 1024 * 1024))


def _nbytes(shape, dtype):
    return math.prod(shape) * jnp.dtype(dtype).itemsize


def _rms(x):
    return x * lax.rsqrt(jnp.mean(x * x, axis=-1, keepdims=True) + RMS_EPS)


def _dot(a, b):
    return jnp.dot(a, b, preferred_element_type=F32)


def _flat(a):
    return a.reshape(-1, a.shape[-1])


def _ada_kernel(c_ref, w_ref, b_ref, o_ref):
    c = c_ref[...]
    s = c * (1.0 / (1.0 + jnp.exp(-c)))
    o_ref[...] = _dot(s.astype(BF16), w_ref[...].astype(BF16)) + b_ref[...]


def _ada(c_all, w_ada, b_ada):
    depth = w_ada.shape[0]
    rows = c_all.shape[0]
    blocks = (_nbytes((rows, D_MODEL), F32) * 2 + _nbytes((D_MODEL, D_MODEL), F32))
    return pl.pallas_call(
        _ada_kernel,
        grid=(depth, N_MOD),
        in_specs=[
            pl.BlockSpec((rows, D_MODEL), lambda l, j: (0, 0)),
            pl.BlockSpec((None, D_MODEL, D_MODEL), lambda l, j: (l, 0, j)),
            pl.BlockSpec((None, None, 1, D_MODEL), lambda l, j: (l, j, 0, 0)),
        ],
        out_specs=pl.BlockSpec((None, None, rows, D_MODEL), lambda l, j: (l, j, 0, 0)),
        out_shape=jax.ShapeDtypeStruct((depth, N_MOD, rows, D_MODEL), F32),
        compiler_params=pltpu.CompilerParams(
            dimension_semantics=("arbitrary", "arbitrary"),
            vmem_limit_bytes=_vmem_limit(blocks, _nbytes((D_MODEL, D_MODEL), BF16) * 2)),
        name="ada_mod",
    )(c_all, w_ada, b_ada.reshape(depth, N_MOD, 1, D_MODEL))


def _bias_kernel(tab_ref, bucket_ref, o_ref):
    bucket = bucket_ref[...]
    for h in range(N_HEADS):
        acc = jnp.zeros(bucket.shape, F32)
        for b in range(N_BUCKETS):
            acc = jnp.where(bucket == b, tab_ref[b * N_HEADS + h], acc)
        o_ref[h] = acc


def _bias_table(rel_table, bucket):
    return pl.pallas_call(
        _bias_kernel,
        in_specs=[
            pl.BlockSpec(memory_space=pltpu.SMEM),
            pl.BlockSpec(bucket.shape, lambda: (0, 0)),
        ],
        out_specs=pl.BlockSpec((N_HEADS,) + bucket.shape, lambda: (0, 0, 0)),
        out_shape=jax.ShapeDtypeStruct((N_HEADS,) + bucket.shape, F32),
        name="bias_table",
    )(rel_table.reshape(-1), bucket)


def _rel_bucket(dist):
    n = jnp.maximum(dist, 0)
    max_exact = N_BUCKETS // 2
    nf = jnp.maximum(n, 1).astype(F32)
    scaled = (jnp.log(nf / max_exact) / math.log(MAX_DISTANCE / max_exact)
              * (N_BUCKETS - max_exact))
    large = jnp.minimum(max_exact + jnp.floor(scaled).astype(jnp.int32), N_BUCKETS - 1)
    return jnp.where(n < max_exact, n, large)


class _Dense(NamedTuple):
    sample: bool
    grid: tuple
    rows: int
    act: Callable[[int], Any]
    layer: Callable[..., Any]
    mod_spec: Any
    sem: tuple


def _dense(x, layer_idx, n_seq, sample):
    lead, mid = x.shape[0], x.shape[1]
    if sample:
        grid = (lead // SAMPLE_POS_TILE,)
        act = lambda w: pl.BlockSpec((SAMPLE_POS_TILE, mid, w), lambda i: (i, 0, 0))
        layer = lambda *shape: pl.BlockSpec((None,) + shape,
                                            lambda i: (layer_idx,) + (0,) * len(shape),
                                            pipeline_mode=pl.Buffered(1))
        mod_spec = pl.BlockSpec((None, N_MOD, n_seq, D_MODEL), lambda i: (layer_idx, 0, 0, 0))
        return _Dense(True, grid, SAMPLE_POS_TILE * mid, act, layer, mod_spec, ("arbitrary",))
    assert n_seq % V7X_SUBLANES == 0 and lead <= V7X_SUBLANES
    grid = (lead, mid // PROMPT_TILE)
    act = lambda w: pl.BlockSpec((None, PROMPT_TILE, w), lambda b, t: (b, t, 0))
    layer = lambda *shape: pl.BlockSpec((None,) + shape,
                                        lambda b, t: (layer_idx,) + (0,) * len(shape),
                                        pipeline_mode=pl.Buffered(1))
    mod_spec = pl.BlockSpec((None, N_MOD, V7X_SUBLANES, D_MODEL),
                            lambda b, t: (layer_idx, 0, n_seq // V7X_SUBLANES, 0))
    return _Dense(False, grid, PROMPT_TILE, act, layer, mod_spec, ("arbitrary", "arbitrary"))


def _mod(mod_ref, j, sample):
    if sample:
        return mod_ref[j]
    return mod_ref[j, pl.ds(pl.program_id(0), 1), :]


def _sink_softmax(s, sink):
    m = jnp.maximum(jnp.max(s, axis=-1, keepdims=True), sink)
    e = jnp.exp(s - m)
    den = jnp.sum(e, axis=-1, keepdims=True) + jnp.exp(sink - m)
    return e * (1.0 / den)


def _stage_keys(kv, kd_ref, vt_ref, first_tile):
    n_blk = kv.shape[0] // BLOCK

    @pl.when(first_tile)
    def _():
        kd_ref[:, 0:BLOCK] = jnp.zeros((N_KV_HEADS, BLOCK, 2 * HEAD_DIM), BF16)
        vt_ref[0] = jnp.zeros((KV_WIDTH, BLOCK), BF16)

    @pl.when(jnp.logical_not(first_tile))
    def _():
        kd_ref[:, 0:BLOCK] = kd_ref[:, n_blk * BLOCK:]
        vt_ref[0] = vt_ref[n_blk]

    for g in range(N_KV_HEADS):
        k_g = kv[:, g * HEAD_DIM:(g + 1) * HEAD_DIM]
        kd_ref[g, BLOCK:] = jnp.concatenate([k_g, k_g], axis=1).astype(BF16)
    v_t = kv[:, KV_WIDTH:].T.astype(BF16)
    for blk in range(n_blk):
        vt_ref[blk + 1] = v_t[:, blk * BLOCK:(blk + 1) * BLOCK]


def _attention_phases(q_ref, kd_ref, vt_ref, s_ref, bias_ref, sink_ref, o_ref, first_tile):
    kj = lax.broadcasted_iota(jnp.int32, (BLOCK, BLOCK), 0)
    qi = lax.broadcasted_iota(jnp.int32, (BLOCK, BLOCK), 1)
    upper = kj > qi
    low_lanes = lax.broadcasted_iota(jnp.int32, (BLOCK, 2 * HEAD_DIM), 1) < HEAD_DIM

    def score_phase(c):
        q = q_ref[c * BLOCK:(c + 1) * BLOCK, :]
        for pair in range(N_HEADS // 2):
            g = (2 * pair) // GROUP
            q_pair = q[:, pair * 2 * HEAD_DIM:(pair + 1) * 2 * HEAD_DIM]
            zero = jnp.zeros_like(q_pair)
            q_both = jnp.concatenate([jnp.where(low_lanes, q_pair, zero),
                                      jnp.where(low_lanes, zero, q_pair)], axis=0)
            s_both = lax.dot_general(kd_ref[g, c * BLOCK:(c + 2) * BLOCK, :], q_both,
                                     (((1,), (1,)), ((), ())), preferred_element_type=F32)
            for par in range(2):
                h = 2 * pair + par
                s_h = s_both[:, par * BLOCK:(par + 1) * BLOCK]
                s = jnp.where(upper, s_h[:BLOCK], s_h[BLOCK:]) + bias_ref[h]
                if c == 0:
                    s = jnp.where(upper & first_tile, NEG_INF, s)
                s_ref[c, h] = s

    def value_phase(c, pair):
        g = (2 * pair) // GROUP
        v_cat = jnp.concatenate([vt_ref[c, g * HEAD_DIM:(g + 1) * HEAD_DIM],
                                 vt_ref[c + 1, g * HEAD_DIM:(g + 1) * HEAD_DIM]], axis=1)
        weights, scales = [], []
        for h in (2 * pair, 2 * pair + 1):
            s = s_ref[c, h]
            sink = sink_ref[h]
            m = jnp.maximum(jnp.max(s, axis=0, keepdims=True), sink)
            e = jnp.exp(s - m)
            den = jnp.sum(e, axis=0, keepdims=True) + jnp.exp(sink - m)
            weights.append(jnp.concatenate([jnp.where(upper, e, 0.0), jnp.where(upper, 0.0, e)],
                                           axis=0).astype(BF16))
            scales.append(1.0 / den)
        o_t = _dot(v_cat, jnp.concatenate(weights, axis=1))
        o_t = jnp.concatenate([o_t[:, :BLOCK] * scales[0], o_t[:, BLOCK:] * scales[1]], axis=0)
        o_ref[c * BLOCK:(c + 1) * BLOCK, pair * 2 * HEAD_DIM:(pair + 1) * 2 * HEAD_DIM] = (
            o_t.T.astype(BF16))

    return score_phase, value_phase


def _in_proj_kernel(*refs, sample):
    if sample:
        (x_ref, mod_ref, g_ref, w_ref, cw_ref, pre_ref,
         z_ref, q_ref, kv_ref, gate_ref, tail_ref, carry_ref) = refs
    else:
        (x_ref, mod_ref, g_ref, w_ref, cw_ref, sink_ref, bias_ref,
         z_ref, a_ref, kv_ref, gate_ref, tail_ref,
         carry_ref, q_ref, kd_ref, vt_ref, s_ref) = refs
    step = pl.program_id(0) if sample else pl.program_id(1)

    @pl.when(step == 0)
    def _():
        if sample:
            carry_ref[...] = pre_ref[...]
        else:
            carry_ref[...] = jnp.zeros(carry_ref.shape, F32)

    x = x_ref[...]
    h = _rms(x) * (g_ref[...] * (1.0 + _mod(mod_ref, 1, sample))) + _mod(mod_ref, 0, sample)
    hb = _flat(h).astype(BF16)
    rows = hb.shape[0]

    def store(ref, lo, val):
        ref[..., lo:lo + val.shape[-1]] = val.reshape(ref.shape[:-1] + (val.shape[-1],))

    def conv_chunk(lo):
        b_g = _dot(hb, w_ref[:, lo:lo + CHUNK])
        c_g = _dot(hb, w_ref[:, CONV_DIM + lo:CONV_DIM + lo + CHUNK])
        x_c = _dot(hb, w_ref[:, 2 * CONV_DIM + lo:2 * CONV_DIM + lo + CHUNK])
        u = c_g * x_c
        if sample:
            pos = x_ref.shape[0]
            u3 = u.reshape(pos, -1, CHUNK)
            ext = jnp.concatenate([carry_ref[:, :, lo:lo + CHUNK], u3], axis=0)
            u2 = _flat(ext[0:pos])
            u1 = _flat(ext[1:pos + 1])
            new_tail = u3[pos - (CONV_WIDTH - 1):]
            carry_ref[:, :, lo:lo + CHUNK] = new_tail
            tail_ref[:, :, lo:lo + CHUNK] = new_tail
        else:
            prev = carry_ref[:, lo:lo + CHUNK]
            p1 = prev[V7X_SUBLANES - 1:V7X_SUBLANES]
            p2 = prev[V7X_SUBLANES - 2:V7X_SUBLANES - 1]
            r = lax.broadcasted_iota(jnp.int32, u.shape, 0)
            u1 = jnp.where(r == 0, p1, pltpu.roll(u, 1, 0))
            u2 = jnp.where(r == 0, p2, jnp.where(r == 1, p1, pltpu.roll(u, 2, 0)))
            new_tail = u[rows - V7X_SUBLANES:]
            carry_ref[:, lo:lo + CHUNK] = new_tail
            tail_ref[:, lo:lo + CHUNK] = new_tail
        conv = (cw_ref[0:1, lo:lo + CHUNK] * u2 + cw_ref[1:2, lo:lo + CHUNK] * u1
                + cw_ref[2:3, lo:lo + CHUNK] * u)
        store(z_ref, lo, (b_g * conv).astype(BF16))

    def q_chunk(lo):
        q = _dot(hb, w_ref[:, Q_OFF + lo:Q_OFF + lo + CHUNK])
        store(q_ref, lo, (q * (HEAD_DIM ** -0.5)).astype(BF16))

    def gate_chunk(lo):
        g = _dot(hb, w_ref[:, GATE_OFF + lo:GATE_OFF + lo + CHUNK])
        store(gate_ref, lo, (1.0 / (1.0 + jnp.exp(-g))).astype(BF16))

    kv = _dot(hb, w_ref[:, KV_OFF:KV_OFF + 2 * KV_WIDTH])
    if sample:
        store(kv_ref, 0, kv)
    else:
        kv_ref[...] = kv[rows - kv_ref.shape[0]:]
    for lo in range(0, ATTN_WIDTH, CHUNK):
        q_chunk(lo)
    dense = ([functools.partial(conv_chunk, lo) for lo in range(0, CONV_DIM, CHUNK)]
             + [functools.partial(gate_chunk, lo) for lo in range(0, 2 * D_MODEL, CHUNK)])
    if sample:
        for work in dense:
            work()
        return

    first_tile = step == 0
    _stage_keys(kv, kd_ref, vt_ref, first_tile)
    score_phase, value_phase = _attention_phases(q_ref, kd_ref, vt_ref, s_ref, bias_ref,
                                                 sink_ref, a_ref, first_tile)
    n_blk = rows // BLOCK
    attention = [functools.partial(score_phase, 0)]
    for c in range(n_blk):
        if c + 1 < n_blk:
            attention.append(functools.partial(score_phase, c + 1))
        attention += [functools.partial(value_phase, c, pair) for pair in range(N_HEADS // 2)]
    done = 0
    for i, work in enumerate(attention):
        work()
        while done < len(dense) and done * len(attention) < (i + 1) * len(dense):
            dense[done]()
            done += 1


def _in_proj(x, mod_all, g_pre1, w_in, conv_w, layer_idx, n_seq, pre=None, attn=None):
    cfg = _dense(x, layer_idx, n_seq, pre is not None)
    lead, mid = x.shape[0], x.shape[1]
    widths = (CONV_DIM, ATTN_WIDTH, 2 * KV_WIDTH, 2 * D_MODEL)
    dtypes = (BF16, BF16, F32, BF16)
    blocks = (_nbytes((cfg.rows, D_MODEL), F32) + _nbytes(w_in.shape[1:], BF16)
              + sum(_nbytes((cfg.rows, w), dt) for w, dt in zip(widths, dtypes)))
    temps = _nbytes((cfg.rows, D_MODEL), F32) * 2 + _nbytes((cfg.rows, CHUNK), F32) * 10
    if cfg.sample:
        tail_shape = (CONV_WIDTH - 1, mid, CONV_DIM)
        tail_spec = pl.BlockSpec(tail_shape, lambda i: (0, 0, 0))
        extra_in = [pre]
        extra_specs = [cfg.layer(*tail_shape)]
        scratch = [pltpu.VMEM(tail_shape, F32)]
    else:
        sinks, bias = attn
        per_tile = cfg.rows // BLOCK
        tail_shape = (lead, V7X_SUBLANES, CONV_DIM)
        tail_spec = pl.BlockSpec((None, V7X_SUBLANES, CONV_DIM), lambda b, t: (b, 0, 0))
        extra_in = [sinks[layer_idx], bias]
        extra_specs = [pl.BlockSpec(memory_space=pltpu.SMEM),
                       pl.BlockSpec(bias.shape, lambda b, t: (0, 0, 0),
                                    pipeline_mode=pl.Buffered(1))]
        scratch = [pltpu.VMEM((V7X_SUBLANES, CONV_DIM), F32),
                   pltpu.VMEM((cfg.rows, ATTN_WIDTH), BF16),
                   pltpu.VMEM((N_KV_HEADS, cfg.rows + BLOCK, 2 * HEAD_DIM), BF16),
                   pltpu.VMEM((per_tile + 1, KV_WIDTH, BLOCK), BF16),
                   pltpu.VMEM((per_tile, N_HEADS, BLOCK, BLOCK), F32)]
        temps += (_nbytes(bias.shape, F32) + _nbytes((cfg.rows, ATTN_WIDTH), BF16)
                  + _nbytes((per_tile, N_HEADS, BLOCK, BLOCK), F32)
                  + _nbytes((BLOCK, 2 * BLOCK), F32) * 16 + _nbytes((BLOCK, ATTN_WIDTH), F32) * 4)

    out_shape = [jax.ShapeDtypeStruct(x.shape[:-1] + (w,), dt) for w, dt in zip(widths, dtypes)]
    out_shape.append(jax.ShapeDtypeStruct(tail_shape, F32))
    out_specs = [cfg.act(w) for w in widths] + [tail_spec]
    if not cfg.sample:
        win = min(WINDOW, mid)
        out_shape[2] = jax.ShapeDtypeStruct((lead, win, 2 * KV_WIDTH), F32)
        out_specs[2] = pl.BlockSpec((None, win, 2 * KV_WIDTH), lambda b, t: (b, 0, 0))
    return pl.pallas_call(
        functools.partial(_in_proj_kernel, sample=cfg.sample),
        grid=cfg.grid,
        in_specs=[cfg.act(D_MODEL), cfg.mod_spec, cfg.layer(1, D_MODEL),
                  _dense(x, 0, n_seq, cfg.sample).layer(*w_in.shape[1:]),
                  cfg.layer(*conv_w.shape[1:])] + extra_specs,
        out_specs=out_specs,
        out_shape=out_shape,
        scratch_shapes=scratch,
        compiler_params=pltpu.CompilerParams(
            dimension_semantics=cfg.sem, vmem_limit_bytes=_vmem_limit(blocks, temps)),
        name="in_proj_sample" if cfg.sample else "in_proj_attn_prompt",
    )(x, mod_all, g_pre1, w_in, conv_w, *extra_in)


def _sample_attn_kernel(q_ref, kc_ref, vc_ref, kn_ref, vn_ref, bias_ref, sink_ref, *rest,
                        layer_idx):
    o_ref, ko_ref, vo_ref = rest[-3:]
    if ko_ref.ndim > kc_ref.ndim:
        for other in range(ko_ref.shape[0]):
            if other != layer_idx:
                ko_ref[other] = jnp.zeros(ko_ref.shape[1:], F32)
                vo_ref[other] = jnp.zeros(vo_ref.shape[1:], F32)
        ko_ref, vo_ref = ko_ref.at[layer_idx], vo_ref.at[layer_idx]
    n_seq = q_ref.shape[0]
    buf = kc_ref.shape[-1]
    new = kn_ref.shape[1]
    rows = new * GROUP
    pos = lax.broadcasted_iota(jnp.int32, (rows, 2 * buf), 0) // GROUP
    kj = lax.broadcasted_iota(jnp.int32, (rows, 2 * buf), 1)
    dist = buf + pos - kj
    mask = (dist >= 0) & (dist < WINDOW) & (kj < buf + new)
    lane = lax.broadcasted_iota(jnp.int32, (n_seq, HEAD_DIM, buf), 2)
    pad = jnp.zeros((n_seq, HEAD_DIM, buf - new), F32)

    def extended(cache_t, fresh, out_ref, g):
        fresh_t = jnp.concatenate([jnp.swapaxes(fresh, 1, 2), pad], axis=2)
        out_ref[:, g] = pltpu.roll(jnp.where(lane < new, fresh_t, cache_t), buf - new, 2)
        return jnp.concatenate([cache_t, fresh_t], axis=2).astype(BF16)

    for g in range(N_KV_HEADS):
        lanes = slice(g * HEAD_DIM, (g + 1) * HEAD_DIM)
        k_ext = extended(kc_ref[:, g], kn_ref[:, :, lanes], ko_ref, g)
        v_ext = extended(vc_ref[:, g], vn_ref[:, :, lanes], vo_ref, g)
        s = jnp.einsum('nqd,ndk->nqk', q_ref[:, g], k_ext, preferred_element_type=F32)
        p = _sink_softmax(jnp.where(mask[None], s + bias_ref[g][None], NEG_INF),
                          sink_ref[g][None])
        o = jnp.einsum('nqk,ndk->nqd', p.astype(BF16), v_ext, preferred_element_type=F32)
        o_ref[:, g] = o.astype(BF16)


def _sample_attn(q, k_cache, v_cache, k_new, v_new, bias, sink_rows, layer_idx, windows):
    _, n, _, _, buf = k_cache.shape
    new = k_new.shape[1]
    rows = new * GROUP
    nt = SAMPLE_SEQ_TILE
    seq3 = lambda a, b: pl.BlockSpec((nt, a, b), lambda i: (i, 0, 0))
    cache = pl.BlockSpec((None, nt, N_KV_HEADS, HEAD_DIM, buf),
                         lambda i: (layer_idx, i, 0, 0, 0))
    q_spec = pl.BlockSpec((nt, N_KV_HEADS, rows, HEAD_DIM), lambda i: (i, 0, 0, 0))
    in_specs = [q_spec, cache, cache, seq3(new, KV_WIDTH), seq3(new, KV_WIDTH),
                pl.BlockSpec(bias.shape, lambda i: (0, 0, 0)),
                pl.BlockSpec(sink_rows.shape, lambda i: (0, 0, 0))]
    args = [q, k_cache, v_cache, k_new, v_new, bias, sink_rows]
    depth = k_cache.shape[0]
    if windows is None:
        aliases = {}
        window_out = pl.BlockSpec((depth, nt, N_KV_HEADS, HEAD_DIM, buf),
                                  lambda i: (0, i, 0, 0, 0))
    else:
        aliases = {len(args): 1, len(args) + 1: 2}
        in_specs += [pl.BlockSpec(memory_space=pl.ANY)] * 2
        args += list(windows)
        window_out = cache
    blocks = (_nbytes((nt, N_KV_HEADS, HEAD_DIM, buf), F32) * (2 + 2 * depth)
              + _nbytes((nt, new, KV_WIDTH), F32) * 2
              + _nbytes((nt, N_KV_HEADS, rows, 2 * HEAD_DIM), BF16) * 2 + _nbytes(bias.shape, F32))
    temps = _nbytes((rows, 2 * buf), F32) * 8 * nt
    return pl.pallas_call(
        functools.partial(_sample_attn_kernel, layer_idx=layer_idx),
        grid=(n // nt,),
        in_specs=in_specs,
        out_specs=[q_spec, window_out, window_out],
        out_shape=[jax.ShapeDtypeStruct(q.shape, BF16),
                   jax.ShapeDtypeStruct(k_cache.shape, F32),
                   jax.ShapeDtypeStruct(v_cache.shape, F32)],
        input_output_aliases=aliases,
        compiler_params=pltpu.CompilerParams(
            dimension_semantics=("arbitrary",), vmem_limit_bytes=_vmem_limit(blocks, temps)),
        name="attn_sample",
    )(*args)


def _merge_residual(x_ref, z_ref, a_ref, gate_ref, mod_ref, gpost1_ref, wbc_ref, wba_ref, wo_ref,
                    sample):
    y_conv = _dot(_flat(z_ref[...]), wbc_ref[...])
    y_attn = _dot(_flat(a_ref[...]), wba_ref[...])
    gates = _flat(gate_ref[...])
    merged = gates[:, :D_MODEL] * y_conv + gates[:, D_MODEL:] * y_attn
    mixed = _dot(merged.astype(BF16), wo_ref[...])
    scale = _mod(mod_ref, 2, sample) * gpost1_ref[...]
    return x_ref[...] + scale * _rms(mixed).reshape(x_ref.shape)


def _mlp_input(x1, mod_ref, gpre2_ref, sample):
    scale = gpre2_ref[...] * (1.0 + _mod(mod_ref, 4, sample))
    return _flat(_rms(x1) * scale + _mod(mod_ref, 3, sample)).astype(BF16)


def _mlp_residual(x1, ff, mod_ref, gpost2_ref, sample):
    scale = _mod(mod_ref, 5, sample) * gpost2_ref[...]
    return x1 + scale * _rms(ff).reshape(x1.shape)


def _post_kernel(x_ref, z_ref, a_ref, gate_ref, mod_ref, gpost1_ref, gpre2_ref, gpost2_ref,
                 wbc_ref, wba_ref, wo_ref, w1_ref, w2_ref, *rest, sample):
    if len(rest) == 4:
        cast_in_ref, o_ref, cast_out_ref, hid_ref = rest
        cast_out_ref[...] = cast_in_ref[...].astype(BF16)
    else:
        o_ref, hid_ref = rest
    x1 = _merge_residual(x_ref, z_ref, a_ref, gate_ref, mod_ref, gpost1_ref,
                         wbc_ref, wba_ref, wo_ref, sample)
    hb = _mlp_input(x1, mod_ref, gpre2_ref, sample)
    for lo in range(0, D_FF, 2 * CHUNK):
        act = jnp.maximum(_dot(hb, w1_ref[:, lo:lo + 2 * CHUNK]), 0.0)
        hid_ref[:, lo:lo + 2 * CHUNK] = (act * act).astype(BF16)
    ff = _dot(hid_ref[...], w2_ref[...])
    o_ref[...] = _mlp_residual(x1, ff, mod_ref, gpost2_ref, sample)


def _post(x, z, a, gates, mod_all, g_post1, g_pre2, g_post2, w_bc, w_ba, w_o, w1, w2,
          layer_idx, mlp_layer_idx, n_seq, cast_next=None):
    cfg = _dense(x, layer_idx, n_seq, False)
    mlp = _dense(x, mlp_layer_idx, n_seq, False)
    sq = (D_MODEL, D_MODEL)
    blocks = (_nbytes((cfg.rows, D_MODEL), F32) * 2 + _nbytes((cfg.rows, D_MODEL), BF16) * 2
              + _nbytes((cfg.rows, 2 * D_MODEL), gates.dtype))
    temps = (_nbytes(sq, BF16) * 3 + _nbytes(w1.shape[1:], BF16) * 2
             + _nbytes((cfg.rows, D_FF), BF16) + _nbytes((cfg.rows, 2 * CHUNK), F32) * 2
             + _nbytes((cfg.rows, D_MODEL), F32) * 5)
    vec = cfg.layer(1, D_MODEL)
    in_specs = [cfg.act(D_MODEL), cfg.act(CONV_DIM), cfg.act(ATTN_WIDTH),
                cfg.act(2 * D_MODEL), cfg.mod_spec, vec, vec, vec,
                cfg.layer(*sq), cfg.layer(*sq), cfg.layer(*sq),
                mlp.layer(*w1.shape[1:]), mlp.layer(*w2.shape[1:])]
    args = [x, z, a, gates, mod_all, g_post1, g_pre2, g_post2, w_bc, w_ba, w_o, w1, w2]
    out_specs = [cfg.act(D_MODEL)]
    out_shape = [jax.ShapeDtypeStruct(x.shape, F32)]
    if cast_next is not None:
        w_next, src_layer = cast_next
        _, k_dim, n_dim = w_next.shape
        last = n_dim // CAST_CHUNK - 1
        assert n_dim % CAST_CHUNK == 0 and last < cfg.grid[0] * cfg.grid[1]
        tiles = cfg.grid[1]
        in_specs.append(pl.BlockSpec(
            (None, k_dim, CAST_CHUNK),
            lambda b, t: (src_layer, 0, jnp.minimum(b * tiles + t, last))))
        out_specs.append(pl.BlockSpec(
            (None, k_dim, CAST_CHUNK), lambda b, t: (0, 0, jnp.minimum(b * tiles + t, last))))
        out_shape.append(jax.ShapeDtypeStruct((1, k_dim, n_dim), BF16))
        args.append(w_next)
        blocks += _nbytes((k_dim, CAST_CHUNK), F32) + _nbytes((k_dim, CAST_CHUNK), BF16)
    return pl.pallas_call(
        functools.partial(_post_kernel, sample=False),
        grid=cfg.grid,
        in_specs=in_specs,
        out_specs=out_specs,
        out_shape=out_shape,
        scratch_shapes=[pltpu.VMEM((cfg.rows, D_FF), BF16)],
        compiler_params=pltpu.CompilerParams(
            dimension_semantics=cfg.sem, vmem_limit_bytes=_vmem_limit(blocks, temps)),
        name="post_prompt",
    )(*args)


def _mix_kernel(x_ref, z_ref, a_ref, gate_ref, mod_ref, gpost1_ref, wbc_ref, wba_ref, wo_ref,
                o_ref):
    o_ref[...] = _merge_residual(x_ref, z_ref, a_ref, gate_ref, mod_ref, gpost1_ref,
                                 wbc_ref, wba_ref, wo_ref, True)


def _mix_sample(x, z, a, gates, mod_all, g_post1, w_bc, w_ba, w_o, layer_idx, n_seq):
    cfg = _dense(x, layer_idx, n_seq, True)
    sq = (D_MODEL, D_MODEL)
    blocks = (_nbytes((cfg.rows, D_MODEL), F32) * 2 + _nbytes((cfg.rows, D_MODEL), BF16) * 2
              + _nbytes((cfg.rows, 2 * D_MODEL), gates.dtype))
    temps = _nbytes(sq, BF16) * 3 + _nbytes((cfg.rows, D_MODEL), F32) * 6
    return pl.pallas_call(
        _mix_kernel,
        grid=cfg.grid,
        in_specs=[cfg.act(D_MODEL), cfg.act(CONV_DIM), cfg.act(ATTN_WIDTH),
                  cfg.act(2 * D_MODEL), cfg.mod_spec, cfg.layer(1, D_MODEL),
                  cfg.layer(*sq), cfg.layer(*sq), cfg.layer(*sq)],
        out_specs=cfg.act(D_MODEL),
        out_shape=jax.ShapeDtypeStruct(x.shape, F32),
        compiler_params=pltpu.CompilerParams(
            dimension_semantics=cfg.sem, vmem_limit_bytes=_vmem_limit(blocks, temps)),
        name="mix_sample",
    )(x, z, a, gates, mod_all, g_post1, w_bc, w_ba, w_o)


def _mlp_stream_kernel(x_ref, mod_ref, gpre2_ref, gpost2_ref, w1_ref, w2_ref,
                       o_ref, w1b_ref, w2b_ref, hb_ref, ff_ref):
    chunk = pl.program_id(0)

    @pl.when(chunk == 0)
    def _():
        hb_ref[...] = _mlp_input(x_ref[...], mod_ref, gpre2_ref, True)
        ff_ref[...] = jnp.zeros(ff_ref.shape, F32)

    w1 = w1_ref[...].astype(BF16)
    w2 = w2_ref[...].astype(BF16)
    w1b_ref[...] = w1
    w2b_ref[...] = w2
    act = jnp.maximum(_dot(hb_ref[...], w1), 0.0)
    ff_ref[...] += _dot((act * act).astype(BF16), w2)

    @pl.when(chunk == pl.num_programs(0) - 1)
    def _():
        o_ref[...] = _mlp_residual(x_ref[...], ff_ref[...], mod_ref, gpost2_ref, True)


def _mlp_stream_sample(x, mod_all, g_pre2, g_post2, w1, w2, layer_idx, n_seq):
    rows = x.shape[0] * x.shape[1]
    const = lambda shape, idx: pl.BlockSpec(shape, lambda c: idx, pipeline_mode=pl.Buffered(1))
    vec = const((None, 1, D_MODEL), (layer_idx, 0, 0))
    blocks = (_nbytes((D_MODEL, MLP_STREAM_CHUNK), F32) * 2
              + _nbytes((D_MODEL, MLP_STREAM_CHUNK), BF16) * 2 + _nbytes(x.shape, F32))
    temps = (_nbytes(x.shape, F32) * 4 + _nbytes((N_MOD, n_seq, D_MODEL), F32)
             + _nbytes((rows, D_MODEL), BF16) + _nbytes((rows, MLP_STREAM_CHUNK), F32) * 3)
    return pl.pallas_call(
        _mlp_stream_kernel,
        grid=(D_FF // MLP_STREAM_CHUNK,),
        in_specs=[const(x.shape, (0, 0, 0)),
                  const((None, N_MOD, n_seq, D_MODEL), (layer_idx, 0, 0, 0)), vec, vec,
                  pl.BlockSpec((None, D_MODEL, MLP_STREAM_CHUNK), lambda c: (layer_idx, 0, c)),
                  pl.BlockSpec((None, MLP_STREAM_CHUNK, D_MODEL), lambda c: (layer_idx, c, 0))],
        out_specs=[pl.BlockSpec(x.shape, lambda c: (0, 0, 0)),
                   pl.BlockSpec((None, D_MODEL, MLP_STREAM_CHUNK), lambda c: (0, 0, c)),
                   pl.BlockSpec((None, MLP_STREAM_CHUNK, D_MODEL), lambda c: (0, c, 0))],
        out_shape=[jax.ShapeDtypeStruct(x.shape, F32),
                   jax.ShapeDtypeStruct((1, D_MODEL, D_FF), BF16),
                   jax.ShapeDtypeStruct((1, D_FF, D_MODEL), BF16)],
        scratch_shapes=[pltpu.VMEM((rows, D_MODEL), BF16), pltpu.VMEM((rows, D_MODEL), F32)],
        compiler_params=pltpu.CompilerParams(
            dimension_semantics=("arbitrary",), vmem_limit_bytes=_vmem_limit(blocks, temps)),
        name="mlp_stream_sample",
    )(x, mod_all, g_pre2, g_post2, w1, w2)


def kernel(x_prompt, x_sample, c_prompt, c_sample, state_conv, cache_k, cache_v, w_ada, b_ada,
           g_pre1, w_in, conv_w, w_br_conv, w_br_attn, w_o, sinks, g_post1, g_pre2, w_ff1, w_ff2,
           g_post2, rel_table):
    depth = w_ada.shape[0]
    batch, seq, _ = x_prompt.shape
    n_seq, n_new, _ = x_sample.shape
    buf = cache_k.shape[2]

    pad = (-(n_seq + batch)) % V7X_SUBLANES
    c_all = jnp.concatenate([c_sample, c_prompt, jnp.zeros((pad, D_MODEL), F32)], axis=0)
    mod_all = _ada(c_all, w_ada, b_ada)

    dist_p = (jnp.arange(BLOCK)[:, None] + BLOCK) - jnp.arange(2 * BLOCK)[None, :]
    bucket_p = _rel_bucket(dist_p)
    upper = jnp.arange(BLOCK)[None, :] > jnp.arange(BLOCK)[:, None]
    bucket_merged = jnp.where(upper, bucket_p[:, :BLOCK], bucket_p[:, BLOCK:])
    bias_p = _bias_table(rel_table, bucket_merged.T)
    dist_s = (buf + jnp.arange(n_new))[:, None] - jnp.arange(buf + n_new)[None, :]
    bias_s = _bias_table(rel_table, _rel_bucket(dist_s))
    bias_s = (bias_s.reshape(N_KV_HEADS, GROUP, n_new, buf + n_new)
              .transpose(0, 2, 1, 3).reshape(N_KV_HEADS, n_new * GROUP, buf + n_new))
    bias_s = jnp.pad(bias_s, ((0, 0), (0, 0), (0, buf - n_new)))

    to_bf16 = lambda w: w.astype(BF16)
    w_bc_b, w_ba_b, w_o_b = map(to_bf16, (w_br_conv, w_br_attn, w_o))
    w_in_b = to_bf16(w_in[0:1])
    vec = lambda g: g.reshape(depth, 1, D_MODEL)
    g_pre1, g_post1, g_pre2, g_post2 = map(vec, (g_pre1, g_post1, g_pre2, g_post2))
    pre_s = state_conv.transpose(0, 2, 1, 3)
    cache_k = cache_k.transpose(0, 1, 3, 4, 2)
    cache_v = cache_v.transpose(0, 1, 3, 4, 2)

    xp = x_prompt
    xs = x_sample.transpose(1, 0, 2)
    conv_p, k_p, v_p, conv_s = [], [], [], []
    windows = None
    win = min(WINDOW, seq)
    for l in range(depth):
        z, q, kv, gates, tail = _in_proj(xs, mod_all, g_pre1, w_in_b, conv_w, l, n_seq, pre_s)
        q_s = (q.reshape(n_new, n_seq, N_KV_HEADS, GROUP, HEAD_DIM)
               .transpose(1, 2, 0, 3, 4).reshape(n_seq, N_KV_HEADS, n_new * GROUP, HEAD_DIM))
        kv_s = kv.transpose(1, 0, 2)
        sink_rows = jnp.tile(sinks[l].reshape(N_KV_HEADS, 1, GROUP),
                             (1, n_new, 1)).reshape(N_KV_HEADS, n_new * GROUP, 1)
        o, *windows = _sample_attn(q_s, cache_k, cache_v, kv_s[:, :, :KV_WIDTH],
                                   kv_s[:, :, KV_WIDTH:], bias_s, sink_rows, l, windows)
        attn = (o.reshape(n_seq, N_KV_HEADS, n_new, GROUP, HEAD_DIM)
                .transpose(2, 0, 1, 3, 4).reshape(n_new, n_seq, ATTN_WIDTH))
        x1 = _mix_sample(xs, z, attn, gates, mod_all, g_post1, w_bc_b, w_ba_b, w_o_b, l, n_seq)
        xs, w1_b, w2_b = _mlp_stream_sample(x1, mod_all, g_pre2, g_post2, w_ff1, w_ff2, l, n_seq)
        conv_s.append(tail.transpose(1, 0, 2))

        z, attn, kv, gates, tail = _in_proj(xp, mod_all, g_pre1, w_in_b, conv_w, l, n_seq,
                                            attn=(sinks, bias_p))
        cast_next = (w_in, l + 1) if l + 1 < depth else None
        xp, *w_next = _post(xp, z, attn, gates, mod_all, g_post1, g_pre2, g_post2, w_bc_b, w_ba_b,
                            w_o_b, w1_b, w2_b, l, 0, n_seq, cast_next)
        w_in_b = w_next[0] if w_next else None
        conv_p.append(tail[:, V7X_SUBLANES - (CONV_WIDTH - 1):])
        k_p.append(kv[:, :, :KV_WIDTH].reshape(batch, win, N_KV_HEADS, HEAD_DIM))
        v_p.append(kv[:, :, KV_WIDTH:].reshape(batch, win, N_KV_HEADS, HEAD_DIM))

    k_s, v_s = (w.transpose(0, 1, 4, 2, 3) for w in windows)
    return (xp, xs.transpose(1, 0, 2), jnp.stack(conv_p), jnp.stack(k_p), jnp.stack(v_p),
            jnp.stack(conv_s), k_s, v_s)
```

```python
import functools
import math
from typing import Any, Callable, NamedTuple

import jax
import jax.numpy as jnp
from jax import lax
from jax.experimental import pallas as pl
from jax.experimental.pallas import tpu as pltpu

D_MODEL = 1024
N_HEADS = 16
N_KV_HEADS = 2
HEAD_DIM = 64
GROUP = N_HEADS // N_KV_HEADS
ATTN_WIDTH = N_HEADS * HEAD_DIM
KV_WIDTH = N_KV_HEADS * HEAD_DIM
CONV_DIM = D_MODEL
CONV_WIDTH = 3
WINDOW = 128
BLOCK = 128
N_BUCKETS = 32
MAX_DISTANCE = 128
D_FF = 4 * D_MODEL
N_MOD = 6
RMS_EPS = 1e-6
NEG_INF = -1e30
PROJ_COLS = 3 * CONV_DIM + ATTN_WIDTH + 2 * KV_WIDTH + 2 * D_MODEL
Q_OFF = 3 * CONV_DIM
KV_OFF = Q_OFF + ATTN_WIDTH
GATE_OFF = KV_OFF + 2 * KV_WIDTH

V7X_SUBLANES = 8
V7X_VMEM_BYTES = 64 * 1024 * 1024

PROMPT_TILE = 512
SAMPLE_POS_TILE = 4
SAMPLE_SEQ_TILE = 32
CHUNK = 256
MLP_STREAM_CHUNK = 1024
CAST_CHUNK = 256
CAST_COLS_SQUARE = 512
CAST_COLS_IN = 640

F32 = jnp.float32
BF16 = jnp.bfloat16


def _vmem_limit(block_bytes, temp_bytes):
    return int(min(2 * block_bytes + temp_bytes, V7X_VMEM_BYTES - 4 * 1024 * 1024))


def _nbytes(shape, dtype):
    return math.prod(shape) * jnp.dtype(dtype).itemsize


def _rms(x):
    return x * lax.rsqrt(jnp.mean(x * x, axis=-1, keepdims=True) + RMS_EPS)


def _dot(a, b):
    return jnp.dot(a, b, preferred_element_type=F32)


def _flat(a):
    return a.reshape(-1, a.shape[-1])


def _ada_kernel(c_ref, w_ref, b_ref, o_ref):
    c = c_ref[...]
    s = c * (1.0 / (1.0 + jnp.exp(-c)))
    o_ref[...] = _dot(s.astype(BF16), w_ref[...].astype(BF16)) + b_ref[...]


def _ada(c_all, w_ada, b_ada):
    depth = w_ada.shape[0]
    rows = c_all.shape[0]
    blocks = (_nbytes((rows, D_MODEL), F32) * 2 + _nbytes((D_MODEL, D_MODEL), F32))
    return pl.pallas_call(
        _ada_kernel,
        grid=(depth, N_MOD),
        in_specs=[
            pl.BlockSpec((rows, D_MODEL), lambda l, j: (0, 0)),
            pl.BlockSpec((None, D_MODEL, D_MODEL), lambda l, j: (l, 0, j)),
            pl.BlockSpec((None, None, 1, D_MODEL), lambda l, j: (l, j, 0, 0)),
        ],
        out_specs=pl.BlockSpec((None, None, rows, D_MODEL), lambda l, j: (l, j, 0, 0)),
        out_shape=jax.ShapeDtypeStruct((depth, N_MOD, rows, D_MODEL), F32),
        compiler_params=pltpu.CompilerParams(
            dimension_semantics=("arbitrary", "arbitrary"),
            vmem_limit_bytes=_vmem_limit(blocks, _nbytes((D_MODEL, D_MODEL), BF16) * 2)),
        name="ada_mod",
    )(c_all, w_ada, b_ada.reshape(depth, N_MOD, 1, D_MODEL))


def _bias_kernel(tab_ref, bucket_ref, o_ref):
    bucket = bucket_ref[...]
    for h in range(N_HEADS):
        acc = jnp.zeros(bucket.shape, F32)
        for b in range(N_BUCKETS):
            acc = jnp.where(bucket == b, tab_ref[b * N_HEADS + h], acc)
        o_ref[h] = acc


def _bias_table(rel_table, bucket):
    return pl.pallas_call(
        _bias_kernel,
        in_specs=[
            pl.BlockSpec(memory_space=pltpu.SMEM),
            pl.BlockSpec(bucket.shape, lambda: (0, 0)),
        ],
        out_specs=pl.BlockSpec((N_HEADS,) + bucket.shape, lambda: (0, 0, 0)),
        out_shape=jax.ShapeDtypeStruct((N_HEADS,) + bucket.shape, F32),
        name="bias_table",
    )(rel_table.reshape(-1), bucket)


def _cast_kernel(*refs):
    n = len(refs) // 2
    for src, dst in zip(refs[:n], refs[n:]):
        dst[...] = src[...].astype(BF16)


def _cast_weights(ws, layers, col_chunk):
    _, k, n = ws[0].shape
    first = layers.start
    spec = lambda base: pl.BlockSpec((None, k, col_chunk), lambda l, j: (base + l, 0, j))
    blocks = (_nbytes((k, col_chunk), F32) + _nbytes((k, col_chunk), BF16)) * len(ws)
    outs = pl.pallas_call(
        _cast_kernel,
        grid=(len(layers), n // col_chunk),
        in_specs=[spec(first)] * len(ws),
        out_specs=[spec(0)] * len(ws),
        out_shape=[jax.ShapeDtypeStruct((len(layers), k, n), BF16)] * len(ws),
        compiler_params=pltpu.CompilerParams(
            dimension_semantics=("arbitrary", "arbitrary"),
            vmem_limit_bytes=_vmem_limit(blocks, 0)),
        name="cast_weights",
    )(*ws)
    return outs


def _rel_bucket(dist):
    n = jnp.maximum(dist, 0)
    max_exact = N_BUCKETS // 2
    nf = jnp.maximum(n, 1).astype(F32)
    scaled = (jnp.log(nf / max_exact) / math.log(MAX_DISTANCE / max_exact)
              * (N_BUCKETS - max_exact))
    large = jnp.minimum(max_exact + jnp.floor(scaled).astype(jnp.int32), N_BUCKETS - 1)
    return jnp.where(n < max_exact, n, large)


class _Dense(NamedTuple):
    sample: bool
    grid: tuple
    rows: int
    act: Callable[[int], Any]
    layer: Callable[..., Any]
    mod_spec: Any
    sem: tuple


def _dense(x, layer_idx, n_seq, sample):
    lead, mid = x.shape[0], x.shape[1]
    if sample:
        grid = (lead // SAMPLE_POS_TILE,)
        act = lambda w: pl.BlockSpec((SAMPLE_POS_TILE, mid, w), lambda i: (i, 0, 0))
        layer = lambda *shape: pl.BlockSpec((None,) + shape,
                                            lambda i: (layer_idx,) + (0,) * len(shape),
                                            pipeline_mode=pl.Buffered(1))
        mod_spec = pl.BlockSpec((None, N_MOD, n_seq, D_MODEL), lambda i: (layer_idx, 0, 0, 0))
        return _Dense(True, grid, SAMPLE_POS_TILE * mid, act, layer, mod_spec, ("arbitrary",))
    assert n_seq % V7X_SUBLANES == 0 and lead <= V7X_SUBLANES
    grid = (lead, mid // PROMPT_TILE)
    act = lambda w: pl.BlockSpec((None, PROMPT_TILE, w), lambda b, t: (b, t, 0))
    layer = lambda *shape: pl.BlockSpec((None,) + shape,
                                        lambda b, t: (layer_idx,) + (0,) * len(shape),
                                        pipeline_mode=pl.Buffered(1))
    mod_spec = pl.BlockSpec((None, N_MOD, V7X_SUBLANES, D_MODEL),
                            lambda b, t: (layer_idx, 0, n_seq // V7X_SUBLANES, 0))
    return _Dense(False, grid, PROMPT_TILE, act, layer, mod_spec, ("arbitrary", "arbitrary"))


def _mod(mod_ref, j, sample):
    if sample:
        return mod_ref[j]
    return mod_ref[j, pl.ds(pl.program_id(0), 1), :]


def _sink_softmax(s, sink):
    m = jnp.maximum(jnp.max(s, axis=-1, keepdims=True), sink)
    e = jnp.exp(s - m)
    den = jnp.sum(e, axis=-1, keepdims=True) + jnp.exp(sink - m)
    return e * (1.0 / den)


def _stage_keys(kv, kd_ref, vt_ref, first_tile):
    n_blk = kv.shape[0] // BLOCK

    @pl.when(first_tile)
    def _():
        kd_ref[:, 0:BLOCK] = jnp.zeros((N_KV_HEADS, BLOCK, 2 * HEAD_DIM), BF16)
        vt_ref[0] = jnp.zeros((KV_WIDTH, BLOCK), BF16)

    @pl.when(jnp.logical_not(first_tile))
    def _():
        kd_ref[:, 0:BLOCK] = kd_ref[:, n_blk * BLOCK:]
        vt_ref[0] = vt_ref[n_blk]

    for g in range(N_KV_HEADS):
        k_g = kv[:, g * HEAD_DIM:(g + 1) * HEAD_DIM]
        kd_ref[g, BLOCK:] = jnp.concatenate([k_g, k_g], axis=1).astype(BF16)
    v_t = kv[:, KV_WIDTH:].T.astype(BF16)
    for blk in range(n_blk):
        vt_ref[blk + 1] = v_t[:, blk * BLOCK:(blk + 1) * BLOCK]


def _attention_phases(q_ref, kd_ref, vt_ref, s_ref, bias_ref, sink_ref, o_ref, first_tile):
    kj = lax.broadcasted_iota(jnp.int32, (BLOCK, BLOCK), 0)
    qi = lax.broadcasted_iota(jnp.int32, (BLOCK, BLOCK), 1)
    upper = kj > qi
    low_lanes = lax.broadcasted_iota(jnp.int32, (BLOCK, 2 * HEAD_DIM), 1) < HEAD_DIM

    def score_phase(c):
        q = q_ref[c * BLOCK:(c + 1) * BLOCK, :]
        for pair in range(N_HEADS // 2):
            g = (2 * pair) // GROUP
            q_pair = q[:, pair * 2 * HEAD_DIM:(pair + 1) * 2 * HEAD_DIM]
            zero = jnp.zeros_like(q_pair)
            q_both = jnp.concatenate([jnp.where(low_lanes, q_pair, zero),
                                      jnp.where(low_lanes, zero, q_pair)], axis=0)
            s_both = lax.dot_general(kd_ref[g, c * BLOCK:(c + 2) * BLOCK, :], q_both,
                                     (((1,), (1,)), ((), ())), preferred_element_type=F32)
            for par in range(2):
                h = 2 * pair + par
                s_h = s_both[:, par * BLOCK:(par + 1) * BLOCK]
                s = jnp.where(upper, s_h[:BLOCK], s_h[BLOCK:]) + bias_ref[h]
                if c == 0:
                    s = jnp.where(upper & first_tile, NEG_INF, s)
                s_ref[c, h] = s

    def value_phase(c, pair):
        g = (2 * pair) // GROUP
        v_cat = jnp.concatenate([vt_ref[c, g * HEAD_DIM:(g + 1) * HEAD_DIM],
                                 vt_ref[c + 1, g * HEAD_DIM:(g + 1) * HEAD_DIM]], axis=1)
        weights, scales = [], []
        for h in (2 * pair, 2 * pair + 1):
            s = s_ref[c, h]
            sink = sink_ref[h]
            m = jnp.maximum(jnp.max(s, axis=0, keepdims=True), sink)
            e = jnp.exp(s - m)
            den = jnp.sum(e, axis=0, keepdims=True) + jnp.exp(sink - m)
            weights.append(jnp.concatenate([jnp.where(upper, e, 0.0), jnp.where(upper, 0.0, e)],
                                           axis=0).astype(BF16))
            scales.append(1.0 / den)
        o_t = _dot(v_cat, jnp.concatenate(weights, axis=1))
        o_t = jnp.concatenate([o_t[:, :BLOCK] * scales[0], o_t[:, BLOCK:] * scales[1]], axis=0)
        o_ref[c * BLOCK:(c + 1) * BLOCK, pair * 2 * HEAD_DIM:(pair + 1) * 2 * HEAD_DIM] = (
            o_t.T.astype(BF16))

    return score_phase, value_phase


def _in_proj_kernel(*refs, sample):
    if sample:
        (x_ref, mod_ref, g_ref, w_ref, cw_ref, pre_ref,
         z_ref, q_ref, kv_ref, gate_ref, tail_ref, carry_ref) = refs
    else:
        (x_ref, mod_ref, g_ref, w_ref, cw_ref, sink_ref, bias_ref,
         z_ref, a_ref, kv_ref, gate_ref, tail_ref,
         carry_ref, q_ref, kd_ref, vt_ref, s_ref) = refs
    step = pl.program_id(0) if sample else pl.program_id(1)

    @pl.when(step == 0)
    def _():
        if sample:
            carry_ref[...] = pre_ref[...]
        else:
            carry_ref[...] = jnp.zeros(carry_ref.shape, F32)

    x = x_ref[...]
    h = _rms(x) * (g_ref[...] * (1.0 + _mod(mod_ref, 1, sample))) + _mod(mod_ref, 0, sample)
    hb = _flat(h).astype(BF16)
    rows = hb.shape[0]

    def store(ref, lo, val):
        ref[..., lo:lo + val.shape[-1]] = val.reshape(ref.shape[:-1] + (val.shape[-1],))

    def conv_chunk(lo):
        b_g = _dot(hb, w_ref[:, lo:lo + CHUNK])
        c_g = _dot(hb, w_ref[:, CONV_DIM + lo:CONV_DIM + lo + CHUNK])
        x_c = _dot(hb, w_ref[:, 2 * CONV_DIM + lo:2 * CONV_DIM + lo + CHUNK])
        u = c_g * x_c
        if sample:
            pos = x_ref.shape[0]
            u3 = u.reshape(pos, -1, CHUNK)
            ext = jnp.concatenate([carry_ref[:, :, lo:lo + CHUNK], u3], axis=0)
            u2 = _flat(ext[0:pos])
            u1 = _flat(ext[1:pos + 1])
            new_tail = u3[pos - (CONV_WIDTH - 1):]
            carry_ref[:, :, lo:lo + CHUNK] = new_tail
            tail_ref[:, :, lo:lo + CHUNK] = new_tail
        else:
            prev = carry_ref[:, lo:lo + CHUNK]
            p1 = prev[V7X_SUBLANES - 1:V7X_SUBLANES]
            p2 = prev[V7X_SUBLANES - 2:V7X_SUBLANES - 1]
            r = lax.broadcasted_iota(jnp.int32, u.shape, 0)
            u1 = jnp.where(r == 0, p1, pltpu.roll(u, 1, 0))
            u2 = jnp.where(r == 0, p2, jnp.where(r == 1, p1, pltpu.roll(u, 2, 0)))
            new_tail = u[rows - V7X_SUBLANES:]
            carry_ref[:, lo:lo + CHUNK] = new_tail
            tail_ref[:, lo:lo + CHUNK] = new_tail
        conv = (cw_ref[0:1, lo:lo + CHUNK] * u2 + cw_ref[1:2, lo:lo + CHUNK] * u1
                + cw_ref[2:3, lo:lo + CHUNK] * u)
        store(z_ref, lo, (b_g * conv).astype(BF16))

    def q_chunk(lo):
        q = _dot(hb, w_ref[:, Q_OFF + lo:Q_OFF + lo + CHUNK])
        store(q_ref, lo, (q * (HEAD_DIM ** -0.5)).astype(BF16))

    def gate_chunk(lo):
        g = _dot(hb, w_ref[:, GATE_OFF + lo:GATE_OFF + lo + CHUNK])
        store(gate_ref, lo, (1.0 / (1.0 + jnp.exp(-g))).astype(BF16))

    kv = _dot(hb, w_ref[:, KV_OFF:KV_OFF + 2 * KV_WIDTH])
    if sample:
        store(kv_ref, 0, kv)
    else:
        kv_ref[...] = kv[rows - kv_ref.shape[0]:]
    for lo in range(0, ATTN_WIDTH, CHUNK):
        q_chunk(lo)
    dense = ([functools.partial(conv_chunk, lo) for lo in range(0, CONV_DIM, CHUNK)]
             + [functools.partial(gate_chunk, lo) for lo in range(0, 2 * D_MODEL, CHUNK)])
    if sample:
        for work in dense:
            work()
        return

    first_tile = step == 0
    _stage_keys(kv, kd_ref, vt_ref, first_tile)
    score_phase, value_phase = _attention_phases(q_ref, kd_ref, vt_ref, s_ref, bias_ref,
                                                 sink_ref, a_ref, first_tile)
    n_blk = rows // BLOCK
    attention = [functools.partial(score_phase, 0)]
    for c in range(n_blk):
        if c + 1 < n_blk:
            attention.append(functools.partial(score_phase, c + 1))
        attention += [functools.partial(value_phase, c, pair) for pair in range(N_HEADS // 2)]
    done = 0
    for i, work in enumerate(attention):
        work()
        while done < len(dense) and done * len(attention) < (i + 1) * len(dense):
            dense[done]()
            done += 1


def _in_proj(x, mod_all, g_pre1, w_in, conv_w, layer_idx, n_seq, pre=None, attn=None):
    cfg = _dense(x, layer_idx, n_seq, pre is not None)
    lead, mid = x.shape[0], x.shape[1]
    widths = (CONV_DIM, ATTN_WIDTH, 2 * KV_WIDTH, 2 * D_MODEL)
    dtypes = (BF16, BF16, F32, BF16)
    blocks = (_nbytes((cfg.rows, D_MODEL), F32) + _nbytes(w_in.shape[1:], BF16)
              + sum(_nbytes((cfg.rows, w), dt) for w, dt in zip(widths, dtypes)))
    temps = _nbytes((cfg.rows, D_MODEL), F32) * 2 + _nbytes((cfg.rows, CHUNK), F32) * 10
    if cfg.sample:
        tail_shape = (CONV_WIDTH - 1, mid, CONV_DIM)
        tail_spec = pl.BlockSpec(tail_shape, lambda i: (0, 0, 0))
        extra_in = [pre]
        extra_specs = [cfg.layer(*tail_shape)]
        scratch = [pltpu.VMEM(tail_shape, F32)]
    else:
        sinks, bias = attn
        per_tile = cfg.rows // BLOCK
        tail_shape = (lead, V7X_SUBLANES, CONV_DIM)
        tail_spec = pl.BlockSpec((None, V7X_SUBLANES, CONV_DIM), lambda b, t: (b, 0, 0))
        extra_in = [sinks[layer_idx], bias]
        extra_specs = [pl.BlockSpec(memory_space=pltpu.SMEM),
                       pl.BlockSpec(bias.shape, lambda b, t: (0, 0, 0),
                                    pipeline_mode=pl.Buffered(1))]
        scratch = [pltpu.VMEM((V7X_SUBLANES, CONV_DIM), F32),
                   pltpu.VMEM((cfg.rows, ATTN_WIDTH), BF16),
                   pltpu.VMEM((N_KV_HEADS, cfg.rows + BLOCK, 2 * HEAD_DIM), BF16),
                   pltpu.VMEM((per_tile + 1, KV_WIDTH, BLOCK), BF16),
                   pltpu.VMEM((per_tile, N_HEADS, BLOCK, BLOCK), F32)]
        temps += (_nbytes(bias.shape, F32) + _nbytes((cfg.rows, ATTN_WIDTH), BF16)
                  + _nbytes((per_tile, N_HEADS, BLOCK, BLOCK), F32)
                  + _nbytes((BLOCK, 2 * BLOCK), F32) * 16 + _nbytes((BLOCK, ATTN_WIDTH), F32) * 4)

    out_shape = [jax.ShapeDtypeStruct(x.shape[:-1] + (w,), dt) for w, dt in zip(widths, dtypes)]
    out_shape.append(jax.ShapeDtypeStruct(tail_shape, F32))
    out_specs = [cfg.act(w) for w in widths] + [tail_spec]
    if not cfg.sample:
        win = min(WINDOW, mid)
        out_shape[2] = jax.ShapeDtypeStruct((lead, win, 2 * KV_WIDTH), F32)
        out_specs[2] = pl.BlockSpec((None, win, 2 * KV_WIDTH), lambda b, t: (b, 0, 0))
    return pl.pallas_call(
        functools.partial(_in_proj_kernel, sample=cfg.sample),
        grid=cfg.grid,
        in_specs=[cfg.act(D_MODEL), cfg.mod_spec, cfg.layer(1, D_MODEL),
                  _dense(x, 0, n_seq, cfg.sample).layer(*w_in.shape[1:]),
                  cfg.layer(*conv_w.shape[1:])] + extra_specs,
        out_specs=out_specs,
        out_shape=out_shape,
        scratch_shapes=scratch,
        compiler_params=pltpu.CompilerParams(
            dimension_semantics=cfg.sem, vmem_limit_bytes=_vmem_limit(blocks, temps)),
        name="in_proj_sample" if cfg.sample else "in_proj_attn_prompt",
    )(x, mod_all, g_pre1, w_in, conv_w, *extra_in)


def _sample_attn_kernel(q_ref, kc_ref, vc_ref, kn_ref, vn_ref, bias_ref, sink_ref, *rest,
                        layer_idx):
    o_ref, ko_ref, vo_ref = rest[-3:]
    if ko_ref.ndim > kc_ref.ndim:
        for other in range(ko_ref.shape[0]):
            if other != layer_idx:
                ko_ref[other] = jnp.zeros(ko_ref.shape[1:], F32)
                vo_ref[other] = jnp.zeros(vo_ref.shape[1:], F32)
        ko_ref, vo_ref = ko_ref.at[layer_idx], vo_ref.at[layer_idx]
    n_seq = q_ref.shape[0]
    buf = kc_ref.shape[-1]
    new = kn_ref.shape[1]
    rows = new * GROUP
    pos = lax.broadcasted_iota(jnp.int32, (rows, 2 * buf), 0) // GROUP
    kj = lax.broadcasted_iota(jnp.int32, (rows, 2 * buf), 1)
    dist = buf + pos - kj
    mask = (dist >= 0) & (dist < WINDOW) & (kj < buf + new)
    lane = lax.broadcasted_iota(jnp.int32, (n_seq, HEAD_DIM, buf), 2)
    pad = jnp.zeros((n_seq, HEAD_DIM, buf - new), F32)

    def extended(cache_t, fresh, out_ref, g):
        fresh_t = jnp.concatenate([jnp.swapaxes(fresh, 1, 2), pad], axis=2)
        out_ref[:, g] = pltpu.roll(jnp.where(lane < new, fresh_t, cache_t), buf - new, 2)
        return jnp.concatenate([cache_t, fresh_t], axis=2).astype(BF16)

    for g in range(N_KV_HEADS):
        lanes = slice(g * HEAD_DIM, (g + 1) * HEAD_DIM)
        k_ext = extended(kc_ref[:, g], kn_ref[:, :, lanes], ko_ref, g)
        v_ext = extended(vc_ref[:, g], vn_ref[:, :, lanes], vo_ref, g)
        s = jnp.einsum('nqd,ndk->nqk', q_ref[:, g], k_ext, preferred_element_type=F32)
        p = _sink_softmax(jnp.where(mask[None], s + bias_ref[g][None], NEG_INF),
                          sink_ref[g][None])
        o = jnp.einsum('nqk,ndk->nqd', p.astype(BF16), v_ext, preferred_element_type=F32)
        o_ref[:, g] = o.astype(BF16)


def _sample_attn(q, k_cache, v_cache, k_new, v_new, bias, sink_rows, layer_idx, windows):
    _, n, _, _, buf = k_cache.shape
    new = k_new.shape[1]
    rows = new * GROUP
    nt = SAMPLE_SEQ_TILE
    seq3 = lambda a, b: pl.BlockSpec((nt, a, b), lambda i: (i, 0, 0))
    cache = pl.BlockSpec((None, nt, N_KV_HEADS, HEAD_DIM, buf),
                         lambda i: (layer_idx, i, 0, 0, 0))
    q_spec = pl.BlockSpec((nt, N_KV_HEADS, rows, HEAD_DIM), lambda i: (i, 0, 0, 0))
    in_specs = [q_spec, cache, cache, seq3(new, KV_WIDTH), seq3(new, KV_WIDTH),
                pl.BlockSpec(bias.shape, lambda i: (0, 0, 0)),
                pl.BlockSpec(sink_rows.shape, lambda i: (0, 0, 0))]
    args = [q, k_cache, v_cache, k_new, v_new, bias, sink_rows]
    depth = k_cache.shape[0]
    if windows is None:
        aliases = {}
        window_out = pl.BlockSpec((depth, nt, N_KV_HEADS, HEAD_DIM, buf),
                                  lambda i: (0, i, 0, 0, 0))
    else:
        aliases = {len(args): 1, len(args) + 1: 2}
        in_specs += [pl.BlockSpec(memory_space=pl.ANY)] * 2
        args += list(windows)
        window_out = cache
    blocks = (_nbytes((nt, N_KV_HEADS, HEAD_DIM, buf), F32) * (2 + 2 * depth)
              + _nbytes((nt, new, KV_WIDTH), F32) * 2
              + _nbytes((nt, N_KV_HEADS, rows, 2 * HEAD_DIM), BF16) * 2 + _nbytes(bias.shape, F32))
    temps = _nbytes((rows, 2 * buf), F32) * 8 * nt
    return pl.pallas_call(
        functools.partial(_sample_attn_kernel, layer_idx=layer_idx),
        grid=(n // nt,),
        in_specs=in_specs,
        out_specs=[q_spec, window_out, window_out],
        out_shape=[jax.ShapeDtypeStruct(q.shape, BF16),
                   jax.ShapeDtypeStruct(k_cache.shape, F32),
                   jax.ShapeDtypeStruct(v_cache.shape, F32)],
        input_output_aliases=aliases,
        compiler_params=pltpu.CompilerParams(
            dimension_semantics=("arbitrary",), vmem_limit_bytes=_vmem_limit(blocks, temps)),
        name="attn_sample",
    )(*args)


def _merge_residual(x_ref, z_ref, a_ref, gate_ref, mod_ref, gpost1_ref, wbc_ref, wba_ref, wo_ref,
                    sample):
    y_conv = _dot(_flat(z_ref[...]), wbc_ref[...])
    y_attn = _dot(_flat(a_ref[...]), wba_ref[...])
    gates = _flat(gate_ref[...])
    merged = gates[:, :D_MODEL] * y_conv + gates[:, D_MODEL:] * y_attn
    mixed = _dot(merged.astype(BF16), wo_ref[...])
    scale = _mod(mod_ref, 2, sample) * gpost1_ref[...]
    return x_ref[...] + scale * _rms(mixed).reshape(x_ref.shape)


def _mlp_input(x1, mod_ref, gpre2_ref, sample):
    scale = gpre2_ref[...] * (1.0 + _mod(mod_ref, 4, sample))
    return _flat(_rms(x1) * scale + _mod(mod_ref, 3, sample)).astype(BF16)


def _mlp_residual(x1, ff, mod_ref, gpost2_ref, sample):
    scale = _mod(mod_ref, 5, sample) * gpost2_ref[...]
    return x1 + scale * _rms(ff).reshape(x1.shape)


def _post_kernel(x_ref, z_ref, a_ref, gate_ref, mod_ref, gpost1_ref, gpre2_ref, gpost2_ref,
                 wbc_ref, wba_ref, wo_ref, w1_ref, w2_ref, *rest, sample):
    if len(rest) == 4:
        cast_in_ref, o_ref, cast_out_ref, hid_ref = rest
        cast_out_ref[...] = cast_in_ref[...].astype(BF16)
    else:
        o_ref, hid_ref = rest
    x1 = _merge_residual(x_ref, z_ref, a_ref, gate_ref, mod_ref, gpost1_ref,
                         wbc_ref, wba_ref, wo_ref, sample)
    hb = _mlp_input(x1, mod_ref, gpre2_ref, sample)
    for lo in range(0, D_FF, 2 * CHUNK):
        act = jnp.maximum(_dot(hb, w1_ref[:, lo:lo + 2 * CHUNK]), 0.0)
        hid_ref[:, lo:lo + 2 * CHUNK] = (act * act).astype(BF16)
    ff = _dot(hid_ref[...], w2_ref[...])
    o_ref[...] = _mlp_residual(x1, ff, mod_ref, gpost2_ref, sample)


def _post(x, z, a, gates, mod_all, g_post1, g_pre2, g_post2, w_bc, w_ba, w_o, w1, w2,
          layer_idx, mlp_layer_idx, n_seq, cast_next=None):
    cfg = _dense(x, layer_idx, n_seq, False)
    mlp = _dense(x, mlp_layer_idx, n_seq, False)
    sq = (D_MODEL, D_MODEL)
    blocks = (_nbytes((cfg.rows, D_MODEL), F32) * 2 + _nbytes((cfg.rows, D_MODEL), BF16) * 2
              + _nbytes((cfg.rows, 2 * D_MODEL), gates.dtype))
    temps = (_nbytes(sq, BF16) * 3 + _nbytes(w1.shape[1:], BF16) * 2
             + _nbytes((cfg.rows, D_FF), BF16) + _nbytes((cfg.rows, 2 * CHUNK), F32) * 2
             + _nbytes((cfg.rows, D_MODEL), F32) * 5)
    vec = cfg.layer(1, D_MODEL)
    in_specs = [cfg.act(D_MODEL), cfg.act(CONV_DIM), cfg.act(ATTN_WIDTH),
                cfg.act(2 * D_MODEL), cfg.mod_spec, vec, vec, vec,
                cfg.layer(*sq), cfg.layer(*sq), cfg.layer(*sq),
                mlp.layer(*w1.shape[1:]), mlp.layer(*w2.shape[1:])]
    args = [x, z, a, gates, mod_all, g_post1, g_pre2, g_post2, w_bc, w_ba, w_o, w1, w2]
    out_specs = [cfg.act(D_MODEL)]
    out_shape = [jax.ShapeDtypeStruct(x.shape, F32)]
    if cast_next is not None:
        w_next, src_layer = cast_next
        _, k_dim, n_dim = w_next.shape
        last = n_dim // CAST_CHUNK - 1
        assert n_dim % CAST_CHUNK == 0 and last < cfg.grid[0] * cfg.grid[1]
        tiles = cfg.grid[1]
        in_specs.append(pl.BlockSpec(
            (None, k_dim, CAST_CHUNK),
            lambda b, t: (src_layer, 0, jnp.minimum(b * tiles + t, last))))
        out_specs.append(pl.BlockSpec(
            (None, k_dim, CAST_CHUNK), lambda b, t: (0, 0, jnp.minimum(b * tiles + t, last))))
        out_shape.append(jax.ShapeDtypeStruct((1, k_dim, n_dim), BF16))
        args.append(w_next)
        blocks += _nbytes((k_dim, CAST_CHUNK), F32) + _nbytes((k_dim, CAST_CHUNK), BF16)
    return pl.pallas_call(
        functools.partial(_post_kernel, sample=False),
        grid=cfg.grid,
        in_specs=in_specs,
        out_specs=out_specs,
        out_shape=out_shape,
        scratch_shapes=[pltpu.VMEM((cfg.rows, D_FF), BF16)],
        compiler_params=pltpu.CompilerParams(
            dimension_semantics=cfg.sem, vmem_limit_bytes=_vmem_limit(blocks, temps)),
        name="post_prompt",
    )(*args)


def _mix_kernel(x_ref, z_ref, a_ref, gate_ref, mod_ref, gpost1_ref, wbc_ref, wba_ref, wo_ref,
                o_ref):
    o_ref[...] = _merge_residual(x_ref, z_ref, a_ref, gate_ref, mod_ref, gpost1_ref,
                                 wbc_ref, wba_ref, wo_ref, True)


def _mix_sample(x, z, a, gates, mod_all, g_post1, w_bc, w_ba, w_o, layer_idx, n_seq):
    cfg = _dense(x, layer_idx, n_seq, True)
    sq = (D_MODEL, D_MODEL)
    blocks = (_nbytes((cfg.rows, D_MODEL), F32) * 2 + _nbytes((cfg.rows, D_MODEL), BF16) * 2
              + _nbytes((cfg.rows, 2 * D_MODEL), gates.dtype))
    temps = _nbytes(sq, BF16) * 3 + _nbytes((cfg.rows, D_MODEL), F32) * 6
    return pl.pallas_call(
        _mix_kernel,
        grid=cfg.grid,
        in_specs=[cfg.act(D_MODEL), cfg.act(CONV_DIM), cfg.act(ATTN_WIDTH),
                  cfg.act(2 * D_MODEL), cfg.mod_spec, cfg.layer(1, D_MODEL),
                  cfg.layer(*sq), cfg.layer(*sq), cfg.layer(*sq)],
        out_specs=cfg.act(D_MODEL),
        out_shape=jax.ShapeDtypeStruct(x.shape, F32),
        compiler_params=pltpu.CompilerParams(
            dimension_semantics=cfg.sem, vmem_limit_bytes=_vmem_limit(blocks, temps)),
        name="mix_sample",
    )(x, z, a, gates, mod_all, g_post1, w_bc, w_ba, w_o)


def _mlp_stream_kernel(x_ref, mod_ref, gpre2_ref, gpost2_ref, w1_ref, w2_ref,
                       o_ref, w1b_ref, w2b_ref, hb_ref, ff_ref):
    chunk = pl.program_id(0)

    @pl.when(chunk == 0)
    def _():
        hb_ref[...] = _mlp_input(x_ref[...], mod_ref, gpre2_ref, True)
        ff_ref[...] = jnp.zeros(ff_ref.shape, F32)

    w1 = w1_ref[...].astype(BF16)
    w2 = w2_ref[...].astype(BF16)
    w1b_ref[...] = w1
    w2b_ref[...] = w2
    act = jnp.maximum(_dot(hb_ref[...], w1), 0.0)
    ff_ref[...] += _dot((act * act).astype(BF16), w2)

    @pl.when(chunk == pl.num_programs(0) - 1)
    def _():
        o_ref[...] = _mlp_residual(x_ref[...], ff_ref[...], mod_ref, gpost2_ref, True)


def _mlp_stream_sample(x, mod_all, g_pre2, g_post2, w1, w2, layer_idx, n_seq):
    rows = x.shape[0] * x.shape[1]
    const = lambda shape, idx: pl.BlockSpec(shape, lambda c: idx, pipeline_mode=pl.Buffered(1))
    vec = const((None, 1, D_MODEL), (layer_idx, 0, 0))
    blocks = (_nbytes((D_MODEL, MLP_STREAM_CHUNK), F32) * 2
              + _nbytes((D_MODEL, MLP_STREAM_CHUNK), BF16) * 2 + _nbytes(x.shape, F32))
    temps = (_nbytes(x.shape, F32) * 4 + _nbytes((N_MOD, n_seq, D_MODEL), F32)
             + _nbytes((rows, D_MODEL), BF16) + _nbytes((rows, MLP_STREAM_CHUNK), F32) * 3)
    return pl.pallas_call(
        _mlp_stream_kernel,
        grid=(D_FF // MLP_STREAM_CHUNK,),
        in_specs=[const(x.shape, (0, 0, 0)),
                  const((None, N_MOD, n_seq, D_MODEL), (layer_idx, 0, 0, 0)), vec, vec,
                  pl.BlockSpec((None, D_MODEL, MLP_STREAM_CHUNK), lambda c: (layer_idx, 0, c)),
                  pl.BlockSpec((None, MLP_STREAM_CHUNK, D_MODEL), lambda c: (layer_idx, c, 0))],
        out_specs=[pl.BlockSpec(x.shape, lambda c: (0, 0, 0)),
                   pl.BlockSpec((None, D_MODEL, MLP_STREAM_CHUNK), lambda c: (0, 0, c)),
                   pl.BlockSpec((None, MLP_STREAM_CHUNK, D_MODEL), lambda c: (0, c, 0))],
        out_shape=[jax.ShapeDtypeStruct(x.shape, F32),
                   jax.ShapeDtypeStruct((1, D_MODEL, D_FF), BF16),
                   jax.ShapeDtypeStruct((1, D_FF, D_MODEL), BF16)],
        scratch_shapes=[pltpu.VMEM((rows, D_MODEL), BF16), pltpu.VMEM((rows, D_MODEL), F32)],
        compiler_params=pltpu.CompilerParams(
            dimension_semantics=("arbitrary",), vmem_limit_bytes=_vmem_limit(blocks, temps)),
        name="mlp_stream_sample",
    )(x, mod_all, g_pre2, g_post2, w1, w2)


def kernel(x_prompt, x_sample, c_prompt, c_sample, state_conv, cache_k, cache_v, w_ada, b_ada,
           g_pre1, w_in, conv_w, w_br_conv, w_br_attn, w_o, sinks, g_post1, g_pre2, w_ff1, w_ff2,
           g_post2, rel_table):
    depth = w_ada.shape[0]
    batch, seq, _ = x_prompt.shape
    n_seq, n_new, _ = x_sample.shape
    buf = cache_k.shape[2]

    pad = (-(n_seq + batch)) % V7X_SUBLANES
    c_all = jnp.concatenate([c_sample, c_prompt, jnp.zeros((pad, D_MODEL), F32)], axis=0)
    mod_all = _ada(c_all, w_ada, b_ada)

    dist_p = (jnp.arange(BLOCK)[:, None] + BLOCK) - jnp.arange(2 * BLOCK)[None, :]
    bucket_p = _rel_bucket(dist_p)
    upper = jnp.arange(BLOCK)[None, :] > jnp.arange(BLOCK)[:, None]
    bucket_merged = jnp.where(upper, bucket_p[:, :BLOCK], bucket_p[:, BLOCK:])
    bias_p = _bias_table(rel_table, bucket_merged.T)
    dist_s = (buf + jnp.arange(n_new))[:, None] - jnp.arange(buf + n_new)[None, :]
    bias_s = _bias_table(rel_table, _rel_bucket(dist_s))
    bias_s = (bias_s.reshape(N_KV_HEADS, GROUP, n_new, buf + n_new)
              .transpose(0, 2, 1, 3).reshape(N_KV_HEADS, n_new * GROUP, buf + n_new))
    bias_s = jnp.pad(bias_s, ((0, 0), (0, 0), (0, buf - n_new)))

    w_bc_b, w_ba_b, w_o_b = _cast_weights([w_br_conv, w_br_attn, w_o], range(depth),
                                          CAST_COLS_SQUARE)
    (w_in_b,) = _cast_weights([w_in], range(1), CAST_COLS_IN)
    vec = lambda g: g.reshape(depth, 1, D_MODEL)
    g_pre1, g_post1, g_pre2, g_post2 = map(vec, (g_pre1, g_post1, g_pre2, g_post2))
    pre_s = state_conv.transpose(0, 2, 1, 3)
    cache_k = cache_k.transpose(0, 1, 3, 4, 2)
    cache_v = cache_v.transpose(0, 1, 3, 4, 2)

    xp = x_prompt
    xs = x_sample.transpose(1, 0, 2)
    conv_p, k_p, v_p, conv_s = [], [], [], []
    windows = None
    win = min(WINDOW, seq)
    for l in range(depth):
        z, q, kv, gates, tail = _in_proj(xs, mod_all, g_pre1, w_in_b, conv_w, l, n_seq, pre_s)
        q_s = (q.reshape(n_new, n_seq, N_KV_HEADS, GROUP, HEAD_DIM)
               .transpose(1, 2, 0, 3, 4).reshape(n_seq, N_KV_HEADS, n_new * GROUP, HEAD_DIM))
        kv_s = kv.transpose(1, 0, 2)
        sink_rows = jnp.tile(sinks[l].reshape(N_KV_HEADS, 1, GROUP),
                             (1, n_new, 1)).reshape(N_KV_HEADS, n_new * GROUP, 1)
        o, *windows = _sample_attn(q_s, cache_k, cache_v, kv_s[:, :, :KV_WIDTH],
                                   kv_s[:, :, KV_WIDTH:], bias_s, sink_rows, l, windows)
        attn = (o.reshape(n_seq, N_KV_HEADS, n_new, GROUP, HEAD_DIM)
                .transpose(2, 0, 1, 3, 4).reshape(n_new, n_seq, ATTN_WIDTH))
        x1 = _mix_sample(xs, z, attn, gates, mod_all, g_post1, w_bc_b, w_ba_b, w_o_b, l, n_seq)
        xs, w1_b, w2_b = _mlp_stream_sample(x1, mod_all, g_pre2, g_post2, w_ff1, w_ff2, l, n_seq)
        conv_s.append(tail.transpose(1, 0, 2))

        z, attn, kv, gates, tail = _in_proj(xp, mod_all, g_pre1, w_in_b, conv_w, l, n_seq,
                                            attn=(sinks, bias_p))
        cast_next = (w_in, l + 1) if l + 1 < depth else None
        xp, *w_next = _post(xp, z, attn, gates, mod_all, g_post1, g_pre2, g_post2, w_bc_b, w_ba_b,
                            w_o_b, w1_b, w2_b, l, 0, n_seq, cast_next)
        w_in_b = w_next[0] if w_next else None
        conv_p.append(tail[:, V7X_SUBLANES - (CONV_WIDTH - 1):])
        k_p.append(kv[:, :, :KV_WIDTH].reshape(batch, win, N_KV_HEADS, HEAD_DIM))
        v_p.append(kv[:, :, KV_WIDTH:].reshape(batch, win, N_KV_HEADS, HEAD_DIM))

    k_s, v_s = (w.transpose(0, 1, 4, 2, 3) for w in windows)
    return (xp, xs.transpose(1, 0, 2), jnp.stack(conv_p), jnp.stack(k_p), jnp.stack(v_p),
            jnp.stack(conv_s), k_s, v_s)
```

```python
import functools
import math
from typing import Any, Callable, NamedTuple

import jax
import jax.numpy as jnp
from jax import lax
from jax.experimental import pallas as pl
from jax.experimental.pallas import tpu as pltpu

D_MODEL = 1024
N_HEADS = 16
N_KV_HEADS = 2
HEAD_DIM = 64
GROUP = N_HEADS // N_KV_HEADS
ATTN_WIDTH = N_HEADS * HEAD_DIM
KV_WIDTH = N_KV_HEADS * HEAD_DIM
CONV_DIM = D_MODEL
CONV_WIDTH = 3
WINDOW = 128
BLOCK = 128
N_BUCKETS = 32
MAX_DISTANCE = 128
D_FF = 4 * D_MODEL
N_MOD = 6
RMS_EPS = 1e-6
NEG_INF = -1e30
PROJ_COLS = 3 * CONV_DIM + ATTN_WIDTH + 2 * KV_WIDTH + 2 * D_MODEL
Q_OFF = 3 * CONV_DIM
KV_OFF = Q_OFF + ATTN_WIDTH
GATE_OFF = KV_OFF + 2 * KV_WIDTH

V7X_SUBLANES = 8
V7X_VMEM_BYTES = 64 * 1024 * 1024

PROMPT_TILE = 512
SAMPLE_POS_TILE = 4
SAMPLE_SEQ_TILE = 32
CHUNK = 256
MLP_STREAM_CHUNK = 1024
CAST_CHUNK = 256

F32 = jnp.float32
BF16 = jnp.bfloat16


def _vmem_limit(block_bytes, temp_bytes):
    return int(min(2 * block_bytes + temp_bytes, V7X_VMEM_BYTES - 4 * 1024 * 1024))


def _nbytes(shape, dtype):
    return math.prod(shape) * jnp.dtype(dtype).itemsize


def _rms(x):
    return x * lax.rsqrt(jnp.mean(x * x, axis=-1, keepdims=True) + RMS_EPS)


def _dot(a, b):
    return jnp.dot(a, b, preferred_element_type=F32)


def _flat(a):
    return a.reshape(-1, a.shape[-1])


def _ada_kernel(c_ref, w_ref, b_ref, o_ref):
    c = c_ref[...]
    s = c * (1.0 / (1.0 + jnp.exp(-c)))
    o_ref[...] = _dot(s.astype(BF16), w_ref[...].astype(BF16)) + b_ref[...]


def _ada(c_all, w_ada, b_ada):
    depth = w_ada.shape[0]
    rows = c_all.shape[0]
    blocks = (_nbytes((rows, D_MODEL), F32) * 2 + _nbytes((D_MODEL, D_MODEL), F32))
    return pl.pallas_call(
        _ada_kernel,
        grid=(depth, N_MOD),
        in_specs=[
            pl.BlockSpec((rows, D_MODEL), lambda l, j: (0, 0)),
            pl.BlockSpec((None, D_MODEL, D_MODEL), lambda l, j: (l, 0, j)),
            pl.BlockSpec((None, None, 1, D_MODEL), lambda l, j: (l, j, 0, 0)),
        ],
        out_specs=pl.BlockSpec((None, None, rows, D_MODEL), lambda l, j: (l, j, 0, 0)),
        out_shape=jax.ShapeDtypeStruct((depth, N_MOD, rows, D_MODEL), F32),
        compiler_params=pltpu.CompilerParams(
            dimension_semantics=("arbitrary", "arbitrary"),
            vmem_limit_bytes=_vmem_limit(blocks, _nbytes((D_MODEL, D_MODEL), BF16) * 2)),
        name="ada_mod",
    )(c_all, w_ada, b_ada.reshape(depth, N_MOD, 1, D_MODEL))


def _bias_kernel(tab_ref, bucket_ref, o_ref):
    bucket = bucket_ref[...]
    for h in range(N_HEADS):
        acc = jnp.zeros(bucket.shape, F32)
        for b in range(N_BUCKETS):
            acc = jnp.where(bucket == b, tab_ref[b * N_HEADS + h], acc)
        o_ref[h] = acc


def _bias_table(rel_table, bucket):
    return pl.pallas_call(
        _bias_kernel,
        in_specs=[
            pl.BlockSpec(memory_space=pltpu.SMEM),
            pl.BlockSpec(bucket.shape, lambda: (0, 0)),
        ],
        out_specs=pl.BlockSpec((N_HEADS,) + bucket.shape, lambda: (0, 0, 0)),
        out_shape=jax.ShapeDtypeStruct((N_HEADS,) + bucket.shape, F32),
        name="bias_table",
    )(rel_table.reshape(-1), bucket)


def _rel_bucket(dist):
    n = jnp.maximum(dist, 0)
    max_exact = N_BUCKETS // 2
    nf = jnp.maximum(n, 1).astype(F32)
    scaled = (jnp.log(nf / max_exact) / math.log(MAX_DISTANCE / max_exact)
              * (N_BUCKETS - max_exact))
    large = jnp.minimum(max_exact + jnp.floor(scaled).astype(jnp.int32), N_BUCKETS - 1)
    return jnp.where(n < max_exact, n, large)


class _Dense(NamedTuple):
    sample: bool
    grid: tuple
    rows: int
    act: Callable[[int], Any]
    layer: Callable[..., Any]
    mod_spec: Any
    sem: tuple


def _dense(x, layer_idx, n_seq, sample):
    lead, mid = x.shape[0], x.shape[1]
    if sample:
        grid = (lead // SAMPLE_POS_TILE,)
        act = lambda w: pl.BlockSpec((SAMPLE_POS_TILE, mid, w), lambda i: (i, 0, 0))
        layer = lambda *shape: pl.BlockSpec((None,) + shape,
                                            lambda i: (layer_idx,) + (0,) * len(shape),
                                            pipeline_mode=pl.Buffered(1))
        mod_spec = pl.BlockSpec((None, N_MOD, n_seq, D_MODEL), lambda i: (layer_idx, 0, 0, 0))
        return _Dense(True, grid, SAMPLE_POS_TILE * mid, act, layer, mod_spec, ("arbitrary",))
    assert n_seq % V7X_SUBLANES == 0 and lead <= V7X_SUBLANES
    grid = (lead, mid // PROMPT_TILE)
    act = lambda w: pl.BlockSpec((None, PROMPT_TILE, w), lambda b, t: (b, t, 0))
    layer = lambda *shape: pl.BlockSpec((None,) + shape,
                                        lambda b, t: (layer_idx,) + (0,) * len(shape),
                                        pipeline_mode=pl.Buffered(1))
    mod_spec = pl.BlockSpec((None, N_MOD, V7X_SUBLANES, D_MODEL),
                            lambda b, t: (layer_idx, 0, n_seq // V7X_SUBLANES, 0))
    return _Dense(False, grid, PROMPT_TILE, act, layer, mod_spec, ("arbitrary", "arbitrary"))


def _mod(mod_ref, j, sample):
    if sample:
        return mod_ref[j]
    return mod_ref[j, pl.ds(pl.program_id(0), 1), :]


def _sink_softmax(s, sink):
    m = jnp.maximum(jnp.max(s, axis=-1, keepdims=True), sink)
    e = jnp.exp(s - m)
    den = jnp.sum(e, axis=-1, keepdims=True) + jnp.exp(sink - m)
    return e * (1.0 / den)


def _stage_keys(kv, kd_ref, vt_ref, first_tile):
    n_blk = kv.shape[0] // BLOCK

    kd_ref[:, 0:BLOCK] = jnp.where(first_tile, jnp.zeros((), BF16), kd_ref[:, n_blk * BLOCK:])
    vt_ref[0] = jnp.where(first_tile, jnp.zeros((), BF16), vt_ref[n_blk])

    for g in range(N_KV_HEADS):
        k_g = kv[:, g * HEAD_DIM:(g + 1) * HEAD_DIM]
        kd_ref[g, BLOCK:] = jnp.concatenate([k_g, k_g], axis=1).astype(BF16)
    v_t = kv[:, KV_WIDTH:].T.astype(BF16)
    for blk in range(n_blk):
        vt_ref[blk + 1] = v_t[:, blk * BLOCK:(blk + 1) * BLOCK]


def _attention_phases(q_ref, kd_ref, vt_ref, s_ref, bias_ref, sink_ref, o_ref, first_tile):
    kj = lax.broadcasted_iota(jnp.int32, (BLOCK, BLOCK), 0)
    qi = lax.broadcasted_iota(jnp.int32, (BLOCK, BLOCK), 1)
    upper = kj > qi
    low_lanes = lax.broadcasted_iota(jnp.int32, (BLOCK, 2 * HEAD_DIM), 1) < HEAD_DIM

    def score_phase(c):
        q = q_ref[c * BLOCK:(c + 1) * BLOCK, :]
        for pair in range(N_HEADS // 2):
            g = (2 * pair) // GROUP
            q_pair = q[:, pair * 2 * HEAD_DIM:(pair + 1) * 2 * HEAD_DIM]
            zero = jnp.zeros_like(q_pair)
            q_both = jnp.concatenate([jnp.where(low_lanes, q_pair, zero),
                                      jnp.where(low_lanes, zero, q_pair)], axis=0)
            s_both = lax.dot_general(kd_ref[g, c * BLOCK:(c + 2) * BLOCK, :], q_both,
                                     (((1,), (1,)), ((), ())), preferred_element_type=F32)
            for par in range(2):
                h = 2 * pair + par
                s_h = s_both[:, par * BLOCK:(par + 1) * BLOCK]
                s = jnp.where(upper, s_h[:BLOCK], s_h[BLOCK:]) + bias_ref[h]
                if c == 0:
                    s = jnp.where(upper & first_tile, NEG_INF, s)
                s_ref[c, h] = s

    def value_phase(c, pair):
        g = (2 * pair) // GROUP
        v_cat = jnp.concatenate([vt_ref[c, g * HEAD_DIM:(g + 1) * HEAD_DIM],
                                 vt_ref[c + 1, g * HEAD_DIM:(g + 1) * HEAD_DIM]], axis=1)
        weights, scales = [], []
        for h in (2 * pair, 2 * pair + 1):
            s = s_ref[c, h]
            sink = sink_ref[h]
            m = jnp.maximum(jnp.max(s, axis=0, keepdims=True), sink)
            e = jnp.exp(s - m)
            den = jnp.sum(e, axis=0, keepdims=True) + jnp.exp(sink - m)
            weights.append(jnp.concatenate([jnp.where(upper, e, 0.0), jnp.where(upper, 0.0, e)],
                                           axis=0).astype(BF16))
            scales.append(1.0 / den)
        o_t = _dot(v_cat, jnp.concatenate(weights, axis=1))
        o_t = jnp.concatenate([o_t[:, :BLOCK] * scales[0], o_t[:, BLOCK:] * scales[1]], axis=0)
        o_ref[c * BLOCK:(c + 1) * BLOCK, pair * 2 * HEAD_DIM:(pair + 1) * 2 * HEAD_DIM] = (
            o_t.T.astype(BF16))

    return score_phase, value_phase


def _in_proj_kernel(*refs, sample):
    if sample:
        (x_ref, mod_ref, g_ref, w_ref, cw_ref, pre_ref,
         z_ref, q_ref, kv_ref, gate_ref, tail_ref, carry_ref) = refs
    else:
        (x_ref, mod_ref, g_ref, w_ref, cw_ref, sink_ref, bias_ref,
         z_ref, a_ref, kv_ref, gate_ref, tail_ref,
         carry_ref, q_ref, kd_ref, vt_ref, s_ref) = refs
    step = pl.program_id(0) if sample else pl.program_id(1)

    first_tile = step == 0
    x = x_ref[...]
    h = _rms(x) * (g_ref[...] * (1.0 + _mod(mod_ref, 1, sample))) + _mod(mod_ref, 0, sample)
    hb = _flat(h).astype(BF16)
    rows = hb.shape[0]

    def store(ref, lo, val):
        ref[..., lo:lo + val.shape[-1]] = val.reshape(ref.shape[:-1] + (val.shape[-1],))

    def conv_chunk(lo):
        b_g = _dot(hb, w_ref[:, lo:lo + CHUNK])
        c_g = _dot(hb, w_ref[:, CONV_DIM + lo:CONV_DIM + lo + CHUNK])
        x_c = _dot(hb, w_ref[:, 2 * CONV_DIM + lo:2 * CONV_DIM + lo + CHUNK])
        u = c_g * x_c
        if sample:
            pos = x_ref.shape[0]
            u3 = u.reshape(pos, -1, CHUNK)
            prev = jnp.where(first_tile, pre_ref[:, :, lo:lo + CHUNK],
                             carry_ref[:, :, lo:lo + CHUNK])
            ext = jnp.concatenate([prev, u3], axis=0)
            u2 = _flat(ext[0:pos])
            u1 = _flat(ext[1:pos + 1])
            new_tail = u3[pos - (CONV_WIDTH - 1):]
            carry_ref[:, :, lo:lo + CHUNK] = new_tail
            tail_ref[:, :, lo:lo + CHUNK] = new_tail
        else:
            prev = jnp.where(first_tile, 0.0, carry_ref[:, lo:lo + CHUNK])
            p1 = prev[V7X_SUBLANES - 1:V7X_SUBLANES]
            p2 = prev[V7X_SUBLANES - 2:V7X_SUBLANES - 1]
            r = lax.broadcasted_iota(jnp.int32, u.shape, 0)
            u1 = jnp.where(r == 0, p1, pltpu.roll(u, 1, 0))
            u2 = jnp.where(r == 0, p2, jnp.where(r == 1, p1, pltpu.roll(u, 2, 0)))
            new_tail = u[rows - V7X_SUBLANES:]
            carry_ref[:, lo:lo + CHUNK] = new_tail
            tail_ref[:, lo:lo + CHUNK] = new_tail
        conv = (cw_ref[0:1, lo:lo + CHUNK] * u2 + cw_ref[1:2, lo:lo + CHUNK] * u1
                + cw_ref[2:3, lo:lo + CHUNK] * u)
        store(z_ref, lo, (b_g * conv).astype(BF16))

    def q_chunk(lo):
        q = _dot(hb, w_ref[:, Q_OFF + lo:Q_OFF + lo + CHUNK])
        store(q_ref, lo, (q * (HEAD_DIM ** -0.5)).astype(BF16))

    def gate_chunk(lo):
        g = _dot(hb, w_ref[:, GATE_OFF + lo:GATE_OFF + lo + CHUNK])
        store(gate_ref, lo, (1.0 / (1.0 + jnp.exp(-g))).astype(BF16))

    kv = _dot(hb, w_ref[:, KV_OFF:KV_OFF + 2 * KV_WIDTH])
    if sample:
        store(kv_ref, 0, kv)
    else:
        kv_ref[...] = kv[rows - kv_ref.shape[0]:]
    for lo in range(0, ATTN_WIDTH, CHUNK):
        q_chunk(lo)
    dense = ([functools.partial(conv_chunk, lo) for lo in range(0, CONV_DIM, CHUNK)]
             + [functools.partial(gate_chunk, lo) for lo in range(0, 2 * D_MODEL, CHUNK)])
    if sample:
        for work in dense:
            work()
        return

    _stage_keys(kv, kd_ref, vt_ref, first_tile)
    score_phase, value_phase = _attention_phases(q_ref, kd_ref, vt_ref, s_ref, bias_ref,
                                                 sink_ref, a_ref, first_tile)
    n_blk = rows // BLOCK
    attention = [functools.partial(score_phase, 0)]
    for c in range(n_blk):
        if c + 1 < n_blk:
            attention.append(functools.partial(score_phase, c + 1))
        attention += [functools.partial(value_phase, c, pair) for pair in range(N_HEADS // 2)]
    done = 0
    for i, work in enumerate(attention):
        work()
        while done < len(dense) and done * len(attention) < (i + 1) * len(dense):
            dense[done]()
            done += 1


def _in_proj(x, mod_all, g_pre1, w_in, conv_w, layer_idx, n_seq, pre=None, attn=None):
    cfg = _dense(x, layer_idx, n_seq, pre is not None)
    lead, mid = x.shape[0], x.shape[1]
    widths = (CONV_DIM, ATTN_WIDTH, 2 * KV_WIDTH, 2 * D_MODEL)
    dtypes = (BF16, BF16, F32, BF16)
    blocks = (_nbytes((cfg.rows, D_MODEL), F32) + _nbytes(w_in.shape[1:], BF16)
              + sum(_nbytes((cfg.rows, w), dt) for w, dt in zip(widths, dtypes)))
    temps = _nbytes((cfg.rows, D_MODEL), F32) * 2 + _nbytes((cfg.rows, CHUNK), F32) * 10
    if cfg.sample:
        tail_shape = (CONV_WIDTH - 1, mid, CONV_DIM)
        tail_spec = pl.BlockSpec(tail_shape, lambda i: (0, 0, 0))
        extra_in = [pre]
        extra_specs = [cfg.layer(*tail_shape)]
        scratch = [pltpu.VMEM(tail_shape, F32)]
    else:
        sinks, bias = attn
        per_tile = cfg.rows // BLOCK
        tail_shape = (lead, V7X_SUBLANES, CONV_DIM)
        tail_spec = pl.BlockSpec((None, V7X_SUBLANES, CONV_DIM), lambda b, t: (b, 0, 0))
        extra_in = [sinks[layer_idx], bias]
        extra_specs = [pl.BlockSpec(memory_space=pltpu.SMEM),
                       pl.BlockSpec(bias.shape, lambda b, t: (0, 0, 0),
                                    pipeline_mode=pl.Buffered(1))]
        scratch = [pltpu.VMEM((V7X_SUBLANES, CONV_DIM), F32),
                   pltpu.VMEM((cfg.rows, ATTN_WIDTH), BF16),
                   pltpu.VMEM((N_KV_HEADS, cfg.rows + BLOCK, 2 * HEAD_DIM), BF16),
                   pltpu.VMEM((per_tile + 1, KV_WIDTH, BLOCK), BF16),
                   pltpu.VMEM((per_tile, N_HEADS, BLOCK, BLOCK), F32)]
        temps += (_nbytes(bias.shape, F32) + _nbytes((cfg.rows, ATTN_WIDTH), BF16)
                  + _nbytes((per_tile, N_HEADS, BLOCK, BLOCK), F32)
                  + _nbytes((BLOCK, 2 * BLOCK), F32) * 16 + _nbytes((BLOCK, ATTN_WIDTH), F32) * 4)

    out_shape = [jax.ShapeDtypeStruct(x.shape[:-1] + (w,), dt) for w, dt in zip(widths, dtypes)]
    out_shape.append(jax.ShapeDtypeStruct(tail_shape, F32))
    out_specs = [cfg.act(w) for w in widths] + [tail_spec]
    if not cfg.sample:
        win = min(WINDOW, mid)
        out_shape[2] = jax.ShapeDtypeStruct((lead, win, 2 * KV_WIDTH), F32)
        out_specs[2] = pl.BlockSpec((None, win, 2 * KV_WIDTH), lambda b, t: (b, 0, 0))
    return pl.pallas_call(
        functools.partial(_in_proj_kernel, sample=cfg.sample),
        grid=cfg.grid,
        in_specs=[cfg.act(D_MODEL), cfg.mod_spec, cfg.layer(1, D_MODEL),
                  _dense(x, 0, n_seq, cfg.sample).layer(*w_in.shape[1:]),
                  cfg.layer(*conv_w.shape[1:])] + extra_specs,
        out_specs=out_specs,
        out_shape=out_shape,
        scratch_shapes=scratch,
        compiler_params=pltpu.CompilerParams(
            dimension_semantics=cfg.sem, vmem_limit_bytes=_vmem_limit(blocks, temps)),
        name="in_proj_sample" if cfg.sample else "in_proj_attn_prompt",
    )(x, mod_all, g_pre1, w_in, conv_w, *extra_in)


def _sample_attn_kernel(q_ref, kc_ref, vc_ref, kn_ref, vn_ref, bias_ref, sink_ref, *rest,
                        layer_idx):
    o_ref, ko_ref, vo_ref = rest[-3:]
    if ko_ref.ndim > kc_ref.ndim:
        for other in range(ko_ref.shape[0]):
            if other != layer_idx:
                ko_ref[other] = jnp.zeros(ko_ref.shape[1:], F32)
                vo_ref[other] = jnp.zeros(vo_ref.shape[1:], F32)
        ko_ref, vo_ref = ko_ref.at[layer_idx], vo_ref.at[layer_idx]
    n_seq = q_ref.shape[0]
    buf = kc_ref.shape[-1]
    new = kn_ref.shape[1]
    rows = new * GROUP
    pos = lax.broadcasted_iota(jnp.int32, (rows, 2 * buf), 0) // GROUP
    kj = lax.broadcasted_iota(jnp.int32, (rows, 2 * buf), 1)
    dist = buf + pos - kj
    mask = (dist >= 0) & (dist < WINDOW) & (kj < buf + new)
    lane = lax.broadcasted_iota(jnp.int32, (n_seq, HEAD_DIM, buf), 2)
    pad = jnp.zeros((n_seq, HEAD_DIM, buf - new), F32)

    def extended(cache_t, fresh, out_ref, g):
        fresh_t = jnp.concatenate([jnp.swapaxes(fresh, 1, 2), pad], axis=2)
        out_ref[:, g] = pltpu.roll(jnp.where(lane < new, fresh_t, cache_t), buf - new, 2)
        return jnp.concatenate([cache_t, fresh_t], axis=2).astype(BF16)

    for g in range(N_KV_HEADS):
        lanes = slice(g * HEAD_DIM, (g + 1) * HEAD_DIM)
        k_ext = extended(kc_ref[:, g], kn_ref[:, :, lanes], ko_ref, g)
        v_ext = extended(vc_ref[:, g], vn_ref[:, :, lanes], vo_ref, g)
        s = jnp.einsum('nqd,ndk->nqk', q_ref[:, g], k_ext, preferred_element_type=F32)
        p = _sink_softmax(jnp.where(mask[None], s + bias_ref[g][None], NEG_INF),
                          sink_ref[g][None])
        o = jnp.einsum('nqk,ndk->nqd', p.astype(BF16), v_ext, preferred_element_type=F32)
        o_ref[:, g] = o.astype(BF16)


def _sample_attn(q, k_cache, v_cache, k_new, v_new, bias, sink_rows, layer_idx, windows):
    _, n, _, _, buf = k_cache.shape
    new = k_new.shape[1]
    rows = new * GROUP
    nt = SAMPLE_SEQ_TILE
    seq3 = lambda a, b: pl.BlockSpec((nt, a, b), lambda i: (i, 0, 0))
    cache = pl.BlockSpec((None, nt, N_KV_HEADS, HEAD_DIM, buf),
                         lambda i: (layer_idx, i, 0, 0, 0))
    q_spec = pl.BlockSpec((nt, N_KV_HEADS, rows, HEAD_DIM), lambda i: (i, 0, 0, 0))
    in_specs = [q_spec, cache, cache, seq3(new, KV_WIDTH), seq3(new, KV_WIDTH),
                pl.BlockSpec(bias.shape, lambda i: (0, 0, 0)),
                pl.BlockSpec(sink_rows.shape, lambda i: (0, 0, 0))]
    args = [q, k_cache, v_cache, k_new, v_new, bias, sink_rows]
    depth = k_cache.shape[0]
    if windows is None:
        aliases = {}
        window_out = pl.BlockSpec((depth, nt, N_KV_HEADS, HEAD_DIM, buf),
                                  lambda i: (0, i, 0, 0, 0))
    else:
        aliases = {len(args): 1, len(args) + 1: 2}
        in_specs += [pl.BlockSpec(memory_space=pl.ANY)] * 2
        args += list(windows)
        window_out = cache
    blocks = (_nbytes((nt, N_KV_HEADS, HEAD_DIM, buf), F32) * (2 + 2 * depth)
              + _nbytes((nt, new, KV_WIDTH), F32) * 2
              + _nbytes((nt, N_KV_HEADS, rows, 2 * HEAD_DIM), BF16) * 2 + _nbytes(bias.shape, F32))
    temps = _nbytes((rows, 2 * buf), F32) * 8 * nt
    return pl.pallas_call(
        functools.partial(_sample_attn_kernel, layer_idx=layer_idx),
        grid=(n // nt,),
        in_specs=in_specs,
        out_specs=[q_spec, window_out, window_out],
        out_shape=[jax.ShapeDtypeStruct(q.shape, BF16),
                   jax.ShapeDtypeStruct(k_cache.shape, F32),
                   jax.ShapeDtypeStruct(v_cache.shape, F32)],
        input_output_aliases=aliases,
        compiler_params=pltpu.CompilerParams(
            dimension_semantics=("arbitrary",), vmem_limit_bytes=_vmem_limit(blocks, temps)),
        name="attn_sample",
    )(*args)


def _merge_residual(x_ref, z_ref, a_ref, gate_ref, mod_ref, gpost1_ref, wbc_ref, wba_ref, wo_ref,
                    sample):
    y_conv = _dot(_flat(z_ref[...]), wbc_ref[...])
    y_attn = _dot(_flat(a_ref[...]), wba_ref[...])
    gates = _flat(gate_ref[...])
    merged = gates[:, :D_MODEL] * y_conv + gates[:, D_MODEL:] * y_attn
    mixed = _dot(merged.astype(BF16), wo_ref[...])
    scale = _mod(mod_ref, 2, sample) * gpost1_ref[...]
    return x_ref[...] + scale * _rms(mixed).reshape(x_ref.shape)


def _mlp_input(x1, mod_ref, gpre2_ref, sample):
    scale = gpre2_ref[...] * (1.0 + _mod(mod_ref, 4, sample))
    return _flat(_rms(x1) * scale + _mod(mod_ref, 3, sample)).astype(BF16)


def _mlp_residual(x1, ff, mod_ref, gpost2_ref, sample):
    scale = _mod(mod_ref, 5, sample) * gpost2_ref[...]
    return x1 + scale * _rms(ff).reshape(x1.shape)


def _post_kernel(x_ref, z_ref, a_ref, gate_ref, mod_ref, gpost1_ref, gpre2_ref, gpost2_ref,
                 wbc_ref, wba_ref, wo_ref, w1_ref, w2_ref, *rest, sample):
    if len(rest) == 4:
        cast_in_ref, o_ref, cast_out_ref, hid_ref = rest
        cast_out_ref[...] = cast_in_ref[...].astype(BF16)
    else:
        o_ref, hid_ref = rest
    x1 = _merge_residual(x_ref, z_ref, a_ref, gate_ref, mod_ref, gpost1_ref,
                         wbc_ref, wba_ref, wo_ref, sample)
    hb = _mlp_input(x1, mod_ref, gpre2_ref, sample)
    for lo in range(0, D_FF, 2 * CHUNK):
        act = jnp.maximum(_dot(hb, w1_ref[:, lo:lo + 2 * CHUNK]), 0.0)
        hid_ref[:, lo:lo + 2 * CHUNK] = (act * act).astype(BF16)
    ff = _dot(hid_ref[...], w2_ref[...])
    o_ref[...] = _mlp_residual(x1, ff, mod_ref, gpost2_ref, sample)


def _post(x, z, a, gates, mod_all, g_post1, g_pre2, g_post2, w_bc, w_ba, w_o, w1, w2,
          layer_idx, mlp_layer_idx, n_seq, cast_next=None):
    cfg = _dense(x, layer_idx, n_seq, False)
    mlp = _dense(x, mlp_layer_idx, n_seq, False)
    sq = (D_MODEL, D_MODEL)
    blocks = (_nbytes((cfg.rows, D_MODEL), F32) * 2 + _nbytes((cfg.rows, D_MODEL), BF16) * 2
              + _nbytes((cfg.rows, 2 * D_MODEL), gates.dtype))
    temps = (_nbytes(sq, BF16) * 3 + _nbytes(w1.shape[1:], BF16) * 2
             + _nbytes((cfg.rows, D_FF), BF16) + _nbytes((cfg.rows, 2 * CHUNK), F32) * 2
             + _nbytes((cfg.rows, D_MODEL), F32) * 5)
    vec = cfg.layer(1, D_MODEL)
    in_specs = [cfg.act(D_MODEL), cfg.act(CONV_DIM), cfg.act(ATTN_WIDTH),
                cfg.act(2 * D_MODEL), cfg.mod_spec, vec, vec, vec,
                cfg.layer(*sq), cfg.layer(*sq), cfg.layer(*sq),
                mlp.layer(*w1.shape[1:]), mlp.layer(*w2.shape[1:])]
    args = [x, z, a, gates, mod_all, g_post1, g_pre2, g_post2, w_bc, w_ba, w_o, w1, w2]
    out_specs = [cfg.act(D_MODEL)]
    out_shape = [jax.ShapeDtypeStruct(x.shape, F32)]
    if cast_next is not None:
        w_next, src_layer = cast_next
        _, k_dim, n_dim = w_next.shape
        last = n_dim // CAST_CHUNK - 1
        assert n_dim % CAST_CHUNK == 0 and last < cfg.grid[0] * cfg.grid[1]
        tiles = cfg.grid[1]
        in_specs.append(pl.BlockSpec(
            (None, k_dim, CAST_CHUNK),
            lambda b, t: (src_layer, 0, jnp.minimum(b * tiles + t, last))))
        out_specs.append(pl.BlockSpec(
            (None, k_dim, CAST_CHUNK), lambda b, t: (0, 0, jnp.minimum(b * tiles + t, last))))
        out_shape.append(jax.ShapeDtypeStruct((1, k_dim, n_dim), BF16))
        args.append(w_next)
        blocks += _nbytes((k_dim, CAST_CHUNK), F32) + _nbytes((k_dim, CAST_CHUNK), BF16)
    return pl.pallas_call(
        functools.partial(_post_kernel, sample=False),
        grid=cfg.grid,
        in_specs=in_specs,
        out_specs=out_specs,
        out_shape=out_shape,
        scratch_shapes=[pltpu.VMEM((cfg.rows, D_FF), BF16)],
        compiler_params=pltpu.CompilerParams(
            dimension_semantics=cfg.sem, vmem_limit_bytes=_vmem_limit(blocks, temps)),
        name="post_prompt",
    )(*args)


def _mix_kernel(x_ref, z_ref, a_ref, gate_ref, mod_ref, gpost1_ref, wbc_ref, wba_ref, wo_ref,
                o_ref):
    o_ref[...] = _merge_residual(x_ref, z_ref, a_ref, gate_ref, mod_ref, gpost1_ref,
                                 wbc_ref, wba_ref, wo_ref, True)


def _mix_sample(x, z, a, gates, mod_all, g_post1, w_bc, w_ba, w_o, layer_idx, n_seq):
    cfg = _dense(x, layer_idx, n_seq, True)
    sq = (D_MODEL, D_MODEL)
    blocks = (_nbytes((cfg.rows, D_MODEL), F32) * 2 + _nbytes((cfg.rows, D_MODEL), BF16) * 2
              + _nbytes((cfg.rows, 2 * D_MODEL), gates.dtype))
    temps = _nbytes(sq, BF16) * 3 + _nbytes((cfg.rows, D_MODEL), F32) * 6
    return pl.pallas_call(
        _mix_kernel,
        grid=cfg.grid,
        in_specs=[cfg.act(D_MODEL), cfg.act(CONV_DIM), cfg.act(ATTN_WIDTH),
                  cfg.act(2 * D_MODEL), cfg.mod_spec, cfg.layer(1, D_MODEL),
                  cfg.layer(*sq), cfg.layer(*sq), cfg.layer(*sq)],
        out_specs=cfg.act(D_MODEL),
        out_shape=jax.ShapeDtypeStruct(x.shape, F32),
        compiler_params=pltpu.CompilerParams(
            dimension_semantics=cfg.sem, vmem_limit_bytes=_vmem_limit(blocks, temps)),
        name="mix_sample",
    )(x, z, a, gates, mod_all, g_post1, w_bc, w_ba, w_o)


def _mlp_stream_kernel(x_ref, mod_ref, gpre2_ref, gpost2_ref, w1_ref, w2_ref,
                       o_ref, w1b_ref, w2b_ref, hb_ref, ff_ref):
    chunk = pl.program_id(0)

    @pl.when(chunk == 0)
    def _():
        hb_ref[...] = _mlp_input(x_ref[...], mod_ref, gpre2_ref, True)
        ff_ref[...] = jnp.zeros(ff_ref.shape, F32)

    w1 = w1_ref[...].astype(BF16)
    w2 = w2_ref[...].astype(BF16)
    w1b_ref[...] = w1
    w2b_ref[...] = w2
    act = jnp.maximum(_dot(hb_ref[...], w1), 0.0)
    ff_ref[...] += _dot((act * act).astype(BF16), w2)

    @pl.when(chunk == pl.num_programs(0) - 1)
    def _():
        o_ref[...] = _mlp_residual(x_ref[...], ff_ref[...], mod_ref, gpost2_ref, True)


def _mlp_stream_sample(x, mod_all, g_pre2, g_post2, w1, w2, layer_idx, n_seq):
    rows = x.shape[0] * x.shape[1]
    const = lambda shape, idx: pl.BlockSpec(shape, lambda c: idx, pipeline_mode=pl.Buffered(1))
    vec = const((None, 1, D_MODEL), (layer_idx, 0, 0))
    blocks = (_nbytes((D_MODEL, MLP_STREAM_CHUNK), F32) * 2
              + _nbytes((D_MODEL, MLP_STREAM_CHUNK), BF16) * 2 + _nbytes(x.shape, F32))
    temps = (_nbytes(x.shape, F32) * 4 + _nbytes((N_MOD, n_seq, D_MODEL), F32)
             + _nbytes((rows, D_MODEL), BF16) + _nbytes((rows, MLP_STREAM_CHUNK), F32) * 3)
    return pl.pallas_call(
        _mlp_stream_kernel,
        grid=(D_FF // MLP_STREAM_CHUNK,),
        in_specs=[const(x.shape, (0, 0, 0)),
                  const((None, N_MOD, n_seq, D_MODEL), (layer_idx, 0, 0, 0)), vec, vec,
                  pl.BlockSpec((None, D_MODEL, MLP_STREAM_CHUNK), lambda c: (layer_idx, 0, c)),
                  pl.BlockSpec((None, MLP_STREAM_CHUNK, D_MODEL), lambda c: (layer_idx, c, 0))],
        out_specs=[pl.BlockSpec(x.shape, lambda c: (0, 0, 0)),
                   pl.BlockSpec((None, D_MODEL, MLP_STREAM_CHUNK), lambda c: (0, 0, c)),
                   pl.BlockSpec((None, MLP_STREAM_CHUNK, D_MODEL), lambda c: (0, c, 0))],
        out_shape=[jax.ShapeDtypeStruct(x.shape, F32),
                   jax.ShapeDtypeStruct((1, D_MODEL, D_FF), BF16),
                   jax.ShapeDtypeStruct((1, D_FF, D_MODEL), BF16)],
        scratch_shapes=[pltpu.VMEM((rows, D_MODEL), BF16), pltpu.VMEM((rows, D_MODEL), F32)],
        compiler_params=pltpu.CompilerParams(
            dimension_semantics=("arbitrary",), vmem_limit_bytes=_vmem_limit(blocks, temps)),
        name="mlp_stream_sample",
    )(x, mod_all, g_pre2, g_post2, w1, w2)


def kernel(x_prompt, x_sample, c_prompt, c_sample, state_conv, cache_k, cache_v, w_ada, b_ada,
           g_pre1, w_in, conv_w, w_br_conv, w_br_attn, w_o, sinks, g_post1, g_pre2, w_ff1, w_ff2,
           g_post2, rel_table):
    depth = w_ada.shape[0]
    batch, seq, _ = x_prompt.shape
    n_seq, n_new, _ = x_sample.shape
    buf = cache_k.shape[2]

    pad = (-(n_seq + batch)) % V7X_SUBLANES
    c_all = jnp.concatenate([c_sample, c_prompt, jnp.zeros((pad, D_MODEL), F32)], axis=0)
    mod_all = _ada(c_all, w_ada, b_ada)

    dist_p = (jnp.arange(BLOCK)[:, None] + BLOCK) - jnp.arange(2 * BLOCK)[None, :]
    bucket_p = _rel_bucket(dist_p)
    upper = jnp.arange(BLOCK)[None, :] > jnp.arange(BLOCK)[:, None]
    bucket_merged = jnp.where(upper, bucket_p[:, :BLOCK], bucket_p[:, BLOCK:])
    bias_p = _bias_table(rel_table, bucket_merged.T)
    dist_s = (buf + jnp.arange(n_new))[:, None] - jnp.arange(buf + n_new)[None, :]
    bias_s = _bias_table(rel_table, _rel_bucket(dist_s))
    bias_s = (bias_s.reshape(N_KV_HEADS, GROUP, n_new, buf + n_new)
              .transpose(0, 2, 1, 3).reshape(N_KV_HEADS, n_new * GROUP, buf + n_new))
    bias_s = jnp.pad(bias_s, ((0, 0), (0, 0), (0, buf - n_new)))

    to_bf16 = lambda w: w.astype(BF16)
    w_bc_b, w_ba_b, w_o_b = map(to_bf16, (w_br_conv, w_br_attn, w_o))
    w_in_b = to_bf16(w_in[0:1])
    vec = lambda g: g.reshape(depth, 1, D_MODEL)
    g_pre1, g_post1, g_pre2, g_post2 = map(vec, (g_pre1, g_post1, g_pre2, g_post2))
    pre_s = state_conv.transpose(0, 2, 1, 3)
    cache_k = cache_k.transpose(0, 1, 3, 4, 2)
    cache_v = cache_v.transpose(0, 1, 3, 4, 2)

    xp = x_prompt
    xs = x_sample.transpose(1, 0, 2)
    conv_p, k_p, v_p, conv_s = [], [], [], []
    windows = None
    win = min(WINDOW, seq)
    for l in range(depth):
        z, q, kv, gates, tail = _in_proj(xs, mod_all, g_pre1, w_in_b, conv_w, l, n_seq, pre_s)
        q_s = (q.reshape(n_new, n_seq, N_KV_HEADS, GROUP, HEAD_DIM)
               .transpose(1, 2, 0, 3, 4).reshape(n_seq, N_KV_HEADS, n_new * GROUP, HEAD_DIM))
        kv_s = kv.transpose(1, 0, 2)
        sink_rows = jnp.tile(sinks[l].reshape(N_KV_HEADS, 1, GROUP),
                             (1, n_new, 1)).reshape(N_KV_HEADS, n_new * GROUP, 1)
        o, *windows = _sample_attn(q_s, cache_k, cache_v, kv_s[:, :, :KV_WIDTH],
                                   kv_s[:, :, KV_WIDTH:], bias_s, sink_rows, l, windows)
        attn = (o.reshape(n_seq, N_KV_HEADS, n_new, GROUP, HEAD_DIM)
                .transpose(2, 0, 1, 3, 4).reshape(n_new, n_seq, ATTN_WIDTH))
        x1 = _mix_sample(xs, z, attn, gates, mod_all, g_post1, w_bc_b, w_ba_b, w_o_b, l, n_seq)
        xs, w1_b, w2_b = _mlp_stream_sample(x1, mod_all, g_pre2, g_post2, w_ff1, w_ff2, l, n_seq)
        conv_s.append(tail.transpose(1, 0, 2))

        z, attn, kv, gates, tail = _in_proj(xp, mod_all, g_pre1, w_in_b, conv_w, l, n_seq,
                                            attn=(sinks, bias_p))
        cast_next = (w_in, l + 1) if l + 1 < depth else None
        xp, *w_next = _post(xp, z, attn, gates, mod_all, g_post1, g_pre2, g_post2, w_bc_b, w_ba_b,
                            w_o_b, w1_b, w2_b, l, 0, n_seq, cast_next)
        w_in_b = w_next[0] if w_next else None
        conv_p.append(tail[:, V7X_SUBLANES - (CONV_WIDTH - 1):])
        k_p.append(kv[:, :, :KV_WIDTH].reshape(batch, win, N_KV_HEADS, HEAD_DIM))
        v_p.append(kv[:, :, KV_WIDTH:].reshape(batch, win, N_KV_HEADS, HEAD_DIM))

    k_s, v_s = (w.transpose(0, 1, 4, 2, 3) for w in windows)
    return (xp, xs.transpose(1, 0, 2), jnp.stack(conv_p), jnp.stack(k_p), jnp.stack(v_p),
            jnp.stack(conv_s), k_s, v_s)
```

```python
import functools
import math
from typing import Any, Callable, NamedTuple

import jax
import jax.numpy as jnp
from jax import lax
from jax.experimental import pallas as pl
from jax.experimental.pallas import tpu as pltpu

D_MODEL = 1024
N_HEADS = 16
N_KV_HEADS = 2
HEAD_DIM = 64
GROUP = N_HEADS // N_KV_HEADS
ATTN_WIDTH = N_HEADS * HEAD_DIM
KV_WIDTH = N_KV_HEADS * HEAD_DIM
CONV_DIM = D_MODEL
CONV_WIDTH = 3
WINDOW = 128
BLOCK = 128
N_BUCKETS = 32
MAX_DISTANCE = 128
D_FF = 4 * D_MODEL
N_MOD = 6
RMS_EPS = 1e-6
NEG_INF = -1e30
PROJ_COLS = 3 * CONV_DIM + ATTN_WIDTH + 2 * KV_WIDTH + 2 * D_MODEL
Q_OFF = 3 * CONV_DIM
KV_OFF = Q_OFF + ATTN_WIDTH
GATE_OFF = KV_OFF + 2 * KV_WIDTH

V7X_SUBLANES = 8
V7X_VMEM_BYTES = 64 * 1024 * 1024

PROMPT_TILE = 512
SAMPLE_POS_TILE = 4
SAMPLE_SEQ_TILE = 32
CHUNK = 256
MLP_STREAM_CHUNK = 1024
CAST_CHUNK = 256

F32 = jnp.float32
BF16 = jnp.bfloat16


def _vmem_limit(block_bytes, temp_bytes):
    return int(min(2 * block_bytes + temp_bytes, V7X_VMEM_BYTES - 4 * 1024 * 1024))


def _nbytes(shape, dtype):
    return math.prod(shape) * jnp.dtype(dtype).itemsize


def _rms(x):
    return x * lax.rsqrt(jnp.mean(x * x, axis=-1, keepdims=True) + RMS_EPS)


def _dot(a, b):
    return jnp.dot(a, b, preferred_element_type=F32)


def _flat(a):
    return a.reshape(-1, a.shape[-1])


def _ada_kernel(c_ref, w_ref, b_ref, o_ref):
    c = c_ref[...]
    s = c * (1.0 / (1.0 + jnp.exp(-c)))
    o_ref[...] = _dot(s.astype(BF16), w_ref[...].astype(BF16)) + b_ref[...]


def _ada(c_all, w_ada, b_ada):
    depth = w_ada.shape[0]
    rows = c_all.shape[0]
    blocks = (_nbytes((rows, D_MODEL), F32) * 2 + _nbytes((D_MODEL, D_MODEL), F32))
    return pl.pallas_call(
        _ada_kernel,
        grid=(depth, N_MOD),
        in_specs=[
            pl.BlockSpec((rows, D_MODEL), lambda l, j: (0, 0)),
            pl.BlockSpec((None, D_MODEL, D_MODEL), lambda l, j: (l, 0, j)),
            pl.BlockSpec((None, None, 1, D_MODEL), lambda l, j: (l, j, 0, 0)),
        ],
        out_specs=pl.BlockSpec((None, None, rows, D_MODEL), lambda l, j: (l, j, 0, 0)),
        out_shape=jax.ShapeDtypeStruct((depth, N_MOD, rows, D_MODEL), F32),
        compiler_params=pltpu.CompilerParams(
            dimension_semantics=("arbitrary", "arbitrary"),
            vmem_limit_bytes=_vmem_limit(blocks, _nbytes((D_MODEL, D_MODEL), BF16) * 2)),
        name="ada_mod",
    )(c_all, w_ada, b_ada.reshape(depth, N_MOD, 1, D_MODEL))


def _bias_kernel(tab_ref, bucket_ref, o_ref):
    bucket = bucket_ref[...]
    for h in range(N_HEADS):
        acc = jnp.zeros(bucket.shape, F32)
        for b in range(N_BUCKETS):
            acc = jnp.where(bucket == b, tab_ref[b * N_HEADS + h], acc)
        o_ref[h] = acc


def _bias_table(rel_table, bucket):
    return pl.pallas_call(
        _bias_kernel,
        in_specs=[
            pl.BlockSpec(memory_space=pltpu.SMEM),
            pl.BlockSpec(bucket.shape, lambda: (0, 0)),
        ],
        out_specs=pl.BlockSpec((N_HEADS,) + bucket.shape, lambda: (0, 0, 0)),
        out_shape=jax.ShapeDtypeStruct((N_HEADS,) + bucket.shape, F32),
        name="bias_table",
    )(rel_table.reshape(-1), bucket)


def _rel_bucket(dist):
    n = jnp.maximum(dist, 0)
    max_exact = N_BUCKETS // 2
    nf = jnp.maximum(n, 1).astype(F32)
    scaled = (jnp.log(nf / max_exact) / math.log(MAX_DISTANCE / max_exact)
              * (N_BUCKETS - max_exact))
    large = jnp.minimum(max_exact + jnp.floor(scaled).astype(jnp.int32), N_BUCKETS - 1)
    return jnp.where(n < max_exact, n, large)


class _Dense(NamedTuple):
    sample: bool
    grid: tuple
    rows: int
    act: Callable[[int], Any]
    layer: Callable[..., Any]
    mod_spec: Any
    sem: tuple


def _dense(x, layer_idx, n_seq, sample):
    lead, mid = x.shape[0], x.shape[1]
    if sample:
        grid = (lead // SAMPLE_POS_TILE,)
        act = lambda w: pl.BlockSpec((SAMPLE_POS_TILE, mid, w), lambda i: (i, 0, 0))
        layer = lambda *shape: pl.BlockSpec((None,) + shape,
                                            lambda i: (layer_idx,) + (0,) * len(shape),
                                            pipeline_mode=pl.Buffered(1))
        mod_spec = pl.BlockSpec((None, N_MOD, n_seq, D_MODEL), lambda i: (layer_idx, 0, 0, 0))
        return _Dense(True, grid, SAMPLE_POS_TILE * mid, act, layer, mod_spec, ("arbitrary",))
    assert n_seq % V7X_SUBLANES == 0 and lead <= V7X_SUBLANES
    grid = (lead, mid // PROMPT_TILE)
    act = lambda w: pl.BlockSpec((None, PROMPT_TILE, w), lambda b, t: (b, t, 0))
    layer = lambda *shape: pl.BlockSpec((None,) + shape,
                                        lambda b, t: (layer_idx,) + (0,) * len(shape),
                                        pipeline_mode=pl.Buffered(1))
    mod_spec = pl.BlockSpec((None, N_MOD, V7X_SUBLANES, D_MODEL),
                            lambda b, t: (layer_idx, 0, n_seq // V7X_SUBLANES, 0))
    return _Dense(False, grid, PROMPT_TILE, act, layer, mod_spec, ("arbitrary", "arbitrary"))


def _mod(mod_ref, j, sample):
    if sample:
        return mod_ref[j]
    return mod_ref[j, pl.ds(pl.program_id(0), 1), :]


def _sink_softmax(s, sink):
    m = jnp.maximum(jnp.max(s, axis=-1, keepdims=True), sink)
    e = jnp.exp(s - m)
    den = jnp.sum(e, axis=-1, keepdims=True) + jnp.exp(sink - m)
    return e * (1.0 / den)


def _stage_keys(kv, kd_ref, vt_ref, first_tile):
    n_blk = kv.shape[0] // BLOCK

    kd_ref[:, 0:BLOCK] = jnp.where(first_tile, jnp.zeros((), BF16), kd_ref[:, n_blk * BLOCK:])
    vt_ref[0] = jnp.where(first_tile, jnp.zeros((), BF16), vt_ref[n_blk])

    for g in range(N_KV_HEADS):
        k_g = kv[:, g * HEAD_DIM:(g + 1) * HEAD_DIM]
        kd_ref[g, BLOCK:] = jnp.concatenate([k_g, k_g], axis=1).astype(BF16)
    v_t = kv[:, KV_WIDTH:].T.astype(BF16)
    for blk in range(n_blk):
        vt_ref[blk + 1] = v_t[:, blk * BLOCK:(blk + 1) * BLOCK]


def _attention_phases(q_ref, kd_ref, vt_ref, s_ref, bias_ref, sink_ref, o_ref, first_tile):
    kj = lax.broadcasted_iota(jnp.int32, (BLOCK, BLOCK), 0)
    qi = lax.broadcasted_iota(jnp.int32, (BLOCK, BLOCK), 1)
    upper = kj > qi
    low_lanes = lax.broadcasted_iota(jnp.int32, (BLOCK, 2 * HEAD_DIM), 1) < HEAD_DIM

    def score_phase(c):
        q = q_ref[c * BLOCK:(c + 1) * BLOCK, :]
        for pair in range(N_HEADS // 2):
            g = (2 * pair) // GROUP
            q_pair = q[:, pair * 2 * HEAD_DIM:(pair + 1) * 2 * HEAD_DIM]
            zero = jnp.zeros_like(q_pair)
            q_both = jnp.concatenate([jnp.where(low_lanes, q_pair, zero),
                                      jnp.where(low_lanes, zero, q_pair)], axis=0)
            s_both = lax.dot_general(kd_ref[g, c * BLOCK:(c + 2) * BLOCK, :], q_both,
                                     (((1,), (1,)), ((), ())), preferred_element_type=F32)
            for par in range(2):
                h = 2 * pair + par
                s_h = s_both[:, par * BLOCK:(par + 1) * BLOCK]
                s = jnp.where(upper, s_h[:BLOCK], s_h[BLOCK:]) + bias_ref[h]
                if c == 0:
                    s = jnp.where(upper & first_tile, NEG_INF, s)
                s_ref[c, h] = s

    def value_phase(c, pair):
        g = (2 * pair) // GROUP
        v_cat = jnp.concatenate([vt_ref[c, g * HEAD_DIM:(g + 1) * HEAD_DIM],
                                 vt_ref[c + 1, g * HEAD_DIM:(g + 1) * HEAD_DIM]], axis=1)
        weights, scales = [], []
        for h in (2 * pair, 2 * pair + 1):
            s = s_ref[c, h]
            sink = sink_ref[h]
            m = jnp.maximum(jnp.max(s, axis=0, keepdims=True), sink)
            e = jnp.exp(s - m)
            den = jnp.sum(e, axis=0, keepdims=True) + jnp.exp(sink - m)
            weights.append(jnp.concatenate([jnp.where(upper, e, 0.0), jnp.where(upper, 0.0, e)],
                                           axis=0).astype(BF16))
            scales.append(1.0 / den)
        o_t = _dot(v_cat, jnp.concatenate(weights, axis=1))
        o_t = jnp.concatenate([o_t[:, :BLOCK] * scales[0], o_t[:, BLOCK:] * scales[1]], axis=0)
        o_ref[c * BLOCK:(c + 1) * BLOCK, pair * 2 * HEAD_DIM:(pair + 1) * 2 * HEAD_DIM] = (
            o_t.T.astype(BF16))

    return score_phase, value_phase


def _in_proj_kernel(*refs, sample):
    if sample:
        (x_ref, mod_ref, g_ref, w_ref, cw_ref, pre_ref,
         z_ref, q_ref, kv_ref, gate_ref, tail_ref, carry_ref) = refs
    else:
        (x_ref, mod_ref, g_ref, w_ref, cw_ref, sink_ref, bias_ref,
         z_ref, a_ref, kv_ref, gate_ref, tail_ref,
         carry_ref, q_ref, kd_ref, vt_ref, s_ref) = refs
    step = pl.program_id(0) if sample else pl.program_id(1)

    first_tile = step == 0
    x = x_ref[...]
    h = _rms(x) * (g_ref[...] * (1.0 + _mod(mod_ref, 1, sample))) + _mod(mod_ref, 0, sample)
    hb = _flat(h).astype(BF16)
    rows = hb.shape[0]

    def store(ref, lo, val):
        ref[..., lo:lo + val.shape[-1]] = val.reshape(ref.shape[:-1] + (val.shape[-1],))

    def conv_chunk(lo):
        b_g = _dot(hb, w_ref[:, lo:lo + CHUNK])
        c_g = _dot(hb, w_ref[:, CONV_DIM + lo:CONV_DIM + lo + CHUNK])
        x_c = _dot(hb, w_ref[:, 2 * CONV_DIM + lo:2 * CONV_DIM + lo + CHUNK])
        u = c_g * x_c
        if sample:
            pos = x_ref.shape[0]
            u3 = u.reshape(pos, -1, CHUNK)
            prev = jnp.where(first_tile, pre_ref[:, :, lo:lo + CHUNK],
                             carry_ref[:, :, lo:lo + CHUNK])
            ext = jnp.concatenate([prev, u3], axis=0)
            u2 = _flat(ext[0:pos])
            u1 = _flat(ext[1:pos + 1])
            new_tail = u3[pos - (CONV_WIDTH - 1):]
            carry_ref[:, :, lo:lo + CHUNK] = new_tail
            tail_ref[:, :, lo:lo + CHUNK] = new_tail
        else:
            prev = jnp.where(first_tile, 0.0, carry_ref[:, lo:lo + CHUNK])
            p1 = prev[V7X_SUBLANES - 1:V7X_SUBLANES]
            p2 = prev[V7X_SUBLANES - 2:V7X_SUBLANES - 1]
            r = lax.broadcasted_iota(jnp.int32, u.shape, 0)
            u1 = jnp.where(r == 0, p1, pltpu.roll(u, 1, 0))
            u2 = jnp.where(r == 0, p2, jnp.where(r == 1, p1, pltpu.roll(u, 2, 0)))
            new_tail = u[rows - V7X_SUBLANES:]
            carry_ref[:, lo:lo + CHUNK] = new_tail
            tail_ref[:, lo:lo + CHUNK] = new_tail
        conv = (cw_ref[0:1, lo:lo + CHUNK] * u2 + cw_ref[1:2, lo:lo + CHUNK] * u1
                + cw_ref[2:3, lo:lo + CHUNK] * u)
        store(z_ref, lo, (b_g * conv).astype(BF16))

    def q_chunk(lo):
        q = _dot(hb, w_ref[:, Q_OFF + lo:Q_OFF + lo + CHUNK])
        store(q_ref, lo, (q * (HEAD_DIM ** -0.5)).astype(BF16))

    def gate_chunk(lo):
        g = _dot(hb, w_ref[:, GATE_OFF + lo:GATE_OFF + lo + CHUNK])
        store(gate_ref, lo, (1.0 / (1.0 + jnp.exp(-g))).astype(BF16))

    kv = _dot(hb, w_ref[:, KV_OFF:KV_OFF + 2 * KV_WIDTH])
    if sample:
        store(kv_ref, 0, kv)
    else:
        kv_ref[...] = kv[rows - kv_ref.shape[0]:]
    for lo in range(0, ATTN_WIDTH, CHUNK):
        q_chunk(lo)
    dense = ([functools.partial(conv_chunk, lo) for lo in range(0, CONV_DIM, CHUNK)]
             + [functools.partial(gate_chunk, lo) for lo in range(0, 2 * D_MODEL, CHUNK)])
    if sample:
        for work in dense:
            work()
        return

    _stage_keys(kv, kd_ref, vt_ref, first_tile)
    score_phase, value_phase = _attention_phases(q_ref, kd_ref, vt_ref, s_ref, bias_ref,
                                                 sink_ref, a_ref, first_tile)
    n_blk = rows // BLOCK
    attention = [functools.partial(score_phase, 0)]
    for c in range(n_blk):
        if c + 1 < n_blk:
            attention.append(functools.partial(score_phase, c + 1))
        attention += [functools.partial(value_phase, c, pair) for pair in range(N_HEADS // 2)]
    done = 0
    for i, work in enumerate(attention):
        work()
        while done < len(dense) and done * len(attention) < (i + 1) * len(dense):
            dense[done]()
            done += 1


def _in_proj(x, mod_all, g_pre1, w_in, conv_w, layer_idx, n_seq, pre=None, attn=None):
    cfg = _dense(x, layer_idx, n_seq, pre is not None)
    lead, mid = x.shape[0], x.shape[1]
    widths = (CONV_DIM, ATTN_WIDTH, 2 * KV_WIDTH, 2 * D_MODEL)
    dtypes = (BF16, BF16, F32, BF16)
    blocks = (_nbytes((cfg.rows, D_MODEL), F32) + _nbytes(w_in.shape[1:], BF16)
              + sum(_nbytes((cfg.rows, w), dt) for w, dt in zip(widths, dtypes)))
    temps = _nbytes((cfg.rows, D_MODEL), F32) * 2 + _nbytes((cfg.rows, CHUNK), F32) * 10
    if cfg.sample:
        tail_shape = (CONV_WIDTH - 1, mid, CONV_DIM)
        tail_spec = pl.BlockSpec(tail_shape, lambda i: (0, 0, 0))
        extra_in = [pre]
        extra_specs = [cfg.layer(*tail_shape)]
        scratch = [pltpu.VMEM(tail_shape, F32)]
    else:
        sinks, bias = attn
        per_tile = cfg.rows // BLOCK
        tail_shape = (lead, V7X_SUBLANES, CONV_DIM)
        tail_spec = pl.BlockSpec((None, V7X_SUBLANES, CONV_DIM), lambda b, t: (b, 0, 0))
        extra_in = [sinks[layer_idx], bias]
        extra_specs = [pl.BlockSpec(memory_space=pltpu.SMEM),
                       pl.BlockSpec(bias.shape, lambda b, t: (0, 0, 0),
                                    pipeline_mode=pl.Buffered(1))]
        scratch = [pltpu.VMEM((V7X_SUBLANES, CONV_DIM), F32),
                   pltpu.VMEM((cfg.rows, ATTN_WIDTH), BF16),
                   pltpu.VMEM((N_KV_HEADS, cfg.rows + BLOCK, 2 * HEAD_DIM), BF16),
                   pltpu.VMEM((per_tile + 1, KV_WIDTH, BLOCK), BF16),
                   pltpu.VMEM((per_tile, N_HEADS, BLOCK, BLOCK), F32)]
        temps += (_nbytes(bias.shape, F32) + _nbytes((cfg.rows, ATTN_WIDTH), BF16)
                  + _nbytes((per_tile, N_HEADS, BLOCK, BLOCK), F32)
                  + _nbytes((BLOCK, 2 * BLOCK), F32) * 16 + _nbytes((BLOCK, ATTN_WIDTH), F32) * 4)

    out_shape = [jax.ShapeDtypeStruct(x.shape[:-1] + (w,), dt) for w, dt in zip(widths, dtypes)]
    out_shape.append(jax.ShapeDtypeStruct(tail_shape, F32))
    out_specs = [cfg.act(w) for w in widths] + [tail_spec]
    if not cfg.sample:
        win = min(WINDOW, mid)
        out_shape[2] = jax.ShapeDtypeStruct((lead, win, 2 * KV_WIDTH), F32)
        out_specs[2] = pl.BlockSpec((None, win, 2 * KV_WIDTH), lambda b, t: (b, 0, 0))
    return pl.pallas_call(
        functools.partial(_in_proj_kernel, sample=cfg.sample),
        grid=cfg.grid,
        in_specs=[cfg.act(D_MODEL), cfg.mod_spec, cfg.layer(1, D_MODEL),
                  _dense(x, 0, n_seq, cfg.sample).layer(*w_in.shape[1:]),
                  cfg.layer(*conv_w.shape[1:])] + extra_specs,
        out_specs=out_specs,
        out_shape=out_shape,
        scratch_shapes=scratch,
        compiler_params=pltpu.CompilerParams(
            dimension_semantics=cfg.sem, vmem_limit_bytes=_vmem_limit(blocks, temps)),
        name="in_proj_sample" if cfg.sample else "in_proj_attn_prompt",
    )(x, mod_all, g_pre1, w_in, conv_w, *extra_in)


def _sample_attn_kernel(q_ref, kc_ref, vc_ref, kn_ref, vn_ref, bias_ref, sink_ref, *rest,
                        layer_idx):
    o_ref, ko_ref, vo_ref = rest[-3:]
    if ko_ref.ndim > kc_ref.ndim:
        for other in range(ko_ref.shape[0]):
            if other != layer_idx:
                ko_ref[other] = jnp.zeros(ko_ref.shape[1:], F32)
                vo_ref[other] = jnp.zeros(vo_ref.shape[1:], F32)
        ko_ref, vo_ref = ko_ref.at[layer_idx], vo_ref.at[layer_idx]
    n_seq = q_ref.shape[0]
    buf = kc_ref.shape[-1]
    new = kn_ref.shape[1]
    rows = new * GROUP
    pos = lax.broadcasted_iota(jnp.int32, (rows, 2 * buf), 0) // GROUP
    kj = lax.broadcasted_iota(jnp.int32, (rows, 2 * buf), 1)
    dist = buf + pos - kj
    mask = (dist >= 0) & (dist < WINDOW) & (kj < buf + new)
    lane = lax.broadcasted_iota(jnp.int32, (n_seq, HEAD_DIM, buf), 2)
    pad = jnp.zeros((n_seq, HEAD_DIM, buf - new), F32)

    def extended(cache_t, fresh, out_ref, g):
        fresh_t = jnp.concatenate([jnp.swapaxes(fresh, 1, 2), pad], axis=2)
        out_ref[:, g] = pltpu.roll(jnp.where(lane < new, fresh_t, cache_t), buf - new, 2)
        return jnp.concatenate([cache_t, fresh_t], axis=2).astype(BF16)

    for g in range(N_KV_HEADS):
        lanes = slice(g * HEAD_DIM, (g + 1) * HEAD_DIM)
        k_ext = extended(kc_ref[:, g], kn_ref[:, :, lanes], ko_ref, g)
        v_ext = extended(vc_ref[:, g], vn_ref[:, :, lanes], vo_ref, g)
        s = jnp.einsum('nqd,ndk->nqk', q_ref[:, g], k_ext, preferred_element_type=F32)
        p = _sink_softmax(jnp.where(mask[None], s + bias_ref[g][None], NEG_INF),
                          sink_ref[g][None])
        o = jnp.einsum('nqk,ndk->nqd', p.astype(BF16), v_ext, preferred_element_type=F32)
        o_ref[:, g] = o.astype(BF16)


def _sample_attn(q, k_cache, v_cache, k_new, v_new, bias, sink_rows, layer_idx, windows):
    _, n, _, _, buf = k_cache.shape
    new = k_new.shape[1]
    rows = new * GROUP
    nt = SAMPLE_SEQ_TILE
    seq3 = lambda a, b: pl.BlockSpec((nt, a, b), lambda i: (i, 0, 0))
    cache = pl.BlockSpec((None, nt, N_KV_HEADS, HEAD_DIM, buf),
                         lambda i: (layer_idx, i, 0, 0, 0))
    q_spec = pl.BlockSpec((nt, N_KV_HEADS, rows, HEAD_DIM), lambda i: (i, 0, 0, 0))
    in_specs = [q_spec, cache, cache, seq3(new, KV_WIDTH), seq3(new, KV_WIDTH),
                pl.BlockSpec(bias.shape, lambda i: (0, 0, 0)),
                pl.BlockSpec(sink_rows.shape, lambda i: (0, 0, 0))]
    args = [q, k_cache, v_cache, k_new, v_new, bias, sink_rows]
    depth = k_cache.shape[0]
    if windows is None:
        aliases = {}
        window_out = pl.BlockSpec((depth, nt, N_KV_HEADS, HEAD_DIM, buf),
                                  lambda i: (0, i, 0, 0, 0))
    else:
        aliases = {len(args): 1, len(args) + 1: 2}
        in_specs += [pl.BlockSpec(memory_space=pl.ANY)] * 2
        args += list(windows)
        window_out = cache
    blocks = (_nbytes((nt, N_KV_HEADS, HEAD_DIM, buf), F32) * (2 + 2 * depth)
              + _nbytes((nt, new, KV_WIDTH), F32) * 2
              + _nbytes((nt, N_KV_HEADS, rows, 2 * HEAD_DIM), BF16) * 2 + _nbytes(bias.shape, F32))
    temps = _nbytes((rows, 2 * buf), F32) * 8 * nt
    return pl.pallas_call(
        functools.partial(_sample_attn_kernel, layer_idx=layer_idx),
        grid=(n // nt,),
        in_specs=in_specs,
        out_specs=[q_spec, window_out, window_out],
        out_shape=[jax.ShapeDtypeStruct(q.shape, BF16),
                   jax.ShapeDtypeStruct(k_cache.shape, F32),
                   jax.ShapeDtypeStruct(v_cache.shape, F32)],
        input_output_aliases=aliases,
        compiler_params=pltpu.CompilerParams(
            dimension_semantics=("arbitrary",), vmem_limit_bytes=_vmem_limit(blocks, temps)),
        name="attn_sample",
    )(*args)


def _merge_residual(x_ref, z_ref, a_ref, gate_ref, mod_ref, gpost1_ref, wbc_ref, wba_ref, wo_ref,
                    sample):
    y_conv = _dot(_flat(z_ref[...]), wbc_ref[...])
    y_attn = _dot(_flat(a_ref[...]), wba_ref[...])
    gates = _flat(gate_ref[...])
    merged = gates[:, :D_MODEL] * y_conv + gates[:, D_MODEL:] * y_attn
    mixed = _dot(merged.astype(BF16), wo_ref[...])
    scale = _mod(mod_ref, 2, sample) * gpost1_ref[...]
    return x_ref[...] + scale * _rms(mixed).reshape(x_ref.shape)


def _mlp_input(x1, mod_ref, gpre2_ref, sample):
    scale = gpre2_ref[...] * (1.0 + _mod(mod_ref, 4, sample))
    return _flat(_rms(x1) * scale + _mod(mod_ref, 3, sample)).astype(BF16)


def _mlp_residual(x1, ff, mod_ref, gpost2_ref, sample):
    scale = _mod(mod_ref, 5, sample) * gpost2_ref[...]
    return x1 + scale * _rms(ff).reshape(x1.shape)


def _post_kernel(x_ref, z_ref, a_ref, gate_ref, mod_ref, gpost1_ref, gpre2_ref, gpost2_ref,
                 wbc_ref, wba_ref, wo_ref, w1_ref, w2_ref, *rest, sample):
    if len(rest) == 4:
        cast_in_ref, o_ref, cast_out_ref, hid_ref = rest
        cast_out_ref[...] = cast_in_ref[...].astype(BF16)
    else:
        o_ref, hid_ref = rest
    x1 = _merge_residual(x_ref, z_ref, a_ref, gate_ref, mod_ref, gpost1_ref,
                         wbc_ref, wba_ref, wo_ref, sample)
    hb = _mlp_input(x1, mod_ref, gpre2_ref, sample)
    for lo in range(0, D_FF, 2 * CHUNK):
        act = jnp.maximum(_dot(hb, w1_ref[:, lo:lo + 2 * CHUNK]), 0.0)
        hid_ref[:, lo:lo + 2 * CHUNK] = (act * act).astype(BF16)
    ff = _dot(hid_ref[...], w2_ref[...])
    o_ref[...] = _mlp_residual(x1, ff, mod_ref, gpost2_ref, sample)


def _post(x, z, a, gates, mod_all, g_post1, g_pre2, g_post2, w_bc, w_ba, w_o, w1, w2,
          layer_idx, mlp_layer_idx, n_seq, cast_next=None):
    cfg = _dense(x, layer_idx, n_seq, False)
    mlp = _dense(x, mlp_layer_idx, n_seq, False)
    sq = (D_MODEL, D_MODEL)
    blocks = (_nbytes((cfg.rows, D_MODEL), F32) * 2 + _nbytes((cfg.rows, D_MODEL), BF16) * 2
              + _nbytes((cfg.rows, 2 * D_MODEL), gates.dtype))
    temps = (_nbytes(sq, BF16) * 3 + _nbytes(w1.shape[1:], BF16) * 2
             + _nbytes((cfg.rows, D_FF), BF16) + _nbytes((cfg.rows, 2 * CHUNK), F32) * 2
             + _nbytes((cfg.rows, D_MODEL), F32) * 5)
    vec = cfg.layer(1, D_MODEL)
    in_specs = [cfg.act(D_MODEL), cfg.act(CONV_DIM), cfg.act(ATTN_WIDTH),
                cfg.act(2 * D_MODEL), cfg.mod_spec, vec, vec, vec,
                cfg.layer(*sq), cfg.layer(*sq), cfg.layer(*sq),
                mlp.layer(*w1.shape[1:]), mlp.layer(*w2.shape[1:])]
    args = [x, z, a, gates, mod_all, g_post1, g_pre2, g_post2, w_bc, w_ba, w_o, w1, w2]
    out_specs = [cfg.act(D_MODEL)]
    out_shape = [jax.ShapeDtypeStruct(x.shape, F32)]
    if cast_next is not None:
        w_next, src_layer = cast_next
        _, k_dim, n_dim = w_next.shape
        last = n_dim // CAST_CHUNK - 1
        assert n_dim % CAST_CHUNK == 0 and last < cfg.grid[0] * cfg.grid[1]
        tiles = cfg.grid[1]
        in_specs.append(pl.BlockSpec(
            (None, k_dim, CAST_CHUNK),
            lambda b, t: (src_layer, 0, jnp.minimum(b * tiles + t, last))))
        out_specs.append(pl.BlockSpec(
            (None, k_dim, CAST_CHUNK), lambda b, t: (0, 0, jnp.minimum(b * tiles + t, last))))
        out_shape.append(jax.ShapeDtypeStruct((1, k_dim, n_dim), BF16))
        args.append(w_next)
        blocks += _nbytes((k_dim, CAST_CHUNK), F32) + _nbytes((k_dim, CAST_CHUNK), BF16)
    return pl.pallas_call(
        functools.partial(_post_kernel, sample=False),
        grid=cfg.grid,
        in_specs=in_specs,
        out_specs=out_specs,
        out_shape=out_shape,
        scratch_shapes=[pltpu.VMEM((cfg.rows, D_FF), BF16)],
        compiler_params=pltpu.CompilerParams(
            dimension_semantics=cfg.sem, vmem_limit_bytes=_vmem_limit(blocks, temps)),
        name="post_prompt",
    )(*args)


def _mix_kernel(x_ref, z_ref, a_ref, gate_ref, mod_ref, gpost1_ref, wbc_ref, wba_ref, wo_ref,
                o_ref):
    o_ref[...] = _merge_residual(x_ref, z_ref, a_ref, gate_ref, mod_ref, gpost1_ref,
                                 wbc_ref, wba_ref, wo_ref, True)


def _mix_sample(x, z, a, gates, mod_all, g_post1, w_bc, w_ba, w_o, layer_idx, n_seq):
    cfg = _dense(x, layer_idx, n_seq, True)
    sq = (D_MODEL, D_MODEL)
    blocks = (_nbytes((cfg.rows, D_MODEL), F32) * 2 + _nbytes((cfg.rows, D_MODEL), BF16) * 2
              + _nbytes((cfg.rows, 2 * D_MODEL), gates.dtype))
    temps = _nbytes(sq, BF16) * 3 + _nbytes((cfg.rows, D_MODEL), F32) * 6
    return pl.pallas_call(
        _mix_kernel,
        grid=cfg.grid,
        in_specs=[cfg.act(D_MODEL), cfg.act(CONV_DIM), cfg.act(ATTN_WIDTH),
                  cfg.act(2 * D_MODEL), cfg.mod_spec, cfg.layer(1, D_MODEL),
                  cfg.layer(*sq), cfg.layer(*sq), cfg.layer(*sq)],
        out_specs=cfg.act(D_MODEL),
        out_shape=jax.ShapeDtypeStruct(x.shape, F32),
        compiler_params=pltpu.CompilerParams(
            dimension_semantics=cfg.sem, vmem_limit_bytes=_vmem_limit(blocks, temps)),
        name="mix_sample",
    )(x, z, a, gates, mod_all, g_post1, w_bc, w_ba, w_o)


def _mlp_stream_kernel(x_ref, mod_ref, gpre2_ref, gpost2_ref, w1_ref, w2_ref,
                       o_ref, w1b_ref, w2b_ref, ff_ref):
    hb = _mlp_input(x_ref[...], mod_ref, gpre2_ref, True)
    w1 = w1_ref[...].astype(BF16)
    w2 = w2_ref[...].astype(BF16)
    w1b_ref[...] = w1
    w2b_ref[...] = w2
    act = jnp.maximum(_dot(hb, w1), 0.0)
    ff = jnp.where(pl.program_id(0) == 0, 0.0, ff_ref[...]) + _dot((act * act).astype(BF16), w2)
    ff_ref[...] = ff
    o_ref[...] = _mlp_residual(x_ref[...], ff, mod_ref, gpost2_ref, True)


def _mlp_stream_sample(x, mod_all, g_pre2, g_post2, w1, w2, layer_idx, n_seq):
    rows = x.shape[0] * x.shape[1]
    const = lambda shape, idx: pl.BlockSpec(shape, lambda c: idx, pipeline_mode=pl.Buffered(1))
    vec = const((None, 1, D_MODEL), (layer_idx, 0, 0))
    blocks = (_nbytes((D_MODEL, MLP_STREAM_CHUNK), F32) * 2
              + _nbytes((D_MODEL, MLP_STREAM_CHUNK), BF16) * 2 + _nbytes(x.shape, F32))
    temps = (_nbytes(x.shape, F32) * 4 + _nbytes((N_MOD, n_seq, D_MODEL), F32)
             + _nbytes((rows, D_MODEL), BF16) + _nbytes((rows, MLP_STREAM_CHUNK), F32) * 3)
    return pl.pallas_call(
        _mlp_stream_kernel,
        grid=(D_FF // MLP_STREAM_CHUNK,),
        in_specs=[const(x.shape, (0, 0, 0)),
                  const((None, N_MOD, n_seq, D_MODEL), (layer_idx, 0, 0, 0)), vec, vec,
                  pl.BlockSpec((None, D_MODEL, MLP_STREAM_CHUNK), lambda c: (layer_idx, 0, c)),
                  pl.BlockSpec((None, MLP_STREAM_CHUNK, D_MODEL), lambda c: (layer_idx, c, 0))],
        out_specs=[pl.BlockSpec(x.shape, lambda c: (0, 0, 0)),
                   pl.BlockSpec((None, D_MODEL, MLP_STREAM_CHUNK), lambda c: (0, 0, c)),
                   pl.BlockSpec((None, MLP_STREAM_CHUNK, D_MODEL), lambda c: (0, c, 0))],
        out_shape=[jax.ShapeDtypeStruct(x.shape, F32),
                   jax.ShapeDtypeStruct((1, D_MODEL, D_FF), BF16),
                   jax.ShapeDtypeStruct((1, D_FF, D_MODEL), BF16)],
        scratch_shapes=[pltpu.VMEM((rows, D_MODEL), F32)],
        compiler_params=pltpu.CompilerParams(
            dimension_semantics=("arbitrary",), vmem_limit_bytes=_vmem_limit(blocks, temps)),
        name="mlp_stream_sample",
    )(x, mod_all, g_pre2, g_post2, w1, w2)


def kernel(x_prompt, x_sample, c_prompt, c_sample, state_conv, cache_k, cache_v, w_ada, b_ada,
           g_pre1, w_in, conv_w, w_br_conv, w_br_attn, w_o, sinks, g_post1, g_pre2, w_ff1, w_ff2,
           g_post2, rel_table):
    depth = w_ada.shape[0]
    batch, seq, _ = x_prompt.shape
    n_seq, n_new, _ = x_sample.shape
    buf = cache_k.shape[2]

    pad = (-(n_seq + batch)) % V7X_SUBLANES
    c_all = jnp.concatenate([c_sample, c_prompt, jnp.zeros((pad, D_MODEL), F32)], axis=0)
    mod_all = _ada(c_all, w_ada, b_ada)

    dist_p = (jnp.arange(BLOCK)[:, None] + BLOCK) - jnp.arange(2 * BLOCK)[None, :]
    bucket_p = _rel_bucket(dist_p)
    upper = jnp.arange(BLOCK)[None, :] > jnp.arange(BLOCK)[:, None]
    bucket_merged = jnp.where(upper, bucket_p[:, :BLOCK], bucket_p[:, BLOCK:])
    bias_p = _bias_table(rel_table, bucket_merged.T)
    dist_s = (buf + jnp.arange(n_new))[:, None] - jnp.arange(buf + n_new)[None, :]
    bias_s = _bias_table(rel_table, _rel_bucket(dist_s))
    bias_s = (bias_s.reshape(N_KV_HEADS, GROUP, n_new, buf + n_new)
              .transpose(0, 2, 1, 3).reshape(N_KV_HEADS, n_new * GROUP, buf + n_new))
    bias_s = jnp.pad(bias_s, ((0, 0), (0, 0), (0, buf - n_new)))

    to_bf16 = lambda w: w.astype(BF16)
    w_bc_b, w_ba_b, w_o_b = map(to_bf16, (w_br_conv, w_br_attn, w_o))
    w_in_b = to_bf16(w_in[0:1])
    vec = lambda g: g.reshape(depth, 1, D_MODEL)
    g_pre1, g_post1, g_pre2, g_post2 = map(vec, (g_pre1, g_post1, g_pre2, g_post2))
    pre_s = state_conv.transpose(0, 2, 1, 3)
    cache_k = cache_k.transpose(0, 1, 3, 4, 2)
    cache_v = cache_v.transpose(0, 1, 3, 4, 2)

    xp = x_prompt
    xs = x_sample.transpose(1, 0, 2)
    conv_p, k_p, v_p, conv_s = [], [], [], []
    windows = None
    win = min(WINDOW, seq)
    for l in range(depth):
        z, q, kv, gates, tail = _in_proj(xs, mod_all, g_pre1, w_in_b, conv_w, l, n_seq, pre_s)
        q_s = (q.reshape(n_new, n_seq, N_KV_HEADS, GROUP, HEAD_DIM)
               .transpose(1, 2, 0, 3, 4).reshape(n_seq, N_KV_HEADS, n_new * GROUP, HEAD_DIM))
        kv_s = kv.transpose(1, 0, 2)
        sink_rows = jnp.tile(sinks[l].reshape(N_KV_HEADS, 1, GROUP),
                             (1, n_new, 1)).reshape(N_KV_HEADS, n_new * GROUP, 1)
        o, *windows = _sample_attn(q_s, cache_k, cache_v, kv_s[:, :, :KV_WIDTH],
                                   kv_s[:, :, KV_WIDTH:], bias_s, sink_rows, l, windows)
        attn = (o.reshape(n_seq, N_KV_HEADS, n_new, GROUP, HEAD_DIM)
                .transpose(2, 0, 1, 3, 4).reshape(n_new, n_seq, ATTN_WIDTH))
        x1 = _mix_sample(xs, z, attn, gates, mod_all, g_post1, w_bc_b, w_ba_b, w_o_b, l, n_seq)
        xs, w1_b, w2_b = _mlp_stream_sample(x1, mod_all, g_pre2, g_post2, w_ff1, w_ff2, l, n_seq)
        conv_s.append(tail.transpose(1, 0, 2))

        z, attn, kv, gates, tail = _in_proj(xp, mod_all, g_pre1, w_in_b, conv_w, l, n_seq,
                                            attn=(sinks, bias_p))
        cast_next = (w_in, l + 1) if l + 1 < depth else None
        xp, *w_next = _post(xp, z, attn, gates, mod_all, g_post1, g_pre2, g_post2, w_bc_b, w_ba_b,
                            w_o_b, w1_b, w2_b, l, 0, n_seq, cast_next)
        w_in_b = w_next[0] if w_next else None
        conv_p.append(tail[:, V7X_SUBLANES - (CONV_WIDTH - 1):])
        k_p.append(kv[:, :, :KV_WIDTH].reshape(batch, win, N_KV_HEADS, HEAD_DIM))
        v_p.append(kv[:, :, KV_WIDTH:].reshape(batch, win, N_KV_HEADS, HEAD_DIM))

    k_s, v_s = (w.transpose(0, 1, 4, 2, 3) for w in windows)
    return (xp, xs.transpose(1, 0, 2), jnp.stack(conv_p), jnp.stack(k_p), jnp.stack(v_p),
            jnp.stack(conv_s), k_s, v_s)
```

```python
import functools
import math
from typing import Any, Callable, NamedTuple

import jax
import jax.numpy as jnp
from jax import lax
from jax.experimental import pallas as pl
from jax.experimental.pallas import tpu as pltpu

D_MODEL = 1024
N_HEADS = 16
N_KV_HEADS = 2
HEAD_DIM = 64
GROUP = N_HEADS // N_KV_HEADS
ATTN_WIDTH = N_HEADS * HEAD_DIM
KV_WIDTH = N_KV_HEADS * HEAD_DIM
CONV_DIM = D_MODEL
CONV_WIDTH = 3
WINDOW = 128
BLOCK = 128
N_BUCKETS = 32
MAX_DISTANCE = 128
D_FF = 4 * D_MODEL
N_MOD = 6
RMS_EPS = 1e-6
NEG_INF = -1e30
PROJ_COLS = 3 * CONV_DIM + ATTN_WIDTH + 2 * KV_WIDTH + 2 * D_MODEL
Q_OFF = 3 * CONV_DIM
KV_OFF = Q_OFF + ATTN_WIDTH
GATE_OFF = KV_OFF + 2 * KV_WIDTH

V7X_SUBLANES = 8
V7X_VMEM_BYTES = 64 * 1024 * 1024

PROMPT_TILE = 512
SAMPLE_POS_TILE = 4
SAMPLE_SEQ_TILE = 32
CHUNK = 256
MLP_STREAM_CHUNK = 1024
CAST_CHUNK = 256

F32 = jnp.float32
BF16 = jnp.bfloat16


def _vmem_limit(block_bytes, temp_bytes):
    return int(min(2 * block_bytes + temp_bytes, V7X_VMEM_BYTES - 4 * 1024 * 1024))


def _nbytes(shape, dtype):
    return math.prod(shape) * jnp.dtype(dtype).itemsize


def _rms(x):
    return x * lax.rsqrt(jnp.mean(x * x, axis=-1, keepdims=True) + RMS_EPS)


def _dot(a, b):
    return jnp.dot(a, b, preferred_element_type=F32)


def _flat(a):
    return a.reshape(-1, a.shape[-1])


def _ada_kernel(c_ref, w_ref, b_ref, o_ref):
    c = c_ref[...]
    s = c * (1.0 / (1.0 + jnp.exp(-c)))
    o_ref[...] = _dot(s.astype(BF16), w_ref[...].astype(BF16)) + b_ref[...]


def _ada(c_all, w_ada, b_ada):
    depth = w_ada.shape[0]
    rows = c_all.shape[0]
    blocks = (_nbytes((rows, D_MODEL), F32) * 2 + _nbytes((D_MODEL, D_MODEL), F32))
    return pl.pallas_call(
        _ada_kernel,
        grid=(depth, N_MOD),
        in_specs=[
            pl.BlockSpec((rows, D_MODEL), lambda l, j: (0, 0)),
            pl.BlockSpec((None, D_MODEL, D_MODEL), lambda l, j: (l, 0, j)),
            pl.BlockSpec((None, None, 1, D_MODEL), lambda l, j: (l, j, 0, 0)),
        ],
        out_specs=pl.BlockSpec((None, None, rows, D_MODEL), lambda l, j: (l, j, 0, 0)),
        out_shape=jax.ShapeDtypeStruct((depth, N_MOD, rows, D_MODEL), F32),
        compiler_params=pltpu.CompilerParams(
            dimension_semantics=("arbitrary", "arbitrary"),
            vmem_limit_bytes=_vmem_limit(blocks, _nbytes((D_MODEL, D_MODEL), BF16) * 2)),
        name="ada_mod",
    )(c_all, w_ada, b_ada.reshape(depth, N_MOD, 1, D_MODEL))


def _bias_kernel(tab_ref, bucket_ref, o_ref):
    bucket = bucket_ref[...]
    for h in range(N_HEADS):
        acc = jnp.zeros(bucket.shape, F32)
        for b in range(N_BUCKETS):
            acc = jnp.where(bucket == b, tab_ref[b * N_HEADS + h], acc)
        o_ref[h] = acc


def _bias_table(rel_table, bucket):
    return pl.pallas_call(
        _bias_kernel,
        in_specs=[
            pl.BlockSpec(memory_space=pltpu.SMEM),
            pl.BlockSpec(bucket.shape, lambda: (0, 0)),
        ],
        out_specs=pl.BlockSpec((N_HEADS,) + bucket.shape, lambda: (0, 0, 0)),
        out_shape=jax.ShapeDtypeStruct((N_HEADS,) + bucket.shape, F32),
        name="bias_table",
    )(rel_table.reshape(-1), bucket)


def _rel_bucket(dist):
    n = jnp.maximum(dist, 0)
    max_exact = N_BUCKETS // 2
    nf = jnp.maximum(n, 1).astype(F32)
    scaled = (jnp.log(nf / max_exact) / math.log(MAX_DISTANCE / max_exact)
              * (N_BUCKETS - max_exact))
    large = jnp.minimum(max_exact + jnp.floor(scaled).astype(jnp.int32), N_BUCKETS - 1)
    return jnp.where(n < max_exact, n, large)


class _Dense(NamedTuple):
    sample: bool
    grid: tuple
    rows: int
    act: Callable[[int], Any]
    layer: Callable[..., Any]
    mod_spec: Any
    sem: tuple


def _dense(x, layer_idx, n_seq, sample):
    lead, mid = x.shape[0], x.shape[1]
    if sample:
        grid = (lead // SAMPLE_POS_TILE,)
        act = lambda w: pl.BlockSpec((SAMPLE_POS_TILE, mid, w), lambda i: (i, 0, 0))
        layer = lambda *shape: pl.BlockSpec((None,) + shape,
                                            lambda i: (layer_idx,) + (0,) * len(shape),
                                            pipeline_mode=pl.Buffered(1))
        mod_spec = pl.BlockSpec((None, N_MOD, n_seq, D_MODEL), lambda i: (layer_idx, 0, 0, 0))
        return _Dense(True, grid, SAMPLE_POS_TILE * mid, act, layer, mod_spec, ("arbitrary",))
    assert n_seq % V7X_SUBLANES == 0 and lead <= V7X_SUBLANES
    grid = (lead, mid // PROMPT_TILE)
    act = lambda w: pl.BlockSpec((None, PROMPT_TILE, w), lambda b, t: (b, t, 0))
    layer = lambda *shape: pl.BlockSpec((None,) + shape,
                                        lambda b, t: (layer_idx,) + (0,) * len(shape),
                                        pipeline_mode=pl.Buffered(1))
    mod_spec = pl.BlockSpec((None, N_MOD, V7X_SUBLANES, D_MODEL),
                            lambda b, t: (layer_idx, 0, n_seq // V7X_SUBLANES, 0))
    return _Dense(False, grid, PROMPT_TILE, act, layer, mod_spec, ("arbitrary", "arbitrary"))


def _mod(mod_ref, j, sample, row=None):
    if sample:
        return mod_ref[j]
    return mod_ref[j, pl.ds(pl.program_id(0) if row is None else row, 1), :]


def _sink_softmax(s, sink):
    m = jnp.maximum(jnp.max(s, axis=-1, keepdims=True), sink)
    e = jnp.exp(s - m)
    den = jnp.sum(e, axis=-1, keepdims=True) + jnp.exp(sink - m)
    return e * (1.0 / den)


def _stage_keys(kv, kd_ref, vt_ref, first_tile):
    n_blk = kv.shape[0] // BLOCK

    kd_ref[:, 0:BLOCK] = jnp.where(first_tile, jnp.zeros((), BF16), kd_ref[:, n_blk * BLOCK:])
    vt_ref[0] = jnp.where(first_tile, jnp.zeros((), BF16), vt_ref[n_blk])

    for g in range(N_KV_HEADS):
        k_g = kv[:, g * HEAD_DIM:(g + 1) * HEAD_DIM]
        kd_ref[g, BLOCK:] = jnp.concatenate([k_g, k_g], axis=1).astype(BF16)
    v_t = kv[:, KV_WIDTH:].T.astype(BF16)
    for blk in range(n_blk):
        vt_ref[blk + 1] = v_t[:, blk * BLOCK:(blk + 1) * BLOCK]


def _attention_phases(q_ref, kd_ref, vt_ref, s_ref, bias_ref, sink_ref, o_ref, first_tile):
    kj = lax.broadcasted_iota(jnp.int32, (BLOCK, BLOCK), 0)
    qi = lax.broadcasted_iota(jnp.int32, (BLOCK, BLOCK), 1)
    upper = kj > qi
    low_lanes = lax.broadcasted_iota(jnp.int32, (BLOCK, 2 * HEAD_DIM), 1) < HEAD_DIM

    def score_phase(c):
        q = q_ref[c * BLOCK:(c + 1) * BLOCK, :]
        for pair in range(N_HEADS // 2):
            g = (2 * pair) // GROUP
            q_pair = q[:, pair * 2 * HEAD_DIM:(pair + 1) * 2 * HEAD_DIM]
            zero = jnp.zeros_like(q_pair)
            q_both = jnp.concatenate([jnp.where(low_lanes, q_pair, zero),
                                      jnp.where(low_lanes, zero, q_pair)], axis=0)
            s_both = lax.dot_general(kd_ref[g, c * BLOCK:(c + 2) * BLOCK, :], q_both,
                                     (((1,), (1,)), ((), ())), preferred_element_type=F32)
            for par in range(2):
                h = 2 * pair + par
                s_h = s_both[:, par * BLOCK:(par + 1) * BLOCK]
                s = jnp.where(upper, s_h[:BLOCK], s_h[BLOCK:]) + bias_ref[h]
                if c == 0:
                    s = jnp.where(upper & first_tile, NEG_INF, s)
                s_ref[c, h] = s

    def value_phase(c, pair):
        g = (2 * pair) // GROUP
        v_cat = jnp.concatenate([vt_ref[c, g * HEAD_DIM:(g + 1) * HEAD_DIM],
                                 vt_ref[c + 1, g * HEAD_DIM:(g + 1) * HEAD_DIM]], axis=1)
        weights, scales = [], []
        for h in (2 * pair, 2 * pair + 1):
            s = s_ref[c, h]
            sink = sink_ref[h]
            m = jnp.maximum(jnp.max(s, axis=0, keepdims=True), sink)
            e = jnp.exp(s - m)
            den = jnp.sum(e, axis=0, keepdims=True) + jnp.exp(sink - m)
            weights.append(jnp.concatenate([jnp.where(upper, e, 0.0), jnp.where(upper, 0.0, e)],
                                           axis=0).astype(BF16))
            scales.append(1.0 / den)
        o_t = _dot(v_cat, jnp.concatenate(weights, axis=1))
        o_t = jnp.concatenate([o_t[:, :BLOCK] * scales[0], o_t[:, BLOCK:] * scales[1]], axis=0)
        o_ref[c * BLOCK:(c + 1) * BLOCK, pair * 2 * HEAD_DIM:(pair + 1) * 2 * HEAD_DIM] = (
            o_t.T.astype(BF16))

    return score_phase, value_phase


def _in_proj_kernel(*refs, sample):
    if sample:
        (x_ref, mod_ref, g_ref, w_ref, cw_ref, pre_ref,
         z_ref, q_ref, kv_ref, gate_ref, tail_ref, carry_ref) = refs
    else:
        (xn_ref, hb0_ref, mod_ref, g_ref, w_ref, cw_ref, sink_ref, bias_ref,
         z_ref, a_ref, kv_ref, gate_ref, tail_ref,
         carry_ref, hn_ref, q_ref, kd_ref, vt_ref, s_ref) = refs
    step = pl.program_id(0) if sample else pl.program_id(1)
    first_tile = step == 0

    def normed(x, row=None):
        scale = g_ref[...] * (1.0 + _mod(mod_ref, 1, sample, row))
        return _flat(_rms(x) * scale + _mod(mod_ref, 0, sample, row)).astype(BF16)

    if sample:
        hb = normed(x_ref[...])
    else:
        hb = jnp.where(first_tile & (pl.program_id(0) == 0), hb0_ref[...], hn_ref[...])
    rows = hb.shape[0]

    def store(ref, lo, val):
        ref[..., lo:lo + val.shape[-1]] = val.reshape(ref.shape[:-1] + (val.shape[-1],))

    def conv_chunk(lo):
        b_g = _dot(hb, w_ref[:, lo:lo + CHUNK])
        c_g = _dot(hb, w_ref[:, CONV_DIM + lo:CONV_DIM + lo + CHUNK])
        x_c = _dot(hb, w_ref[:, 2 * CONV_DIM + lo:2 * CONV_DIM + lo + CHUNK])
        u = c_g * x_c
        if sample:
            pos = x_ref.shape[0]
            u3 = u.reshape(pos, -1, CHUNK)
            prev = jnp.where(first_tile, pre_ref[:, :, lo:lo + CHUNK],
                             carry_ref[:, :, lo:lo + CHUNK])
            ext = jnp.concatenate([prev, u3], axis=0)
            u2 = _flat(ext[0:pos])
            u1 = _flat(ext[1:pos + 1])
            new_tail = u3[pos - (CONV_WIDTH - 1):]
            carry_ref[:, :, lo:lo + CHUNK] = new_tail
            tail_ref[:, :, lo:lo + CHUNK] = new_tail
        else:
            prev = jnp.where(first_tile, 0.0, carry_ref[:, lo:lo + CHUNK])
            p1 = prev[V7X_SUBLANES - 1:V7X_SUBLANES]
            p2 = prev[V7X_SUBLANES - 2:V7X_SUBLANES - 1]
            r = lax.broadcasted_iota(jnp.int32, u.shape, 0)
            u1 = jnp.where(r == 0, p1, pltpu.roll(u, 1, 0))
            u2 = jnp.where(r == 0, p2, jnp.where(r == 1, p1, pltpu.roll(u, 2, 0)))
            new_tail = u[rows - V7X_SUBLANES:]
            carry_ref[:, lo:lo + CHUNK] = new_tail
            tail_ref[:, lo:lo + CHUNK] = new_tail
        conv = (cw_ref[0:1, lo:lo + CHUNK] * u2 + cw_ref[1:2, lo:lo + CHUNK] * u1
                + cw_ref[2:3, lo:lo + CHUNK] * u)
        store(z_ref, lo, (b_g * conv).astype(BF16))

    def q_chunk(lo):
        q = _dot(hb, w_ref[:, Q_OFF + lo:Q_OFF + lo + CHUNK])
        store(q_ref, lo, (q * (HEAD_DIM ** -0.5)).astype(BF16))

    def gate_chunk(lo):
        g = _dot(hb, w_ref[:, GATE_OFF + lo:GATE_OFF + lo + CHUNK])
        store(gate_ref, lo, (1.0 / (1.0 + jnp.exp(-g))).astype(BF16))

    kv = _dot(hb, w_ref[:, KV_OFF:KV_OFF + 2 * KV_WIDTH])
    if sample:
        store(kv_ref, 0, kv)
    else:
        kv_ref[...] = kv[rows - kv_ref.shape[0]:]
    for lo in range(0, ATTN_WIDTH, CHUNK):
        q_chunk(lo)
    dense = ([functools.partial(conv_chunk, lo) for lo in range(0, CONV_DIM, CHUNK)]
             + [functools.partial(gate_chunk, lo) for lo in range(0, 2 * D_MODEL, CHUNK)])
    if sample:
        for work in dense:
            work()
        return

    _stage_keys(kv, kd_ref, vt_ref, first_tile)
    last_tile = step == pl.num_programs(1) - 1
    next_row = jnp.minimum(pl.program_id(0) + last_tile.astype(jnp.int32),
                           pl.num_programs(0) - 1)
    hn_ref[...] = normed(xn_ref[...], next_row)
    score_phase, value_phase = _attention_phases(q_ref, kd_ref, vt_ref, s_ref, bias_ref,
                                                 sink_ref, a_ref, first_tile)
    n_blk = rows // BLOCK
    attention = [functools.partial(score_phase, 0)]
    for c in range(n_blk):
        if c + 1 < n_blk:
            attention.append(functools.partial(score_phase, c + 1))
        attention += [functools.partial(value_phase, c, pair) for pair in range(N_HEADS // 2)]
    done = 0
    for i, work in enumerate(attention):
        work()
        while done < len(dense) and done * len(attention) < (i + 1) * len(dense):
            dense[done]()
            done += 1


def _first_tile_norm_kernel(x_ref, mod_ref, g_ref, o_ref):
    scale = g_ref[...] * (1.0 + _mod(mod_ref, 1, False))
    o_ref[...] = (_rms(x_ref[...]) * scale + _mod(mod_ref, 0, False)).astype(BF16)


def _first_tile_norm(x, mod_all, g_pre1, layer_idx, n_seq):
    cfg = _dense(x, layer_idx, n_seq, False)
    return pl.pallas_call(
        _first_tile_norm_kernel,
        grid=(1, 1),
        in_specs=[cfg.act(D_MODEL), cfg.mod_spec, cfg.layer(1, D_MODEL)],
        out_specs=pl.BlockSpec((cfg.rows, D_MODEL), lambda b, t: (0, 0)),
        out_shape=jax.ShapeDtypeStruct((cfg.rows, D_MODEL), BF16),
        name="first_tile_norm",
    )(x, mod_all, g_pre1)


def _in_proj(x, mod_all, g_pre1, w_in, conv_w, layer_idx, n_seq, pre=None, attn=None):
    cfg = _dense(x, layer_idx, n_seq, pre is not None)
    lead, mid = x.shape[0], x.shape[1]
    widths = (CONV_DIM, ATTN_WIDTH, 2 * KV_WIDTH, 2 * D_MODEL)
    dtypes = (BF16, BF16, F32, BF16)
    blocks = (_nbytes((cfg.rows, D_MODEL), F32) + _nbytes(w_in.shape[1:], BF16)
              + sum(_nbytes((cfg.rows, w), dt) for w, dt in zip(widths, dtypes)))
    temps = _nbytes((cfg.rows, D_MODEL), F32) * 2 + _nbytes((cfg.rows, CHUNK), F32) * 10
    if cfg.sample:
        tail_shape = (CONV_WIDTH - 1, mid, CONV_DIM)
        tail_spec = pl.BlockSpec(tail_shape, lambda i: (0, 0, 0))
        extra_in = [pre]
        extra_specs = [cfg.layer(*tail_shape)]
        scratch = [pltpu.VMEM(tail_shape, F32)]
        x_specs, x_args = [cfg.act(D_MODEL)], [x]
    else:
        sinks, bias = attn
        per_tile = cfg.rows // BLOCK
        tail_shape = (lead, V7X_SUBLANES, CONV_DIM)
        tail_spec = pl.BlockSpec((None, V7X_SUBLANES, CONV_DIM), lambda b, t: (b, 0, 0))
        tiles = mid // cfg.rows
        next_tile = pl.BlockSpec(
            (None, cfg.rows, D_MODEL),
            lambda b, t: (jnp.minimum(b + (t + 1) // tiles, lead - 1), (t + 1) % tiles, 0))
        x_specs = [next_tile, pl.BlockSpec((cfg.rows, D_MODEL), lambda b, t: (0, 0),
                                           pipeline_mode=pl.Buffered(1))]
        x_args = [x, _first_tile_norm(x, mod_all, g_pre1, layer_idx, n_seq)]
        extra_in = [sinks[layer_idx], bias]
        extra_specs = [pl.BlockSpec(memory_space=pltpu.SMEM),
                       pl.BlockSpec(bias.shape, lambda b, t: (0, 0, 0),
                                    pipeline_mode=pl.Buffered(1))]
        scratch = [pltpu.VMEM((V7X_SUBLANES, CONV_DIM), F32),
                   pltpu.VMEM((cfg.rows, D_MODEL), BF16),
                   pltpu.VMEM((cfg.rows, ATTN_WIDTH), BF16),
                   pltpu.VMEM((N_KV_HEADS, cfg.rows + BLOCK, 2 * HEAD_DIM), BF16),
                   pltpu.VMEM((per_tile + 1, KV_WIDTH, BLOCK), BF16),
                   pltpu.VMEM((per_tile, N_HEADS, BLOCK, BLOCK), F32)]
        temps += (_nbytes(bias.shape, F32) + _nbytes((cfg.rows, ATTN_WIDTH), BF16) * 3
                  + _nbytes((per_tile, N_HEADS, BLOCK, BLOCK), F32)
                  + _nbytes((BLOCK, 2 * BLOCK), F32) * 16 + _nbytes((BLOCK, ATTN_WIDTH), F32) * 4)

    out_shape = [jax.ShapeDtypeStruct(x.shape[:-1] + (w,), dt) for w, dt in zip(widths, dtypes)]
    out_shape.append(jax.ShapeDtypeStruct(tail_shape, F32))
    out_specs = [cfg.act(w) for w in widths] + [tail_spec]
    if not cfg.sample:
        win = min(WINDOW, mid)
        out_shape[2] = jax.ShapeDtypeStruct((lead, win, 2 * KV_WIDTH), F32)
        out_specs[2] = pl.BlockSpec((None, win, 2 * KV_WIDTH), lambda b, t: (b, 0, 0))
    return pl.pallas_call(
        functools.partial(_in_proj_kernel, sample=cfg.sample),
        grid=cfg.grid,
        in_specs=x_specs + [cfg.mod_spec, cfg.layer(1, D_MODEL),
                            _dense(x, 0, n_seq, cfg.sample).layer(*w_in.shape[1:]),
                            cfg.layer(*conv_w.shape[1:])] + extra_specs,
        out_specs=out_specs,
        out_shape=out_shape,
        scratch_shapes=scratch,
        compiler_params=pltpu.CompilerParams(
            dimension_semantics=cfg.sem, vmem_limit_bytes=_vmem_limit(blocks, temps)),
        name="in_proj_sample" if cfg.sample else "in_proj_attn_prompt",
    )(*x_args, mod_all, g_pre1, w_in, conv_w, *extra_in)


def _sample_attn_kernel(q_ref, kc_ref, vc_ref, kn_ref, vn_ref, bias_ref, sink_ref, *rest,
                        layer_idx):
    o_ref, ko_ref, vo_ref = rest[-3:]
    if ko_ref.ndim > kc_ref.ndim:
        for other in range(ko_ref.shape[0]):
            if other != layer_idx:
                ko_ref[other] = jnp.zeros(ko_ref.shape[1:], F32)
                vo_ref[other] = jnp.zeros(vo_ref.shape[1:], F32)
        ko_ref, vo_ref = ko_ref.at[layer_idx], vo_ref.at[layer_idx]
    n_seq = q_ref.shape[0]
    buf = kc_ref.shape[-1]
    new = kn_ref.shape[1]
    rows = new * GROUP
    pos = lax.broadcasted_iota(jnp.int32, (rows, 2 * buf), 0) // GROUP
    kj = lax.broadcasted_iota(jnp.int32, (rows, 2 * buf), 1)
    dist = buf + pos - kj
    mask = (dist >= 0) & (dist < WINDOW) & (kj < buf + new)
    lane = lax.broadcasted_iota(jnp.int32, (n_seq, HEAD_DIM, buf), 2)
    pad = jnp.zeros((n_seq, HEAD_DIM, buf - new), F32)

    def extended(cache_t, fresh, out_ref, g):
        fresh_t = jnp.concatenate([jnp.swapaxes(fresh, 1, 2), pad], axis=2)
        out_ref[:, g] = pltpu.roll(jnp.where(lane < new, fresh_t, cache_t), buf - new, 2)
        return jnp.concatenate([cache_t, fresh_t], axis=2).astype(BF16)

    for g in range(N_KV_HEADS):
        lanes = slice(g * HEAD_DIM, (g + 1) * HEAD_DIM)
        k_ext = extended(kc_ref[:, g], kn_ref[:, :, lanes], ko_ref, g)
        v_ext = extended(vc_ref[:, g], vn_ref[:, :, lanes], vo_ref, g)
        s = jnp.einsum('nqd,ndk->nqk', q_ref[:, g], k_ext, preferred_element_type=F32)
        p = _sink_softmax(jnp.where(mask[None], s + bias_ref[g][None], NEG_INF),
                          sink_ref[g][None])
        o = jnp.einsum('nqk,ndk->nqd', p.astype(BF16), v_ext, preferred_element_type=F32)
        o_ref[:, g] = o.astype(BF16)


def _sample_attn(q, k_cache, v_cache, k_new, v_new, bias, sink_rows, layer_idx, windows):
    _, n, _, _, buf = k_cache.shape
    new = k_new.shape[1]
    rows = new * GROUP
    nt = SAMPLE_SEQ_TILE
    seq3 = lambda a, b: pl.BlockSpec((nt, a, b), lambda i: (i, 0, 0))
    cache = pl.BlockSpec((None, nt, N_KV_HEADS, HEAD_DIM, buf),
                         lambda i: (layer_idx, i, 0, 0, 0))
    q_spec = pl.BlockSpec((nt, N_KV_HEADS, rows, HEAD_DIM), lambda i: (i, 0, 0, 0))
    in_specs = [q_spec, cache, cache, seq3(new, KV_WIDTH), seq3(new, KV_WIDTH),
                pl.BlockSpec(bias.shape, lambda i: (0, 0, 0)),
                pl.BlockSpec(sink_rows.shape, lambda i: (0, 0, 0))]
    args = [q, k_cache, v_cache, k_new, v_new, bias, sink_rows]
    depth = k_cache.shape[0]
    if windows is None:
        aliases = {}
        window_out = pl.BlockSpec((depth, nt, N_KV_HEADS, HEAD_DIM, buf),
                                  lambda i: (0, i, 0, 0, 0))
    else:
        aliases = {len(args): 1, len(args) + 1: 2}
        in_specs += [pl.BlockSpec(memory_space=pl.ANY)] * 2
        args += list(windows)
        window_out = cache
    blocks = (_nbytes((nt, N_KV_HEADS, HEAD_DIM, buf), F32) * (2 + 2 * depth)
              + _nbytes((nt, new, KV_WIDTH), F32) * 2
              + _nbytes((nt, N_KV_HEADS, rows, 2 * HEAD_DIM), BF16) * 2 + _nbytes(bias.shape, F32))
    temps = _nbytes((rows, 2 * buf), F32) * 8 * nt
    return pl.pallas_call(
        functools.partial(_sample_attn_kernel, layer_idx=layer_idx),
        grid=(n // nt,),
        in_specs=in_specs,
        out_specs=[q_spec, window_out, window_out],
        out_shape=[jax.ShapeDtypeStruct(q.shape, BF16),
                   jax.ShapeDtypeStruct(k_cache.shape, F32),
                   jax.ShapeDtypeStruct(v_cache.shape, F32)],
        input_output_aliases=aliases,
        compiler_params=pltpu.CompilerParams(
            dimension_semantics=("arbitrary",), vmem_limit_bytes=_vmem_limit(blocks, temps)),
        name="attn_sample",
    )(*args)


def _merge_residual(x_ref, z_ref, a_ref, gate_ref, mod_ref, gpost1_ref, wbc_ref, wba_ref, wo_ref,
                    sample):
    y_conv = _dot(_flat(z_ref[...]), wbc_ref[...])
    y_attn = _dot(_flat(a_ref[...]), wba_ref[...])
    gates = _flat(gate_ref[...])
    merged = gates[:, :D_MODEL] * y_conv + gates[:, D_MODEL:] * y_attn
    mixed = _dot(merged.astype(BF16), wo_ref[...])
    scale = _mod(mod_ref, 2, sample) * gpost1_ref[...]
    return x_ref[...] + scale * _rms(mixed).reshape(x_ref.shape)


def _mlp_input(x1, mod_ref, gpre2_ref, sample):
    scale = gpre2_ref[...] * (1.0 + _mod(mod_ref, 4, sample))
    return _flat(_rms(x1) * scale + _mod(mod_ref, 3, sample)).astype(BF16)


def _mlp_residual(x1, ff, mod_ref, gpost2_ref, sample):
    scale = _mod(mod_ref, 5, sample) * gpost2_ref[...]
    return x1 + scale * _rms(ff).reshape(x1.shape)


def _post_kernel(x_ref, z_ref, a_ref, gate_ref, mod_ref, gpost1_ref, gpre2_ref, gpost2_ref,
                 wbc_ref, wba_ref, wo_ref, w1_ref, w2_ref, *rest, sample):
    if len(rest) == 4:
        cast_in_ref, o_ref, cast_out_ref, hid_ref = rest
        cast_out_ref[...] = cast_in_ref[...].astype(BF16)
    else:
        o_ref, hid_ref = rest
    x1 = _merge_residual(x_ref, z_ref, a_ref, gate_ref, mod_ref, gpost1_ref,
                         wbc_ref, wba_ref, wo_ref, sample)
    hb = _mlp_input(x1, mod_ref, gpre2_ref, sample)
    for lo in range(0, D_FF, 2 * CHUNK):
        act = jnp.maximum(_dot(hb, w1_ref[:, lo:lo + 2 * CHUNK]), 0.0)
        hid_ref[:, lo:lo + 2 * CHUNK] = (act * act).astype(BF16)
    ff = _dot(hid_ref[...], w2_ref[...])
    o_ref[...] = _mlp_residual(x1, ff, mod_ref, gpost2_ref, sample)


def _post(x, z, a, gates, mod_all, g_post1, g_pre2, g_post2, w_bc, w_ba, w_o, w1, w2,
          layer_idx, mlp_layer_idx, n_seq, cast_next=None):
    cfg = _dense(x, layer_idx, n_seq, False)
    mlp = _dense(x, mlp_layer_idx, n_seq, False)
    sq = (D_MODEL, D_MODEL)
    blocks = (_nbytes((cfg.rows, D_MODEL), F32) * 2 + _nbytes((cfg.rows, D_MODEL), BF16) * 2
              + _nbytes((cfg.rows, 2 * D_MODEL), gates.dtype))
    temps = (_nbytes(sq, BF16) * 3 + _nbytes(w1.shape[1:], BF16) * 2
             + _nbytes((cfg.rows, D_FF), BF16) + _nbytes((cfg.rows, 2 * CHUNK), F32) * 2
             + _nbytes((cfg.rows, D_MODEL), F32) * 5)
    vec = cfg.layer(1, D_MODEL)
    in_specs = [cfg.act(D_MODEL), cfg.act(CONV_DIM), cfg.act(ATTN_WIDTH),
                cfg.act(2 * D_MODEL), cfg.mod_spec, vec, vec, vec,
                cfg.layer(*sq), cfg.layer(*sq), cfg.layer(*sq),
                mlp.layer(*w1.shape[1:]), mlp.layer(*w2.shape[1:])]
    args = [x, z, a, gates, mod_all, g_post1, g_pre2, g_post2, w_bc, w_ba, w_o, w1, w2]
    out_specs = [cfg.act(D_MODEL)]
    out_shape = [jax.ShapeDtypeStruct(x.shape, F32)]
    if cast_next is not None:
        w_next, src_layer = cast_next
        _, k_dim, n_dim = w_next.shape
        last = n_dim // CAST_CHUNK - 1
        assert n_dim % CAST_CHUNK == 0 and last < cfg.grid[0] * cfg.grid[1]
        tiles = cfg.grid[1]
        in_specs.append(pl.BlockSpec(
            (None, k_dim, CAST_CHUNK),
            lambda b, t: (src_layer, 0, jnp.minimum(b * tiles + t, last))))
        out_specs.append(pl.BlockSpec(
            (None, k_dim, CAST_CHUNK), lambda b, t: (0, 0, jnp.minimum(b * tiles + t, last))))
        out_shape.append(jax.ShapeDtypeStruct((1, k_dim, n_dim), BF16))
        args.append(w_next)
        blocks += _nbytes((k_dim, CAST_CHUNK), F32) + _nbytes((k_dim, CAST_CHUNK), BF16)
    return pl.pallas_call(
        functools.partial(_post_kernel, sample=False),
        grid=cfg.grid,
        in_specs=in_specs,
        out_specs=out_specs,
        out_shape=out_shape,
        scratch_shapes=[pltpu.VMEM((cfg.rows, D_FF), BF16)],
        compiler_params=pltpu.CompilerParams(
            dimension_semantics=cfg.sem, vmem_limit_bytes=_vmem_limit(blocks, temps)),
        name="post_prompt",
    )(*args)


def _mix_kernel(x_ref, z_ref, a_ref, gate_ref, mod_ref, gpost1_ref, wbc_ref, wba_ref, wo_ref,
                o_ref):
    o_ref[...] = _merge_residual(x_ref, z_ref, a_ref, gate_ref, mod_ref, gpost1_ref,
                                 wbc_ref, wba_ref, wo_ref, True)


def _mix_sample(x, z, a, gates, mod_all, g_post1, w_bc, w_ba, w_o, layer_idx, n_seq):
    cfg = _dense(x, layer_idx, n_seq, True)
    sq = (D_MODEL, D_MODEL)
    blocks = (_nbytes((cfg.rows, D_MODEL), F32) * 2 + _nbytes((cfg.rows, D_MODEL), BF16) * 2
              + _nbytes((cfg.rows, 2 * D_MODEL), gates.dtype))
    temps = _nbytes(sq, BF16) * 3 + _nbytes((cfg.rows, D_MODEL), F32) * 6
    return pl.pallas_call(
        _mix_kernel,
        grid=cfg.grid,
        in_specs=[cfg.act(D_MODEL), cfg.act(CONV_DIM), cfg.act(ATTN_WIDTH),
                  cfg.act(2 * D_MODEL), cfg.mod_spec, cfg.layer(1, D_MODEL),
                  cfg.layer(*sq), cfg.layer(*sq), cfg.layer(*sq)],
        out_specs=cfg.act(D_MODEL),
        out_shape=jax.ShapeDtypeStruct(x.shape, F32),
        compiler_params=pltpu.CompilerParams(
            dimension_semantics=cfg.sem, vmem_limit_bytes=_vmem_limit(blocks, temps)),
        name="mix_sample",
    )(x, z, a, gates, mod_all, g_post1, w_bc, w_ba, w_o)


def _mlp_stream_kernel(x_ref, mod_ref, gpre2_ref, gpost2_ref, w1_ref, w2_ref,
                       o_ref, w1b_ref, w2b_ref, ff_ref):
    hb = _mlp_input(x_ref[...], mod_ref, gpre2_ref, True)
    w1 = w1_ref[...].astype(BF16)
    w2 = w2_ref[...].astype(BF16)
    w1b_ref[...] = w1
    w2b_ref[...] = w2
    act = jnp.maximum(_dot(hb, w1), 0.0)
    ff = jnp.where(pl.program_id(0) == 0, 0.0, ff_ref[...]) + _dot((act * act).astype(BF16), w2)
    ff_ref[...] = ff
    o_ref[...] = _mlp_residual(x_ref[...], ff, mod_ref, gpost2_ref, True)


def _mlp_stream_sample(x, mod_all, g_pre2, g_post2, w1, w2, layer_idx, n_seq):
    rows = x.shape[0] * x.shape[1]
    const = lambda shape, idx: pl.BlockSpec(shape, lambda c: idx, pipeline_mode=pl.Buffered(1))
    vec = const((None, 1, D_MODEL), (layer_idx, 0, 0))
    blocks = (_nbytes((D_MODEL, MLP_STREAM_CHUNK), F32) * 2
              + _nbytes((D_MODEL, MLP_STREAM_CHUNK), BF16) * 2 + _nbytes(x.shape, F32))
    temps = (_nbytes(x.shape, F32) * 4 + _nbytes((N_MOD, n_seq, D_MODEL), F32)
             + _nbytes((rows, D_MODEL), BF16) + _nbytes((rows, MLP_STREAM_CHUNK), F32) * 3)
    return pl.pallas_call(
        _mlp_stream_kernel,
        grid=(D_FF // MLP_STREAM_CHUNK,),
        in_specs=[const(x.shape, (0, 0, 0)),
                  const((None, N_MOD, n_seq, D_MODEL), (layer_idx, 0, 0, 0)), vec, vec,
                  pl.BlockSpec((None, D_MODEL, MLP_STREAM_CHUNK), lambda c: (layer_idx, 0, c)),
                  pl.BlockSpec((None, MLP_STREAM_CHUNK, D_MODEL), lambda c: (layer_idx, c, 0))],
        out_specs=[pl.BlockSpec(x.shape, lambda c: (0, 0, 0)),
                   pl.BlockSpec((None, D_MODEL, MLP_STREAM_CHUNK), lambda c: (0, 0, c)),
                   pl.BlockSpec((None, MLP_STREAM_CHUNK, D_MODEL), lambda c: (0, c, 0))],
        out_shape=[jax.ShapeDtypeStruct(x.shape, F32),
                   jax.ShapeDtypeStruct((1, D_MODEL, D_FF), BF16),
                   jax.ShapeDtypeStruct((1, D_FF, D_MODEL), BF16)],
        scratch_shapes=[pltpu.VMEM((rows, D_MODEL), F32)],
        compiler_params=pltpu.CompilerParams(
            dimension_semantics=("arbitrary",), vmem_limit_bytes=_vmem_limit(blocks, temps)),
        name="mlp_stream_sample",
    )(x, mod_all, g_pre2, g_post2, w1, w2)


def kernel(x_prompt, x_sample, c_prompt, c_sample, state_conv, cache_k, cache_v, w_ada, b_ada,
           g_pre1, w_in, conv_w, w_br_conv, w_br_attn, w_o, sinks, g_post1, g_pre2, w_ff1, w_ff2,
           g_post2, rel_table):
    depth = w_ada.shape[0]
    batch, seq, _ = x_prompt.shape
    n_seq, n_new, _ = x_sample.shape
    buf = cache_k.shape[2]

    pad = (-(n_seq + batch)) % V7X_SUBLANES
    c_all = jnp.concatenate([c_sample, c_prompt, jnp.zeros((pad, D_MODEL), F32)], axis=0)
    mod_all = _ada(c_all, w_ada, b_ada)

    dist_p = (jnp.arange(BLOCK)[:, None] + BLOCK) - jnp.arange(2 * BLOCK)[None, :]
    bucket_p = _rel_bucket(dist_p)
    upper = jnp.arange(BLOCK)[None, :] > jnp.arange(BLOCK)[:, None]
    bucket_merged = jnp.where(upper, bucket_p[:, :BLOCK], bucket_p[:, BLOCK:])
    bias_p = _bias_table(rel_table, bucket_merged.T)
    dist_s = (buf + jnp.arange(n_new))[:, None] - jnp.arange(buf + n_new)[None, :]
    bias_s = _bias_table(rel_table, _rel_bucket(dist_s))
    bias_s = (bias_s.reshape(N_KV_HEADS, GROUP, n_new, buf + n_new)
              .transpose(0, 2, 1, 3).reshape(N_KV_HEADS, n_new * GROUP, buf + n_new))
    bias_s = jnp.pad(bias_s, ((0, 0), (0, 0), (0, buf - n_new)))

    to_bf16 = lambda w: w.astype(BF16)
    w_bc_b, w_ba_b, w_o_b = map(to_bf16, (w_br_conv, w_br_attn, w_o))
    w_in_b = to_bf16(w_in[0:1])
    vec = lambda g: g.reshape(depth, 1, D_MODEL)
    g_pre1, g_post1, g_pre2, g_post2 = map(vec, (g_pre1, g_post1, g_pre2, g_post2))
    pre_s = state_conv.transpose(0, 2, 1, 3)
    cache_k = cache_k.transpose(0, 1, 3, 4, 2)
    cache_v = cache_v.transpose(0, 1, 3, 4, 2)

    xp = x_prompt
    xs = x_sample.transpose(1, 0, 2)
    conv_p, k_p, v_p, conv_s = [], [], [], []
    windows = None
    win = min(WINDOW, seq)
    for l in range(depth):
        z, q, kv, gates, tail = _in_proj(xs, mod_all, g_pre1, w_in_b, conv_w, l, n_seq, pre_s)
        q_s = (q.reshape(n_new, n_seq, N_KV_HEADS, GROUP, HEAD_DIM)
               .transpose(1, 2, 0, 3, 4).reshape(n_seq, N_KV_HEADS, n_new * GROUP, HEAD_DIM))
        kv_s = kv.transpose(1, 0, 2)
        sink_rows = jnp.tile(sinks[l].reshape(N_KV_HEADS, 1, GROUP),
                             (1, n_new, 1)).reshape(N_KV_HEADS, n_new * GROUP, 1)
        o, *windows = _sample_attn(q_s, cache_k, cache_v, kv_s[:, :, :KV_WIDTH],
                                   kv_s[:, :, KV_WIDTH:], bias_s, sink_rows, l, windows)
        attn = (o.reshape(n_seq, N_KV_HEADS, n_new, GROUP, HEAD_DIM)
                .transpose(2, 0, 1, 3, 4).reshape(n_new, n_seq, ATTN_WIDTH))
        x1 = _mix_sample(xs, z, attn, gates, mod_all, g_post1, w_bc_b, w_ba_b, w_o_b, l, n_seq)
        xs, w1_b, w2_b = _mlp_stream_sample(x1, mod_all, g_pre2, g_post2, w_ff1, w_ff2, l, n_seq)
        conv_s.append(tail.transpose(1, 0, 2))

        z, attn, kv, gates, tail = _in_proj(xp, mod_all, g_pre1, w_in_b, conv_w, l, n_seq,
                                            attn=(sinks, bias_p))
        cast_next = (w_in, l + 1) if l + 1 < depth else None
        xp, *w_next = _post(xp, z, attn, gates, mod_all, g_post1, g_pre2, g_post2, w_bc_b, w_ba_b,
                            w_o_b, w1_b, w2_b, l, 0, n_seq, cast_next)
        w_in_b = w_next[0] if w_next else None
        conv_p.append(tail[:, V7X_SUBLANES - (CONV_WIDTH - 1):])
        k_p.append(kv[:, :, :KV_WIDTH].reshape(batch, win, N_KV_HEADS, HEAD_DIM))
        v_p.append(kv[:, :, KV_WIDTH:].reshape(batch, win, N_KV_HEADS, HEAD_DIM))

    k_s, v_s = (w.transpose(0, 1, 4, 2, 3) for w in windows)
    return (xp, xs.transpose(1, 0, 2), jnp.stack(conv_p), jnp.stack(k_p), jnp.stack(v_p),
            jnp.stack(conv_s), k_s, v_s)
```

```python
import functools
import math
from typing import Any, Callable, NamedTuple

import jax
import jax.numpy as jnp
from jax import lax
from jax.experimental import pallas as pl
from jax.experimental.pallas import tpu as pltpu

D_MODEL = 1024
N_HEADS = 16
N_KV_HEADS = 2
HEAD_DIM = 64
GROUP = N_HEADS // N_KV_HEADS
ATTN_WIDTH = N_HEADS * HEAD_DIM
KV_WIDTH = N_KV_HEADS * HEAD_DIM
CONV_DIM = D_MODEL
CONV_WIDTH = 3
WINDOW = 128
BLOCK = 128
N_BUCKETS = 32
MAX_DISTANCE = 128
D_FF = 4 * D_MODEL
N_MOD = 6
RMS_EPS = 1e-6
NEG_INF = -1e30
PROJ_COLS = 3 * CONV_DIM + ATTN_WIDTH + 2 * KV_WIDTH + 2 * D_MODEL
Q_OFF = 3 * CONV_DIM
KV_OFF = Q_OFF + ATTN_WIDTH
GATE_OFF = KV_OFF + 2 * KV_WIDTH

V7X_SUBLANES = 8
V7X_VMEM_BYTES = 64 * 1024 * 1024

PROMPT_TILE = 512
SAMPLE_POS_TILE = 4
SAMPLE_SEQ_TILE = 32
CHUNK = 256
MLP_STREAM_CHUNK = 1024
CAST_CHUNK = 256

F32 = jnp.float32
BF16 = jnp.bfloat16


def _vmem_limit(block_bytes, temp_bytes):
    return int(min(2 * block_bytes + temp_bytes, V7X_VMEM_BYTES - 4 * 1024 * 1024))


def _nbytes(shape, dtype):
    return math.prod(shape) * jnp.dtype(dtype).itemsize


def _rms(x):
    return x * lax.rsqrt(jnp.mean(x * x, axis=-1, keepdims=True) + RMS_EPS)


def _dot(a, b):
    return jnp.dot(a, b, preferred_element_type=F32)


def _flat(a):
    return a.reshape(-1, a.shape[-1])


def _ada_kernel(c_ref, w_ref, b_ref, o_ref):
    c = c_ref[...]
    s = c * (1.0 / (1.0 + jnp.exp(-c)))
    o_ref[...] = _dot(s.astype(BF16), w_ref[...].astype(BF16)) + b_ref[...]


def _ada(c_all, w_ada, b_ada):
    depth = w_ada.shape[0]
    rows = c_all.shape[0]
    blocks = (_nbytes((rows, D_MODEL), F32) * 2 + _nbytes((D_MODEL, D_MODEL), F32))
    return pl.pallas_call(
        _ada_kernel,
        grid=(depth, N_MOD),
        in_specs=[
            pl.BlockSpec((rows, D_MODEL), lambda l, j: (0, 0)),
            pl.BlockSpec((None, D_MODEL, D_MODEL), lambda l, j: (l, 0, j)),
            pl.BlockSpec((None, None, 1, D_MODEL), lambda l, j: (l, j, 0, 0)),
        ],
        out_specs=pl.BlockSpec((None, None, rows, D_MODEL), lambda l, j: (l, j, 0, 0)),
        out_shape=jax.ShapeDtypeStruct((depth, N_MOD, rows, D_MODEL), F32),
        compiler_params=pltpu.CompilerParams(
            dimension_semantics=("arbitrary", "arbitrary"),
            vmem_limit_bytes=_vmem_limit(blocks, _nbytes((D_MODEL, D_MODEL), BF16) * 2)),
        name="ada_mod",
    )(c_all, w_ada, b_ada.reshape(depth, N_MOD, 1, D_MODEL))


def _bias_kernel(tab_ref, bucket_ref, o_ref):
    bucket = bucket_ref[...]
    for h in range(N_HEADS):
        acc = jnp.zeros(bucket.shape, F32)
        for b in range(N_BUCKETS):
            acc = jnp.where(bucket == b, tab_ref[b * N_HEADS + h], acc)
        o_ref[h] = acc


def _bias_table(rel_table, bucket):
    return pl.pallas_call(
        _bias_kernel,
        in_specs=[
            pl.BlockSpec(memory_space=pltpu.SMEM),
            pl.BlockSpec(bucket.shape, lambda: (0, 0)),
        ],
        out_specs=pl.BlockSpec((N_HEADS,) + bucket.shape, lambda: (0, 0, 0)),
        out_shape=jax.ShapeDtypeStruct((N_HEADS,) + bucket.shape, F32),
        name="bias_table",
    )(rel_table.reshape(-1), bucket)


def _rel_bucket(dist):
    n = jnp.maximum(dist, 0)
    max_exact = N_BUCKETS // 2
    nf = jnp.maximum(n, 1).astype(F32)
    scaled = (jnp.log(nf / max_exact) / math.log(MAX_DISTANCE / max_exact)
              * (N_BUCKETS - max_exact))
    large = jnp.minimum(max_exact + jnp.floor(scaled).astype(jnp.int32), N_BUCKETS - 1)
    return jnp.where(n < max_exact, n, large)


class _Dense(NamedTuple):
    sample: bool
    grid: tuple
    rows: int
    act: Callable[[int], Any]
    layer: Callable[..., Any]
    mod_spec: Any
    sem: tuple


def _dense(x, layer_idx, n_seq, sample):
    lead, mid = x.shape[0], x.shape[1]
    if sample:
        grid = (lead // SAMPLE_POS_TILE,)
        act = lambda w: pl.BlockSpec((SAMPLE_POS_TILE, mid, w), lambda i: (i, 0, 0))
        layer = lambda *shape: pl.BlockSpec((None,) + shape,
                                            lambda i: (layer_idx,) + (0,) * len(shape),
                                            pipeline_mode=pl.Buffered(1))
        mod_spec = pl.BlockSpec((None, N_MOD, n_seq, D_MODEL), lambda i: (layer_idx, 0, 0, 0))
        return _Dense(True, grid, SAMPLE_POS_TILE * mid, act, layer, mod_spec, ("arbitrary",))
    assert n_seq % V7X_SUBLANES == 0 and lead <= V7X_SUBLANES
    grid = (lead, mid // PROMPT_TILE)
    act = lambda w: pl.BlockSpec((None, PROMPT_TILE, w), lambda b, t: (b, t, 0))
    layer = lambda *shape: pl.BlockSpec((None,) + shape,
                                        lambda b, t: (layer_idx,) + (0,) * len(shape),
                                        pipeline_mode=pl.Buffered(1))
    mod_spec = pl.BlockSpec((None, N_MOD, V7X_SUBLANES, D_MODEL),
                            lambda b, t: (layer_idx, 0, n_seq // V7X_SUBLANES, 0))
    return _Dense(False, grid, PROMPT_TILE, act, layer, mod_spec, ("arbitrary", "arbitrary"))


def _mod(mod_ref, j, sample, row=None):
    if sample:
        return mod_ref[j]
    return mod_ref[j, pl.ds(pl.program_id(0) if row is None else row, 1), :]


def _sink_softmax(s, sink):
    m = jnp.maximum(jnp.max(s, axis=-1, keepdims=True), sink)
    e = jnp.exp(s - m)
    den = jnp.sum(e, axis=-1, keepdims=True) + jnp.exp(sink - m)
    return e * (1.0 / den)


def _stage_keys(kv, kd_ref, vt_ref, first_tile):
    n_blk = kv.shape[0] // BLOCK

    kd_ref[:, 0:BLOCK] = jnp.where(first_tile, jnp.zeros((), BF16), kd_ref[:, n_blk * BLOCK:])
    vt_ref[0] = jnp.where(first_tile, jnp.zeros((), BF16), vt_ref[n_blk])

    for g in range(N_KV_HEADS):
        k_g = kv[:, g * HEAD_DIM:(g + 1) * HEAD_DIM]
        kd_ref[g, BLOCK:] = jnp.concatenate([k_g, k_g], axis=1).astype(BF16)
    v_t = kv[:, KV_WIDTH:].T.astype(BF16)
    for blk in range(n_blk):
        vt_ref[blk + 1] = v_t[:, blk * BLOCK:(blk + 1) * BLOCK]


def _attention_phases(q_ref, kd_ref, vt_ref, s_ref, bias_ref, sink_ref, o_ref, first_tile):
    kj = lax.broadcasted_iota(jnp.int32, (BLOCK, BLOCK), 0)
    qi = lax.broadcasted_iota(jnp.int32, (BLOCK, BLOCK), 1)
    upper = kj > qi
    low_lanes = lax.broadcasted_iota(jnp.int32, (BLOCK, 2 * HEAD_DIM), 1) < HEAD_DIM

    def score_phase(c):
        q = q_ref[c * BLOCK:(c + 1) * BLOCK, :]
        for pair in range(N_HEADS // 2):
            g = (2 * pair) // GROUP
            q_pair = q[:, pair * 2 * HEAD_DIM:(pair + 1) * 2 * HEAD_DIM]
            zero = jnp.zeros_like(q_pair)
            q_both = jnp.concatenate([jnp.where(low_lanes, q_pair, zero),
                                      jnp.where(low_lanes, zero, q_pair)], axis=0)
            s_both = lax.dot_general(kd_ref[g, c * BLOCK:(c + 2) * BLOCK, :], q_both,
                                     (((1,), (1,)), ((), ())), preferred_element_type=F32)
            for par in range(2):
                h = 2 * pair + par
                s_h = s_both[:, par * BLOCK:(par + 1) * BLOCK]
                s = jnp.where(upper, s_h[:BLOCK], s_h[BLOCK:]) + bias_ref[h]
                if c == 0:
                    s = jnp.where(upper & first_tile, NEG_INF, s)
                s_ref[c, h] = s

    def value_phase(c, pair):
        g = (2 * pair) // GROUP
        v_cat = jnp.concatenate([vt_ref[c, g * HEAD_DIM:(g + 1) * HEAD_DIM],
                                 vt_ref[c + 1, g * HEAD_DIM:(g + 1) * HEAD_DIM]], axis=1)
        weights, scales = [], []
        for h in (2 * pair, 2 * pair + 1):
            s = s_ref[c, h]
            sink = sink_ref[h]
            m = jnp.maximum(jnp.max(s, axis=0, keepdims=True), sink)
            e = jnp.exp(s - m)
            den = jnp.sum(e, axis=0, keepdims=True) + jnp.exp(sink - m)
            weights.append(jnp.concatenate([jnp.where(upper, e, 0.0), jnp.where(upper, 0.0, e)],
                                           axis=0).astype(BF16))
            scales.append(1.0 / den)
        o_t = _dot(v_cat, jnp.concatenate(weights, axis=1))
        o_t = jnp.concatenate([o_t[:, :BLOCK] * scales[0], o_t[:, BLOCK:] * scales[1]], axis=0)
        o_ref[c * BLOCK:(c + 1) * BLOCK, pair * 2 * HEAD_DIM:(pair + 1) * 2 * HEAD_DIM] = (
            o_t.T.astype(BF16))

    return score_phase, value_phase


def _in_proj_kernel(*refs, sample):
    if sample:
        (x_ref, mod_ref, g_ref, w_ref, cw_ref, pre_ref,
         z_ref, q_ref, kv_ref, gate_ref, tail_ref, carry_ref) = refs
    else:
        (x_ref, xn_ref, mod_ref, g_ref, w_ref, cw_ref, sink_ref, bias_ref,
         z_ref, a_ref, kv_ref, gate_ref, tail_ref,
         carry_ref, hn_ref, q_ref, kd_ref, vt_ref, s_ref) = refs
    step = pl.program_id(0) if sample else pl.program_id(1)
    first_tile = step == 0

    def normed(x, row=None):
        scale = g_ref[...] * (1.0 + _mod(mod_ref, 1, sample, row))
        return _flat(_rms(x) * scale + _mod(mod_ref, 0, sample, row)).astype(BF16)

    if sample:
        hb = normed(x_ref[...])
    else:
        @pl.when(first_tile & (pl.program_id(0) == 0))
        def _():
            hn_ref[...] = normed(x_ref[...])

        hb = hn_ref[...]
    rows = hb.shape[0]

    def store(ref, lo, val):
        ref[..., lo:lo + val.shape[-1]] = val.reshape(ref.shape[:-1] + (val.shape[-1],))

    def conv_chunk(lo):
        b_g = _dot(hb, w_ref[:, lo:lo + CHUNK])
        c_g = _dot(hb, w_ref[:, CONV_DIM + lo:CONV_DIM + lo + CHUNK])
        x_c = _dot(hb, w_ref[:, 2 * CONV_DIM + lo:2 * CONV_DIM + lo + CHUNK])
        u = c_g * x_c
        if sample:
            pos = x_ref.shape[0]
            u3 = u.reshape(pos, -1, CHUNK)
            prev = jnp.where(first_tile, pre_ref[:, :, lo:lo + CHUNK],
                             carry_ref[:, :, lo:lo + CHUNK])
            ext = jnp.concatenate([prev, u3], axis=0)
            u2 = _flat(ext[0:pos])
            u1 = _flat(ext[1:pos + 1])
            new_tail = u3[pos - (CONV_WIDTH - 1):]
            carry_ref[:, :, lo:lo + CHUNK] = new_tail
            tail_ref[:, :, lo:lo + CHUNK] = new_tail
        else:
            prev = jnp.where(first_tile, 0.0, carry_ref[:, lo:lo + CHUNK])
            p1 = prev[V7X_SUBLANES - 1:V7X_SUBLANES]
            p2 = prev[V7X_SUBLANES - 2:V7X_SUBLANES - 1]
            r = lax.broadcasted_iota(jnp.int32, u.shape, 0)
            u1 = jnp.where(r == 0, p1, pltpu.roll(u, 1, 0))
            u2 = jnp.where(r == 0, p2, jnp.where(r == 1, p1, pltpu.roll(u, 2, 0)))
            new_tail = u[rows - V7X_SUBLANES:]
            carry_ref[:, lo:lo + CHUNK] = new_tail
            tail_ref[:, lo:lo + CHUNK] = new_tail
        conv = (cw_ref[0:1, lo:lo + CHUNK] * u2 + cw_ref[1:2, lo:lo + CHUNK] * u1
                + cw_ref[2:3, lo:lo + CHUNK] * u)
        store(z_ref, lo, (b_g * conv).astype(BF16))

    def q_chunk(lo):
        q = _dot(hb, w_ref[:, Q_OFF + lo:Q_OFF + lo + CHUNK])
        store(q_ref, lo, (q * (HEAD_DIM ** -0.5)).astype(BF16))

    def gate_chunk(lo):
        g = _dot(hb, w_ref[:, GATE_OFF + lo:GATE_OFF + lo + CHUNK])
        store(gate_ref, lo, (1.0 / (1.0 + jnp.exp(-g))).astype(BF16))

    kv = _dot(hb, w_ref[:, KV_OFF:KV_OFF + 2 * KV_WIDTH])
    if sample:
        store(kv_ref, 0, kv)
    else:
        kv_ref[...] = kv[rows - kv_ref.shape[0]:]
    for lo in range(0, ATTN_WIDTH, CHUNK):
        q_chunk(lo)
    dense = ([functools.partial(conv_chunk, lo) for lo in range(0, CONV_DIM, CHUNK)]
             + [functools.partial(gate_chunk, lo) for lo in range(0, 2 * D_MODEL, CHUNK)])
    if sample:
        for work in dense:
            work()
        return

    _stage_keys(kv, kd_ref, vt_ref, first_tile)
    last_tile = step == pl.num_programs(1) - 1
    next_row = jnp.minimum(pl.program_id(0) + last_tile.astype(jnp.int32),
                           pl.num_programs(0) - 1)
    hn_ref[...] = normed(xn_ref[...], next_row)
    score_phase, value_phase = _attention_phases(q_ref, kd_ref, vt_ref, s_ref, bias_ref,
                                                 sink_ref, a_ref, first_tile)
    n_blk = rows // BLOCK
    attention = [functools.partial(score_phase, 0)]
    for c in range(n_blk):
        if c + 1 < n_blk:
            attention.append(functools.partial(score_phase, c + 1))
        attention += [functools.partial(value_phase, c, pair) for pair in range(N_HEADS // 2)]
    done = 0
    for i, work in enumerate(attention):
        work()
        while done < len(dense) and done * len(attention) < (i + 1) * len(dense):
            dense[done]()
            done += 1


def _in_proj(x, mod_all, g_pre1, w_in, conv_w, layer_idx, n_seq, pre=None, attn=None):
    cfg = _dense(x, layer_idx, n_seq, pre is not None)
    lead, mid = x.shape[0], x.shape[1]
    widths = (CONV_DIM, ATTN_WIDTH, 2 * KV_WIDTH, 2 * D_MODEL)
    dtypes = (BF16, BF16, F32, BF16)
    blocks = (_nbytes((cfg.rows, D_MODEL), F32) + _nbytes(w_in.shape[1:], BF16)
              + sum(_nbytes((cfg.rows, w), dt) for w, dt in zip(widths, dtypes)))
    temps = _nbytes((cfg.rows, D_MODEL), F32) * 2 + _nbytes((cfg.rows, CHUNK), F32) * 10
    if cfg.sample:
        tail_shape = (CONV_WIDTH - 1, mid, CONV_DIM)
        tail_spec = pl.BlockSpec(tail_shape, lambda i: (0, 0, 0))
        extra_in = [pre]
        extra_specs = [cfg.layer(*tail_shape)]
        scratch = [pltpu.VMEM(tail_shape, F32)]
        x_specs, x_args = [cfg.act(D_MODEL)], [x]
    else:
        sinks, bias = attn
        per_tile = cfg.rows // BLOCK
        tail_shape = (lead, V7X_SUBLANES, CONV_DIM)
        tail_spec = pl.BlockSpec((None, V7X_SUBLANES, CONV_DIM), lambda b, t: (b, 0, 0))
        tiles = mid // cfg.rows
        next_tile = pl.BlockSpec(
            (None, cfg.rows, D_MODEL),
            lambda b, t: (jnp.minimum(b + (t + 1) // tiles, lead - 1), (t + 1) % tiles, 0))
        first_tile_only = pl.BlockSpec((None, cfg.rows, D_MODEL), lambda b, t: (0, 0, 0),
                                       pipeline_mode=pl.Buffered(1))
        x_specs, x_args = [first_tile_only, next_tile], [x, x]
        extra_in = [sinks[layer_idx], bias]
        extra_specs = [pl.BlockSpec(memory_space=pltpu.SMEM),
                       pl.BlockSpec(bias.shape, lambda b, t: (0, 0, 0),
                                    pipeline_mode=pl.Buffered(1))]
        scratch = [pltpu.VMEM((V7X_SUBLANES, CONV_DIM), F32),
                   pltpu.VMEM((cfg.rows, D_MODEL), BF16),
                   pltpu.VMEM((cfg.rows, ATTN_WIDTH), BF16),
                   pltpu.VMEM((N_KV_HEADS, cfg.rows + BLOCK, 2 * HEAD_DIM), BF16),
                   pltpu.VMEM((per_tile + 1, KV_WIDTH, BLOCK), BF16),
                   pltpu.VMEM((per_tile, N_HEADS, BLOCK, BLOCK), F32)]
        temps += (_nbytes(bias.shape, F32) + _nbytes((cfg.rows, ATTN_WIDTH), BF16) * 3
                  + _nbytes((per_tile, N_HEADS, BLOCK, BLOCK), F32)
                  + _nbytes((BLOCK, 2 * BLOCK), F32) * 16 + _nbytes((BLOCK, ATTN_WIDTH), F32) * 4)

    out_shape = [jax.ShapeDtypeStruct(x.shape[:-1] + (w,), dt) for w, dt in zip(widths, dtypes)]
    out_shape.append(jax.ShapeDtypeStruct(tail_shape, F32))
    out_specs = [cfg.act(w) for w in widths] + [tail_spec]
    if not cfg.sample:
        win = min(WINDOW, mid)
        out_shape[2] = jax.ShapeDtypeStruct((lead, win, 2 * KV_WIDTH), F32)
        out_specs[2] = pl.BlockSpec((None, win, 2 * KV_WIDTH), lambda b, t: (b, 0, 0))
    return pl.pallas_call(
        functools.partial(_in_proj_kernel, sample=cfg.sample),
        grid=cfg.grid,
        in_specs=x_specs + [cfg.mod_spec, cfg.layer(1, D_MODEL),
                            _dense(x, 0, n_seq, cfg.sample).layer(*w_in.shape[1:]),
                            cfg.layer(*conv_w.shape[1:])] + extra_specs,
        out_specs=out_specs,
        out_shape=out_shape,
        scratch_shapes=scratch,
        compiler_params=pltpu.CompilerParams(
            dimension_semantics=cfg.sem, vmem_limit_bytes=_vmem_limit(blocks, temps)),
        name="in_proj_sample" if cfg.sample else "in_proj_attn_prompt",
    )(*x_args, mod_all, g_pre1, w_in, conv_w, *extra_in)


def _sample_attn_kernel(q_ref, kc_ref, vc_ref, kn_ref, vn_ref, bias_ref, sink_ref, *rest,
                        layer_idx):
    o_ref, ko_ref, vo_ref = rest[-3:]
    if ko_ref.ndim > kc_ref.ndim:
        for other in range(ko_ref.shape[0]):
            if other != layer_idx:
                ko_ref[other] = jnp.zeros(ko_ref.shape[1:], F32)
                vo_ref[other] = jnp.zeros(vo_ref.shape[1:], F32)
        ko_ref, vo_ref = ko_ref.at[layer_idx], vo_ref.at[layer_idx]
    n_seq = q_ref.shape[0]
    buf = kc_ref.shape[-1]
    new = kn_ref.shape[1]
    rows = new * GROUP
    pos = lax.broadcasted_iota(jnp.int32, (rows, 2 * buf), 0) // GROUP
    kj = lax.broadcasted_iota(jnp.int32, (rows, 2 * buf), 1)
    dist = buf + pos - kj
    mask = (dist >= 0) & (dist < WINDOW) & (kj < buf + new)
    lane = lax.broadcasted_iota(jnp.int32, (n_seq, HEAD_DIM, buf), 2)
    pad = jnp.zeros((n_seq, HEAD_DIM, buf - new), F32)

    def extended(cache_t, fresh, out_ref, g):
        fresh_t = jnp.concatenate([jnp.swapaxes(fresh, 1, 2), pad], axis=2)
        out_ref[:, g] = pltpu.roll(jnp.where(lane < new, fresh_t, cache_t), buf - new, 2)
        return jnp.concatenate([cache_t, fresh_t], axis=2).astype(BF16)

    for g in range(N_KV_HEADS):
        lanes = slice(g * HEAD_DIM, (g + 1) * HEAD_DIM)
        k_ext = extended(kc_ref[:, g], kn_ref[:, :, lanes], ko_ref, g)
        v_ext = extended(vc_ref[:, g], vn_ref[:, :, lanes], vo_ref, g)
        s = jnp.einsum('nqd,ndk->nqk', q_ref[:, g], k_ext, preferred_element_type=F32)
        p = _sink_softmax(jnp.where(mask[None], s + bias_ref[g][None], NEG_INF),
                          sink_ref[g][None])
        o = jnp.einsum('nqk,ndk->nqd', p.astype(BF16), v_ext, preferred_element_type=F32)
        o_ref[:, g] = o.astype(BF16)


def _sample_attn(q, k_cache, v_cache, k_new, v_new, bias, sink_rows, layer_idx, windows):
    _, n, _, _, buf = k_cache.shape
    new = k_new.shape[1]
    rows = new * GROUP
    nt = SAMPLE_SEQ_TILE
    seq3 = lambda a, b: pl.BlockSpec((nt, a, b), lambda i: (i, 0, 0))
    cache = pl.BlockSpec((None, nt, N_KV_HEADS, HEAD_DIM, buf),
                         lambda i: (layer_idx, i, 0, 0, 0))
    q_spec = pl.BlockSpec((nt, N_KV_HEADS, rows, HEAD_DIM), lambda i: (i, 0, 0, 0))
    in_specs = [q_spec, cache, cache, seq3(new, KV_WIDTH), seq3(new, KV_WIDTH),
                pl.BlockSpec(bias.shape, lambda i: (0, 0, 0)),
                pl.BlockSpec(sink_rows.shape, lambda i: (0, 0, 0))]
    args = [q, k_cache, v_cache, k_new, v_new, bias, sink_rows]
    depth = k_cache.shape[0]
    if windows is None:
        aliases = {}
        window_out = pl.BlockSpec((depth, nt, N_KV_HEADS, HEAD_DIM, buf),
                                  lambda i: (0, i, 0, 0, 0))
    else:
        aliases = {len(args): 1, len(args) + 1: 2}
        in_specs += [pl.BlockSpec(memory_space=pl.ANY)] * 2
        args += list(windows)
        window_out = cache
    blocks = (_nbytes((nt, N_KV_HEADS, HEAD_DIM, buf), F32) * (2 + 2 * depth)
              + _nbytes((nt, new, KV_WIDTH), F32) * 2
              + _nbytes((nt, N_KV_HEADS, rows, 2 * HEAD_DIM), BF16) * 2 + _nbytes(bias.shape, F32))
    temps = _nbytes((rows, 2 * buf), F32) * 8 * nt
    return pl.pallas_call(
        functools.partial(_sample_attn_kernel, layer_idx=layer_idx),
        grid=(n // nt,),
        in_specs=in_specs,
        out_specs=[q_spec, window_out, window_out],
        out_shape=[jax.ShapeDtypeStruct(q.shape, BF16),
                   jax.ShapeDtypeStruct(k_cache.shape, F32),
                   jax.ShapeDtypeStruct(v_cache.shape, F32)],
        input_output_aliases=aliases,
        compiler_params=pltpu.CompilerParams(
            dimension_semantics=("arbitrary",), vmem_limit_bytes=_vmem_limit(blocks, temps)),
        name="attn_sample",
    )(*args)


def _merge_residual(x_ref, z_ref, a_ref, gate_ref, mod_ref, gpost1_ref, wbc_ref, wba_ref, wo_ref,
                    sample):
    y_conv = _dot(_flat(z_ref[...]), wbc_ref[...])
    y_attn = _dot(_flat(a_ref[...]), wba_ref[...])
    gates = _flat(gate_ref[...])
    merged = gates[:, :D_MODEL] * y_conv + gates[:, D_MODEL:] * y_attn
    mixed = _dot(merged.astype(BF16), wo_ref[...])
    scale = _mod(mod_ref, 2, sample) * gpost1_ref[...]
    return x_ref[...] + scale * _rms(mixed).reshape(x_ref.shape)


def _mlp_input(x1, mod_ref, gpre2_ref, sample):
    scale = gpre2_ref[...] * (1.0 + _mod(mod_ref, 4, sample))
    return _flat(_rms(x1) * scale + _mod(mod_ref, 3, sample)).astype(BF16)


def _mlp_residual(x1, ff, mod_ref, gpost2_ref, sample):
    scale = _mod(mod_ref, 5, sample) * gpost2_ref[...]
    return x1 + scale * _rms(ff).reshape(x1.shape)


def _post_kernel(x_ref, z_ref, a_ref, gate_ref, mod_ref, gpost1_ref, gpre2_ref, gpost2_ref,
                 wbc_ref, wba_ref, wo_ref, w1_ref, w2_ref, *rest, sample):
    if len(rest) == 4:
        cast_in_ref, o_ref, cast_out_ref, hid_ref = rest
        cast_out_ref[...] = cast_in_ref[...].astype(BF16)
    else:
        o_ref, hid_ref = rest
    x1 = _merge_residual(x_ref, z_ref, a_ref, gate_ref, mod_ref, gpost1_ref,
                         wbc_ref, wba_ref, wo_ref, sample)
    hb = _mlp_input(x1, mod_ref, gpre2_ref, sample)
    for lo in range(0, D_FF, 2 * CHUNK):
        act = jnp.maximum(_dot(hb, w1_ref[:, lo:lo + 2 * CHUNK]), 0.0)
        hid_ref[:, lo:lo + 2 * CHUNK] = (act * act).astype(BF16)
    ff = _dot(hid_ref[...], w2_ref[...])
    o_ref[...] = _mlp_residual(x1, ff, mod_ref, gpost2_ref, sample)


def _post(x, z, a, gates, mod_all, g_post1, g_pre2, g_post2, w_bc, w_ba, w_o, w1, w2,
          layer_idx, mlp_layer_idx, n_seq, cast_next=None):
    cfg = _dense(x, layer_idx, n_seq, False)
    mlp = _dense(x, mlp_layer_idx, n_seq, False)
    sq = (D_MODEL, D_MODEL)
    blocks = (_nbytes((cfg.rows, D_MODEL), F32) * 2 + _nbytes((cfg.rows, D_MODEL), BF16) * 2
              + _nbytes((cfg.rows, 2 * D_MODEL), gates.dtype))
    temps = (_nbytes(sq, BF16) * 3 + _nbytes(w1.shape[1:], BF16) * 2
             + _nbytes((cfg.rows, D_FF), BF16) + _nbytes((cfg.rows, 2 * CHUNK), F32) * 2
             + _nbytes((cfg.rows, D_MODEL), F32) * 5)
    vec = cfg.layer(1, D_MODEL)
    in_specs = [cfg.act(D_MODEL), cfg.act(CONV_DIM), cfg.act(ATTN_WIDTH),
                cfg.act(2 * D_MODEL), cfg.mod_spec, vec, vec, vec,
                cfg.layer(*sq), cfg.layer(*sq), cfg.layer(*sq),
                mlp.layer(*w1.shape[1:]), mlp.layer(*w2.shape[1:])]
    args = [x, z, a, gates, mod_all, g_post1, g_pre2, g_post2, w_bc, w_ba, w_o, w1, w2]
    out_specs = [cfg.act(D_MODEL)]
    out_shape = [jax.ShapeDtypeStruct(x.shape, F32)]
    if cast_next is not None:
        w_next, src_layer = cast_next
        _, k_dim, n_dim = w_next.shape
        last = n_dim // CAST_CHUNK - 1
        assert n_dim % CAST_CHUNK == 0 and last < cfg.grid[0] * cfg.grid[1]
        tiles = cfg.grid[1]
        in_specs.append(pl.BlockSpec(
            (None, k_dim, CAST_CHUNK),
            lambda b, t: (src_layer, 0, jnp.minimum(b * tiles + t, last))))
        out_specs.append(pl.BlockSpec(
            (None, k_dim, CAST_CHUNK), lambda b, t: (0, 0, jnp.minimum(b * tiles + t, last))))
        out_shape.append(jax.ShapeDtypeStruct((1, k_dim, n_dim), BF16))
        args.append(w_next)
        blocks += _nbytes((k_dim, CAST_CHUNK), F32) + _nbytes((k_dim, CAST_CHUNK), BF16)
    return pl.pallas_call(
        functools.partial(_post_kernel, sample=False),
        grid=cfg.grid,
        in_specs=in_specs,
        out_specs=out_specs,
        out_shape=out_shape,
        scratch_shapes=[pltpu.VMEM((cfg.rows, D_FF), BF16)],
        compiler_params=pltpu.CompilerParams(
            dimension_semantics=cfg.sem, vmem_limit_bytes=_vmem_limit(blocks, temps)),
        name="post_prompt",
    )(*args)


def _mix_kernel(x_ref, z_ref, a_ref, gate_ref, mod_ref, gpost1_ref, wbc_ref, wba_ref, wo_ref,
                o_ref):
    o_ref[...] = _merge_residual(x_ref, z_ref, a_ref, gate_ref, mod_ref, gpost1_ref,
                                 wbc_ref, wba_ref, wo_ref, True)


def _mix_sample(x, z, a, gates, mod_all, g_post1, w_bc, w_ba, w_o, layer_idx, n_seq):
    cfg = _dense(x, layer_idx, n_seq, True)
    sq = (D_MODEL, D_MODEL)
    blocks = (_nbytes((cfg.rows, D_MODEL), F32) * 2 + _nbytes((cfg.rows, D_MODEL), BF16) * 2
              + _nbytes((cfg.rows, 2 * D_MODEL), gates.dtype))
    temps = _nbytes(sq, BF16) * 3 + _nbytes((cfg.rows, D_MODEL), F32) * 6
    return pl.pallas_call(
        _mix_kernel,
        grid=cfg.grid,
        in_specs=[cfg.act(D_MODEL), cfg.act(CONV_DIM), cfg.act(ATTN_WIDTH),
                  cfg.act(2 * D_MODEL), cfg.mod_spec, cfg.layer(1, D_MODEL),
                  cfg.layer(*sq), cfg.layer(*sq), cfg.layer(*sq)],
        out_specs=cfg.act(D_MODEL),
        out_shape=jax.ShapeDtypeStruct(x.shape, F32),
        compiler_params=pltpu.CompilerParams(
            dimension_semantics=cfg.sem, vmem_limit_bytes=_vmem_limit(blocks, temps)),
        name="mix_sample",
    )(x, z, a, gates, mod_all, g_post1, w_bc, w_ba, w_o)


def _mlp_stream_kernel(x_ref, mod_ref, gpre2_ref, gpost2_ref, w1_ref, w2_ref,
                       o_ref, w1b_ref, w2b_ref, ff_ref):
    hb = _mlp_input(x_ref[...], mod_ref, gpre2_ref, True)
    w1 = w1_ref[...].astype(BF16)
    w2 = w2_ref[...].astype(BF16)
    w1b_ref[...] = w1
    w2b_ref[...] = w2
    act = jnp.maximum(_dot(hb, w1), 0.0)
    ff = jnp.where(pl.program_id(0) == 0, 0.0, ff_ref[...]) + _dot((act * act).astype(BF16), w2)
    ff_ref[...] = ff
    o_ref[...] = _mlp_residual(x_ref[...], ff, mod_ref, gpost2_ref, True)


def _mlp_stream_sample(x, mod_all, g_pre2, g_post2, w1, w2, layer_idx, n_seq):
    rows = x.shape[0] * x.shape[1]
    const = lambda shape, idx: pl.BlockSpec(shape, lambda c: idx, pipeline_mode=pl.Buffered(1))
    vec = const((None, 1, D_MODEL), (layer_idx, 0, 0))
    blocks = (_nbytes((D_MODEL, MLP_STREAM_CHUNK), F32) * 2
              + _nbytes((D_MODEL, MLP_STREAM_CHUNK), BF16) * 2 + _nbytes(x.shape, F32))
    temps = (_nbytes(x.shape, F32) * 4 + _nbytes((N_MOD, n_seq, D_MODEL), F32)
             + _nbytes((rows, D_MODEL), BF16) + _nbytes((rows, MLP_STREAM_CHUNK), F32) * 3)
    return pl.pallas_call(
        _mlp_stream_kernel,
        grid=(D_FF // MLP_STREAM_CHUNK,),
        in_specs=[const(x.shape, (0, 0, 0)),
                  const((None, N_MOD, n_seq, D_MODEL), (layer_idx, 0, 0, 0)), vec, vec,
                  pl.BlockSpec((None, D_MODEL, MLP_STREAM_CHUNK), lambda c: (layer_idx, 0, c)),
                  pl.BlockSpec((None, MLP_STREAM_CHUNK, D_MODEL), lambda c: (layer_idx, c, 0))],
        out_specs=[pl.BlockSpec(x.shape, lambda c: (0, 0, 0)),
                   pl.BlockSpec((None, D_MODEL, MLP_STREAM_CHUNK), lambda c: (0, 0, c)),
                   pl.BlockSpec((None, MLP_STREAM_CHUNK, D_MODEL), lambda c: (0, c, 0))],
        out_shape=[jax.ShapeDtypeStruct(x.shape, F32),
                   jax.ShapeDtypeStruct((1, D_MODEL, D_FF), BF16),
                   jax.ShapeDtypeStruct((1, D_FF, D_MODEL), BF16)],
        scratch_shapes=[pltpu.VMEM((rows, D_MODEL), F32)],
        compiler_params=pltpu.CompilerParams(
            dimension_semantics=("arbitrary",), vmem_limit_bytes=_vmem_limit(blocks, temps)),
        name="mlp_stream_sample",
    )(x, mod_all, g_pre2, g_post2, w1, w2)


def kernel(x_prompt, x_sample, c_prompt, c_sample, state_conv, cache_k, cache_v, w_ada, b_ada,
           g_pre1, w_in, conv_w, w_br_conv, w_br_attn, w_o, sinks, g_post1, g_pre2, w_ff1, w_ff2,
           g_post2, rel_table):
    depth = w_ada.shape[0]
    batch, seq, _ = x_prompt.shape
    n_seq, n_new, _ = x_sample.shape
    buf = cache_k.shape[2]

    pad = (-(n_seq + batch)) % V7X_SUBLANES
    c_all = jnp.concatenate([c_sample, c_prompt, jnp.zeros((pad, D_MODEL), F32)], axis=0)
    mod_all = _ada(c_all, w_ada, b_ada)

    dist_p = (jnp.arange(BLOCK)[:, None] + BLOCK) - jnp.arange(2 * BLOCK)[None, :]
    bucket_p = _rel_bucket(dist_p)
    upper = jnp.arange(BLOCK)[None, :] > jnp.arange(BLOCK)[:, None]
    bucket_merged = jnp.where(upper, bucket_p[:, :BLOCK], bucket_p[:, BLOCK:])
    bias_p = _bias_table(rel_table, bucket_merged.T)
    dist_s = (buf + jnp.arange(n_new))[:, None] - jnp.arange(buf + n_new)[None, :]
    bias_s = _bias_table(rel_table, _rel_bucket(dist_s))
    bias_s = (bias_s.reshape(N_KV_HEADS, GROUP, n_new, buf + n_new)
              .transpose(0, 2, 1, 3).reshape(N_KV_HEADS, n_new * GROUP, buf + n_new))
    bias_s = jnp.pad(bias_s, ((0, 0), (0, 0), (0, buf - n_new)))

    to_bf16 = lambda w: w.astype(BF16)
    w_bc_b, w_ba_b, w_o_b = map(to_bf16, (w_br_conv, w_br_attn, w_o))
    w_in_b = to_bf16(w_in[0:1])
    vec = lambda g: g.reshape(depth, 1, D_MODEL)
    g_pre1, g_post1, g_pre2, g_post2 = map(vec, (g_pre1, g_post1, g_pre2, g_post2))
    pre_s = state_conv.transpose(0, 2, 1, 3)
    cache_k = cache_k.transpose(0, 1, 3, 4, 2)
    cache_v = cache_v.transpose(0, 1, 3, 4, 2)

    xp = x_prompt
    xs = x_sample.transpose(1, 0, 2)
    conv_p, k_p, v_p, conv_s = [], [], [], []
    windows = None
    win = min(WINDOW, seq)
    for l in range(depth):
        z, q, kv, gates, tail = _in_proj(xs, mod_all, g_pre1, w_in_b, conv_w, l, n_seq, pre_s)
        q_s = (q.reshape(n_new, n_seq, N_KV_HEADS, GROUP, HEAD_DIM)
               .transpose(1, 2, 0, 3, 4).reshape(n_seq, N_KV_HEADS, n_new * GROUP, HEAD_DIM))
        kv_s = kv.transpose(1, 0, 2)
        sink_rows = jnp.tile(sinks[l].reshape(N_KV_HEADS, 1, GROUP),
                             (1, n_new, 1)).reshape(N_KV_HEADS, n_new * GROUP, 1)
        o, *windows = _sample_attn(q_s, cache_k, cache_v, kv_s[:, :, :KV_WIDTH],
                                   kv_s[:, :, KV_WIDTH:], bias_s, sink_rows, l, windows)
        attn = (o.reshape(n_seq, N_KV_HEADS, n_new, GROUP, HEAD_DIM)
                .transpose(2, 0, 1, 3, 4).reshape(n_new, n_seq, ATTN_WIDTH))
        x1 = _mix_sample(xs, z, attn, gates, mod_all, g_post1, w_bc_b, w_ba_b, w_o_b, l, n_seq)
        xs, w1_b, w2_b = _mlp_stream_sample(x1, mod_all, g_pre2, g_post2, w_ff1, w_ff2, l, n_seq)
        conv_s.append(tail.transpose(1, 0, 2))

        z, attn, kv, gates, tail = _in_proj(xp, mod_all, g_pre1, w_in_b, conv_w, l, n_seq,
                                            attn=(sinks, bias_p))
        cast_next = (w_in, l + 1) if l + 1 < depth else None
        xp, *w_next = _post(xp, z, attn, gates, mod_all, g_post1, g_pre2, g_post2, w_bc_b, w_ba_b,
                            w_o_b, w1_b, w2_b, l, 0, n_seq, cast_next)
        w_in_b = w_next[0] if w_next else None
        conv_p.append(tail[:, V7X_SUBLANES - (CONV_WIDTH - 1):])
        k_p.append(kv[:, :, :KV_WIDTH].reshape(batch, win, N_KV_HEADS, HEAD_DIM))
        v_p.append(kv[:, :, KV_WIDTH:].reshape(batch, win, N_KV_HEADS, HEAD_DIM))

    k_s, v_s = (w.transpose(0, 1, 4, 2, 3) for w in windows)
    return (xp, xs.transpose(1, 0, 2), jnp.stack(conv_p), jnp.stack(k_p), jnp.stack(v_p),
            jnp.stack(conv_s), k_s, v_s)
```

```python
import functools
import math
from typing import Any, Callable, NamedTuple

import jax
import jax.numpy as jnp
from jax import lax
from jax.experimental import pallas as pl
from jax.experimental.pallas import tpu as pltpu

D_MODEL = 1024
N_HEADS = 16
N_KV_HEADS = 2
HEAD_DIM = 64
GROUP = N_HEADS // N_KV_HEADS
ATTN_WIDTH = N_HEADS * HEAD_DIM
KV_WIDTH = N_KV_HEADS * HEAD_DIM
CONV_DIM = D_MODEL
CONV_WIDTH = 3
WINDOW = 128
BLOCK = 128
N_BUCKETS = 32
MAX_DISTANCE = 128
D_FF = 4 * D_MODEL
N_MOD = 6
RMS_EPS = 1e-6
NEG_INF = -1e30
PROJ_COLS = 3 * CONV_DIM + ATTN_WIDTH + 2 * KV_WIDTH + 2 * D_MODEL
Q_OFF = 3 * CONV_DIM
KV_OFF = Q_OFF + ATTN_WIDTH
GATE_OFF = KV_OFF + 2 * KV_WIDTH

V7X_SUBLANES = 8
V7X_VMEM_BYTES = 64 * 1024 * 1024

PROMPT_TILE = 512
SAMPLE_POS_TILE = 4
SAMPLE_SEQ_TILE = 32
CHUNK = 256
MLP_STREAM_CHUNK = 1024
CAST_CHUNK = 256

F32 = jnp.float32
BF16 = jnp.bfloat16


def _vmem_limit(block_bytes, temp_bytes):
    return int(min(2 * block_bytes + temp_bytes, V7X_VMEM_BYTES - 4 * 1024 * 1024))


def _nbytes(shape, dtype):
    return math.prod(shape) * jnp.dtype(dtype).itemsize


def _rms(x):
    return x * lax.rsqrt(jnp.mean(x * x, axis=-1, keepdims=True) + RMS_EPS)


def _dot(a, b):
    return jnp.dot(a, b, preferred_element_type=F32)


def _flat(a):
    return a.reshape(-1, a.shape[-1])


def _ada_kernel(c_ref, w_ref, b_ref, o_ref):
    c = c_ref[...]
    s = c * (1.0 / (1.0 + jnp.exp(-c)))
    o_ref[...] = _dot(s.astype(BF16), w_ref[...].astype(BF16)) + b_ref[...]


def _ada(c_all, w_ada, b_ada):
    depth = w_ada.shape[0]
    rows = c_all.shape[0]
    blocks = (_nbytes((rows, D_MODEL), F32) * 2 + _nbytes((D_MODEL, D_MODEL), F32))
    return pl.pallas_call(
        _ada_kernel,
        grid=(depth, N_MOD),
        in_specs=[
            pl.BlockSpec((rows, D_MODEL), lambda l, j: (0, 0)),
            pl.BlockSpec((None, D_MODEL, D_MODEL), lambda l, j: (l, 0, j)),
            pl.BlockSpec((None, None, 1, D_MODEL), lambda l, j: (l, j, 0, 0)),
        ],
        out_specs=pl.BlockSpec((None, None, rows, D_MODEL), lambda l, j: (l, j, 0, 0)),
        out_shape=jax.ShapeDtypeStruct((depth, N_MOD, rows, D_MODEL), F32),
        compiler_params=pltpu.CompilerParams(
            dimension_semantics=("arbitrary", "arbitrary"),
            vmem_limit_bytes=_vmem_limit(blocks, _nbytes((D_MODEL, D_MODEL), BF16) * 2)),
        name="ada_mod",
    )(c_all, w_ada, b_ada.reshape(depth, N_MOD, 1, D_MODEL))


def _bias_kernel(tab_ref, bucket_ref, o_ref):
    bucket = bucket_ref[...]
    for h in range(N_HEADS):
        acc = jnp.zeros(bucket.shape, F32)
        for b in range(N_BUCKETS):
            acc = jnp.where(bucket == b, tab_ref[b * N_HEADS + h], acc)
        o_ref[h] = acc


def _bias_table(rel_table, bucket):
    return pl.pallas_call(
        _bias_kernel,
        in_specs=[
            pl.BlockSpec(memory_space=pltpu.SMEM),
            pl.BlockSpec(bucket.shape, lambda: (0, 0)),
        ],
        out_specs=pl.BlockSpec((N_HEADS,) + bucket.shape, lambda: (0, 0, 0)),
        out_shape=jax.ShapeDtypeStruct((N_HEADS,) + bucket.shape, F32),
        name="bias_table",
    )(rel_table.reshape(-1), bucket)


def _rel_bucket(dist):
    n = jnp.maximum(dist, 0)
    max_exact = N_BUCKETS // 2
    nf = jnp.maximum(n, 1).astype(F32)
    scaled = (jnp.log(nf / max_exact) / math.log(MAX_DISTANCE / max_exact)
              * (N_BUCKETS - max_exact))
    large = jnp.minimum(max_exact + jnp.floor(scaled).astype(jnp.int32), N_BUCKETS - 1)
    return jnp.where(n < max_exact, n, large)


class _Dense(NamedTuple):
    sample: bool
    grid: tuple
    rows: int
    act: Callable[[int], Any]
    layer: Callable[..., Any]
    mod_spec: Any
    sem: tuple


def _dense(x, layer_idx, n_seq, sample):
    lead, mid = x.shape[0], x.shape[1]
    if sample:
        grid = (lead // SAMPLE_POS_TILE,)
        act = lambda w: pl.BlockSpec((SAMPLE_POS_TILE, mid, w), lambda i: (i, 0, 0))
        layer = lambda *shape: pl.BlockSpec((None,) + shape,
                                            lambda i: (layer_idx,) + (0,) * len(shape),
                                            pipeline_mode=pl.Buffered(1))
        mod_spec = pl.BlockSpec((None, N_MOD, n_seq, D_MODEL), lambda i: (layer_idx, 0, 0, 0))
        return _Dense(True, grid, SAMPLE_POS_TILE * mid, act, layer, mod_spec, ("arbitrary",))
    assert n_seq % V7X_SUBLANES == 0 and lead <= V7X_SUBLANES
    grid = (lead, mid // PROMPT_TILE)
    act = lambda w: pl.BlockSpec((None, PROMPT_TILE, w), lambda b, t: (b, t, 0))
    layer = lambda *shape: pl.BlockSpec((None,) + shape,
                                        lambda b, t: (layer_idx,) + (0,) * len(shape),
                                        pipeline_mode=pl.Buffered(1))
    mod_spec = pl.BlockSpec((None, N_MOD, V7X_SUBLANES, D_MODEL),
                            lambda b, t: (layer_idx, 0, n_seq // V7X_SUBLANES, 0))
    return _Dense(False, grid, PROMPT_TILE, act, layer, mod_spec, ("arbitrary", "arbitrary"))


def _mod(mod_ref, j, sample, row=None):
    if sample:
        return mod_ref[j]
    return mod_ref[j, pl.ds(pl.program_id(0) if row is None else row, 1), :]


def _sink_softmax(s, sink):
    m = jnp.maximum(jnp.max(s, axis=-1, keepdims=True), sink)
    e = jnp.exp(s - m)
    den = jnp.sum(e, axis=-1, keepdims=True) + jnp.exp(sink - m)
    return e * (1.0 / den)


def _stage_keys(kv, kd_ref, vt_ref, first_tile):
    n_blk = kv.shape[0] // BLOCK

    kd_ref[:, 0:BLOCK] = jnp.where(first_tile, jnp.zeros((), BF16), kd_ref[:, n_blk * BLOCK:])
    vt_ref[0] = jnp.where(first_tile, jnp.zeros((), BF16), vt_ref[n_blk])

    for g in range(N_KV_HEADS):
        k_g = kv[:, g * HEAD_DIM:(g + 1) * HEAD_DIM]
        kd_ref[g, BLOCK:] = jnp.concatenate([k_g, k_g], axis=1).astype(BF16)
    v_t = kv[:, KV_WIDTH:].T.astype(BF16)
    for blk in range(n_blk):
        vt_ref[blk + 1] = v_t[:, blk * BLOCK:(blk + 1) * BLOCK]


def _attention_phases(q_ref, kd_ref, vt_ref, s_ref, bias_ref, sink_ref, o_ref, first_tile):
    kj = lax.broadcasted_iota(jnp.int32, (BLOCK, BLOCK), 0)
    qi = lax.broadcasted_iota(jnp.int32, (BLOCK, BLOCK), 1)
    upper = kj > qi
    low_lanes = lax.broadcasted_iota(jnp.int32, (BLOCK, 2 * HEAD_DIM), 1) < HEAD_DIM

    def score_phase(c):
        q = q_ref[c * BLOCK:(c + 1) * BLOCK, :]
        for pair in range(N_HEADS // 2):
            g = (2 * pair) // GROUP
            q_pair = q[:, pair * 2 * HEAD_DIM:(pair + 1) * 2 * HEAD_DIM]
            zero = jnp.zeros_like(q_pair)
            q_both = jnp.concatenate([jnp.where(low_lanes, q_pair, zero),
                                      jnp.where(low_lanes, zero, q_pair)], axis=0)
            s_both = lax.dot_general(kd_ref[g, c * BLOCK:(c + 2) * BLOCK, :], q_both,
                                     (((1,), (1,)), ((), ())), preferred_element_type=F32)
            for par in range(2):
                h = 2 * pair + par
                s_h = s_both[:, par * BLOCK:(par + 1) * BLOCK]
                s = jnp.where(upper, s_h[:BLOCK], s_h[BLOCK:]) + bias_ref[h]
                if c == 0:
                    s = jnp.where(upper & first_tile, NEG_INF, s)
                s_ref[c, h] = s

    def value_phase(c, pair):
        g = (2 * pair) // GROUP
        v_cat = jnp.concatenate([vt_ref[c, g * HEAD_DIM:(g + 1) * HEAD_DIM],
                                 vt_ref[c + 1, g * HEAD_DIM:(g + 1) * HEAD_DIM]], axis=1)
        weights, scales = [], []
        for h in (2 * pair, 2 * pair + 1):
            s = s_ref[c, h]
            sink = sink_ref[h]
            m = jnp.maximum(jnp.max(s, axis=0, keepdims=True), sink)
            e = jnp.exp(s - m)
            den = jnp.sum(e, axis=0, keepdims=True) + jnp.exp(sink - m)
            e_b = e.astype(BF16)
            zero = jnp.zeros_like(e_b)
            weights.append(jnp.concatenate([jnp.where(upper, e_b, zero),
                                            jnp.where(upper, zero, e_b)], axis=0))
            scales.append(1.0 / den)
        o_t = _dot(v_cat, jnp.concatenate(weights, axis=1))
        o_t = jnp.concatenate([o_t[:, :BLOCK] * scales[0], o_t[:, BLOCK:] * scales[1]], axis=0)
        o_ref[c * BLOCK:(c + 1) * BLOCK, pair * 2 * HEAD_DIM:(pair + 1) * 2 * HEAD_DIM] = (
            o_t.T.astype(BF16))

    return score_phase, value_phase


def _in_proj_kernel(*refs, sample):
    if sample:
        (x_ref, mod_ref, g_ref, w_ref, cw_ref, pre_ref,
         z_ref, q_ref, kv_ref, gate_ref, tail_ref, carry_ref) = refs
    else:
        (x_ref, xn_ref, mod_ref, g_ref, w_ref, cw_ref, sink_ref, bias_ref,
         z_ref, a_ref, kv_ref, gate_ref, tail_ref,
         carry_ref, hn_ref, q_ref, kd_ref, vt_ref, s_ref) = refs
    step = pl.program_id(0) if sample else pl.program_id(1)
    first_tile = step == 0

    def normed(x, row=None):
        scale = g_ref[...] * (1.0 + _mod(mod_ref, 1, sample, row))
        return _flat(_rms(x) * scale + _mod(mod_ref, 0, sample, row)).astype(BF16)

    if sample:
        hb = normed(x_ref[...])
    else:
        @pl.when(first_tile & (pl.program_id(0) == 0))
        def _():
            hn_ref[...] = normed(x_ref[...])

        hb = hn_ref[...]
    rows = hb.shape[0]

    def store(ref, lo, val):
        ref[..., lo:lo + val.shape[-1]] = val.reshape(ref.shape[:-1] + (val.shape[-1],))

    def conv_chunk(lo):
        b_g = _dot(hb, w_ref[:, lo:lo + CHUNK])
        c_g = _dot(hb, w_ref[:, CONV_DIM + lo:CONV_DIM + lo + CHUNK])
        x_c = _dot(hb, w_ref[:, 2 * CONV_DIM + lo:2 * CONV_DIM + lo + CHUNK])
        u = c_g * x_c
        if sample:
            pos = x_ref.shape[0]
            u3 = u.reshape(pos, -1, CHUNK)
            prev = jnp.where(first_tile, pre_ref[:, :, lo:lo + CHUNK],
                             carry_ref[:, :, lo:lo + CHUNK])
            ext = jnp.concatenate([prev, u3], axis=0)
            u2 = _flat(ext[0:pos])
            u1 = _flat(ext[1:pos + 1])
            new_tail = u3[pos - (CONV_WIDTH - 1):]
            carry_ref[:, :, lo:lo + CHUNK] = new_tail
            tail_ref[:, :, lo:lo + CHUNK] = new_tail
        else:
            prev = jnp.where(first_tile, 0.0, carry_ref[:, lo:lo + CHUNK])
            p1 = prev[V7X_SUBLANES - 1:V7X_SUBLANES]
            p2 = prev[V7X_SUBLANES - 2:V7X_SUBLANES - 1]
            r = lax.broadcasted_iota(jnp.int32, u.shape, 0)
            u1 = jnp.where(r == 0, p1, pltpu.roll(u, 1, 0))
            u2 = jnp.where(r == 0, p2, jnp.where(r == 1, p1, pltpu.roll(u, 2, 0)))
            new_tail = u[rows - V7X_SUBLANES:]
            carry_ref[:, lo:lo + CHUNK] = new_tail
            tail_ref[:, lo:lo + CHUNK] = new_tail
        conv = (cw_ref[0:1, lo:lo + CHUNK] * u2 + cw_ref[1:2, lo:lo + CHUNK] * u1
                + cw_ref[2:3, lo:lo + CHUNK] * u)
        store(z_ref, lo, (b_g * conv).astype(BF16))

    def q_chunk(lo):
        q = _dot(hb, w_ref[:, Q_OFF + lo:Q_OFF + lo + CHUNK])
        store(q_ref, lo, (q * (HEAD_DIM ** -0.5)).astype(BF16))

    def gate_chunk(lo):
        g = _dot(hb, w_ref[:, GATE_OFF + lo:GATE_OFF + lo + CHUNK])
        store(gate_ref, lo, (1.0 / (1.0 + jnp.exp(-g))).astype(BF16))

    kv = _dot(hb, w_ref[:, KV_OFF:KV_OFF + 2 * KV_WIDTH])
    if sample:
        store(kv_ref, 0, kv)
    else:
        kv_ref[...] = kv[rows - kv_ref.shape[0]:]
    for lo in range(0, ATTN_WIDTH, CHUNK):
        q_chunk(lo)
    dense = ([functools.partial(conv_chunk, lo) for lo in range(0, CONV_DIM, CHUNK)]
             + [functools.partial(gate_chunk, lo) for lo in range(0, 2 * D_MODEL, CHUNK)])
    if sample:
        for work in dense:
            work()
        return

    _stage_keys(kv, kd_ref, vt_ref, first_tile)
    last_tile = step == pl.num_programs(1) - 1
    next_row = jnp.minimum(pl.program_id(0) + last_tile.astype(jnp.int32),
                           pl.num_programs(0) - 1)
    hn_ref[...] = normed(xn_ref[...], next_row)
    score_phase, value_phase = _attention_phases(q_ref, kd_ref, vt_ref, s_ref, bias_ref,
                                                 sink_ref, a_ref, first_tile)
    n_blk = rows // BLOCK
    attention = [functools.partial(score_phase, 0)]
    for c in range(n_blk):
        if c + 1 < n_blk:
            attention.append(functools.partial(score_phase, c + 1))
        attention += [functools.partial(value_phase, c, pair) for pair in range(N_HEADS // 2)]
    done = 0
    for i, work in enumerate(attention):
        work()
        while done < len(dense) and done * len(attention) < (i + 1) * len(dense):
            dense[done]()
            done += 1


def _in_proj(x, mod_all, g_pre1, w_in, conv_w, layer_idx, n_seq, pre=None, attn=None):
    cfg = _dense(x, layer_idx, n_seq, pre is not None)
    lead, mid = x.shape[0], x.shape[1]
    widths = (CONV_DIM, ATTN_WIDTH, 2 * KV_WIDTH, 2 * D_MODEL)
    dtypes = (BF16, BF16, F32, BF16)
    blocks = (_nbytes((cfg.rows, D_MODEL), F32) + _nbytes(w_in.shape[1:], BF16)
              + sum(_nbytes((cfg.rows, w), dt) for w, dt in zip(widths, dtypes)))
    temps = _nbytes((cfg.rows, D_MODEL), F32) * 2 + _nbytes((cfg.rows, CHUNK), F32) * 10
    if cfg.sample:
        tail_shape = (CONV_WIDTH - 1, mid, CONV_DIM)
        tail_spec = pl.BlockSpec(tail_shape, lambda i: (0, 0, 0))
        extra_in = [pre]
        extra_specs = [cfg.layer(*tail_shape)]
        scratch = [pltpu.VMEM(tail_shape, F32)]
        x_specs, x_args = [cfg.act(D_MODEL)], [x]
    else:
        sinks, bias = attn
        per_tile = cfg.rows // BLOCK
        tail_shape = (lead, V7X_SUBLANES, CONV_DIM)
        tail_spec = pl.BlockSpec((None, V7X_SUBLANES, CONV_DIM), lambda b, t: (b, 0, 0))
        tiles = mid // cfg.rows
        next_tile = pl.BlockSpec(
            (None, cfg.rows, D_MODEL),
            lambda b, t: (jnp.minimum(b + (t + 1) // tiles, lead - 1), (t + 1) % tiles, 0))
        first_tile_only = pl.BlockSpec((None, cfg.rows, D_MODEL), lambda b, t: (0, 0, 0),
                                       pipeline_mode=pl.Buffered(1))
        x_specs, x_args = [first_tile_only, next_tile], [x, x]
        extra_in = [sinks[layer_idx], bias]
        extra_specs = [pl.BlockSpec(memory_space=pltpu.SMEM),
                       pl.BlockSpec(bias.shape, lambda b, t: (0, 0, 0),
                                    pipeline_mode=pl.Buffered(1))]
        scratch = [pltpu.VMEM((V7X_SUBLANES, CONV_DIM), F32),
                   pltpu.VMEM((cfg.rows, D_MODEL), BF16),
                   pltpu.VMEM((cfg.rows, ATTN_WIDTH), BF16),
                   pltpu.VMEM((N_KV_HEADS, cfg.rows + BLOCK, 2 * HEAD_DIM), BF16),
                   pltpu.VMEM((per_tile + 1, KV_WIDTH, BLOCK), BF16),
                   pltpu.VMEM((per_tile, N_HEADS, BLOCK, BLOCK), F32)]
        temps += (_nbytes(bias.shape, F32) + _nbytes((cfg.rows, ATTN_WIDTH), BF16) * 3
                  + _nbytes((per_tile, N_HEADS, BLOCK, BLOCK), F32)
                  + _nbytes((BLOCK, 2 * BLOCK), F32) * 16 + _nbytes((BLOCK, ATTN_WIDTH), F32) * 4)

    out_shape = [jax.ShapeDtypeStruct(x.shape[:-1] + (w,), dt) for w, dt in zip(widths, dtypes)]
    out_shape.append(jax.ShapeDtypeStruct(tail_shape, F32))
    out_specs = [cfg.act(w) for w in widths] + [tail_spec]
    if not cfg.sample:
        win = min(WINDOW, mid)
        out_shape[2] = jax.ShapeDtypeStruct((lead, win, 2 * KV_WIDTH), F32)
        out_specs[2] = pl.BlockSpec((None, win, 2 * KV_WIDTH), lambda b, t: (b, 0, 0))
    return pl.pallas_call(
        functools.partial(_in_proj_kernel, sample=cfg.sample),
        grid=cfg.grid,
        in_specs=x_specs + [cfg.mod_spec, cfg.layer(1, D_MODEL),
                            _dense(x, 0, n_seq, cfg.sample).layer(*w_in.shape[1:]),
                            cfg.layer(*conv_w.shape[1:])] + extra_specs,
        out_specs=out_specs,
        out_shape=out_shape,
        scratch_shapes=scratch,
        compiler_params=pltpu.CompilerParams(
            dimension_semantics=cfg.sem, vmem_limit_bytes=_vmem_limit(blocks, temps)),
        name="in_proj_sample" if cfg.sample else "in_proj_attn_prompt",
    )(*x_args, mod_all, g_pre1, w_in, conv_w, *extra_in)


def _sample_attn_kernel(q_ref, kc_ref, vc_ref, kn_ref, vn_ref, bias_ref, sink_ref, *rest,
                        layer_idx):
    o_ref, ko_ref, vo_ref = rest[-3:]
    if ko_ref.ndim > kc_ref.ndim:
        for other in range(ko_ref.shape[0]):
            if other != layer_idx:
                ko_ref[other] = jnp.zeros(ko_ref.shape[1:], F32)
                vo_ref[other] = jnp.zeros(vo_ref.shape[1:], F32)
        ko_ref, vo_ref = ko_ref.at[layer_idx], vo_ref.at[layer_idx]
    n_seq = q_ref.shape[0]
    buf = kc_ref.shape[-1]
    new = kn_ref.shape[1]
    rows = new * GROUP
    pos = lax.broadcasted_iota(jnp.int32, (rows, 2 * buf), 0) // GROUP
    kj = lax.broadcasted_iota(jnp.int32, (rows, 2 * buf), 1)
    dist = buf + pos - kj
    mask = (dist >= 0) & (dist < WINDOW) & (kj < buf + new)
    lane = lax.broadcasted_iota(jnp.int32, (n_seq, HEAD_DIM, buf), 2)
    pad = jnp.zeros((n_seq, HEAD_DIM, buf - new), F32)

    def extended(cache_t, fresh, out_ref, g):
        fresh_t = jnp.concatenate([jnp.swapaxes(fresh, 1, 2), pad], axis=2)
        out_ref[:, g] = pltpu.roll(jnp.where(lane < new, fresh_t, cache_t), buf - new, 2)
        return jnp.concatenate([cache_t, fresh_t], axis=2).astype(BF16)

    for g in range(N_KV_HEADS):
        lanes = slice(g * HEAD_DIM, (g + 1) * HEAD_DIM)
        k_ext = extended(kc_ref[:, g], kn_ref[:, :, lanes], ko_ref, g)
        v_ext = extended(vc_ref[:, g], vn_ref[:, :, lanes], vo_ref, g)
        s = jnp.einsum('nqd,ndk->nqk', q_ref[:, g], k_ext, preferred_element_type=F32)
        p = _sink_softmax(jnp.where(mask[None], s + bias_ref[g][None], NEG_INF),
                          sink_ref[g][None])
        o = jnp.einsum('nqk,ndk->nqd', p.astype(BF16), v_ext, preferred_element_type=F32)
        o_ref[:, g] = o.astype(BF16)


def _sample_attn(q, k_cache, v_cache, k_new, v_new, bias, sink_rows, layer_idx, windows):
    _, n, _, _, buf = k_cache.shape
    new = k_new.shape[1]
    rows = new * GROUP
    nt = SAMPLE_SEQ_TILE
    seq3 = lambda a, b: pl.BlockSpec((nt, a, b), lambda i: (i, 0, 0))
    cache = pl.BlockSpec((None, nt, N_KV_HEADS, HEAD_DIM, buf),
                         lambda i: (layer_idx, i, 0, 0, 0))
    q_spec = pl.BlockSpec((nt, N_KV_HEADS, rows, HEAD_DIM), lambda i: (i, 0, 0, 0))
    in_specs = [q_spec, cache, cache, seq3(new, KV_WIDTH), seq3(new, KV_WIDTH),
                pl.BlockSpec(bias.shape, lambda i: (0, 0, 0)),
                pl.BlockSpec(sink_rows.shape, lambda i: (0, 0, 0))]
    args = [q, k_cache, v_cache, k_new, v_new, bias, sink_rows]
    depth = k_cache.shape[0]
    if windows is None:
        aliases = {}
        window_out = pl.BlockSpec((depth, nt, N_KV_HEADS, HEAD_DIM, buf),
                                  lambda i: (0, i, 0, 0, 0))
    else:
        aliases = {len(args): 1, len(args) + 1: 2}
        in_specs += [pl.BlockSpec(memory_space=pl.ANY)] * 2
        args += list(windows)
        window_out = cache
    blocks = (_nbytes((nt, N_KV_HEADS, HEAD_DIM, buf), F32) * (2 + 2 * depth)
              + _nbytes((nt, new, KV_WIDTH), F32) * 2
              + _nbytes((nt, N_KV_HEADS, rows, 2 * HEAD_DIM), BF16) * 2 + _nbytes(bias.shape, F32))
    temps = _nbytes((rows, 2 * buf), F32) * 8 * nt
    return pl.pallas_call(
        functools.partial(_sample_attn_kernel, layer_idx=layer_idx),
        grid=(n // nt,),
        in_specs=in_specs,
        out_specs=[q_spec, window_out, window_out],
        out_shape=[jax.ShapeDtypeStruct(q.shape, BF16),
                   jax.ShapeDtypeStruct(k_cache.shape, F32),
                   jax.ShapeDtypeStruct(v_cache.shape, F32)],
        input_output_aliases=aliases,
        compiler_params=pltpu.CompilerParams(
            dimension_semantics=("arbitrary",), vmem_limit_bytes=_vmem_limit(blocks, temps)),
        name="attn_sample",
    )(*args)


def _merge_residual(x_ref, z_ref, a_ref, gate_ref, mod_ref, gpost1_ref, wbc_ref, wba_ref, wo_ref,
                    sample):
    y_conv = _dot(_flat(z_ref[...]), wbc_ref[...])
    y_attn = _dot(_flat(a_ref[...]), wba_ref[...])
    gates = _flat(gate_ref[...])
    merged = gates[:, :D_MODEL] * y_conv + gates[:, D_MODEL:] * y_attn
    mixed = _dot(merged.astype(BF16), wo_ref[...])
    scale = _mod(mod_ref, 2, sample) * gpost1_ref[...]
    return x_ref[...] + scale * _rms(mixed).reshape(x_ref.shape)


def _mlp_input(x1, mod_ref, gpre2_ref, sample):
    scale = gpre2_ref[...] * (1.0 + _mod(mod_ref, 4, sample))
    return _flat(_rms(x1) * scale + _mod(mod_ref, 3, sample)).astype(BF16)


def _mlp_residual(x1, ff, mod_ref, gpost2_ref, sample):
    scale = _mod(mod_ref, 5, sample) * gpost2_ref[...]
    return x1 + scale * _rms(ff).reshape(x1.shape)


def _post_kernel(x_ref, z_ref, a_ref, gate_ref, mod_ref, gpost1_ref, gpre2_ref, gpost2_ref,
                 wbc_ref, wba_ref, wo_ref, w1_ref, w2_ref, *rest, sample):
    if len(rest) == 4:
        cast_in_ref, o_ref, cast_out_ref, hid_ref = rest
        cast_out_ref[...] = cast_in_ref[...].astype(BF16)
    else:
        o_ref, hid_ref = rest
    x1 = _merge_residual(x_ref, z_ref, a_ref, gate_ref, mod_ref, gpost1_ref,
                         wbc_ref, wba_ref, wo_ref, sample)
    hb = _mlp_input(x1, mod_ref, gpre2_ref, sample)
    for lo in range(0, D_FF, 2 * CHUNK):
        act = jnp.maximum(_dot(hb, w1_ref[:, lo:lo + 2 * CHUNK]), 0.0)
        hid_ref[:, lo:lo + 2 * CHUNK] = (act * act).astype(BF16)
    ff = _dot(hid_ref[...], w2_ref[...])
    o_ref[...] = _mlp_residual(x1, ff, mod_ref, gpost2_ref, sample)


def _post(x, z, a, gates, mod_all, g_post1, g_pre2, g_post2, w_bc, w_ba, w_o, w1, w2,
          layer_idx, mlp_layer_idx, n_seq, cast_next=None):
    cfg = _dense(x, layer_idx, n_seq, False)
    mlp = _dense(x, mlp_layer_idx, n_seq, False)
    sq = (D_MODEL, D_MODEL)
    blocks = (_nbytes((cfg.rows, D_MODEL), F32) * 2 + _nbytes((cfg.rows, D_MODEL), BF16) * 2
              + _nbytes((cfg.rows, 2 * D_MODEL), gates.dtype))
    temps = (_nbytes(sq, BF16) * 3 + _nbytes(w1.shape[1:], BF16) * 2
             + _nbytes((cfg.rows, D_FF), BF16) + _nbytes((cfg.rows, 2 * CHUNK), F32) * 2
             + _nbytes((cfg.rows, D_MODEL), F32) * 5)
    vec = cfg.layer(1, D_MODEL)
    in_specs = [cfg.act(D_MODEL), cfg.act(CONV_DIM), cfg.act(ATTN_WIDTH),
                cfg.act(2 * D_MODEL), cfg.mod_spec, vec, vec, vec,
                cfg.layer(*sq), cfg.layer(*sq), cfg.layer(*sq),
                mlp.layer(*w1.shape[1:]), mlp.layer(*w2.shape[1:])]
    args = [x, z, a, gates, mod_all, g_post1, g_pre2, g_post2, w_bc, w_ba, w_o, w1, w2]
    out_specs = [cfg.act(D_MODEL)]
    out_shape = [jax.ShapeDtypeStruct(x.shape, F32)]
    if cast_next is not None:
        w_next, src_layer = cast_next
        _, k_dim, n_dim = w_next.shape
        last = n_dim // CAST_CHUNK - 1
        assert n_dim % CAST_CHUNK == 0 and last < cfg.grid[0] * cfg.grid[1]
        tiles = cfg.grid[1]
        in_specs.append(pl.BlockSpec(
            (None, k_dim, CAST_CHUNK),
            lambda b, t: (src_layer, 0, jnp.minimum(b * tiles + t, last))))
        out_specs.append(pl.BlockSpec(
            (None, k_dim, CAST_CHUNK), lambda b, t: (0, 0, jnp.minimum(b * tiles + t, last))))
        out_shape.append(jax.ShapeDtypeStruct((1, k_dim, n_dim), BF16))
        args.append(w_next)
        blocks += _nbytes((k_dim, CAST_CHUNK), F32) + _nbytes((k_dim, CAST_CHUNK), BF16)
    return pl.pallas_call(
        functools.partial(_post_kernel, sample=False),
        grid=cfg.grid,
        in_specs=in_specs,
        out_specs=out_specs,
        out_shape=out_shape,
        scratch_shapes=[pltpu.VMEM((cfg.rows, D_FF), BF16)],
        compiler_params=pltpu.CompilerParams(
            dimension_semantics=cfg.sem, vmem_limit_bytes=_vmem_limit(blocks, temps)),
        name="post_prompt",
    )(*args)


def _mix_kernel(x_ref, z_ref, a_ref, gate_ref, mod_ref, gpost1_ref, wbc_ref, wba_ref, wo_ref,
                o_ref):
    o_ref[...] = _merge_residual(x_ref, z_ref, a_ref, gate_ref, mod_ref, gpost1_ref,
                                 wbc_ref, wba_ref, wo_ref, True)


def _mix_sample(x, z, a, gates, mod_all, g_post1, w_bc, w_ba, w_o, layer_idx, n_seq):
    cfg = _dense(x, layer_idx, n_seq, True)
    sq = (D_MODEL, D_MODEL)
    blocks = (_nbytes((cfg.rows, D_MODEL), F32) * 2 + _nbytes((cfg.rows, D_MODEL), BF16) * 2
              + _nbytes((cfg.rows, 2 * D_MODEL), gates.dtype))
    temps = _nbytes(sq, BF16) * 3 + _nbytes((cfg.rows, D_MODEL), F32) * 6
    return pl.pallas_call(
        _mix_kernel,
        grid=cfg.grid,
        in_specs=[cfg.act(D_MODEL), cfg.act(CONV_DIM), cfg.act(ATTN_WIDTH),
                  cfg.act(2 * D_MODEL), cfg.mod_spec, cfg.layer(1, D_MODEL),
                  cfg.layer(*sq), cfg.layer(*sq), cfg.layer(*sq)],
        out_specs=cfg.act(D_MODEL),
        out_shape=jax.ShapeDtypeStruct(x.shape, F32),
        compiler_params=pltpu.CompilerParams(
            dimension_semantics=cfg.sem, vmem_limit_bytes=_vmem_limit(blocks, temps)),
        name="mix_sample",
    )(x, z, a, gates, mod_all, g_post1, w_bc, w_ba, w_o)


def _mlp_stream_kernel(x_ref, mod_ref, gpre2_ref, gpost2_ref, w1_ref, w2_ref,
                       o_ref, w1b_ref, w2b_ref, ff_ref):
    hb = _mlp_input(x_ref[...], mod_ref, gpre2_ref, True)
    w1 = w1_ref[...].astype(BF16)
    w2 = w2_ref[...].astype(BF16)
    w1b_ref[...] = w1
    w2b_ref[...] = w2
    act = jnp.maximum(_dot(hb, w1), 0.0)
    ff = jnp.where(pl.program_id(0) == 0, 0.0, ff_ref[...]) + _dot((act * act).astype(BF16), w2)
    ff_ref[...] = ff
    o_ref[...] = _mlp_residual(x_ref[...], ff, mod_ref, gpost2_ref, True)


def _mlp_stream_sample(x, mod_all, g_pre2, g_post2, w1, w2, layer_idx, n_seq):
    rows = x.shape[0] * x.shape[1]
    const = lambda shape, idx: pl.BlockSpec(shape, lambda c: idx, pipeline_mode=pl.Buffered(1))
    vec = const((None, 1, D_MODEL), (layer_idx, 0, 0))
    blocks = (_nbytes((D_MODEL, MLP_STREAM_CHUNK), F32) * 2
              + _nbytes((D_MODEL, MLP_STREAM_CHUNK), BF16) * 2 + _nbytes(x.shape, F32))
    temps = (_nbytes(x.shape, F32) * 4 + _nbytes((N_MOD, n_seq, D_MODEL), F32)
             + _nbytes((rows, D_MODEL), BF16) + _nbytes((rows, MLP_STREAM_CHUNK), F32) * 3)
    return pl.pallas_call(
        _mlp_stream_kernel,
        grid=(D_FF // MLP_STREAM_CHUNK,),
        in_specs=[const(x.shape, (0, 0, 0)),
                  const((None, N_MOD, n_seq, D_MODEL), (layer_idx, 0, 0, 0)), vec, vec,
                  pl.BlockSpec((None, D_MODEL, MLP_STREAM_CHUNK), lambda c: (layer_idx, 0, c)),
                  pl.BlockSpec((None, MLP_STREAM_CHUNK, D_MODEL), lambda c: (layer_idx, c, 0))],
        out_specs=[pl.BlockSpec(x.shape, lambda c: (0, 0, 0)),
                   pl.BlockSpec((None, D_MODEL, MLP_STREAM_CHUNK), lambda c: (0, 0, c)),
                   pl.BlockSpec((None, MLP_STREAM_CHUNK, D_MODEL), lambda c: (0, c, 0))],
        out_shape=[jax.ShapeDtypeStruct(x.shape, F32),
                   jax.ShapeDtypeStruct((1, D_MODEL, D_FF), BF16),
                   jax.ShapeDtypeStruct((1, D_FF, D_MODEL), BF16)],
        scratch_shapes=[pltpu.VMEM((rows, D_MODEL), F32)],
        compiler_params=pltpu.CompilerParams(
            dimension_semantics=("arbitrary",), vmem_limit_bytes=_vmem_limit(blocks, temps)),
        name="mlp_stream_sample",
    )(x, mod_all, g_pre2, g_post2, w1, w2)


def kernel(x_prompt, x_sample, c_prompt, c_sample, state_conv, cache_k, cache_v, w_ada, b_ada,
           g_pre1, w_in, conv_w, w_br_conv, w_br_attn, w_o, sinks, g_post1, g_pre2, w_ff1, w_ff2,
           g_post2, rel_table):
    depth = w_ada.shape[0]
    batch, seq, _ = x_prompt.shape
    n_seq, n_new, _ = x_sample.shape
    buf = cache_k.shape[2]

    pad = (-(n_seq + batch)) % V7X_SUBLANES
    c_all = jnp.concatenate([c_sample, c_prompt, jnp.zeros((pad, D_MODEL), F32)], axis=0)
    mod_all = _ada(c_all, w_ada, b_ada)

    dist_p = (jnp.arange(BLOCK)[:, None] + BLOCK) - jnp.arange(2 * BLOCK)[None, :]
    bucket_p = _rel_bucket(dist_p)
    upper = jnp.arange(BLOCK)[None, :] > jnp.arange(BLOCK)[:, None]
    bucket_merged = jnp.where(upper, bucket_p[:, :BLOCK], bucket_p[:, BLOCK:])
    bias_p = _bias_table(rel_table, bucket_merged.T)
    dist_s = (buf + jnp.arange(n_new))[:, None] - jnp.arange(buf + n_new)[None, :]
    bias_s = _bias_table(rel_table, _rel_bucket(dist_s))
    bias_s = (bias_s.reshape(N_KV_HEADS, GROUP, n_new, buf + n_new)
              .transpose(0, 2, 1, 3).reshape(N_KV_HEADS, n_new * GROUP, buf + n_new))
    bias_s = jnp.pad(bias_s, ((0, 0), (0, 0), (0, buf - n_new)))

    to_bf16 = lambda w: w.astype(BF16)
    w_bc_b, w_ba_b, w_o_b = map(to_bf16, (w_br_conv, w_br_attn, w_o))
    w_in_b = to_bf16(w_in[0:1])
    vec = lambda g: g.reshape(depth, 1, D_MODEL)
    g_pre1, g_post1, g_pre2, g_post2 = map(vec, (g_pre1, g_post1, g_pre2, g_post2))
    pre_s = state_conv.transpose(0, 2, 1, 3)
    cache_k = cache_k.transpose(0, 1, 3, 4, 2)
    cache_v = cache_v.transpose(0, 1, 3, 4, 2)

    xp = x_prompt
    xs = x_sample.transpose(1, 0, 2)
    conv_p, k_p, v_p, conv_s = [], [], [], []
    windows = None
    win = min(WINDOW, seq)
    for l in range(depth):
        z, q, kv, gates, tail = _in_proj(xs, mod_all, g_pre1, w_in_b, conv_w, l, n_seq, pre_s)
        q_s = (q.reshape(n_new, n_seq, N_KV_HEADS, GROUP, HEAD_DIM)
               .transpose(1, 2, 0, 3, 4).reshape(n_seq, N_KV_HEADS, n_new * GROUP, HEAD_DIM))
        kv_s = kv.transpose(1, 0, 2)
        sink_rows = jnp.tile(sinks[l].reshape(N_KV_HEADS, 1, GROUP),
                             (1, n_new, 1)).reshape(N_KV_HEADS, n_new * GROUP, 1)
        o, *windows = _sample_attn(q_s, cache_k, cache_v, kv_s[:, :, :KV_WIDTH],
                                   kv_s[:, :, KV_WIDTH:], bias_s, sink_rows, l, windows)
        attn = (o.reshape(n_seq, N_KV_HEADS, n_new, GROUP, HEAD_DIM)
                .transpose(2, 0, 1, 3, 4).reshape(n_new, n_seq, ATTN_WIDTH))
        x1 = _mix_sample(xs, z, attn, gates, mod_all, g_post1, w_bc_b, w_ba_b, w_o_b, l, n_seq)
        xs, w1_b, w2_b = _mlp_stream_sample(x1, mod_all, g_pre2, g_post2, w_ff1, w_ff2, l, n_seq)
        conv_s.append(tail.transpose(1, 0, 2))

        z, attn, kv, gates, tail = _in_proj(xp, mod_all, g_pre1, w_in_b, conv_w, l, n_seq,
                                            attn=(sinks, bias_p))
        cast_next = (w_in, l + 1) if l + 1 < depth else None
        xp, *w_next = _post(xp, z, attn, gates, mod_all, g_post1, g_pre2, g_post2, w_bc_b, w_ba_b,
                            w_o_b, w1_b, w2_b, l, 0, n_seq, cast_next)
        w_in_b = w_next[0] if w_next else None
        conv_p.append(tail[:, V7X_SUBLANES - (CONV_WIDTH - 1):])
        k_p.append(kv[:, :, :KV_WIDTH].reshape(batch, win, N_KV_HEADS, HEAD_DIM))
        v_p.append(kv[:, :, KV_WIDTH:].reshape(batch, win, N_KV_HEADS, HEAD_DIM))

    k_s, v_s = (w.transpose(0, 1, 4, 2, 3) for w in windows)
    return (xp, xs.transpose(1, 0, 2), jnp.stack(conv_p), jnp.stack(k_p), jnp.stack(v_p),
            jnp.stack(conv_s), k_s, v_s)
```
